```python
import jax, jax.numpy as jnp
from jax import lax
import numpy as np

D_MODEL = 1024
BATCH = 4
SEQ = 4096
DEPTH = 2

HEAD_DIM = 64
A_HEADS = D_MODEL // 256
B_HEADS = D_MODEL // 256
C_HEADS = D_MODEL // 128
A_WIDTH = A_HEADS * HEAD_DIM
B_WIDTH = B_HEADS * HEAD_DIM
C_WIDTH = C_HEADS * HEAD_DIM
D_MIX = A_WIDTH + B_WIDTH + C_WIDTH
CHUNK = 128
LN_EPS = 1e-5
RW_DECAY_LORA = 64
RW_A_LORA = 64
RW_GATE_LORA = 128
GN_EPS = 64e-5
C_KV_GROUPS = 2
C_Q_PER_GROUP = C_HEADS // C_KV_GROUPS
C_KV_WIDTH = C_KV_GROUPS * HEAD_DIM
CMP_BLOCK = 32
CMP_STRIDE = 16
CMP_HIDDEN = 2 * HEAD_DIM
SLC_BLOCK = 64
SLC_TOPK = 16
WINDOW = 512
Q_BLOCK = 128
NEG_INF = -1e30
FORCE = 1e4
ROPE_THETA = 500000.0
ROPE_DIM = HEAD_DIM // 4
D_FF = 2816
N_EXPERTS = 8
TOP_K = 2
PLE_DIM = 256
RMS_EPS = 1e-6
N_DENSE = (DEPTH + 1) // 2
N_MOE = DEPTH // 2
A_SIZES = (A_WIDTH, A_WIDTH)
B_SIZES = (B_WIDTH, B_WIDTH, B_WIDTH, RW_DECAY_LORA, RW_A_LORA, RW_GATE_LORA)
C_SIZES = (C_WIDTH,) + (C_KV_WIDTH,) * 6 + (3 * C_HEADS,)
A_COLS = 2 * A_WIDTH
B_COLS = 3 * B_WIDTH + RW_DECAY_LORA + RW_A_LORA + RW_GATE_LORA
C_COLS = C_WIDTH + 6 * C_KV_WIDTH + 3 * C_HEADS
D_IN = A_COLS + B_COLS + C_COLS

kernel_name = "hybrid_gmlp_rwkv7_nsa_moe_block"


def split_cols(z, sizes):
    out, off = [], 0
    for s in sizes:
        out.append(z[..., off:off + s])
        off += s
    return out


def rms_norm(x, g):
    xf = x.astype(jnp.float32)
    y = xf * lax.rsqrt(jnp.mean(xf * xf, axis=-1, keepdims=True) + RMS_EPS)
    return (y * g.astype(jnp.float32)).astype(x.dtype)


def layer_norm(x, g, b, eps):
    xf = x.astype(jnp.float32)
    mu = jnp.mean(xf, axis=-1, keepdims=True)
    var = jnp.mean(jnp.square(xf - mu), axis=-1, keepdims=True)
    return ((xf - mu) * lax.rsqrt(var + eps) * g.astype(jnp.float32) + b.astype(jnp.float32)).astype(x.dtype)


def rope_tables(positions):
    inv = 1.0 / (ROPE_THETA ** (jnp.arange(0, ROPE_DIM, 2, dtype=jnp.float32) / ROPE_DIM))
    ang = positions.astype(jnp.float32)[..., None] * inv
    return jnp.cos(ang), jnp.sin(ang)


def apply_partial_rope(x, cos, sin):
    bshape = cos.shape[:2] + (1,) * (x.ndim - 3) + cos.shape[-1:]
    c = cos.reshape(bshape).astype(x.dtype)
    s = sin.reshape(bshape).astype(x.dtype)
    half = ROPE_DIM // 2
    x1, x2, xp = x[..., :half], x[..., half:ROPE_DIM], x[..., ROPE_DIM:]
    return jnp.concatenate([x1 * c - x2 * s, x1 * s + x2 * c, xp], axis=-1)


def masked_softmax(s, mask):
    s32 = jnp.where(mask, s.astype(jnp.float32), NEG_INF)
    pr = jax.nn.softmax(s32, axis=-1)
    return jnp.where(mask, pr, 0.0).astype(s.dtype)


def swiglu(x, w1, w3, w2):
    return (jax.nn.silu(x @ w1) * (x @ w3)) @ w2


def chunked_spatial_gating(z_a, ln_g, ln_b, w_s, b_s):
    Bn, S, _ = z_a.shape
    nc = S // CHUNK
    u, v = split_cols(jax.nn.gelu(z_a), A_SIZES)
    u = u.reshape(Bn, nc, CHUNK, A_HEADS, HEAD_DIM)
    v = layer_norm(v.reshape(Bn, nc, CHUNK, A_HEADS, HEAD_DIM), ln_g, ln_b, LN_EPS)
    causal = jnp.tril(jnp.ones((CHUNK, CHUNK), dtype=bool))
    w = jnp.where(causal[None], w_s, 0.0).astype(v.dtype)
    mixed = jnp.einsum('hts,bnshd->bnthd', w, v) + b_s.T[None, None, :, :, None]
    return (u * mixed).reshape(Bn, S, A_WIDTH)


def rwkv7_time_mix(z_b, mu, w0, w_up, a0, a_up, g_up, k_k, k_a, r_k, gn_g, gn_b):
    Bn, S, _ = z_b.shape
    f32 = jnp.float32
    z_prev = jnp.pad(z_b[:, :-1], ((0, 0), (1, 0), (0, 0)))
    z = z_b + (z_prev - z_b) * mu
    r, k, v, wd, ad, gd = split_cols(z, B_SIZES)
    w_logit = -jax.nn.softplus(-(w0 + jnp.tanh(wd) @ w_up).astype(f32)) - 0.5
    decay = jnp.exp(-jnp.exp(w_logit))
    a = jax.nn.sigmoid((a0 + ad @ a_up).astype(f32))
    g = jax.nn.sigmoid(gd) @ g_up
    hs = lambda t: t.astype(f32).reshape(Bn, S, B_HEADS, HEAD_DIM)
    kk = hs(k * k_k)
    kk = kk * lax.rsqrt(jnp.maximum(jnp.sum(kk * kk, axis=-1, keepdims=True), 1e-24))
    k_a_h = k_a.astype(f32).reshape(B_HEADS, HEAD_DIM)
    k = hs(k) * (1.0 + (hs(a) - 1.0) * k_a_h)
    r_h, v_h, a_h, w_h = hs(r), hs(v), hs(a), hs(decay)

    def step(state, inp):
        r_t, w_t, k_t, v_t, kk_t, a_t = inp
        sa = jnp.einsum('bhvk,bhk->bhv', state, -kk_t)
        state = (state * w_t[:, :, None, :] + sa[..., None] * (kk_t * a_t)[:, :, None, :]
                 + v_t[..., None] * k_t[:, :, None, :])
        return state, jnp.einsum('bhvk,bhk->bhv', state, r_t)

    xs = tuple(jnp.moveaxis(t, 1, 0) for t in (r_h, w_h, k, v_h, kk, a_h))
    s0 = jnp.zeros((Bn, B_HEADS, HEAD_DIM, HEAD_DIM), f32)
    _, ys = lax.scan(step, s0, xs)
    y = jnp.moveaxis(ys, 0, 1)
    mu_y = jnp.mean(y, axis=-1, keepdims=True)
    var_y = jnp.mean(jnp.square(y - mu_y), axis=-1, keepdims=True)
    y = ((y - mu_y) * lax.rsqrt(var_y + GN_EPS)).reshape(Bn, S, B_WIDTH)
    y = y * gn_g.astype(f32) + gn_b.astype(f32)
    bonus = jnp.sum(r_h * k * r_k.astype(f32), axis=-1, keepdims=True) * v_h
    y = y + bonus.reshape(Bn, S, B_WIDTH)
    return (y * g.astype(f32)).astype(z_b.dtype)


def nsa_attention(z_c, cos, sin, cmp_pos, kc_w1, kc_w2, vc_w1, vc_w2):
    Bn, S, _ = z_c.shape
    G, J, Dh = C_KV_GROUPS, C_Q_PER_GROUP, HEAD_DIM
    scale = HEAD_DIM ** -0.5
    q, kc, vc, ks, vs, kw, vw, gates = split_cols(z_c, C_SIZES)
    q = q.reshape(Bn, S, G, J, Dh)
    kv = lambda t: t.reshape(Bn, S, G, Dh)
    kc, vc, ks, vs, kw, vw = map(kv, (kc, vc, ks, vs, kw, vw))
    gates = jax.nn.sigmoid(gates.astype(jnp.float32)).astype(z_c.dtype).reshape(Bn, S, G, J, 3)
    q_rot = apply_partial_rope(q, cos, sin)
    ks = apply_partial_rope(ks, cos, sin)
    kw = apply_partial_rope(kw, cos, sin)

    n_cmp = (S - CMP_BLOCK) // CMP_STRIDE + 1
    cmp_idx = np.arange(n_cmp)[:, None] * CMP_STRIDE + np.arange(CMP_BLOCK)[None, :]
    cmp_end = jnp.asarray(cmp_idx[:, -1])

    def compress(t, w1, w2):
        blk = t[:, cmp_idx] + cmp_pos[None, None, :, None, :]
        blk = blk.transpose(0, 1, 3, 2, 4).reshape(Bn, n_cmp, G, CMP_BLOCK * Dh)
        return jax.nn.gelu(blk @ w1) @ w2

    k_cmp = compress(kc, kc_w1, kc_w2)
    v_cmp = compress(vc, vc_w1, vc_w2)

    n_slc = S // SLC_BLOCK
    n_sel = min(SLC_TOPK, n_slc)
    slc_start = np.arange(n_slc) * SLC_BLOCK
    overlap = jnp.asarray(((cmp_idx[:, :1] < slc_start[None, :] + SLC_BLOCK)
                           & (cmp_idx[:, -1:] >= slc_start[None, :])).astype(np.float32))
    ks_blk = ks.reshape(Bn, n_slc, SLC_BLOCK, G, Dh).transpose(0, 3, 1, 2, 4)
    vs_blk = vs.reshape(Bn, n_slc, SLC_BLOCK, G, Dh).transpose(0, 3, 1, 2, 4)
    kw_pad = jnp.pad(kw, ((0, 0), (WINDOW, 0), (0, 0), (0, 0)))
    vw_pad = jnp.pad(vw, ((0, 0), (WINDOW, 0), (0, 0), (0, 0)))
    bi = jnp.arange(Bn)[:, None, None, None]
    gi = jnp.arange(G)[None, :, None, None]
    m_ids = jnp.arange(n_slc)

    def query_block(qb):
        start = qb * Q_BLOCK
        t = start + jnp.arange(Q_BLOCK)
        q_raw = lax.dynamic_slice_in_dim(q, start, Q_BLOCK, axis=1)
        q_r = lax.dynamic_slice_in_dim(q_rot, start, Q_BLOCK, axis=1)
        g_t = lax.dynamic_slice_in_dim(gates, start, Q_BLOCK, axis=1)
        s_c = jnp.einsum('btgjd,bngd->bgjtn', q_raw, k_cmp) * scale
        p_c = masked_softmax(s_c, cmp_end[None, :] <= t[:, None])
        o_c = jnp.einsum('bgjtn,bngd->btgjd', p_c, v_cmp)
        imp = jnp.einsum('bgjtn,nm->bgtm', p_c.astype(jnp.float32), overlap)
        blk_t = (t // SLC_BLOCK)[:, None]
        valid = m_ids[None, :] <= blk_t
        forced = (m_ids[None, :] == 0) | (m_ids[None, :] == blk_t) | (m_ids[None, :] == blk_t - 1)
        imp = jnp.where(valid, imp + FORCE * forced.astype(jnp.float32), -FORCE)
        top_v, top_i = lax.top_k(imp, n_sel)
        sel_ok = top_v > -0.5 * FORCE
        k_sel = ks_blk[bi, gi, top_i]
        v_sel = vs_blk[bi, gi, top_i]
        kpos = top_i[..., None] * SLC_BLOCK + jnp.arange(SLC_BLOCK)
        m_s = (sel_ok[..., None] & (kpos <= t[None, None, :, None, None]))
        m_s = m_s.reshape(Bn, G, 1, Q_BLOCK, n_sel * SLC_BLOCK)
        s_s = jnp.einsum('btgjd,bgtksd->bgjtks', q_r, k_sel) * scale
        p_s = masked_softmax(s_s.reshape(Bn, G, J, Q_BLOCK, n_sel * SLC_BLOCK), m_s)
        p_s = p_s.reshape(Bn, G, J, Q_BLOCK, n_sel, SLC_BLOCK)
        o_s = jnp.einsum('bgjtks,bgtksd->btgjd', p_s, v_sel)
        k_win = lax.dynamic_slice_in_dim(kw_pad, start, WINDOW + Q_BLOCK, axis=1)
        v_win = lax.dynamic_slice_in_dim(vw_pad, start, WINDOW + Q_BLOCK, axis=1)
        kpos_w = start - WINDOW + jnp.arange(WINDOW + Q_BLOCK)
        m_w = ((kpos_w[None, :] <= t[:, None]) & (kpos_w[None, :] > t[:, None] - WINDOW)
               & (kpos_w[None, :] >= 0))
        s_w = jnp.einsum('btgjd,bkgd->bgjtk', q_r, k_win) * scale
        p_w = masked_softmax(s_w, m_w)
        o_w = jnp.einsum('bgjtk,bkgd->btgjd', p_w, v_win)
        o = g_t[..., 0:1] * o_c + g_t[..., 1:2] * o_s + g_t[..., 2:3] * o_w
        return o.reshape(Bn, Q_BLOCK, C_WIDTH)

    out = lax.map(query_block, jnp.arange(S // Q_BLOCK))
    return out.transpose(1, 0, 2, 3).reshape(Bn, S, C_WIDTH)


def moe_swiglu(x, router_w, router_b, w1, w3, w2):
    logits = (x @ router_w + router_b).astype(jnp.float32)
    top_v, top_i = lax.top_k(logits, TOP_K)
    top_p = jax.nn.softmax(top_v, axis=-1)
    gate = jnp.sum(jax.nn.one_hot(top_i, N_EXPERTS, dtype=jnp.float32) * top_p[..., None], axis=-2)
    gate = gate.astype(x.dtype)
    out = jnp.zeros_like(x)
    for e in range(N_EXPERTS):
        out = out + gate[..., e:e + 1] * swiglu(x, w1[e], w3[e], w2[e])
    return out


def setup_inputs(seed: int = 0) -> dict:
    key = jax.random.key(seed)
    ks = iter(jax.random.split(key, 64))
    f32 = jnp.float32

    def nrm(shape, scale):
        return scale * jax.random.normal(next(ks), shape, f32)

    def gain(shape):
        return 1.0 + 0.1 * jax.random.normal(next(ks), shape, f32)

    L = DEPTH
    x = nrm((BATCH, SEQ, D_MODEL), 1.0)
    p = nrm((DEPTH, BATCH, SEQ, PLE_DIM), 1.0)
    offset = jax.random.randint(next(ks), (BATCH, 1), 0, 1024, dtype=jnp.int32)
    positions = offset + jnp.arange(SEQ, dtype=jnp.int32)[None, :]
    return {
        "x": x, "p": p, "positions": positions,
        "g_mix": gain((L, D_MODEL)),
        "w_in": nrm((L, D_MODEL, D_IN), D_MODEL ** -0.5),
        "w_out": nrm((L, D_MIX, D_MODEL), D_MIX ** -0.5),
        "gm_ln_g": gain((L, A_HEADS, HEAD_DIM)),
        "gm_ln_b": nrm((L, A_HEADS, HEAD_DIM), 0.1),
        "gm_ws": nrm((L, A_HEADS, CHUNK, CHUNK), CHUNK ** -0.5),
        "gm_bs": gain((L, A_HEADS, CHUNK)),
        "rw_mu": jax.random.uniform(next(ks), (L, B_COLS), f32),
        "rw_w0": nrm((L, B_WIDTH), 0.5),
        "rw_w_up": nrm((L, RW_DECAY_LORA, B_WIDTH), 0.1),
        "rw_a0": nrm((L, B_WIDTH), 0.5),
        "rw_a_up": nrm((L, RW_A_LORA, B_WIDTH), RW_A_LORA ** -0.5),
        "rw_g_up": nrm((L, RW_GATE_LORA, B_WIDTH), RW_GATE_LORA ** -0.5),
        "rw_k_k": 0.85 + nrm((L, B_WIDTH), 0.1),
        "rw_k_a": gain((L, B_WIDTH)),
        "rw_r_k": nrm((L, B_HEADS, HEAD_DIM), 0.1),
        "rw_gn_g": gain((L, B_WIDTH)),
        "rw_gn_b": nrm((L, B_WIDTH), 0.1),
        "nsa_cmp_pos": nrm((L, CMP_BLOCK, HEAD_DIM), 0.1),
        "nsa_kc_w1": nrm((L, CMP_BLOCK * HEAD_DIM, CMP_HIDDEN), (CMP_BLOCK * HEAD_DIM) ** -0.5),
        "nsa_kc_w2": nrm((L, CMP_HIDDEN, HEAD_DIM), CMP_HIDDEN ** -0.5),
        "nsa_vc_w1": nrm((L, CMP_BLOCK * HEAD_DIM, CMP_HIDDEN), (CMP_BLOCK * HEAD_DIM) ** -0.5),
        "nsa_vc_w2": nrm((L, CMP_HIDDEN, HEAD_DIM), CMP_HIDDEN ** -0.5),
        "g_ffn": gain((L, D_MODEL)),
        "ffn_w1": nrm((N_DENSE, D_MODEL, D_FF), D_MODEL ** -0.5),
        "ffn_w3": nrm((N_DENSE, D_MODEL, D_FF), D_MODEL ** -0.5),
        "ffn_w2": nrm((N_DENSE, D_FF, D_MODEL), D_FF ** -0.5),
        "router_w": nrm((N_MOE, D_MODEL, N_EXPERTS), D_MODEL ** -0.5),
        "router_b": nrm((N_MOE, N_EXPERTS), 0.01),
        "moe_w1": nrm((N_MOE, N_EXPERTS, D_MODEL, D_FF), D_MODEL ** -0.5),
        "moe_w3": nrm((N_MOE, N_EXPERTS, D_MODEL, D_FF), D_MODEL ** -0.5),
        "moe_w2": nrm((N_MOE, N_EXPERTS, D_FF, D_MODEL), D_FF ** -0.5),
        "g_ple": gain((L, D_MODEL)),
        "ple_gate_w": nrm((L, D_MODEL, D_MODEL), D_MODEL ** -0.5),
        "ple_proj_w": nrm((L, PLE_DIM, D_MODEL), PLE_DIM ** -0.5),
        "g_final": gain((D_MODEL,)),
    }


def reference(x, p, positions, g_mix, w_in, w_out, gm_ln_g, gm_ln_b, gm_ws, gm_bs,
              rw_mu, rw_w0, rw_w_up, rw_a0, rw_a_up, rw_g_up, rw_k_k, rw_k_a, rw_r_k, rw_gn_g, rw_gn_b,
              nsa_cmp_pos, nsa_kc_w1, nsa_kc_w2, nsa_vc_w1, nsa_vc_w2,
              g_ffn, ffn_w1, ffn_w3, ffn_w2, router_w, router_b, moe_w1, moe_w3, moe_w2,
              g_ple, ple_gate_w, ple_proj_w, g_final):
    cos, sin = rope_tables(positions)
    h = x
    for i in range(DEPTH):
        hn = rms_norm(h, g_mix[i])
        z = hn @ w_in[i]
        z_a, z_b, z_c = split_cols(z, (A_COLS, B_COLS, C_COLS))
        y_a = chunked_spatial_gating(z_a, gm_ln_g[i], gm_ln_b[i], gm_ws[i], gm_bs[i])
        y_b = rwkv7_time_mix(z_b, rw_mu[i], rw_w0[i], rw_w_up[i], rw_a0[i], rw_a_up[i], rw_g_up[i],
                             rw_k_k[i], rw_k_a[i], rw_r_k[i], rw_gn_g[i], rw_gn_b[i])
        y_c = nsa_attention(z_c, cos, sin, nsa_cmp_pos[i], nsa_kc_w1[i], nsa_kc_w2[i],
                            nsa_vc_w1[i], nsa_vc_w2[i])
        h = h + jnp.concatenate([y_a, y_b, y_c], axis=-1) @ w_out[i]
        hn = rms_norm(h, g_ffn[i])
        if i % 2 == 0:
            j = i // 2
            f = swiglu(hn, ffn_w1[j], ffn_w3[j], ffn_w2[j])
        else:
            j = i // 2
            f = moe_swiglu(hn, router_w[j], router_b[j], moe_w1[j], moe_w3[j], moe_w2[j])
        h = h + f
        gate = jax.nn.sigmoid(rms_norm(h, g_ple[i]) @ ple_gate_w[i])
        h = h + (p[i] @ ple_proj_w[i]) * gate
    return rms_norm(h, g_final)
```

```python
import functools
import math

import jax
import jax.numpy as jnp
from jax import lax
from jax.experimental import pallas as pl
from jax.experimental.pallas import tpu as pltpu

F32 = jnp.float32
BF16 = jnp.bfloat16

D_MODEL = 1024
HEAD_DIM = 64
A_HEADS = 4
B_HEADS = 4
A_WIDTH = 256
B_WIDTH = 256
C_WIDTH = 512
CHUNK = 128
LN_EPS = 1e-5
GN_EPS = 64e-5
RMS_EPS = 1e-6
RW_LORA = 64
KV_GROUPS = 2
Q_PER_GROUP = 4
CMP_BLOCK = 32
CMP_STRIDE = 16
CMP_HIDDEN = 128
SLC_BLOCK = 64
SLC_TOPK = 16
WINDOW = 512
Q_BLOCK = 128
NEG_INF = -1e30
FORCE = 1e4
ROPE_THETA = 500000.0
ROPE_DIM = 16
D_FF = 2816
N_EXPERTS = 8
PLE_DIM = 256

LANES = 128
Z_B = 0
Z_A = 1024
Z_Q = 1536
Z_KV = 2048
Z_G = 2816
Z_W = 2944
N_GATE = 24

VMEM_LIMIT = 56 * 1024 * 1024
TM_PROJ = 512
TM_FFN = 512
TF_FFN = 1408
TS_PREP = 512
RW_T = 128
RW_C = 64
TK_SEL = 512

_NN = (((1,), (0,)), ((), ()))
_NT = (((1,), (1,)), ((), ()))


def _params(*sem):
    return pltpu.CompilerParams(dimension_semantics=sem, vmem_limit_bytes=VMEM_LIMIT)


def _dot(a, b, dn=_NN, passes=1):
    if passes == 6:
        return lax.dot_general(a.astype(F32), b.astype(F32), dn, precision=lax.Precision.HIGHEST,
                               preferred_element_type=F32)
    a_hi = a.astype(BF16)
    b_hi = b.astype(BF16)
    out = lax.dot_general(a_hi, b_hi, dn, preferred_element_type=F32)
    if passes == 1:
        return out
    a_lo = (a - a_hi.astype(F32)).astype(BF16)
    out = out + lax.dot_general(a_lo, b_hi, dn, preferred_element_type=F32)
    if passes == 2:
        return out
    b_lo = (b - b_hi.astype(F32)).astype(BF16)
    return out + lax.dot_general(a_hi, b_lo, dn, preferred_element_type=F32)


def _rms(x, g):
    return x * lax.rsqrt(jnp.mean(x * x, axis=-1, keepdims=True) + RMS_EPS) * g


def _sigmoid(x):
    return 1.0 / (1.0 + jnp.exp(-x))


def _div(x, d):
    return lax.shift_right_logical(x, jnp.int32(int(math.log2(d))))


def _head_block_ones(n):
    r = lax.broadcasted_iota(jnp.int32, (n, n), 0)
    c = lax.broadcasted_iota(jnp.int32, (n, n), 1)
    return jnp.where(_div(r, HEAD_DIM) == _div(c, HEAD_DIM), 1.0, 0.0).astype(F32)


def _rope_kernel(inv_ref, pos_ref, cos_ref, sin_ref):
    p = pos_ref[0].astype(F32)
    for j in range(ROPE_DIM // 2):
        ang = p * inv_ref[j]
        cos_ref[0, j] = jnp.cos(ang)
        sin_ref[0, j] = jnp.sin(ang)


def _rope_tables(positions):
    bn, s = positions.shape
    half = ROPE_DIM // 2
    inv = 1.0 / (ROPE_THETA ** (jnp.arange(0, ROPE_DIM, 2, dtype=F32) / ROPE_DIM))
    pos3 = positions.reshape(bn, s // LANES, LANES)
    cos, sin = pl.pallas_call(
        _rope_kernel,
        grid=(bn,),
        in_specs=[pl.BlockSpec(memory_space=pltpu.SMEM),
                  pl.BlockSpec((1, s // LANES, LANES), lambda b: (b, 0, 0))],
        out_specs=[pl.BlockSpec((1, half, s // LANES, LANES), lambda b: (b, 0, 0, 0))] * 2,
        out_shape=[jax.ShapeDtypeStruct((bn, half, s // LANES, LANES), F32)] * 2,
        compiler_params=_params("parallel"),
        name="rope_tables",
    )(inv, pos3)
    cos = cos.reshape(bn, half, s).transpose(0, 2, 1)
    sin = sin.reshape(bn, half, s).transpose(0, 2, 1)
    one = jnp.ones((bn, s, HEAD_DIM - ROPE_DIM), F32)
    zero = jnp.zeros((bn, s, HEAD_DIM - half), F32)
    tab_c = jnp.concatenate([cos, cos, one], axis=-1)
    tab_lo = jnp.concatenate([sin, zero], axis=-1)
    tab_hi = jnp.concatenate([zero[..., :half], sin, zero[..., :HEAD_DIM - ROPE_DIM]], axis=-1)
    rep = lambda t: jnp.tile(t, (1, 1, LANES // HEAD_DIM))
    return rep(tab_c), rep(tab_lo), rep(tab_hi)


def _proj_in_kernel(h_ref, g_ref, w_ref, z_ref):
    y = _rms(h_ref[...], g_ref[...])
    z_ref[...] = jnp.dot(y.astype(BF16), w_ref[...], preferred_element_type=F32)


def _proj_in(h, g, w):
    n = h.shape[0]
    tm = min(TM_PROJ, n)
    return pl.pallas_call(
        _proj_in_kernel,
        grid=(n // tm,),
        in_specs=[pl.BlockSpec((tm, D_MODEL), lambda i: (i, 0)),
                  pl.BlockSpec((1, D_MODEL), lambda i: (0, 0)),
                  pl.BlockSpec((D_MODEL, Z_W), lambda i: (0, 0))],
        out_specs=pl.BlockSpec((tm, Z_W), lambda i: (i, 0)),
        out_shape=jax.ShapeDtypeStruct((n, Z_W), F32),
        compiler_params=_params("parallel"),
        name="proj_in",
    )(h, g, w)


def _gmlp_kernel(z_ref, lng_ref, lnb_ref, ws_ref, bias_ref, o_ref):
    gz = jax.nn.gelu(z_ref[0])
    u = gz[:, :A_WIDTH]
    v = gz[:, A_WIDTH:]
    ones = _head_block_ones(A_WIDTH)
    mu = _dot(v, ones, passes=2) * (1.0 / HEAD_DIM)
    d = v - mu
    var = _dot(d * d, ones, passes=2) * (1.0 / HEAD_DIM)
    vn = d * lax.rsqrt(var + LN_EPS) * lng_ref[...] + lnb_ref[...]
    r = lax.broadcasted_iota(jnp.int32, (CHUNK, CHUNK), 0)
    c = lax.broadcasted_iota(jnp.int32, (CHUNK, CHUNK), 1)
    lane_head = _div(lax.broadcasted_iota(jnp.int32, (CHUNK, A_WIDTH), 1), HEAD_DIM)
    mixed = bias_ref[...]
    for hd in range(A_HEADS):
        w = jnp.where(c <= r, ws_ref[hd], 0.0)
        vh = jnp.where(lane_head == hd, vn, 0.0)
        mixed = mixed + _dot(w, vh)
    o_ref[0] = (u * mixed).astype(o_ref.dtype)


def _gmlp(z3, ln_g, ln_b, w_s, b_s):
    bn, s, _ = z3.shape
    bias = jnp.repeat(b_s.T, HEAD_DIM, axis=1)
    return pl.pallas_call(
        _gmlp_kernel,
        grid=(bn, s // CHUNK),
        in_specs=[pl.BlockSpec((1, CHUNK, 2 * A_WIDTH), lambda b, i: (b, i, Z_A // (2 * A_WIDTH))),
                  pl.BlockSpec((1, A_WIDTH), lambda b, i: (0, 0)),
                  pl.BlockSpec((1, A_WIDTH), lambda b, i: (0, 0)),
                  pl.BlockSpec((A_HEADS, CHUNK, CHUNK), lambda b, i: (0, 0, 0)),
                  pl.BlockSpec((CHUNK, A_WIDTH), lambda b, i: (0, 0))],
        out_specs=pl.BlockSpec((1, CHUNK, A_WIDTH), lambda b, i: (b, i, 0)),
        out_shape=jax.ShapeDtypeStruct((bn, s, A_WIDTH), BF16),
        compiler_params=_params("parallel", "parallel"),
        name="gmlp",
    )(z3, ln_g.reshape(1, A_WIDTH), ln_b.reshape(1, A_WIDTH), w_s, bias)


RW_PASSES = 1


def _rwkv_chunk(r_c, lw_c, k_c, v_c, kk_c, a_c, s0, c_len):
    nh = B_HEADS
    m = nh * c_len
    dot = functools.partial(_dot, passes=RW_PASSES)
    ti = lax.broadcasted_iota(jnp.int32, (c_len, c_len), 0)
    tj = lax.broadcasted_iota(jnp.int32, (c_len, c_len), 1)
    tril = jnp.where(tj <= ti, 1.0, 0.0).astype(F32)
    cum = _dot(tril, lw_c, passes=6)
    w_t = jnp.exp(cum)
    w_p = jnp.exp(cum - lw_c)
    w_i = jnp.exp(-cum)
    a_t = -kk_c * w_p
    b_t = kk_c * a_c * w_i
    k_t = k_c * w_i
    r_t = r_c * w_t
    w_c = w_t[c_len - 1:c_len, :]

    row_head = _div(lax.broadcasted_iota(jnp.int32, (m, B_WIDTH), 0), c_len)
    lane_head = _div(lax.broadcasted_iota(jnp.int32, (m, B_WIDTH), 1), HEAD_DIM)
    head_mask = row_head == lane_head
    rep = lambda x: jnp.concatenate([x] * nh, axis=0)
    stack = lambda x: jnp.where(head_mask, rep(x), 0.0)
    a_st, r_st, v_st = stack(a_t), stack(r_t), stack(v_c)

    x1 = jnp.concatenate([a_st, r_st], axis=0)
    x2 = jnp.concatenate([rep(b_t), rep(k_t)], axis=0)
    cross = dot(x1, x2, _NT)
    ri = lax.broadcasted_iota(jnp.int32, (m, m), 0)
    ci = lax.broadcasted_iota(jnp.int32, (m, m), 1)
    same = _div(ri, c_len) == _div(ci, c_len)
    strict = same & (ci < ri)
    incl = same & (ci <= ri)
    l_ab = jnp.where(strict, cross[:m, :m], 0.0)
    l_ak = jnp.where(strict, cross[:m, m:], 0.0)
    l_rb = jnp.where(incl, cross[m:, :m], 0.0)
    l_rk = jnp.where(incl, cross[m:, m:], 0.0)

    p_inv = jnp.where(ri == ci, 1.0, 0.0).astype(F32) + l_ab
    pw = l_ab
    for _ in range(int(math.log2(c_len)) - 1):
        pw = dot(pw, pw)
        p_inv = p_inv + dot(p_inv, pw)

    q_m = dot(p_inv, dot(l_ak, v_st))
    p_m = dot(p_inv, a_st)
    g_m = r_st + dot(l_rb, p_m)
    h_m = dot(l_rb, q_m) + dot(l_rk, v_st)

    y_st = dot(g_m, s0, _NT) + h_m
    u_st = dot(p_m, s0, _NT) + q_m
    s_new = s0 * w_c + dot(u_st.T, stack(b_t * w_c)) + dot(v_st.T, stack(k_t * w_c))
    y = y_st[0:c_len]
    for hd in range(1, nh):
        y = y + y_st[hd * c_len:(hd + 1) * c_len]
    return y, s_new


def _rwkv_kernel(z_ref, mu_ref, w0_ref, a0_ref, wwa_ref, gup_ref, kk_ref, ka_ref, rk_ref, gng_ref, gnb_ref,
                 o_ref, carry_ref, state_ref, *, c_len):
    @pl.when(pl.program_id(1) == 0)
    def _():
        carry_ref[...] = jnp.zeros_like(carry_ref)
        state_ref[...] = jnp.zeros_like(state_ref)

    zb = z_ref[0]
    t_len = zb.shape[0]
    row = lax.broadcasted_iota(jnp.int32, zb.shape, 0)
    z_prev = jnp.where(row == 0, carry_ref[...], pltpu.roll(zb, 1, axis=0))
    carry_ref[...] = zb[t_len - 1:t_len, :]
    zz = zb + (z_prev - zb) * mu_ref[...]
    r = zz[:, 0:B_WIDTH]
    k = zz[:, B_WIDTH:2 * B_WIDTH]
    v = zz[:, 2 * B_WIDTH:3 * B_WIDTH]
    wa = zz[:, 3 * B_WIDTH:3 * B_WIDTH + 2 * RW_LORA]
    gd = zz[:, 3 * B_WIDTH + 2 * RW_LORA:]
    lane = lax.broadcasted_iota(jnp.int32, wa.shape, 1)
    proj = _dot(jnp.where(lane < RW_LORA, jnp.tanh(wa), wa), wwa_ref[...])
    x = -(w0_ref[...] + proj[:, :B_WIDTH])
    softplus = jnp.maximum(x, 0.0) + jnp.log(1.0 + jnp.exp(-jnp.abs(x)))
    lw = -jnp.exp(-softplus - 0.5)
    a = _sigmoid(a0_ref[...] + proj[:, B_WIDTH:])
    g = _dot(_sigmoid(gd), gup_ref[...])
    ones = _head_block_ones(B_WIDTH)
    kk = k * kk_ref[...]
    kk = kk * lax.rsqrt(jnp.maximum(_dot(kk * kk, ones, passes=2), 1e-24))
    k2 = k * (1.0 + (a - 1.0) * ka_ref[...])

    s_cur = state_ref[...]
    ys = []
    for c in range(t_len // c_len):
        sl = slice(c * c_len, (c + 1) * c_len)
        y_c, s_cur = _rwkv_chunk(r[sl], lw[sl], k2[sl], v[sl], kk[sl], a[sl], s_cur, c_len)
        ys.append(y_c)
    state_ref[...] = s_cur
    y = jnp.concatenate(ys, axis=0)

    mu_y = _dot(y, ones, passes=2) * (1.0 / HEAD_DIM)
    d = y - mu_y
    var = _dot(d * d, ones, passes=2) * (1.0 / HEAD_DIM)
    yn = d * lax.rsqrt(var + GN_EPS) * gng_ref[...] + gnb_ref[...]
    bonus = _dot(r * k2 * rk_ref[...], ones, passes=2) * v
    o_ref[0] = ((yn + bonus) * g).astype(o_ref.dtype)


def _rwkv(z3, mu, w0, w_up, a0, a_up, g_up, k_k, k_a, r_k, gn_g, gn_b):
    bn, s, _ = z3.shape
    t_len = min(RW_T, s)
    zero = jnp.zeros((RW_LORA, B_WIDTH), F32)
    wwa = jnp.concatenate([jnp.concatenate([w_up, zero], axis=1),
                           jnp.concatenate([zero, a_up], axis=1)], axis=0).astype(BF16)
    row = lambda t: t.reshape(1, -1)
    vec = pl.BlockSpec((1, B_WIDTH), lambda b, i: (0, 0))
    return pl.pallas_call(
        functools.partial(_rwkv_kernel, c_len=RW_C),
        grid=(bn, s // t_len),
        in_specs=[pl.BlockSpec((1, t_len, 4 * B_WIDTH), lambda b, i: (b, i, Z_B // (4 * B_WIDTH))),
                  pl.BlockSpec((1, 4 * B_WIDTH), lambda b, i: (0, 0)),
                  vec, vec,
                  pl.BlockSpec((2 * RW_LORA, 2 * B_WIDTH), lambda b, i: (0, 0)),
                  pl.BlockSpec((2 * RW_LORA, B_WIDTH), lambda b, i: (0, 0)),
                  vec, vec, vec, vec, vec],
        out_specs=pl.BlockSpec((1, t_len, B_WIDTH), lambda b, i: (b, i, 0)),
        out_shape=jax.ShapeDtypeStruct((bn, s, B_WIDTH), BF16),
        scratch_shapes=[pltpu.VMEM((1, 4 * B_WIDTH), F32), pltpu.VMEM((B_WIDTH, B_WIDTH), F32)],
        compiler_params=_params("parallel", "arbitrary"),
        name="rwkv7",
    )(z3, row(mu), row(w0), row(a0), wwa, g_up.astype(BF16), row(k_k), row(k_a), row(r_k), row(gn_g), row(gn_b))


def _rope(x, tab_c, tab_lo, tab_hi):
    n = x.shape[-1]
    half = ROPE_DIM // 2
    return x * tab_c - pltpu.roll(x, n - half, axis=1) * tab_lo + pltpu.roll(x, half, axis=1) * tab_hi


def _nsa_prep_kernel(zq_ref, zkc_ref, zks_ref, zkw_ref, zg_ref, tc_ref, tl_ref, th_ref,
                     qraw_ref, qrot_ref, kc_ref, vc_ref, ks_ref, vs_ref, kw_ref, vw_ref, gate_ref):
    tab_c, tab_lo, tab_hi = tc_ref[0], tl_ref[0], th_ref[0]
    nrep = C_WIDTH // LANES
    wide = lambda t: jnp.concatenate([t] * nrep, axis=1)
    q = zq_ref[0] * (HEAD_DIM ** -0.5)
    q_rot = _rope(q, wide(tab_c), wide(tab_lo), wide(tab_hi))
    group0 = lax.broadcasted_iota(jnp.int32, (q.shape[0], LANES), 1) < HEAD_DIM

    def grouped(x):
        parts = []
        for g in range(KV_GROUPS):
            keep = group0 if g == 0 else jnp.logical_not(group0)
            for j in range(Q_PER_GROUP):
                parts.append(jnp.where(keep, x[:, j * LANES:(j + 1) * LANES], 0.0))
        return jnp.concatenate(parts, axis=1)

    qraw_ref[0] = grouped(q).astype(BF16)
    qrot_ref[0] = grouped(q_rot).astype(BF16)
    kc_ref[0] = zkc_ref[0][:, :LANES].astype(BF16)
    vc_ref[0] = zkc_ref[0][:, LANES:].astype(BF16)
    ks_ref[0] = _rope(zks_ref[0][:, :LANES], tab_c, tab_lo, tab_hi).astype(BF16)
    vs_ref[0] = zks_ref[0][:, LANES:].astype(BF16)
    kw_ref[0] = _rope(zkw_ref[0][:, :LANES], tab_c, tab_lo, tab_hi).astype(BF16)
    vw_ref[0] = zkw_ref[0][:, LANES:].astype(BF16)
    gate_ref[0] = _sigmoid(zg_ref[0])


def _nsa_prep(z3, tabs):
    bn, s, _ = z3.shape
    ts = min(TS_PREP, s)
    zspec = lambda width, off: pl.BlockSpec((1, ts, width), lambda b, i: (b, i, off // width))
    tspec = pl.BlockSpec((1, ts, LANES), lambda b, i: (b, i, 0))
    ospec = lambda width: pl.BlockSpec((1, ts, width), lambda b, i: (b, i, 0))
    osds = lambda width, dt: jax.ShapeDtypeStruct((bn, s, width), dt)
    qw = KV_GROUPS * Q_PER_GROUP * LANES
    return pl.pallas_call(
        _nsa_prep_kernel,
        grid=(bn, s // ts),
        in_specs=[zspec(C_WIDTH, Z_Q), zspec(2 * LANES, Z_KV), zspec(2 * LANES, Z_KV + 2 * LANES),
                  zspec(2 * LANES, Z_KV + 4 * LANES), zspec(LANES, Z_G), tspec, tspec, tspec],
        out_specs=[ospec(qw), ospec(qw)] + [ospec(LANES)] * 7,
        out_shape=[osds(qw, BF16), osds(qw, BF16)] + [osds(LANES, BF16)] * 6 + [osds(LANES, F32)],
        compiler_params=_params("parallel", "parallel"),
        name="nsa_prep",
    )(z3, z3, z3, z3, z3, *tabs)


def _nsa_compress_kernel(xk_ref, xv_ref, pos_ref, kw1a_ref, kw1b_ref, kw2_ref, vw1a_ref, vw1b_ref, vw2_ref,
                         ko_ref, vo_ref):
    n = xk_ref.shape[1]
    pos_a = pos_ref[0]
    pos_b = pos_ref[1]
    for x_ref, w1a, w1b, w2, o_ref in ((xk_ref, kw1a_ref, kw1b_ref, kw2_ref, ko_ref),
                                        (xv_ref, vw1a_ref, vw1b_ref, vw2_ref, vo_ref)):
        x = x_ref[0]
        first = _dot(x, w1a[...])
        second = _dot(x, w1b[...])
        pc = _dot(pos_a, w1a[...]) + _dot(pos_b, w1b[...])
        hid = jax.nn.gelu(first + pltpu.roll(second, n - 1, axis=0) + pc[0:1, :])
        o_ref[0] = _dot(hid, w2[...]).astype(o_ref.dtype)


def _expand_cmp_weights(w1, w2):
    half = CMP_BLOCK // 2
    w1r = w1.reshape(CMP_BLOCK, HEAD_DIM, CMP_HIDDEN)
    eye = jnp.eye(KV_GROUPS, dtype=F32)
    w1e = jnp.einsum('ldh,gk->lgdkh', w1r, eye).reshape(CMP_BLOCK * LANES, KV_GROUPS * CMP_HIDDEN)
    w2e = jnp.einsum('hd,gk->ghkd', w2, eye).reshape(KV_GROUPS * CMP_HIDDEN, LANES)
    return (w1e[:half * LANES].astype(BF16), w1e[half * LANES:].astype(BF16), w2e.astype(BF16))


def _nsa_compress(kc, vc, cmp_pos, kc_w1, kc_w2, vc_w1, vc_w2):
    bn, s, _ = kc.shape
    n = s // CMP_STRIDE
    xw = CMP_STRIDE * LANES
    xk = kc.reshape(bn, n, xw)
    xv = vc.reshape(bn, n, xw)
    pos = jnp.tile(cmp_pos[:, None, :], (1, KV_GROUPS, 1)).reshape(2, 1, xw)
    pos = jnp.broadcast_to(pos, (2, 8, xw)).astype(BF16)
    kw = _expand_cmp_weights(kc_w1, kc_w2)
    vw = _expand_cmp_weights(vc_w1, vc_w2)
    full = lambda a: pl.BlockSpec(a.shape, lambda b: (0,) * a.ndim)
    xspec = pl.BlockSpec((1, n, xw), lambda b: (b, 0, 0))
    ospec = pl.BlockSpec((1, n, LANES), lambda b: (b, 0, 0))
    return pl.pallas_call(
        _nsa_compress_kernel,
        grid=(bn,),
        in_specs=[xspec, xspec, full(pos)] + [full(a) for a in kw + vw],
        out_specs=[ospec, ospec],
        out_shape=[jax.ShapeDtypeStruct((bn, n, LANES), BF16)] * 2,
        compiler_params=_params("parallel"),
        name="nsa_compress",
    )(xk, xv, pos, *kw, *vw)


def _softmax_rows(s, mask):
    s = jnp.where(mask, s, NEG_INF)
    e = jnp.exp(s - jnp.max(s, axis=-1, keepdims=True))
    p = e / jnp.sum(e, axis=-1, keepdims=True)
    return jnp.where(mask, p, 0.0)


def _nsa_attn_kernel(qraw_ref, qrot_ref, kcmp_ref, vcmp_ref, ks_ref, vs_ref, kw_ref, vw_ref, gate_ref, o_ref,
                     m_ref, l_ref, acc_ref):
    tq = Q_BLOCK
    nh = KV_GROUPS * Q_PER_GROUP
    rows = nh * tq
    seq = ks_ref.shape[1]
    n_cmp = kcmp_ref.shape[1]
    n_slc = seq // SLC_BLOCK
    n_sel = min(SLC_TOPK, n_slc)
    qb = pl.program_id(1)
    t0 = qb * tq

    def stacked(ref):
        return jnp.concatenate([ref[0, :, i * LANES:(i + 1) * LANES] for i in range(nh)], axis=0)

    q_raw = stacked(qraw_ref)
    q_rot = stacked(qrot_ref)

    t_c = t0 + lax.broadcasted_iota(jnp.int32, (tq, n_cmp), 0)
    n_c = lax.broadcasted_iota(jnp.int32, (tq, n_cmp), 1)
    cmask = (n_c * CMP_STRIDE + (CMP_BLOCK - 1)) <= t_c
    s_c = _dot(q_raw, kcmp_ref[0], _NT).reshape(nh, tq, n_cmp)
    p_c = _softmax_rows(s_c, cmask[None])
    o_c = _dot(p_c.reshape(rows, n_cmp), vcmp_ref[0])

    m_o = lax.broadcasted_iota(jnp.int32, (n_slc, n_cmp), 0) * SLC_BLOCK
    n_o = lax.broadcasted_iota(jnp.int32, (n_slc, n_cmp), 1) * CMP_STRIDE
    overlap_t = jnp.where((n_o < m_o + SLC_BLOCK) & (n_o + (CMP_BLOCK - 1) >= m_o), 1.0, 0.0).astype(F32)
    m_i = lax.broadcasted_iota(jnp.int32, (n_slc, tq), 0)
    blk_t = _div(t0 + lax.broadcasted_iota(jnp.int32, (n_slc, tq), 1), SLC_BLOCK)
    valid = m_i <= blk_t
    forced = (m_i == 0) | (m_i == blk_t) | (m_i == blk_t - 1)
    eye_q = jnp.where(lax.broadcasted_iota(jnp.int32, (tq, tq), 0) == lax.broadcasted_iota(jnp.int32, (tq, tq), 1),
                      1.0, 0.0).astype(BF16)
    sel = []
    for g in range(KV_GROUPS):
        p_sum = p_c[g * Q_PER_GROUP]
        for j in range(1, Q_PER_GROUP):
            p_sum = p_sum + p_c[g * Q_PER_GROUP + j]
        imp = _dot(overlap_t, p_sum, _NT, passes=3)
        imp = jnp.where(valid, imp + jnp.where(forced, FORCE, 0.0), -FORCE)
        rank = jnp.zeros((n_slc, tq), F32)
        for mp in range(n_slc):
            other = imp[mp:mp + 1, :]
            rank = rank + jnp.where(m_i > mp, jnp.where(other >= imp, 1.0, 0.0), jnp.where(other > imp, 1.0, 0.0))
        sel_t = jnp.where((rank < n_sel) & valid, 1.0, 0.0).astype(BF16)
        sel.append(_dot(eye_q, sel_t, _NT).astype(BF16))

    m_ref[...] = jnp.full(m_ref.shape, NEG_INF, F32)
    l_ref[...] = jnp.zeros(l_ref.shape, F32)
    acc_ref[...] = jnp.zeros(acc_ref.shape, F32)
    t_s = t0 + lax.broadcasted_iota(jnp.int32, (tq, TK_SEL), 0)
    k_s = lax.broadcasted_iota(jnp.int32, (tq, TK_SEL), 1)
    m_e = lax.broadcasted_iota(jnp.int32, (n_slc, TK_SEL), 0)
    k_e = lax.broadcasted_iota(jnp.int32, (n_slc, TK_SEL), 1)

    def sel_tile(kt, carry):
        k0 = pl.multiple_of(kt * TK_SEL, TK_SEL)
        keys = ks_ref[0, pl.ds(k0, TK_SEL), :]
        vals = vs_ref[0, pl.ds(k0, TK_SEL), :]
        s_all = _dot(q_rot, keys, _NT)
        causal = (k_s + k0) <= t_s
        expand = jnp.where(_div(k_e + k0, SLC_BLOCK) == m_e, 1.0, 0.0).astype(BF16)
        grows = Q_PER_GROUP * tq
        for g in range(KV_GROUPS):
            rs = slice(g * grows, (g + 1) * grows)
            mask = ((_dot(sel[g], expand) > 0.5) & causal)[None]
            s = jnp.where(mask, s_all[rs].reshape(Q_PER_GROUP, tq, TK_SEL), NEG_INF)
            m_old = m_ref[rs, :].reshape(Q_PER_GROUP, tq, 1)
            m_new = jnp.maximum(m_old, jnp.max(s, axis=-1, keepdims=True))
            alpha = jnp.exp(m_old - m_new)
            p = jnp.where(mask, jnp.exp(s - m_new), 0.0)
            l_old = l_ref[rs, :].reshape(Q_PER_GROUP, tq, 1)
            l_ref[rs, :] = (alpha * l_old + jnp.sum(p, axis=-1, keepdims=True)).reshape(grows, 1)
            m_ref[rs, :] = m_new.reshape(grows, 1)
            acc_ref[rs, :] = alpha.reshape(grows, 1) * acc_ref[rs, :] + _dot(p.reshape(grows, TK_SEL), vals)
        return carry

    lax.fori_loop(0, (t0 + tq + TK_SEL - 1) // TK_SEL, sel_tile, 0)
    o_s = acc_ref[...] / l_ref[...]

    band = WINDOW + tq
    w0 = pl.multiple_of(jnp.maximum(t0 - WINDOW, 0), tq)
    t_w = t0 + lax.broadcasted_iota(jnp.int32, (tq, band), 0)
    k_w = w0 + lax.broadcasted_iota(jnp.int32, (tq, band), 1)
    wmask = (k_w <= t_w) & (k_w > t_w - WINDOW)
    s_w = _dot(q_rot, kw_ref[0, pl.ds(w0, band), :], _NT).reshape(nh, tq, band)
    p_w = _softmax_rows(s_w, wmask[None])
    o_w = _dot(p_w.reshape(rows, band), vw_ref[0, pl.ds(w0, band), :])

    gates = gate_ref[0]
    group0 = lax.broadcasted_iota(jnp.int32, (tq, LANES), 1) < HEAD_DIM
    for j in range(Q_PER_GROUP):
        outs = []
        for g in range(KV_GROUPS):
            hd = g * Q_PER_GROUP + j
            sl = slice(hd * tq, (hd + 1) * tq)
            col = hd * 3
            outs.append(gates[:, col:col + 1] * o_c[sl] + gates[:, col + 1:col + 2] * o_s[sl]
                        + gates[:, col + 2:col + 3] * o_w[sl])
        o_ref[0, :, j * LANES:(j + 1) * LANES] = jnp.where(group0, outs[0], outs[1]).astype(o_ref.dtype)


def _nsa_attn(q_raw, q_rot, k_cmp, v_cmp, ks, vs, kw, vw, gates):
    bn, s, qw = q_raw.shape
    n_cmp = k_cmp.shape[1]
    rows = KV_GROUPS * Q_PER_GROUP * Q_BLOCK
    qspec = pl.BlockSpec((1, Q_BLOCK, qw), lambda b, i: (b, i, 0))
    cspec = pl.BlockSpec((1, n_cmp, LANES), lambda b, i: (b, 0, 0))
    kspec = pl.BlockSpec((1, s, LANES), lambda b, i: (b, 0, 0))
    return pl.pallas_call(
        _nsa_attn_kernel,
        grid=(bn, s // Q_BLOCK),
        in_specs=[qspec, qspec, cspec, cspec, kspec, kspec, kspec, kspec,
                  pl.BlockSpec((1, Q_BLOCK, LANES), lambda b, i: (b, i, 0))],
        out_specs=pl.BlockSpec((1, Q_BLOCK, C_WIDTH), lambda b, i: (b, i, 0)),
        out_shape=jax.ShapeDtypeStruct((bn, s, C_WIDTH), BF16),
        scratch_shapes=[pltpu.VMEM((rows, 1), F32), pltpu.VMEM((rows, 1), F32), pltpu.VMEM((rows, LANES), F32)],
        compiler_params=_params("parallel", "arbitrary"),
        name="nsa_attn",
    )(q_raw, q_rot, k_cmp, v_cmp, ks, vs, kw, vw, gates)


def _out_proj_kernel(h_ref, ya_ref, yb_ref, yc_ref, w_ref, o_ref):
    acc = jnp.dot(ya_ref[...], w_ref[0:A_WIDTH, :], preferred_element_type=F32)
    acc = acc + jnp.dot(yb_ref[...], w_ref[A_WIDTH:A_WIDTH + B_WIDTH, :], preferred_element_type=F32)
    acc = acc + jnp.dot(yc_ref[...], w_ref[A_WIDTH + B_WIDTH:, :], preferred_element_type=F32)
    o_ref[...] = h_ref[...] + acc


def _out_proj(h, ya, yb, yc, w):
    n = h.shape[0]
    tm = min(TM_PROJ, n)
    tok = lambda width: pl.BlockSpec((tm, width), lambda i: (i, 0))
    return pl.pallas_call(
        _out_proj_kernel,
        grid=(n // tm,),
        in_specs=[tok(D_MODEL), tok(A_WIDTH), tok(B_WIDTH), tok(C_WIDTH),
                  pl.BlockSpec((D_MODEL, D_MODEL), lambda i: (0, 0))],
        out_specs=tok(D_MODEL),
        out_shape=jax.ShapeDtypeStruct((n, D_MODEL), F32),
        compiler_params=_params("parallel"),
        name="out_proj",
    )(h, ya, yb, yc, w)


def _ffn_kernel(h_ref, g_ref, rw_ref, rb_ref, w1_ref, w3_ref, w2_ref, o_ref, hn_ref, acc_ref, gate_ref, *, routed):
    e = pl.program_id(1)
    f = pl.program_id(2)

    @pl.when((e == 0) & (f == 0))
    def _():
        hn = _rms(h_ref[...], g_ref[...])
        hn_ref[...] = hn.astype(BF16)
        acc_ref[...] = jnp.zeros_like(acc_ref)
        if routed:
            lane = lax.broadcasted_iota(jnp.int32, gate_ref.shape, 1).astype(F32)
            logits = jnp.where(lane < N_EXPERTS, _dot(hn, rw_ref[...], passes=3) + rb_ref[...], NEG_INF)
            top1 = jnp.max(logits, axis=-1, keepdims=True)
            idx1 = jnp.min(jnp.where(logits == top1, lane, float(LANES)), axis=-1, keepdims=True)
            rest = jnp.where(lane == idx1, NEG_INF, logits)
            top2 = jnp.max(rest, axis=-1, keepdims=True)
            idx2 = jnp.min(jnp.where(rest == top2, lane, float(LANES)), axis=-1, keepdims=True)
            ex = jnp.exp(top2 - top1)
            gate_ref[...] = jnp.where(lane == idx1, 1.0 / (1.0 + ex), jnp.where(lane == idx2, ex / (1.0 + ex), 0.0))

    x = hn_ref[...]
    h1 = jnp.dot(x, w1_ref[0], preferred_element_type=F32)
    h3 = jnp.dot(x, w3_ref[0], preferred_element_type=F32)
    hid = h1 * _sigmoid(h1) * h3
    if routed:
        lane = lax.broadcasted_iota(jnp.int32, gate_ref.shape, 1)
        hid = hid * jnp.sum(jnp.where(lane == e, gate_ref[...], 0.0), axis=-1, keepdims=True)
    acc_ref[...] += jnp.dot(hid.astype(BF16), w2_ref[0], preferred_element_type=F32)

    @pl.when((e == pl.num_programs(1) - 1) & (f == pl.num_programs(2) - 1))
    def _():
        o_ref[...] = h_ref[...] + acc_ref[...]


def _ffn(h, g, w1, w3, w2, router_w=None, router_b=None):
    n = h.shape[0]
    ne = w1.shape[0]
    routed = router_w is not None
    tm = min(TM_FFN, n)
    if routed:
        rw = jnp.pad(router_w, ((0, 0), (0, LANES - N_EXPERTS)))
        rb = jnp.pad(router_b.reshape(1, -1), ((0, 0), (0, LANES - N_EXPERTS)))
    else:
        rw = jnp.zeros((D_MODEL, LANES), F32)
        rb = jnp.zeros((1, LANES), F32)
    return pl.pallas_call(
        functools.partial(_ffn_kernel, routed=routed),
        grid=(n // tm, ne, D_FF // TF_FFN),
        in_specs=[pl.BlockSpec((tm, D_MODEL), lambda i, e, f: (i, 0)),
                  pl.BlockSpec((1, D_MODEL), lambda i, e, f: (0, 0)),
                  pl.BlockSpec((D_MODEL, LANES), lambda i, e, f: (0, 0)),
                  pl.BlockSpec((1, LANES), lambda i, e, f: (0, 0)),
                  pl.BlockSpec((1, D_MODEL, TF_FFN), lambda i, e, f: (e, 0, f)),
                  pl.BlockSpec((1, D_MODEL, TF_FFN), lambda i, e, f: (e, 0, f)),
                  pl.BlockSpec((1, TF_FFN, D_MODEL), lambda i, e, f: (e, f, 0))],
        out_specs=pl.BlockSpec((tm, D_MODEL), lambda i, e, f: (i, 0)),
        out_shape=jax.ShapeDtypeStruct((n, D_MODEL), F32),
        scratch_shapes=[pltpu.VMEM((tm, D_MODEL), BF16), pltpu.VMEM((tm, D_MODEL), F32),
                        pltpu.VMEM((tm, LANES), F32)],
        compiler_params=_params("parallel", "arbitrary", "arbitrary"),
        name="moe_swiglu" if routed else "ffn_swiglu",
    )(h, g, rw, rb, w1, w3, w2)


def _ple_kernel(h_ref, p_ref, g_ref, wg_ref, wp_ref, gf_ref, o_ref, *, final):
    h = h_ref[...]
    gate = _sigmoid(jnp.dot(_rms(h, g_ref[...]).astype(BF16), wg_ref[...], preferred_element_type=F32))
    out = h + jnp.dot(p_ref[...].astype(BF16), wp_ref[...], preferred_element_type=F32) * gate
    if final:
        out = _rms(out, gf_ref[...])
    o_ref[...] = out


def _ple(h, p, g, wg, wp, g_final, final):
    n = h.shape[0]
    tm = min(TM_PROJ, n)
    return pl.pallas_call(
        functools.partial(_ple_kernel, final=final),
        grid=(n // tm,),
        in_specs=[pl.BlockSpec((tm, D_MODEL), lambda i: (i, 0)),
                  pl.BlockSpec((tm, PLE_DIM), lambda i: (i, 0)),
                  pl.BlockSpec((1, D_MODEL), lambda i: (0, 0)),
                  pl.BlockSpec((D_MODEL, D_MODEL), lambda i: (0, 0)),
                  pl.BlockSpec((PLE_DIM, D_MODEL), lambda i: (0, 0)),
                  pl.BlockSpec((1, D_MODEL), lambda i: (0, 0))],
        out_specs=pl.BlockSpec((tm, D_MODEL), lambda i: (i, 0)),
        out_shape=jax.ShapeDtypeStruct((n, D_MODEL), F32),
        compiler_params=_params("parallel"),
        name="ple",
    )(h, p, g, wg, wp, g_final)


def _q_perm():
    idx = []
    for j in range(Q_PER_GROUP):
        for g in range(KV_GROUPS):
            base = (g * Q_PER_GROUP + j) * HEAD_DIM
            idx.extend(range(base, base + HEAD_DIM))
    return jnp.asarray(idx, dtype=jnp.int32)


def _layout_w_in(w):
    a1 = 2 * A_WIDTH
    b1 = a1 + 4 * B_WIDTH
    q1 = b1 + C_WIDTH
    w = jnp.concatenate([w[:, a1:b1], w[:, :a1], w[:, b1:q1][:, _q_perm()], w[:, q1:]], axis=1)
    return jnp.pad(w, ((0, 0), (0, Z_W - w.shape[1]))).astype(BF16)


def _layout_w_out(w):
    c0 = A_WIDTH + B_WIDTH
    return jnp.concatenate([w[:c0], w[c0:][_q_perm()]], axis=0).astype(BF16)


def _mix_layer(h, bn, s, tabs, g_mix, w_in, w_out, gm_ln_g, gm_ln_b, gm_ws, gm_bs,
               rw_mu, rw_w0, rw_w_up, rw_a0, rw_a_up, rw_g_up, rw_k_k, rw_k_a, rw_r_k, rw_gn_g, rw_gn_b,
               nsa_cmp_pos, nsa_kc_w1, nsa_kc_w2, nsa_vc_w1, nsa_vc_w2):
    n = bn * s
    z = _proj_in(h, g_mix.reshape(1, -1), _layout_w_in(w_in))
    z3 = z.reshape(bn, s, Z_W)
    y_a = _gmlp(z3, gm_ln_g, gm_ln_b, gm_ws, gm_bs)
    y_b = _rwkv(z3, rw_mu, rw_w0, rw_w_up, rw_a0, rw_a_up, rw_g_up, rw_k_k, rw_k_a, rw_r_k, rw_gn_g, rw_gn_b)
    q_raw, q_rot, kc, vc, ks, vs, kw, vw, gates = _nsa_prep(z3, tabs)
    k_cmp, v_cmp = _nsa_compress(kc, vc, nsa_cmp_pos, nsa_kc_w1, nsa_kc_w2, nsa_vc_w1, nsa_vc_w2)
    y_c = _nsa_attn(q_raw, q_rot, k_cmp, v_cmp, ks, vs, kw, vw, gates)
    return _out_proj(h, y_a.reshape(n, -1), y_b.reshape(n, -1), y_c.reshape(n, -1), _layout_w_out(w_out))


def kernel(x, p, positions, g_mix, w_in, w_out, gm_ln_g, gm_ln_b, gm_ws, gm_bs, rw_mu, rw_w0, rw_w_up, rw_a0,
           rw_a_up, rw_g_up, rw_k_k, rw_k_a, rw_r_k, rw_gn_g, rw_gn_b, nsa_cmp_pos, nsa_kc_w1, nsa_kc_w2,
           nsa_vc_w1, nsa_vc_w2, g_ffn, ffn_w1, ffn_w3, ffn_w2, router_w, router_b, moe_w1, moe_w3, moe_w2,
           g_ple, ple_gate_w, ple_proj_w, g_final):
    bn, s, _ = x.shape
    n = bn * s
    depth = g_mix.shape[0]
    tabs = _rope_tables(positions)
    h = x.reshape(n, D_MODEL)
    for i in range(depth):
        h = _mix_layer(h, bn, s, tabs, g_mix[i], w_in[i], w_out[i], gm_ln_g[i], gm_ln_b[i], gm_ws[i], gm_bs[i],
                       rw_mu[i], rw_w0[i], rw_w_up[i], rw_a0[i], rw_a_up[i], rw_g_up[i], rw_k_k[i], rw_k_a[i],
                       rw_r_k[i], rw_gn_g[i], rw_gn_b[i], nsa_cmp_pos[i], nsa_kc_w1[i], nsa_kc_w2[i],
                       nsa_vc_w1[i], nsa_vc_w2[i])
        j = i // 2
        if i % 2 == 0:
            h = _ffn(h, g_ffn[i].reshape(1, -1), ffn_w1[j:j + 1].astype(BF16), ffn_w3[j:j + 1].astype(BF16),
                     ffn_w2[j:j + 1].astype(BF16))
        else:
            h = _ffn(h, g_ffn[i].reshape(1, -1), moe_w1[j].astype(BF16), moe_w3[j].astype(BF16),
                     moe_w2[j].astype(BF16), router_w[j], router_b[j])
        h = _ple(h, p[i].reshape(n, PLE_DIM), g_ple[i].reshape(1, -1), ple_gate_w[i].astype(BF16),
                 ple_proj_w[i].astype(BF16), g_final.reshape(1, -1), final=(i == depth - 1))
    return h.reshape(bn, s, D_MODEL)
```

```python
import functools
import math

import jax
import jax.numpy as jnp
from jax import lax
from jax.experimental import pallas as pl
from jax.experimental.pallas import tpu as pltpu

F32 = jnp.float32
BF16 = jnp.bfloat16

D_MODEL = 1024
HEAD_DIM = 64
A_HEADS = 4
B_HEADS = 4
A_WIDTH = 256
B_WIDTH = 256
C_WIDTH = 512
CHUNK = 128
LN_EPS = 1e-5
GN_EPS = 64e-5
RMS_EPS = 1e-6
RW_LORA = 64
KV_GROUPS = 2
Q_PER_GROUP = 4
CMP_BLOCK = 32
CMP_STRIDE = 16
CMP_HIDDEN = 128
SLC_BLOCK = 64
SLC_TOPK = 16
WINDOW = 512
Q_BLOCK = 128
NEG_INF = -1e30
FORCE = 1e4
ROPE_THETA = 500000.0
ROPE_DIM = 16
D_FF = 2816
N_EXPERTS = 8
TOP_K = 2
PLE_DIM = 256

LANES = 128
Z_B = 0
Z_A = 1024
Z_Q = 1536
Z_KV = 2048
Z_G = 2816
Z_W = 2944
N_GATE = 24

VMEM_LIMIT = 56 * 1024 * 1024
TM_PROJ = 512
TM_FFN = 512
TF_FFN = 1408
TM_GROUP = 512
TS_PREP = 512
RW_T = 256
RW_C = 64
TK_SEL = 512

_NN = (((1,), (0,)), ((), ()))
_NT = (((1,), (1,)), ((), ()))


def _params(*sem):
    return pltpu.CompilerParams(dimension_semantics=sem, vmem_limit_bytes=VMEM_LIMIT)


def _dot(a, b, dn=_NN, passes=1):
    if passes == 6:
        return lax.dot_general(a.astype(F32), b.astype(F32), dn, precision=lax.Precision.HIGHEST,
                               preferred_element_type=F32)
    a_hi = a.astype(BF16)
    b_hi = b.astype(BF16)
    out = lax.dot_general(a_hi, b_hi, dn, preferred_element_type=F32)
    if passes == 1:
        return out
    a_lo = (a - a_hi.astype(F32)).astype(BF16)
    out = out + lax.dot_general(a_lo, b_hi, dn, preferred_element_type=F32)
    if passes == 2:
        return out
    b_lo = (b - b_hi.astype(F32)).astype(BF16)
    return out + lax.dot_general(a_hi, b_lo, dn, preferred_element_type=F32)


def _rms(x, g):
    return x * lax.rsqrt(jnp.mean(x * x, axis=-1, keepdims=True) + RMS_EPS) * g


def _sigmoid(x):
    return 1.0 / (1.0 + jnp.exp(-x))


def _div(x, d):
    return lax.shift_right_logical(x, jnp.int32(int(math.log2(d))))


def _head_block_ones(n):
    r = lax.broadcasted_iota(jnp.int32, (n, n), 0)
    c = lax.broadcasted_iota(jnp.int32, (n, n), 1)
    return jnp.where(_div(r, HEAD_DIM) == _div(c, HEAD_DIM), 1.0, 0.0).astype(F32)


def _rope_kernel(inv_ref, pos_ref, cos_ref, sin_ref):
    p = pos_ref[0].astype(F32)
    for j in range(ROPE_DIM // 2):
        ang = p * inv_ref[j]
        cos_ref[0, j] = jnp.cos(ang)
        sin_ref[0, j] = jnp.sin(ang)


def _rope_tables(positions):
    bn, s = positions.shape
    half = ROPE_DIM // 2
    inv = 1.0 / (ROPE_THETA ** (jnp.arange(0, ROPE_DIM, 2, dtype=F32) / ROPE_DIM))
    pos3 = positions.reshape(bn, s // LANES, LANES)
    cos, sin = pl.pallas_call(
        _rope_kernel,
        grid=(bn,),
        in_specs=[pl.BlockSpec(memory_space=pltpu.SMEM),
                  pl.BlockSpec((1, s // LANES, LANES), lambda b: (b, 0, 0))],
        out_specs=[pl.BlockSpec((1, half, s // LANES, LANES), lambda b: (b, 0, 0, 0))] * 2,
        out_shape=[jax.ShapeDtypeStruct((bn, half, s // LANES, LANES), F32)] * 2,
        compiler_params=_params("parallel"),
        name="rope_tables",
    )(inv, pos3)
    cos = cos.reshape(bn, half, s).transpose(0, 2, 1)
    sin = sin.reshape(bn, half, s).transpose(0, 2, 1)
    one = jnp.ones((bn, s, HEAD_DIM - ROPE_DIM), F32)
    zero = jnp.zeros((bn, s, HEAD_DIM - half), F32)
    tab_c = jnp.concatenate([cos, cos, one], axis=-1)
    tab_lo = jnp.concatenate([sin, zero], axis=-1)
    tab_hi = jnp.concatenate([zero[..., :half], sin, zero[..., :HEAD_DIM - ROPE_DIM]], axis=-1)
    rep = lambda t: jnp.tile(t, (1, 1, LANES // HEAD_DIM))
    return rep(tab_c), rep(tab_lo), rep(tab_hi)


def _proj_in_kernel(h_ref, g_ref, w_ref, z_ref):
    y = _rms(h_ref[...], g_ref[...])
    z_ref[...] = jnp.dot(y.astype(BF16), w_ref[...], preferred_element_type=F32)


def _proj_in(h, g, w):
    n = h.shape[0]
    tm = min(TM_PROJ, n)
    return pl.pallas_call(
        _proj_in_kernel,
        grid=(n // tm,),
        in_specs=[pl.BlockSpec((tm, D_MODEL), lambda i: (i, 0)),
                  pl.BlockSpec((1, D_MODEL), lambda i: (0, 0)),
                  pl.BlockSpec((D_MODEL, Z_W), lambda i: (0, 0))],
        out_specs=pl.BlockSpec((tm, Z_W), lambda i: (i, 0)),
        out_shape=jax.ShapeDtypeStruct((n, Z_W), F32),
        compiler_params=_params("parallel"),
        name="proj_in",
    )(h, g, w)


def _gmlp_kernel(z_ref, lng_ref, lnb_ref, ws_ref, bias_ref, o_ref):
    gz = jax.nn.gelu(z_ref[0])
    u = gz[:, :A_WIDTH]
    v = gz[:, A_WIDTH:]
    ones = _head_block_ones(A_WIDTH)
    mu = _dot(v, ones, passes=2) * (1.0 / HEAD_DIM)
    d = v - mu
    var = _dot(d * d, ones, passes=2) * (1.0 / HEAD_DIM)
    vn = d * lax.rsqrt(var + LN_EPS) * lng_ref[...] + lnb_ref[...]
    r = lax.broadcasted_iota(jnp.int32, (CHUNK, CHUNK), 0)
    c = lax.broadcasted_iota(jnp.int32, (CHUNK, CHUNK), 1)
    lane_head = _div(lax.broadcasted_iota(jnp.int32, (CHUNK, A_WIDTH), 1), HEAD_DIM)
    mixed = bias_ref[...]
    for hd in range(A_HEADS):
        w = jnp.where(c <= r, ws_ref[hd], 0.0)
        vh = jnp.where(lane_head == hd, vn, 0.0)
        mixed = mixed + _dot(w, vh)
    o_ref[0] = (u * mixed).astype(o_ref.dtype)


def _gmlp(z3, ln_g, ln_b, w_s, b_s):
    bn, s, _ = z3.shape
    bias = jnp.repeat(b_s.T, HEAD_DIM, axis=1)
    return pl.pallas_call(
        _gmlp_kernel,
        grid=(bn, s // CHUNK),
        in_specs=[pl.BlockSpec((1, CHUNK, 2 * A_WIDTH), lambda b, i: (b, i, Z_A // (2 * A_WIDTH))),
                  pl.BlockSpec((1, A_WIDTH), lambda b, i: (0, 0)),
                  pl.BlockSpec((1, A_WIDTH), lambda b, i: (0, 0)),
                  pl.BlockSpec((A_HEADS, CHUNK, CHUNK), lambda b, i: (0, 0, 0)),
                  pl.BlockSpec((CHUNK, A_WIDTH), lambda b, i: (0, 0))],
        out_specs=pl.BlockSpec((1, CHUNK, A_WIDTH), lambda b, i: (b, i, 0)),
        out_shape=jax.ShapeDtypeStruct((bn, s, A_WIDTH), BF16),
        compiler_params=_params("parallel", "parallel"),
        name="gmlp",
    )(z3, ln_g.reshape(1, A_WIDTH), ln_b.reshape(1, A_WIDTH), w_s, bias)


RW_PASSES = 1


def _rwkv_chunk(r_c, lw_c, k_c, v_c, kk_c, a_c, s0, c_len):
    nh = B_HEADS
    m = nh * c_len
    dot = functools.partial(_dot, passes=RW_PASSES)
    ti = lax.broadcasted_iota(jnp.int32, (c_len, c_len), 0)
    tj = lax.broadcasted_iota(jnp.int32, (c_len, c_len), 1)
    tril = jnp.where(tj <= ti, 1.0, 0.0).astype(F32)
    cum = _dot(tril, lw_c, passes=6)
    w_t = jnp.exp(cum)
    w_p = jnp.exp(cum - lw_c)
    w_i = jnp.exp(-cum)
    a_t = -kk_c * w_p
    b_t = kk_c * a_c * w_i
    k_t = k_c * w_i
    r_t = r_c * w_t
    w_c = w_t[c_len - 1:c_len, :]

    row_head = _div(lax.broadcasted_iota(jnp.int32, (m, B_WIDTH), 0), c_len)
    lane_head = _div(lax.broadcasted_iota(jnp.int32, (m, B_WIDTH), 1), HEAD_DIM)
    head_mask = row_head == lane_head
    rep = lambda x: jnp.concatenate([x] * nh, axis=0)
    stack = lambda x: jnp.where(head_mask, rep(x), 0.0)
    a_st, r_st, v_st = stack(a_t), stack(r_t), stack(v_c)

    x1 = jnp.concatenate([a_st, r_st], axis=0)
    x2 = jnp.concatenate([rep(b_t), rep(k_t)], axis=0)
    cross = dot(x1, x2, _NT)
    ri = lax.broadcasted_iota(jnp.int32, (m, m), 0)
    ci = lax.broadcasted_iota(jnp.int32, (m, m), 1)
    same = _div(ri, c_len) == _div(ci, c_len)
    strict = same & (ci < ri)
    incl = same & (ci <= ri)
    l_ab = jnp.where(strict, cross[:m, :m], 0.0)
    l_ak = jnp.where(strict, cross[:m, m:], 0.0)
    l_rb = jnp.where(incl, cross[m:, :m], 0.0)
    l_rk = jnp.where(incl, cross[m:, m:], 0.0)

    p_inv = jnp.where(ri == ci, 1.0, 0.0).astype(F32) + l_ab
    pw = l_ab
    for _ in range(int(math.log2(c_len)) - 1):
        pw = dot(pw, pw)
        p_inv = p_inv + dot(p_inv, pw)

    q_m = dot(p_inv, dot(l_ak, v_st))
    p_m = dot(p_inv, a_st)
    g_m = r_st + dot(l_rb, p_m)
    h_m = dot(l_rb, q_m) + dot(l_rk, v_st)

    y_st = dot(g_m, s0, _NT) + h_m
    u_st = dot(p_m, s0, _NT) + q_m
    s_new = s0 * w_c + dot(u_st.T, stack(b_t * w_c)) + dot(v_st.T, stack(k_t * w_c))
    y = y_st[0:c_len]
    for hd in range(1, nh):
        y = y + y_st[hd * c_len:(hd + 1) * c_len]
    return y, s_new


def _rwkv_kernel(z_ref, mu_ref, w0_ref, a0_ref, wwa_ref, gup_ref, kk_ref, ka_ref, rk_ref, gng_ref, gnb_ref,
                 o_ref, carry_ref, state_ref, *, c_len):
    @pl.when(pl.program_id(1) == 0)
    def _():
        carry_ref[...] = jnp.zeros_like(carry_ref)
        state_ref[...] = jnp.zeros_like(state_ref)

    zb = z_ref[0]
    t_len = zb.shape[0]
    row = lax.broadcasted_iota(jnp.int32, zb.shape, 0)
    z_prev = jnp.where(row == 0, carry_ref[...], pltpu.roll(zb, 1, axis=0))
    carry_ref[...] = zb[t_len - 1:t_len, :]
    zz = zb + (z_prev - zb) * mu_ref[...]
    r = zz[:, 0:B_WIDTH]
    k = zz[:, B_WIDTH:2 * B_WIDTH]
    v = zz[:, 2 * B_WIDTH:3 * B_WIDTH]
    wa = zz[:, 3 * B_WIDTH:3 * B_WIDTH + 2 * RW_LORA]
    gd = zz[:, 3 * B_WIDTH + 2 * RW_LORA:]
    lane = lax.broadcasted_iota(jnp.int32, wa.shape, 1)
    proj = _dot(jnp.where(lane < RW_LORA, jnp.tanh(wa), wa), wwa_ref[...])
    x = -(w0_ref[...] + proj[:, :B_WIDTH])
    softplus = jnp.maximum(x, 0.0) + jnp.log(1.0 + jnp.exp(-jnp.abs(x)))
    lw = -jnp.exp(-softplus - 0.5)
    a = _sigmoid(a0_ref[...] + proj[:, B_WIDTH:])
    g = _dot(_sigmoid(gd), gup_ref[...])
    ones = _head_block_ones(B_WIDTH)
    kk = k * kk_ref[...]
    kk = kk * lax.rsqrt(jnp.maximum(_dot(kk * kk, ones, passes=2), 1e-24))
    k2 = k * (1.0 + (a - 1.0) * ka_ref[...])

    s_cur = state_ref[...]
    ys = []
    for c in range(t_len // c_len):
        sl = slice(c * c_len, (c + 1) * c_len)
        y_c, s_cur = _rwkv_chunk(r[sl], lw[sl], k2[sl], v[sl], kk[sl], a[sl], s_cur, c_len)
        ys.append(y_c)
    state_ref[...] = s_cur
    y = jnp.concatenate(ys, axis=0)

    mu_y = _dot(y, ones, passes=2) * (1.0 / HEAD_DIM)
    d = y - mu_y
    var = _dot(d * d, ones, passes=2) * (1.0 / HEAD_DIM)
    yn = d * lax.rsqrt(var + GN_EPS) * gng_ref[...] + gnb_ref[...]
    bonus = _dot(r * k2 * rk_ref[...], ones, passes=2) * v
    o_ref[0] = ((yn + bonus) * g).astype(o_ref.dtype)


def _rwkv(z3, mu, w0, w_up, a0, a_up, g_up, k_k, k_a, r_k, gn_g, gn_b):
    bn, s, _ = z3.shape
    t_len = min(RW_T, s)
    zero = jnp.zeros((RW_LORA, B_WIDTH), F32)
    wwa = jnp.concatenate([jnp.concatenate([w_up, zero], axis=1),
                           jnp.concatenate([zero, a_up], axis=1)], axis=0).astype(BF16)
    row = lambda t: t.reshape(1, -1)
    vec = pl.BlockSpec((1, B_WIDTH), lambda b, i: (0, 0))
    return pl.pallas_call(
        functools.partial(_rwkv_kernel, c_len=RW_C),
        grid=(bn, s // t_len),
        in_specs=[pl.BlockSpec((1, t_len, 4 * B_WIDTH), lambda b, i: (b, i, Z_B // (4 * B_WIDTH))),
                  pl.BlockSpec((1, 4 * B_WIDTH), lambda b, i: (0, 0)),
                  vec, vec,
                  pl.BlockSpec((2 * RW_LORA, 2 * B_WIDTH), lambda b, i: (0, 0)),
                  pl.BlockSpec((2 * RW_LORA, B_WIDTH), lambda b, i: (0, 0)),
                  vec, vec, vec, vec, vec],
        out_specs=pl.BlockSpec((1, t_len, B_WIDTH), lambda b, i: (b, i, 0)),
        out_shape=jax.ShapeDtypeStruct((bn, s, B_WIDTH), BF16),
        scratch_shapes=[pltpu.VMEM((1, 4 * B_WIDTH), F32), pltpu.VMEM((B_WIDTH, B_WIDTH), F32)],
        compiler_params=_params("parallel", "arbitrary"),
        name="rwkv7",
    )(z3, row(mu), row(w0), row(a0), wwa, g_up.astype(BF16), row(k_k), row(k_a), row(r_k), row(gn_g), row(gn_b))


def _rope(x, tab_c, tab_lo, tab_hi):
    n = x.shape[-1]
    half = ROPE_DIM // 2
    return x * tab_c - pltpu.roll(x, n - half, axis=1) * tab_lo + pltpu.roll(x, half, axis=1) * tab_hi


def _nsa_prep_kernel(zq_ref, zkc_ref, zks_ref, zkw_ref, zg_ref, tc_ref, tl_ref, th_ref,
                     qraw_ref, qrot_ref, kc_ref, vc_ref, ks_ref, vs_ref, kw_ref, vw_ref, gate_ref):
    tab_c, tab_lo, tab_hi = tc_ref[0], tl_ref[0], th_ref[0]
    nrep = C_WIDTH // LANES
    wide = lambda t: jnp.concatenate([t] * nrep, axis=1)
    q = zq_ref[0] * (HEAD_DIM ** -0.5)
    q_rot = _rope(q, wide(tab_c), wide(tab_lo), wide(tab_hi))
    group0 = lax.broadcasted_iota(jnp.int32, (q.shape[0], LANES), 1) < HEAD_DIM

    def grouped(x):
        parts = []
        for g in range(KV_GROUPS):
            keep = group0 if g == 0 else jnp.logical_not(group0)
            for j in range(Q_PER_GROUP):
                parts.append(jnp.where(keep, x[:, j * LANES:(j + 1) * LANES], 0.0))
        return jnp.concatenate(parts, axis=1)

    qraw_ref[0] = grouped(q).astype(BF16)
    qrot_ref[0] = grouped(q_rot).astype(BF16)
    kc_ref[0] = zkc_ref[0][:, :LANES].astype(BF16)
    vc_ref[0] = zkc_ref[0][:, LANES:].astype(BF16)
    ks_ref[0] = _rope(zks_ref[0][:, :LANES], tab_c, tab_lo, tab_hi).astype(BF16)
    vs_ref[0] = zks_ref[0][:, LANES:].astype(BF16)
    kw_ref[0] = _rope(zkw_ref[0][:, :LANES], tab_c, tab_lo, tab_hi).astype(BF16)
    vw_ref[0] = zkw_ref[0][:, LANES:].astype(BF16)
    gate_ref[0] = _sigmoid(zg_ref[0])


def _nsa_prep(z3, tabs):
    bn, s, _ = z3.shape
    ts = min(TS_PREP, s)
    zspec = lambda width, off: pl.BlockSpec((1, ts, width), lambda b, i: (b, i, off // width))
    tspec = pl.BlockSpec((1, ts, LANES), lambda b, i: (b, i, 0))
    ospec = lambda width: pl.BlockSpec((1, ts, width), lambda b, i: (b, i, 0))
    osds = lambda width, dt: jax.ShapeDtypeStruct((bn, s, width), dt)
    qw = KV_GROUPS * Q_PER_GROUP * LANES
    return pl.pallas_call(
        _nsa_prep_kernel,
        grid=(bn, s // ts),
        in_specs=[zspec(C_WIDTH, Z_Q), zspec(2 * LANES, Z_KV), zspec(2 * LANES, Z_KV + 2 * LANES),
                  zspec(2 * LANES, Z_KV + 4 * LANES), zspec(LANES, Z_G), tspec, tspec, tspec],
        out_specs=[ospec(qw), ospec(qw)] + [ospec(LANES)] * 7,
        out_shape=[osds(qw, BF16), osds(qw, BF16)] + [osds(LANES, BF16)] * 6 + [osds(LANES, F32)],
        compiler_params=_params("parallel", "parallel"),
        name="nsa_prep",
    )(z3, z3, z3, z3, z3, *tabs)


def _nsa_compress_kernel(xk_ref, xv_ref, pos_ref, kw1a_ref, kw1b_ref, kw2_ref, vw1a_ref, vw1b_ref, vw2_ref,
                         ko_ref, vo_ref):
    n = xk_ref.shape[1]
    pos_a = pos_ref[0]
    pos_b = pos_ref[1]
    for x_ref, w1a, w1b, w2, o_ref in ((xk_ref, kw1a_ref, kw1b_ref, kw2_ref, ko_ref),
                                        (xv_ref, vw1a_ref, vw1b_ref, vw2_ref, vo_ref)):
        x = x_ref[0]
        first = _dot(x, w1a[...])
        second = _dot(x, w1b[...])
        pc = _dot(pos_a, w1a[...]) + _dot(pos_b, w1b[...])
        hid = jax.nn.gelu(first + pltpu.roll(second, n - 1, axis=0) + pc[0:1, :])
        o_ref[0] = _dot(hid, w2[...]).astype(o_ref.dtype)


def _expand_cmp_weights(w1, w2):
    half = CMP_BLOCK // 2
    w1r = w1.reshape(CMP_BLOCK, HEAD_DIM, CMP_HIDDEN)
    eye = jnp.eye(KV_GROUPS, dtype=F32)
    w1e = jnp.einsum('ldh,gk->lgdkh', w1r, eye).reshape(CMP_BLOCK * LANES, KV_GROUPS * CMP_HIDDEN)
    w2e = jnp.einsum('hd,gk->ghkd', w2, eye).reshape(KV_GROUPS * CMP_HIDDEN, LANES)
    return (w1e[:half * LANES].astype(BF16), w1e[half * LANES:].astype(BF16), w2e.astype(BF16))


def _nsa_compress(kc, vc, cmp_pos, kc_w1, kc_w2, vc_w1, vc_w2):
    bn, s, _ = kc.shape
    n = s // CMP_STRIDE
    xw = CMP_STRIDE * LANES
    xk = kc.reshape(bn, n, xw)
    xv = vc.reshape(bn, n, xw)
    pos = jnp.tile(cmp_pos[:, None, :], (1, KV_GROUPS, 1)).reshape(2, 1, xw)
    pos = jnp.broadcast_to(pos, (2, 8, xw)).astype(BF16)
    kw = _expand_cmp_weights(kc_w1, kc_w2)
    vw = _expand_cmp_weights(vc_w1, vc_w2)
    full = lambda a: pl.BlockSpec(a.shape, lambda b: (0,) * a.ndim)
    xspec = pl.BlockSpec((1, n, xw), lambda b: (b, 0, 0))
    ospec = pl.BlockSpec((1, n, LANES), lambda b: (b, 0, 0))
    return pl.pallas_call(
        _nsa_compress_kernel,
        grid=(bn,),
        in_specs=[xspec, xspec, full(pos)] + [full(a) for a in kw + vw],
        out_specs=[ospec, ospec],
        out_shape=[jax.ShapeDtypeStruct((bn, n, LANES), BF16)] * 2,
        compiler_params=_params("parallel"),
        name="nsa_compress",
    )(xk, xv, pos, *kw, *vw)


def _softmax_rows(s, mask):
    s = jnp.where(mask, s, NEG_INF)
    e = jnp.exp(s - jnp.max(s, axis=-1, keepdims=True))
    p = e / jnp.sum(e, axis=-1, keepdims=True)
    return jnp.where(mask, p, 0.0)


def _nsa_attn_kernel(qraw_ref, qrot_ref, kcmp_ref, vcmp_ref, ks_ref, vs_ref, kw_ref, vw_ref, gate_ref, o_ref,
                     m_ref, l_ref, acc_ref):
    tq = Q_BLOCK
    nh = KV_GROUPS * Q_PER_GROUP
    rows = nh * tq
    seq = ks_ref.shape[1]
    n_cmp = kcmp_ref.shape[1]
    n_slc = seq // SLC_BLOCK
    n_sel = min(SLC_TOPK, n_slc)
    qb = pl.program_id(1)
    t0 = qb * tq

    def stacked(ref):
        return jnp.concatenate([ref[0, :, i * LANES:(i + 1) * LANES] for i in range(nh)], axis=0)

    q_raw = stacked(qraw_ref)
    q_rot = stacked(qrot_ref)

    t_c = t0 + lax.broadcasted_iota(jnp.int32, (tq, n_cmp), 0)
    n_c = lax.broadcasted_iota(jnp.int32, (tq, n_cmp), 1)
    cmask = (n_c * CMP_STRIDE + (CMP_BLOCK - 1)) <= t_c
    s_c = _dot(q_raw, kcmp_ref[0], _NT).reshape(nh, tq, n_cmp)
    p_c = _softmax_rows(s_c, cmask[None])
    o_c = _dot(p_c.reshape(rows, n_cmp), vcmp_ref[0])

    m_o = lax.broadcasted_iota(jnp.int32, (n_slc, n_cmp), 0) * SLC_BLOCK
    n_o = lax.broadcasted_iota(jnp.int32, (n_slc, n_cmp), 1) * CMP_STRIDE
    overlap_t = jnp.where((n_o < m_o + SLC_BLOCK) & (n_o + (CMP_BLOCK - 1) >= m_o), 1.0, 0.0).astype(F32)
    m_i = lax.broadcasted_iota(jnp.int32, (n_slc, tq), 0)
    blk_t = _div(t0 + lax.broadcasted_iota(jnp.int32, (n_slc, tq), 1), SLC_BLOCK)
    valid = m_i <= blk_t
    forced = (m_i == 0) | (m_i == blk_t) | (m_i == blk_t - 1)
    eye_q = jnp.where(lax.broadcasted_iota(jnp.int32, (tq, tq), 0) == lax.broadcasted_iota(jnp.int32, (tq, tq), 1),
                      1.0, 0.0).astype(BF16)
    sel = []
    for g in range(KV_GROUPS):
        p_sum = p_c[g * Q_PER_GROUP]
        for j in range(1, Q_PER_GROUP):
            p_sum = p_sum + p_c[g * Q_PER_GROUP + j]
        imp = _dot(overlap_t, p_sum, _NT, passes=3)
        imp = jnp.where(valid, imp + jnp.where(forced, FORCE, 0.0), -FORCE)
        rank = jnp.zeros((n_slc, tq), F32)
        for mp in range(n_slc):
            other = imp[mp:mp + 1, :]
            rank = rank + jnp.where(m_i > mp, jnp.where(other >= imp, 1.0, 0.0), jnp.where(other > imp, 1.0, 0.0))
        sel_t = jnp.where((rank < n_sel) & valid, 1.0, 0.0).astype(BF16)
        sel.append(_dot(eye_q, sel_t, _NT).astype(BF16))

    m_ref[...] = jnp.full(m_ref.shape, NEG_INF, F32)
    l_ref[...] = jnp.zeros(l_ref.shape, F32)
    acc_ref[...] = jnp.zeros(acc_ref.shape, F32)
    t_s = t0 + lax.broadcasted_iota(jnp.int32, (tq, TK_SEL), 0)
    k_s = lax.broadcasted_iota(jnp.int32, (tq, TK_SEL), 1)
    m_e = lax.broadcasted_iota(jnp.int32, (n_slc, TK_SEL), 0)
    k_e = lax.broadcasted_iota(jnp.int32, (n_slc, TK_SEL), 1)

    def sel_tile(kt, carry):
        k0 = pl.multiple_of(kt * TK_SEL, TK_SEL)
        keys = ks_ref[0, pl.ds(k0, TK_SEL), :]
        vals = vs_ref[0, pl.ds(k0, TK_SEL), :]
        s_all = _dot(q_rot, keys, _NT)
        causal = (k_s + k0) <= t_s
        expand = jnp.where(_div(k_e + k0, SLC_BLOCK) == m_e, 1.0, 0.0).astype(BF16)
        grows = Q_PER_GROUP * tq
        for g in range(KV_GROUPS):
            rs = slice(g * grows, (g + 1) * grows)
            mask = ((_dot(sel[g], expand) > 0.5) & causal)[None]
            s = jnp.where(mask, s_all[rs].reshape(Q_PER_GROUP, tq, TK_SEL), NEG_INF)
            m_old = m_ref[rs, :].reshape(Q_PER_GROUP, tq, 1)
            m_new = jnp.maximum(m_old, jnp.max(s, axis=-1, keepdims=True))
            alpha = jnp.exp(m_old - m_new)
            p = jnp.where(mask, jnp.exp(s - m_new), 0.0)
            l_old = l_ref[rs, :].reshape(Q_PER_GROUP, tq, 1)
            l_ref[rs, :] = (alpha * l_old + jnp.sum(p, axis=-1, keepdims=True)).reshape(grows, 1)
            m_ref[rs, :] = m_new.reshape(grows, 1)
            acc_ref[rs, :] = alpha.reshape(grows, 1) * acc_ref[rs, :] + _dot(p.reshape(grows, TK_SEL), vals)
        return carry

    lax.fori_loop(0, (t0 + tq + TK_SEL - 1) // TK_SEL, sel_tile, 0)
    o_s = acc_ref[...] / l_ref[...]

    band = WINDOW + tq
    w0 = pl.multiple_of(jnp.maximum(t0 - WINDOW, 0), tq)
    t_w = t0 + lax.broadcasted_iota(jnp.int32, (tq, band), 0)
    k_w = w0 + lax.broadcasted_iota(jnp.int32, (tq, band), 1)
    wmask = (k_w <= t_w) & (k_w > t_w - WINDOW)
    s_w = _dot(q_rot, kw_ref[0, pl.ds(w0, band), :], _NT).reshape(nh, tq, band)
    p_w = _softmax_rows(s_w, wmask[None])
    o_w = _dot(p_w.reshape(rows, band), vw_ref[0, pl.ds(w0, band), :])

    gates = gate_ref[0]
    group0 = lax.broadcasted_iota(jnp.int32, (tq, LANES), 1) < HEAD_DIM
    for j in range(Q_PER_GROUP):
        outs = []
        for g in range(KV_GROUPS):
            hd = g * Q_PER_GROUP + j
            sl = slice(hd * tq, (hd + 1) * tq)
            col = hd * 3
            outs.append(gates[:, col:col + 1] * o_c[sl] + gates[:, col + 1:col + 2] * o_s[sl]
                        + gates[:, col + 2:col + 3] * o_w[sl])
        o_ref[0, :, j * LANES:(j + 1) * LANES] = jnp.where(group0, outs[0], outs[1]).astype(o_ref.dtype)


def _nsa_attn(q_raw, q_rot, k_cmp, v_cmp, ks, vs, kw, vw, gates):
    bn, s, qw = q_raw.shape
    n_cmp = k_cmp.shape[1]
    rows = KV_GROUPS * Q_PER_GROUP * Q_BLOCK
    qspec = pl.BlockSpec((1, Q_BLOCK, qw), lambda b, i: (b, i, 0))
    cspec = pl.BlockSpec((1, n_cmp, LANES), lambda b, i: (b, 0, 0))
    kspec = pl.BlockSpec((1, s, LANES), lambda b, i: (b, 0, 0))
    return pl.pallas_call(
        _nsa_attn_kernel,
        grid=(bn, s // Q_BLOCK),
        in_specs=[qspec, qspec, cspec, cspec, kspec, kspec, kspec, kspec,
                  pl.BlockSpec((1, Q_BLOCK, LANES), lambda b, i: (b, i, 0))],
        out_specs=pl.BlockSpec((1, Q_BLOCK, C_WIDTH), lambda b, i: (b, i, 0)),
        out_shape=jax.ShapeDtypeStruct((bn, s, C_WIDTH), BF16),
        scratch_shapes=[pltpu.VMEM((rows, 1), F32), pltpu.VMEM((rows, 1), F32), pltpu.VMEM((rows, LANES), F32)],
        compiler_params=_params("parallel", "arbitrary"),
        name="nsa_attn",
    )(q_raw, q_rot, k_cmp, v_cmp, ks, vs, kw, vw, gates)


def _out_proj_kernel(h_ref, ya_ref, yb_ref, yc_ref, w_ref, o_ref):
    acc = jnp.dot(ya_ref[...], w_ref[0:A_WIDTH, :], preferred_element_type=F32)
    acc = acc + jnp.dot(yb_ref[...], w_ref[A_WIDTH:A_WIDTH + B_WIDTH, :], preferred_element_type=F32)
    acc = acc + jnp.dot(yc_ref[...], w_ref[A_WIDTH + B_WIDTH:, :], preferred_element_type=F32)
    o_ref[...] = h_ref[...] + acc


def _out_proj(h, ya, yb, yc, w):
    n = h.shape[0]
    tm = min(TM_PROJ, n)
    tok = lambda width: pl.BlockSpec((tm, width), lambda i: (i, 0))
    return pl.pallas_call(
        _out_proj_kernel,
        grid=(n // tm,),
        in_specs=[tok(D_MODEL), tok(A_WIDTH), tok(B_WIDTH), tok(C_WIDTH),
                  pl.BlockSpec((D_MODEL, D_MODEL), lambda i: (0, 0))],
        out_specs=tok(D_MODEL),
        out_shape=jax.ShapeDtypeStruct((n, D_MODEL), F32),
        compiler_params=_params("parallel"),
        name="out_proj",
    )(h, ya, yb, yc, w)


def _swiglu_step(x, w1_ref, w3_ref, w2_ref):
    h1 = jnp.dot(x, w1_ref[0], preferred_element_type=F32)
    h3 = jnp.dot(x, w3_ref[0], preferred_element_type=F32)
    hid = h1 * _sigmoid(h1) * h3
    return jnp.dot(hid.astype(BF16), w2_ref[0], preferred_element_type=F32)


def _ffn_kernel(h_ref, g_ref, w1_ref, w3_ref, w2_ref, o_ref, hn_ref, acc_ref):
    f = pl.program_id(1)

    @pl.when(f == 0)
    def _():
        hn_ref[...] = _rms(h_ref[...], g_ref[...]).astype(BF16)
        acc_ref[...] = jnp.zeros_like(acc_ref)

    acc_ref[...] += _swiglu_step(hn_ref[...], w1_ref, w3_ref, w2_ref)

    @pl.when(f == pl.num_programs(1) - 1)
    def _():
        o_ref[...] = h_ref[...] + acc_ref[...]


def _ffn(h, g, w1, w3, w2):
    n = h.shape[0]
    tm = min(TM_FFN, n)
    return pl.pallas_call(
        _ffn_kernel,
        grid=(n // tm, D_FF // TF_FFN),
        in_specs=[pl.BlockSpec((tm, D_MODEL), lambda i, f: (i, 0)),
                  pl.BlockSpec((1, D_MODEL), lambda i, f: (0, 0)),
                  pl.BlockSpec((1, D_MODEL, TF_FFN), lambda i, f: (0, 0, f)),
                  pl.BlockSpec((1, D_MODEL, TF_FFN), lambda i, f: (0, 0, f)),
                  pl.BlockSpec((1, TF_FFN, D_MODEL), lambda i, f: (0, f, 0))],
        out_specs=pl.BlockSpec((tm, D_MODEL), lambda i, f: (i, 0)),
        out_shape=jax.ShapeDtypeStruct((n, D_MODEL), F32),
        scratch_shapes=[pltpu.VMEM((tm, D_MODEL), BF16), pltpu.VMEM((tm, D_MODEL), F32)],
        compiler_params=_params("parallel", "arbitrary"),
        name="ffn_swiglu",
    )(h, g, w1, w3, w2)


META_E1, META_E2, META_R1, META_R2, META_P1, META_P2 = range(6)


def _router_kernel(h_ref, g_ref, rw_ref, rb_ref, hn_ref, meta_ref, cnt_ref, carry_ref):
    @pl.when(pl.program_id(0) == 0)
    def _():
        carry_ref[...] = jnp.zeros_like(carry_ref)

    hn = _rms(h_ref[...], g_ref[...])
    hn_ref[...] = hn
    tm = hn.shape[0]
    lane = lax.broadcasted_iota(jnp.int32, (tm, LANES), 1).astype(F32)
    logits = jnp.where(lane < N_EXPERTS, _dot(hn, rw_ref[...], passes=3) + rb_ref[...], NEG_INF)
    top1 = jnp.max(logits, axis=-1, keepdims=True)
    idx1 = jnp.min(jnp.where(logits == top1, lane, float(LANES)), axis=-1, keepdims=True)
    rest = jnp.where(lane == idx1, NEG_INF, logits)
    top2 = jnp.max(rest, axis=-1, keepdims=True)
    idx2 = jnp.min(jnp.where(rest == top2, lane, float(LANES)), axis=-1, keepdims=True)
    ex = jnp.exp(top2 - top1)
    picked = jnp.where((lane == idx1) | (lane == idx2), 1.0, 0.0)
    r = lax.broadcasted_iota(jnp.int32, (tm, tm), 0)
    c = lax.broadcasted_iota(jnp.int32, (tm, tm), 1)
    before = _dot(jnp.where(c < r, 1.0, 0.0), picked) + carry_ref[...]
    rank1 = jnp.sum(jnp.where(lane == idx1, before, 0.0), axis=-1, keepdims=True)
    rank2 = jnp.sum(jnp.where(lane == idx2, before, 0.0), axis=-1, keepdims=True)
    carry_ref[...] += jnp.sum(picked, axis=0, keepdims=True)
    cnt_ref[...] = carry_ref[...]
    meta = jnp.zeros((tm, LANES), F32)
    for pos, val in ((META_E1, idx1), (META_E2, idx2), (META_R1, rank1), (META_R2, rank2),
                     (META_P1, 1.0 / (1.0 + ex)), (META_P2, ex / (1.0 + ex))):
        meta = jnp.where(lane == pos, val, meta)
    meta_ref[...] = meta


def _dispatch_kernel(d1_ref, d2_ref, hn_ref, xs_in_ref, xs_ref, sem):
    del xs_in_ref
    tm = hn_ref.shape[0]

    def row_copy(r, dest):
        return pltpu.make_async_copy(hn_ref.at[pl.ds(r, 1)], xs_ref.at[pl.ds(dest, 1)], sem)

    def issue(r, carry):
        row_copy(r, d1_ref[0, 0, r]).start()
        row_copy(r, d2_ref[0, 0, r]).start()
        return carry

    lax.fori_loop(0, tm, issue, 0, unroll=8)
    for _ in range(2):
        pltpu.make_async_copy(hn_ref, xs_ref.at[pl.ds(0, tm)], sem).wait()


def _grouped_kernel(te_ref, nu_ref, xs_ref, w1_ref, w3_ref, w2_ref, ys_ref, acc_ref):
    i = pl.program_id(0)
    f = pl.program_id(1)
    last = f == pl.num_programs(1) - 1
    used = i < nu_ref[0]

    @pl.when(used & (f == 0))
    def _():
        acc_ref[...] = jnp.zeros_like(acc_ref)

    @pl.when(used)
    def _():
        acc_ref[...] += _swiglu_step(xs_ref[...].astype(BF16), w1_ref, w3_ref, w2_ref)

    @pl.when(used & last)
    def _():
        ys_ref[...] = acc_ref[...]

    @pl.when(jnp.logical_not(used) & last)
    def _():
        ys_ref[...] = jnp.zeros_like(ys_ref)


def _combine_kernel(d1_ref, d2_ref, h_ref, meta_ref, ys_ref, o_ref, buf1, buf2, sem):
    tm = h_ref.shape[0]

    def row_copy(dest, buf, r):
        return pltpu.make_async_copy(ys_ref.at[pl.ds(dest, 1)], buf.at[pl.ds(r, 1)], sem)

    def issue(r, carry):
        row_copy(d1_ref[0, 0, r], buf1, r).start()
        row_copy(d2_ref[0, 0, r], buf2, r).start()
        return carry

    lax.fori_loop(0, tm, issue, 0, unroll=8)
    for buf in (buf1, buf2):
        pltpu.make_async_copy(ys_ref.at[pl.ds(0, tm)], buf, sem).wait()
    meta = meta_ref[...]
    o_ref[...] = (h_ref[...] + meta[:, META_P1:META_P1 + 1] * buf1[...]
                  + meta[:, META_P2:META_P2 + 1] * buf2[...])


def _moe(h, g, router_w, router_b, w1, w3, w2):
    n = h.shape[0]
    tm = min(TM_FFN, n)
    tg = min(TM_GROUP, n)
    nt = n // tm
    rw = jnp.pad(router_w, ((0, 0), (0, LANES - N_EXPERTS)))
    rb = jnp.pad(router_b.reshape(1, -1), ((0, 0), (0, LANES - N_EXPERTS)))
    tok = lambda width: pl.BlockSpec((tm, width), lambda i: (i, 0))
    hn, meta, cnt = pl.pallas_call(
        _router_kernel,
        grid=(nt,),
        in_specs=[tok(D_MODEL), pl.BlockSpec((1, D_MODEL), lambda i: (0, 0)),
                  pl.BlockSpec((D_MODEL, LANES), lambda i: (0, 0)), pl.BlockSpec((1, LANES), lambda i: (0, 0))],
        out_specs=[tok(D_MODEL), tok(LANES), pl.BlockSpec((1, LANES), lambda i: (0, 0))],
        out_shape=[jax.ShapeDtypeStruct((n, D_MODEL), F32), jax.ShapeDtypeStruct((n, LANES), F32),
                   jax.ShapeDtypeStruct((1, LANES), F32)],
        scratch_shapes=[pltpu.VMEM((1, LANES), F32)],
        compiler_params=_params("arbitrary"),
        name="moe_router",
    )(h, g, rw, rb)

    counts = cnt[0, :N_EXPERTS].astype(jnp.int32)
    padded = ((counts + tg - 1) // tg) * tg
    ends = jnp.cumsum(padded)
    offs = ends - padded
    rows = TOP_K * n + N_EXPERTS * tg
    n_tiles = rows // tg
    tile_start = jnp.arange(n_tiles, dtype=jnp.int32) * tg
    n_used = (ends[-1:] // tg).astype(jnp.int32)
    tile_expert = jnp.sum(tile_start[:, None] >= ends[None, :], axis=1).astype(jnp.int32)
    tile_expert = jnp.minimum(tile_expert, tile_expert[n_used[0] - 1])
    onehot = lambda e: (e[:, None] == jnp.arange(N_EXPERTS, dtype=jnp.int32)[None, :]).astype(jnp.int32)
    dest = lambda e, r: (jnp.sum(onehot(e.astype(jnp.int32)) * offs[None, :], axis=1)
                         + r.astype(jnp.int32)).reshape(nt, 1, tm)
    d1 = dest(meta[:, META_E1], meta[:, META_R1])
    d2 = dest(meta[:, META_E2], meta[:, META_R2])
    dspec = pl.BlockSpec((1, 1, tm), lambda i: (i, 0, 0), memory_space=pltpu.SMEM)

    xs = pl.pallas_call(
        _dispatch_kernel,
        grid=(nt,),
        in_specs=[dspec, dspec, tok(D_MODEL), pl.BlockSpec(memory_space=pl.ANY)],
        out_specs=pl.BlockSpec(memory_space=pl.ANY),
        out_shape=jax.ShapeDtypeStruct((rows, D_MODEL), F32),
        scratch_shapes=[pltpu.SemaphoreType.DMA(())],
        input_output_aliases={3: 0},
        compiler_params=_params("arbitrary"),
        name="moe_dispatch",
    )(d1, d2, hn, jnp.zeros((rows, D_MODEL), F32))

    nf = D_FF // TF_FFN
    fidx = lambda i, f, nu: jnp.where(i < nu[0], f, nf - 1)
    ys = pl.pallas_call(
        _grouped_kernel,
        grid_spec=pltpu.PrefetchScalarGridSpec(
            num_scalar_prefetch=2,
            grid=(n_tiles, D_FF // TF_FFN),
            in_specs=[pl.BlockSpec((tg, D_MODEL), lambda i, f, te, nu: (jnp.minimum(i, nu[0] - 1), 0)),
                      pl.BlockSpec((1, D_MODEL, TF_FFN), lambda i, f, te, nu: (te[i], 0, fidx(i, f, nu))),
                      pl.BlockSpec((1, D_MODEL, TF_FFN), lambda i, f, te, nu: (te[i], 0, fidx(i, f, nu))),
                      pl.BlockSpec((1, TF_FFN, D_MODEL), lambda i, f, te, nu: (te[i], fidx(i, f, nu), 0))],
            out_specs=pl.BlockSpec((tg, D_MODEL), lambda i, f, te, nu: (i, 0)),
            scratch_shapes=[pltpu.VMEM((tg, D_MODEL), F32)]),
        out_shape=jax.ShapeDtypeStruct((rows, D_MODEL), F32),
        compiler_params=_params("arbitrary", "arbitrary"),
        name="moe_grouped",
    )(tile_expert, n_used, xs, w1, w3, w2)

    return pl.pallas_call(
        _combine_kernel,
        grid=(nt,),
        in_specs=[dspec, dspec, tok(D_MODEL), tok(LANES), pl.BlockSpec(memory_space=pl.ANY)],
        out_specs=tok(D_MODEL),
        out_shape=jax.ShapeDtypeStruct((n, D_MODEL), F32),
        scratch_shapes=[pltpu.VMEM((tm, D_MODEL), F32), pltpu.VMEM((tm, D_MODEL), F32),
                        pltpu.SemaphoreType.DMA(())],
        compiler_params=_params("arbitrary"),
        name="moe_combine",
    )(d1, d2, h, meta, ys)


def _ple_kernel(h_ref, p_ref, g_ref, wg_ref, wp_ref, gf_ref, o_ref, *, final):
    h = h_ref[...]
    gate = _sigmoid(jnp.dot(_rms(h, g_ref[...]).astype(BF16), wg_ref[...], preferred_element_type=F32))
    out = h + jnp.dot(p_ref[...].astype(BF16), wp_ref[...], preferred_element_type=F32) * gate
    if final:
        out = _rms(out, gf_ref[...])
    o_ref[...] = out


def _ple(h, p, g, wg, wp, g_final, final):
    n = h.shape[0]
    tm = min(TM_PROJ, n)
    return pl.pallas_call(
        functools.partial(_ple_kernel, final=final),
        grid=(n // tm,),
        in_specs=[pl.BlockSpec((tm, D_MODEL), lambda i: (i, 0)),
                  pl.BlockSpec((tm, PLE_DIM), lambda i: (i, 0)),
                  pl.BlockSpec((1, D_MODEL), lambda i: (0, 0)),
                  pl.BlockSpec((D_MODEL, D_MODEL), lambda i: (0, 0)),
                  pl.BlockSpec((PLE_DIM, D_MODEL), lambda i: (0, 0)),
                  pl.BlockSpec((1, D_MODEL), lambda i: (0, 0))],
        out_specs=pl.BlockSpec((tm, D_MODEL), lambda i: (i, 0)),
        out_shape=jax.ShapeDtypeStruct((n, D_MODEL), F32),
        compiler_params=_params("parallel"),
        name="ple",
    )(h, p, g, wg, wp, g_final)


def _q_perm():
    idx = []
    for j in range(Q_PER_GROUP):
        for g in range(KV_GROUPS):
            base = (g * Q_PER_GROUP + j) * HEAD_DIM
            idx.extend(range(base, base + HEAD_DIM))
    return jnp.asarray(idx, dtype=jnp.int32)


def _layout_w_in(w):
    a1 = 2 * A_WIDTH
    b1 = a1 + 4 * B_WIDTH
    q1 = b1 + C_WIDTH
    w = jnp.concatenate([w[:, a1:b1], w[:, :a1], w[:, b1:q1][:, _q_perm()], w[:, q1:]], axis=1)
    return jnp.pad(w, ((0, 0), (0, Z_W - w.shape[1]))).astype(BF16)


def _layout_w_out(w):
    c0 = A_WIDTH + B_WIDTH
    return jnp.concatenate([w[:c0], w[c0:][_q_perm()]], axis=0).astype(BF16)


def _mix_layer(h, bn, s, tabs, g_mix, w_in, w_out, gm_ln_g, gm_ln_b, gm_ws, gm_bs,
               rw_mu, rw_w0, rw_w_up, rw_a0, rw_a_up, rw_g_up, rw_k_k, rw_k_a, rw_r_k, rw_gn_g, rw_gn_b,
               nsa_cmp_pos, nsa_kc_w1, nsa_kc_w2, nsa_vc_w1, nsa_vc_w2):
    n = bn * s
    z = _proj_in(h, g_mix.reshape(1, -1), _layout_w_in(w_in))
    z3 = z.reshape(bn, s, Z_W)
    y_a = _gmlp(z3, gm_ln_g, gm_ln_b, gm_ws, gm_bs)
    y_b = _rwkv(z3, rw_mu, rw_w0, rw_w_up, rw_a0, rw_a_up, rw_g_up, rw_k_k, rw_k_a, rw_r_k, rw_gn_g, rw_gn_b)
    q_raw, q_rot, kc, vc, ks, vs, kw, vw, gates = _nsa_prep(z3, tabs)
    k_cmp, v_cmp = _nsa_compress(kc, vc, nsa_cmp_pos, nsa_kc_w1, nsa_kc_w2, nsa_vc_w1, nsa_vc_w2)
    y_c = _nsa_attn(q_raw, q_rot, k_cmp, v_cmp, ks, vs, kw, vw, gates)
    return _out_proj(h, y_a.reshape(n, -1), y_b.reshape(n, -1), y_c.reshape(n, -1), _layout_w_out(w_out))


def kernel(x, p, positions, g_mix, w_in, w_out, gm_ln_g, gm_ln_b, gm_ws, gm_bs, rw_mu, rw_w0, rw_w_up, rw_a0,
           rw_a_up, rw_g_up, rw_k_k, rw_k_a, rw_r_k, rw_gn_g, rw_gn_b, nsa_cmp_pos, nsa_kc_w1, nsa_kc_w2,
           nsa_vc_w1, nsa_vc_w2, g_ffn, ffn_w1, ffn_w3, ffn_w2, router_w, router_b, moe_w1, moe_w3, moe_w2,
           g_ple, ple_gate_w, ple_proj_w, g_final):
    bn, s, _ = x.shape
    n = bn * s
    depth = g_mix.shape[0]
    tabs = _rope_tables(positions)
    h = x.reshape(n, D_MODEL)
    for i in range(depth):
        h = _mix_layer(h, bn, s, tabs, g_mix[i], w_in[i], w_out[i], gm_ln_g[i], gm_ln_b[i], gm_ws[i], gm_bs[i],
                       rw_mu[i], rw_w0[i], rw_w_up[i], rw_a0[i], rw_a_up[i], rw_g_up[i], rw_k_k[i], rw_k_a[i],
                       rw_r_k[i], rw_gn_g[i], rw_gn_b[i], nsa_cmp_pos[i], nsa_kc_w1[i], nsa_kc_w2[i],
                       nsa_vc_w1[i], nsa_vc_w2[i])
        j = i // 2
        if i % 2 == 0:
            h = _ffn(h, g_ffn[i].reshape(1, -1), ffn_w1[j:j + 1].astype(BF16), ffn_w3[j:j + 1].astype(BF16),
                     ffn_w2[j:j + 1].astype(BF16))
        else:
            h = _moe(h, g_ffn[i].reshape(1, -1), router_w[j], router_b[j], moe_w1[j].astype(BF16),
                     moe_w3[j].astype(BF16), moe_w2[j].astype(BF16))
        h = _ple(h, p[i].reshape(n, PLE_DIM), g_ple[i].reshape(1, -1), ple_gate_w[i].astype(BF16),
                 ple_proj_w[i].astype(BF16), g_final.reshape(1, -1), final=(i == depth - 1))
    return h.reshape(bn, s, D_MODEL)
```

```python
import functools
import math

import jax
import jax.numpy as jnp
from jax import lax
from jax.experimental import pallas as pl
from jax.experimental.pallas import tpu as pltpu

F32 = jnp.float32
BF16 = jnp.bfloat16

D_MODEL = 1024
HEAD_DIM = 64
A_HEADS = 4
B_HEADS = 4
A_WIDTH = 256
B_WIDTH = 256
C_WIDTH = 512
CHUNK = 128
LN_EPS = 1e-5
GN_EPS = 64e-5
RMS_EPS = 1e-6
RW_LORA = 64
KV_GROUPS = 2
Q_PER_GROUP = 4
CMP_BLOCK = 32
CMP_STRIDE = 16
CMP_HIDDEN = 128
SLC_BLOCK = 64
SLC_TOPK = 16
WINDOW = 512
Q_BLOCK = 128
NEG_INF = -1e30
FORCE = 1e4
ROPE_THETA = 500000.0
ROPE_DIM = 16
D_FF = 2816
N_EXPERTS = 8
TOP_K = 2
PLE_DIM = 256

LANES = 128
Z_B = 0
Z_A = 1024
Z_Q = 1536
Z_KV = 2048
Z_G = 2816
Z_W = 2944
N_GATE = 24

VMEM_LIMIT = 56 * 1024 * 1024
TM_PROJ = 512
TM_FFN = 512
TF_FFN = 1408
TM_GROUP = 512
RW_T = 256
RW_C = 64
TK_SEL = 512

_NN = (((1,), (0,)), ((), ()))
_NT = (((1,), (1,)), ((), ()))


def _params(*sem):
    return pltpu.CompilerParams(dimension_semantics=sem, vmem_limit_bytes=VMEM_LIMIT)


def _dot(a, b, dn=_NN, passes=1):
    if passes == 6:
        return lax.dot_general(a.astype(F32), b.astype(F32), dn, precision=lax.Precision.HIGHEST,
                               preferred_element_type=F32)
    a_hi = a.astype(BF16)
    b_hi = b.astype(BF16)
    out = lax.dot_general(a_hi, b_hi, dn, preferred_element_type=F32)
    if passes == 1:
        return out
    a_lo = (a - a_hi.astype(F32)).astype(BF16)
    out = out + lax.dot_general(a_lo, b_hi, dn, preferred_element_type=F32)
    if passes == 2:
        return out
    b_lo = (b - b_hi.astype(F32)).astype(BF16)
    return out + lax.dot_general(a_hi, b_lo, dn, preferred_element_type=F32)


def _rms(x, g):
    return x * lax.rsqrt(jnp.mean(x * x, axis=-1, keepdims=True) + RMS_EPS) * g


def _sigmoid(x):
    return 1.0 / (1.0 + jnp.exp(-x))


def _div(x, d):
    return lax.shift_right_logical(x, jnp.int32(int(math.log2(d))))


def _head_block_ones(n):
    r = lax.broadcasted_iota(jnp.int32, (n, n), 0)
    c = lax.broadcasted_iota(jnp.int32, (n, n), 1)
    return jnp.where(_div(r, HEAD_DIM) == _div(c, HEAD_DIM), 1.0, 0.0).astype(F32)


def _rope_kernel(inv_ref, pos_ref, cos_ref, sin_ref):
    p = pos_ref[0].astype(F32)
    for j in range(ROPE_DIM // 2):
        ang = p * inv_ref[j]
        cos_ref[0, j] = jnp.cos(ang)
        sin_ref[0, j] = jnp.sin(ang)


def _rope_tables(positions):
    bn, s = positions.shape
    half = ROPE_DIM // 2
    inv = 1.0 / (ROPE_THETA ** (jnp.arange(0, ROPE_DIM, 2, dtype=F32) / ROPE_DIM))
    pos3 = positions.reshape(bn, s // LANES, LANES)
    cos, sin = pl.pallas_call(
        _rope_kernel,
        grid=(bn,),
        in_specs=[pl.BlockSpec(memory_space=pltpu.SMEM),
                  pl.BlockSpec((1, s // LANES, LANES), lambda b: (b, 0, 0))],
        out_specs=[pl.BlockSpec((1, half, s // LANES, LANES), lambda b: (b, 0, 0, 0))] * 2,
        out_shape=[jax.ShapeDtypeStruct((bn, half, s // LANES, LANES), F32)] * 2,
        compiler_params=_params("parallel"),
        name="rope_tables",
    )(inv, pos3)
    cos = cos.reshape(bn, half, s).transpose(0, 2, 1)
    sin = sin.reshape(bn, half, s).transpose(0, 2, 1)
    one = jnp.ones((bn, s, HEAD_DIM - ROPE_DIM), F32)
    zero = jnp.zeros((bn, s, HEAD_DIM - half), F32)
    tab_c = jnp.concatenate([cos, cos, one], axis=-1)
    tab_lo = jnp.concatenate([sin, zero], axis=-1)
    tab_hi = jnp.concatenate([zero[..., :half], sin, zero[..., :HEAD_DIM - ROPE_DIM]], axis=-1)
    rep = lambda t: jnp.tile(t, (1, 1, LANES // HEAD_DIM))
    return rep(tab_c), rep(tab_lo), rep(tab_hi)


def _proj_in_kernel(h_ref, g_ref, w_ref, z_ref):
    y = _rms(h_ref[...], g_ref[...])
    z_ref[...] = jnp.dot(y.astype(BF16), w_ref[...], preferred_element_type=F32)


def _proj_in(h, g, w):
    n = h.shape[0]
    tm = min(TM_PROJ, n)
    return pl.pallas_call(
        _proj_in_kernel,
        grid=(n // tm,),
        in_specs=[pl.BlockSpec((tm, D_MODEL), lambda i: (i, 0)),
                  pl.BlockSpec((1, D_MODEL), lambda i: (0, 0)),
                  pl.BlockSpec((D_MODEL, Z_W), lambda i: (0, 0))],
        out_specs=pl.BlockSpec((tm, Z_W), lambda i: (i, 0)),
        out_shape=jax.ShapeDtypeStruct((n, Z_W), F32),
        compiler_params=_params("parallel"),
        name="proj_in",
    )(h, g, w)


def _gmlp_kernel(z_ref, lng_ref, lnb_ref, ws_ref, bias_ref, o_ref):
    gz = jax.nn.gelu(z_ref[0])
    u = gz[:, :A_WIDTH]
    v = gz[:, A_WIDTH:]
    ones = _head_block_ones(A_WIDTH)
    mu = _dot(v, ones, passes=2) * (1.0 / HEAD_DIM)
    d = v - mu
    var = _dot(d * d, ones, passes=2) * (1.0 / HEAD_DIM)
    vn = d * lax.rsqrt(var + LN_EPS) * lng_ref[...] + lnb_ref[...]
    r = lax.broadcasted_iota(jnp.int32, (CHUNK, CHUNK), 0)
    c = lax.broadcasted_iota(jnp.int32, (CHUNK, CHUNK), 1)
    lane_head = _div(lax.broadcasted_iota(jnp.int32, (CHUNK, A_WIDTH), 1), HEAD_DIM)
    mixed = bias_ref[...]
    for hd in range(A_HEADS):
        w = jnp.where(c <= r, ws_ref[hd], 0.0)
        vh = jnp.where(lane_head == hd, vn, 0.0)
        mixed = mixed + _dot(w, vh)
    o_ref[0] = (u * mixed).astype(o_ref.dtype)


def _gmlp(z3, ln_g, ln_b, w_s, b_s):
    bn, s, _ = z3.shape
    bias = jnp.repeat(b_s.T, HEAD_DIM, axis=1)
    return pl.pallas_call(
        _gmlp_kernel,
        grid=(bn, s // CHUNK),
        in_specs=[pl.BlockSpec((1, CHUNK, 2 * A_WIDTH), lambda b, i: (b, i, Z_A // (2 * A_WIDTH))),
                  pl.BlockSpec((1, A_WIDTH), lambda b, i: (0, 0)),
                  pl.BlockSpec((1, A_WIDTH), lambda b, i: (0, 0)),
                  pl.BlockSpec((A_HEADS, CHUNK, CHUNK), lambda b, i: (0, 0, 0)),
                  pl.BlockSpec((CHUNK, A_WIDTH), lambda b, i: (0, 0))],
        out_specs=pl.BlockSpec((1, CHUNK, A_WIDTH), lambda b, i: (b, i, 0)),
        out_shape=jax.ShapeDtypeStruct((bn, s, A_WIDTH), BF16),
        compiler_params=_params("parallel", "parallel"),
        name="gmlp",
    )(z3, ln_g.reshape(1, A_WIDTH), ln_b.reshape(1, A_WIDTH), w_s, bias)


RW_PASSES = 1


def _rwkv_chunk(r_c, lw_c, k_c, v_c, kk_c, a_c, s0, c_len):
    nh = B_HEADS
    m = nh * c_len
    dot = functools.partial(_dot, passes=RW_PASSES)
    ti = lax.broadcasted_iota(jnp.int32, (c_len, c_len), 0)
    tj = lax.broadcasted_iota(jnp.int32, (c_len, c_len), 1)
    tril = jnp.where(tj <= ti, 1.0, 0.0).astype(F32)
    cum = _dot(tril, lw_c, passes=6)
    w_t = jnp.exp(cum)
    w_p = jnp.exp(cum - lw_c)
    w_i = jnp.exp(-cum)
    a_t = -kk_c * w_p
    b_t = kk_c * a_c * w_i
    k_t = k_c * w_i
    r_t = r_c * w_t
    w_c = w_t[c_len - 1:c_len, :]

    row_head = _div(lax.broadcasted_iota(jnp.int32, (m, B_WIDTH), 0), c_len)
    lane_head = _div(lax.broadcasted_iota(jnp.int32, (m, B_WIDTH), 1), HEAD_DIM)
    head_mask = row_head == lane_head
    rep = lambda x: jnp.concatenate([x] * nh, axis=0)
    stack = lambda x: jnp.where(head_mask, rep(x), 0.0)
    a_st, r_st, v_st = stack(a_t), stack(r_t), stack(v_c)

    x1 = jnp.concatenate([a_st, r_st], axis=0)
    x2 = jnp.concatenate([rep(b_t), rep(k_t)], axis=0)
    cross = dot(x1, x2, _NT)
    ri = lax.broadcasted_iota(jnp.int32, (m, m), 0)
    ci = lax.broadcasted_iota(jnp.int32, (m, m), 1)
    same = _div(ri, c_len) == _div(ci, c_len)
    strict = same & (ci < ri)
    incl = same & (ci <= ri)
    l_ab = jnp.where(strict, cross[:m, :m], 0.0)
    l_ak = jnp.where(strict, cross[:m, m:], 0.0)
    l_rb = jnp.where(incl, cross[m:, :m], 0.0)
    l_rk = jnp.where(incl, cross[m:, m:], 0.0)

    p_inv = jnp.where(ri == ci, 1.0, 0.0).astype(F32) + l_ab
    pw = l_ab
    for _ in range(int(math.log2(c_len)) - 1):
        pw = dot(pw, pw)
        p_inv = p_inv + dot(p_inv, pw)

    q_m = dot(p_inv, dot(l_ak, v_st))
    p_m = dot(p_inv, a_st)
    g_m = r_st + dot(l_rb, p_m)
    h_m = dot(l_rb, q_m) + dot(l_rk, v_st)

    y_st = dot(g_m, s0, _NT) + h_m
    u_st = dot(p_m, s0, _NT) + q_m
    s_new = s0 * w_c + dot(u_st.T, stack(b_t * w_c)) + dot(v_st.T, stack(k_t * w_c))
    y = y_st[0:c_len]
    for hd in range(1, nh):
        y = y + y_st[hd * c_len:(hd + 1) * c_len]
    return y, s_new


def _rwkv_kernel(z_ref, mu_ref, w0_ref, a0_ref, wwa_ref, gup_ref, kk_ref, ka_ref, rk_ref, gng_ref, gnb_ref,
                 o_ref, carry_ref, state_ref, *, c_len):
    @pl.when(pl.program_id(1) == 0)
    def _():
        carry_ref[...] = jnp.zeros_like(carry_ref)
        state_ref[...] = jnp.zeros_like(state_ref)

    zb = z_ref[0]
    t_len = zb.shape[0]
    row = lax.broadcasted_iota(jnp.int32, zb.shape, 0)
    z_prev = jnp.where(row == 0, carry_ref[...], pltpu.roll(zb, 1, axis=0))
    carry_ref[...] = zb[t_len - 1:t_len, :]
    zz = zb + (z_prev - zb) * mu_ref[...]
    r = zz[:, 0:B_WIDTH]
    k = zz[:, B_WIDTH:2 * B_WIDTH]
    v = zz[:, 2 * B_WIDTH:3 * B_WIDTH]
    wa = zz[:, 3 * B_WIDTH:3 * B_WIDTH + 2 * RW_LORA]
    gd = zz[:, 3 * B_WIDTH + 2 * RW_LORA:]
    lane = lax.broadcasted_iota(jnp.int32, wa.shape, 1)
    proj = _dot(jnp.where(lane < RW_LORA, jnp.tanh(wa), wa), wwa_ref[...])
    x = -(w0_ref[...] + proj[:, :B_WIDTH])
    softplus = jnp.maximum(x, 0.0) + jnp.log(1.0 + jnp.exp(-jnp.abs(x)))
    lw = -jnp.exp(-softplus - 0.5)
    a = _sigmoid(a0_ref[...] + proj[:, B_WIDTH:])
    g = _dot(_sigmoid(gd), gup_ref[...])
    ones = _head_block_ones(B_WIDTH)
    kk = k * kk_ref[...]
    kk = kk * lax.rsqrt(jnp.maximum(_dot(kk * kk, ones, passes=2), 1e-24))
    k2 = k * (1.0 + (a - 1.0) * ka_ref[...])

    s_cur = state_ref[...]
    ys = []
    for c in range(t_len // c_len):
        sl = slice(c * c_len, (c + 1) * c_len)
        y_c, s_cur = _rwkv_chunk(r[sl], lw[sl], k2[sl], v[sl], kk[sl], a[sl], s_cur, c_len)
        ys.append(y_c)
    state_ref[...] = s_cur
    y = jnp.concatenate(ys, axis=0)

    mu_y = _dot(y, ones, passes=2) * (1.0 / HEAD_DIM)
    d = y - mu_y
    var = _dot(d * d, ones, passes=2) * (1.0 / HEAD_DIM)
    yn = d * lax.rsqrt(var + GN_EPS) * gng_ref[...] + gnb_ref[...]
    bonus = _dot(r * k2 * rk_ref[...], ones, passes=2) * v
    o_ref[0] = ((yn + bonus) * g).astype(o_ref.dtype)


def _rwkv(z3, mu, w0, w_up, a0, a_up, g_up, k_k, k_a, r_k, gn_g, gn_b):
    bn, s, _ = z3.shape
    t_len = min(RW_T, s)
    zero = jnp.zeros((RW_LORA, B_WIDTH), F32)
    wwa = jnp.concatenate([jnp.concatenate([w_up, zero], axis=1),
                           jnp.concatenate([zero, a_up], axis=1)], axis=0).astype(BF16)
    row = lambda t: t.reshape(1, -1)
    vec = pl.BlockSpec((1, B_WIDTH), lambda b, i: (0, 0))
    return pl.pallas_call(
        functools.partial(_rwkv_kernel, c_len=RW_C),
        grid=(bn, s // t_len),
        in_specs=[pl.BlockSpec((1, t_len, 4 * B_WIDTH), lambda b, i: (b, i, Z_B // (4 * B_WIDTH))),
                  pl.BlockSpec((1, 4 * B_WIDTH), lambda b, i: (0, 0)),
                  vec, vec,
                  pl.BlockSpec((2 * RW_LORA, 2 * B_WIDTH), lambda b, i: (0, 0)),
                  pl.BlockSpec((2 * RW_LORA, B_WIDTH), lambda b, i: (0, 0)),
                  vec, vec, vec, vec, vec],
        out_specs=pl.BlockSpec((1, t_len, B_WIDTH), lambda b, i: (b, i, 0)),
        out_shape=jax.ShapeDtypeStruct((bn, s, B_WIDTH), BF16),
        scratch_shapes=[pltpu.VMEM((1, 4 * B_WIDTH), F32), pltpu.VMEM((B_WIDTH, B_WIDTH), F32)],
        compiler_params=_params("parallel", "arbitrary"),
        name="rwkv7",
    )(z3, row(mu), row(w0), row(a0), wwa, g_up.astype(BF16), row(k_k), row(k_a), row(r_k), row(gn_g), row(gn_b))


def _rope(x, tab_c, tab_lo, tab_hi):
    n = x.shape[-1]
    half = ROPE_DIM // 2
    return x * tab_c - pltpu.roll(x, n - half, axis=1) * tab_lo + pltpu.roll(x, half, axis=1) * tab_hi


N_HEADS_C = KV_GROUPS * Q_PER_GROUP
QT_COLS = N_HEADS_C * Q_BLOCK
GATE_ROWS = 32
LOG2E = 1.4426950408889634


def _nsa_prep_kernel(zq_ref, zkc_ref, zks_ref, zkw_ref, zg_ref, tc_ref, tl_ref, th_ref,
                     qraw_ref, qrot_ref, kc_ref, vc_ref, ks_ref, vst_ref, kw_ref, vwt_ref, gate_ref):
    tab_c, tab_lo, tab_hi = tc_ref[0], tl_ref[0], th_ref[0]
    nrep = C_WIDTH // LANES
    wide = lambda t: jnp.concatenate([t] * nrep, axis=1)
    q = zq_ref[0] * (HEAD_DIM ** -0.5 * LOG2E)
    q_rot = _rope(q, wide(tab_c), wide(tab_lo), wide(tab_hi))
    ts = q.shape[0]
    row_group0 = lax.broadcasted_iota(jnp.int32, (LANES, ts), 0) < HEAD_DIM

    def put_queries(ref, x):
        for j in range(Q_PER_GROUP):
            xt = x[:, j * LANES:(j + 1) * LANES].T
            for g in range(KV_GROUPS):
                keep = row_group0 if g == 0 else jnp.logical_not(row_group0)
                xm = jnp.where(keep, xt, 0.0).astype(BF16)
                for qq in range(ts // Q_BLOCK):
                    col = qq * QT_COLS + (g * Q_PER_GROUP + j) * Q_BLOCK
                    ref[0, :, col:col + Q_BLOCK] = xm[:, qq * Q_BLOCK:(qq + 1) * Q_BLOCK]

    put_queries(qraw_ref, q)
    put_queries(qrot_ref, q_rot)
    kc_ref[0] = zkc_ref[0][:, :LANES].astype(BF16)
    vc_ref[0] = zkc_ref[0][:, LANES:].astype(BF16)
    ks_ref[0] = _rope(zks_ref[0][:, :LANES], tab_c, tab_lo, tab_hi).astype(BF16)
    vst_ref[0, 0] = zks_ref[0][:, LANES:].T.astype(BF16)
    kw_ref[0] = _rope(zkw_ref[0][:, :LANES], tab_c, tab_lo, tab_hi).astype(BF16)
    vwt = zkw_ref[0][:, LANES:].T.astype(BF16)
    for qq in range(ts // Q_BLOCK):
        vwt_ref[0, qq] = vwt[:, qq * Q_BLOCK:(qq + 1) * Q_BLOCK]
    gate_ref[0] = _sigmoid(zg_ref[0]).T[:GATE_ROWS, :]


def _nsa_prep(z3, tabs):
    bn, s, _ = z3.shape
    ts = TK_SEL
    zspec = lambda width, off: pl.BlockSpec((1, ts, width), lambda b, i: (b, i, off // width))
    tspec = pl.BlockSpec((1, ts, LANES), lambda b, i: (b, i, 0))
    rowmajor = pl.BlockSpec((1, ts, LANES), lambda b, i: (b, i, 0))
    qspec = pl.BlockSpec((1, LANES, N_HEADS_C * ts), lambda b, i: (b, 0, i))
    rm_sds = jax.ShapeDtypeStruct((bn, s, LANES), BF16)
    q_sds = jax.ShapeDtypeStruct((bn, LANES, N_HEADS_C * s), BF16)
    return pl.pallas_call(
        _nsa_prep_kernel,
        grid=(bn, s // ts),
        in_specs=[zspec(C_WIDTH, Z_Q), zspec(2 * LANES, Z_KV), zspec(2 * LANES, Z_KV + 2 * LANES),
                  zspec(2 * LANES, Z_KV + 4 * LANES), zspec(LANES, Z_G), tspec, tspec, tspec],
        out_specs=[qspec, qspec, rowmajor, rowmajor, rowmajor,
                   pl.BlockSpec((1, 1, LANES, ts), lambda b, i: (b, i, 0, 0)), rowmajor,
                   pl.BlockSpec((1, ts // Q_BLOCK, LANES, Q_BLOCK), lambda b, i: (b, i, 0, 0)),
                   pl.BlockSpec((1, GATE_ROWS, ts), lambda b, i: (b, 0, i))],
        out_shape=[q_sds, q_sds, rm_sds, rm_sds, rm_sds,
                   jax.ShapeDtypeStruct((bn, s // ts, LANES, ts), BF16), rm_sds,
                   jax.ShapeDtypeStruct((bn, s // Q_BLOCK, LANES, Q_BLOCK), BF16),
                   jax.ShapeDtypeStruct((bn, GATE_ROWS, s), F32)],
        compiler_params=_params("parallel", "parallel"),
        name="nsa_prep",
    )(z3, z3, z3, z3, z3, *tabs)


def _nsa_compress_kernel(xk_ref, xv_ref, pos_ref, kw1a_ref, kw1b_ref, kw2_ref, vw1a_ref, vw1b_ref, vw2_ref,
                         ko_ref, vo_ref):
    n = xk_ref.shape[1]
    pos_a = pos_ref[0]
    pos_b = pos_ref[1]
    for x_ref, w1a, w1b, w2, o_ref, transposed in ((xk_ref, kw1a_ref, kw1b_ref, kw2_ref, ko_ref, False),
                                                    (xv_ref, vw1a_ref, vw1b_ref, vw2_ref, vo_ref, True)):
        x = x_ref[0]
        first = _dot(x, w1a[...])
        second = _dot(x, w1b[...])
        pc = _dot(pos_a, w1a[...]) + _dot(pos_b, w1b[...])
        hid = jax.nn.gelu(first + pltpu.roll(second, n - 1, axis=0) + pc[0:1, :])
        out = _dot(hid, w2[...])
        o_ref[0] = (out.T if transposed else out).astype(o_ref.dtype)


def _expand_cmp_weights(w1, w2):
    half = CMP_BLOCK // 2
    w1r = w1.reshape(CMP_BLOCK, HEAD_DIM, CMP_HIDDEN)
    eye = jnp.eye(KV_GROUPS, dtype=F32)
    w1e = jnp.einsum('ldh,gk->lgdkh', w1r, eye).reshape(CMP_BLOCK * LANES, KV_GROUPS * CMP_HIDDEN)
    w2e = jnp.einsum('hd,gk->ghkd', w2, eye).reshape(KV_GROUPS * CMP_HIDDEN, LANES)
    return (w1e[:half * LANES].astype(BF16), w1e[half * LANES:].astype(BF16), w2e.astype(BF16))


def _nsa_compress(kc, vc, cmp_pos, kc_w1, kc_w2, vc_w1, vc_w2):
    bn, s, _ = kc.shape
    n = s // CMP_STRIDE
    xw = CMP_STRIDE * LANES
    xk = kc.reshape(bn, n, xw)
    xv = vc.reshape(bn, n, xw)
    pos = jnp.tile(cmp_pos[:, None, :], (1, KV_GROUPS, 1)).reshape(2, 1, xw)
    pos = jnp.broadcast_to(pos, (2, 8, xw)).astype(BF16)
    kw = _expand_cmp_weights(kc_w1, kc_w2)
    vw = _expand_cmp_weights(vc_w1, vc_w2)
    full = lambda a: pl.BlockSpec(a.shape, lambda b: (0,) * a.ndim)
    xspec = pl.BlockSpec((1, n, xw), lambda b: (b, 0, 0))
    return pl.pallas_call(
        _nsa_compress_kernel,
        grid=(bn,),
        in_specs=[xspec, xspec, full(pos)] + [full(a) for a in kw + vw],
        out_specs=[pl.BlockSpec((1, n, LANES), lambda b: (b, 0, 0)), pl.BlockSpec((1, LANES, n), lambda b: (b, 0, 0))],
        out_shape=[jax.ShapeDtypeStruct((bn, n, LANES), BF16), jax.ShapeDtypeStruct((bn, LANES, n), BF16)],
        compiler_params=_params("parallel"),
        name="nsa_compress",
    )(xk, xv, pos, *kw, *vw)


def _nsa_attn_kernel(qraw_ref, qrot_ref, kcmp_ref, vcmpt_ref, ks_ref, vst_ref, kw_ref, vwt_ref, gate_ref, o_ref,
                     m_ref, l_ref, acc_ref):
    tq = Q_BLOCK
    gw = Q_PER_GROUP * tq
    seq = ks_ref.shape[1]
    n_cmp = kcmp_ref.shape[1]
    n_slc = seq // SLC_BLOCK
    n_sel = min(SLC_TOPK, n_slc)
    qb = pl.program_id(1)
    t0 = qb * tq
    gcols = lambda g: slice(g * gw, (g + 1) * gw)
    grows = lambda g: slice(g * HEAD_DIM, (g + 1) * HEAD_DIM)
    jcols = lambda j: slice(j * tq, (j + 1) * tq)

    t_c = t0 + lax.broadcasted_iota(jnp.int32, (n_cmp, tq), 1)
    n_c = lax.broadcasted_iota(jnp.int32, (n_cmp, tq), 0)
    cmask = (n_c * CMP_STRIDE + (CMP_BLOCK - 1)) <= t_c
    o_c, p_sum = [], []
    for g in range(KV_GROUPS):
        s = _dot(kcmp_ref[0], qraw_ref[0, :, gcols(g)])
        parts, total = [], None
        for j in range(Q_PER_GROUP):
            sj = jnp.where(cmask, s[:, jcols(j)], NEG_INF)
            e = jnp.exp2(sj - jnp.max(sj, axis=0, keepdims=True))
            p = jnp.where(cmask, e * (1.0 / jnp.sum(e, axis=0, keepdims=True)), 0.0)
            total = p if total is None else total + p
            parts.append(p.astype(BF16))
        p_sum.append(total)
        o_c.append(_dot(vcmpt_ref[0, grows(g), :], jnp.concatenate(parts, axis=1)))

    m_o = lax.broadcasted_iota(jnp.int32, (n_slc, n_cmp), 0) * SLC_BLOCK
    n_o = lax.broadcasted_iota(jnp.int32, (n_slc, n_cmp), 1) * CMP_STRIDE
    overlap_t = jnp.where((n_o < m_o + SLC_BLOCK) & (n_o + (CMP_BLOCK - 1) >= m_o), 1.0, 0.0).astype(F32)
    m_i = lax.broadcasted_iota(jnp.int32, (n_slc, tq), 0)
    blk_t = _div(t0 + lax.broadcasted_iota(jnp.int32, (n_slc, tq), 1), SLC_BLOCK)
    valid = m_i <= blk_t
    forced = (m_i == 0) | (m_i == blk_t) | (m_i == blk_t - 1)
    sel = []
    for g in range(KV_GROUPS):
        imp = _dot(overlap_t, p_sum[g], passes=3)
        imp = jnp.where(valid, imp + jnp.where(forced, FORCE, 0.0), -FORCE)
        rank = jnp.zeros((n_slc, tq), F32)
        for mp in range(n_slc):
            other = imp[mp:mp + 1, :]
            rank = rank + jnp.where(m_i > mp, jnp.where(other >= imp, 1.0, 0.0), jnp.where(other > imp, 1.0, 0.0))
        sel.append(jnp.where((rank < n_sel) & valid, 1.0, 0.0).astype(BF16))

    m_ref[...] = jnp.full(m_ref.shape, NEG_INF, F32)
    l_ref[...] = jnp.zeros(l_ref.shape, F32)
    acc_ref[...] = jnp.zeros(acc_ref.shape, F32)
    t_s = t0 + lax.broadcasted_iota(jnp.int32, (TK_SEL, tq), 1)
    k_s = lax.broadcasted_iota(jnp.int32, (TK_SEL, tq), 0)
    k_e = lax.broadcasted_iota(jnp.int32, (TK_SEL, n_slc), 0)
    m_e = lax.broadcasted_iota(jnp.int32, (TK_SEL, n_slc), 1)

    def sel_tile(kt, carry):
        k0 = pl.multiple_of(kt * TK_SEL, TK_SEL)
        keys = ks_ref[0, pl.ds(k0, TK_SEL), :]
        causal = (k_s + k0) <= t_s
        expand = jnp.where(_div(k_e + k0, SLC_BLOCK) == m_e, 1.0, 0.0).astype(BF16)
        for g in range(KV_GROUPS):
            mask = (_dot(expand, sel[g]) > 0.5) & causal
            s = _dot(keys, qrot_ref[0, :, gcols(g)])
            parts, alphas = [], []
            for j in range(Q_PER_GROUP):
                cs = slice(g * gw + j * tq, g * gw + (j + 1) * tq)
                sj = jnp.where(mask, s[:, jcols(j)], NEG_INF)
                m_old = m_ref[:, cs]
                m_new = jnp.maximum(m_old, jnp.max(sj, axis=0, keepdims=True))
                alpha = jnp.exp2(m_old - m_new)
                p = jnp.exp2(sj - m_new)
                l_ref[:, cs] = alpha * l_ref[:, cs] + jnp.sum(p, axis=0, keepdims=True)
                m_ref[:, cs] = m_new
                alphas.append(alpha)
                parts.append(p.astype(BF16))
            pv = _dot(vst_ref[0, kt, grows(g), :], jnp.concatenate(parts, axis=1))
            acc_ref[g] = acc_ref[g] * jnp.concatenate(alphas, axis=1) + pv
        return carry

    lax.fori_loop(0, (t0 + tq + TK_SEL - 1) // TK_SEL, sel_tile, 0)

    n_band = WINDOW // tq + 1
    band = n_band * tq
    wb = jnp.maximum(qb - WINDOW // tq, 0)
    w0 = pl.multiple_of(wb * tq, tq)
    t_w = t0 + lax.broadcasted_iota(jnp.int32, (band, tq), 1)
    k_w = w0 + lax.broadcasted_iota(jnp.int32, (band, tq), 0)
    wmask = (k_w <= t_w) & (k_w > t_w - WINDOW)
    kband = kw_ref[0, pl.ds(w0, band), :]
    o_w = []
    for g in range(KV_GROUPS):
        s = _dot(kband, qrot_ref[0, :, gcols(g)])
        parts, inv = [], []
        for j in range(Q_PER_GROUP):
            sj = jnp.where(wmask, s[:, jcols(j)], NEG_INF)
            e = jnp.exp2(sj - jnp.max(sj, axis=0, keepdims=True))
            inv.append(1.0 / jnp.sum(e, axis=0, keepdims=True))
            parts.append(e.astype(BF16))
        vband = jnp.concatenate([vwt_ref[0, wb + i, grows(g), :] for i in range(n_band)], axis=1)
        o_w.append(_dot(vband, jnp.concatenate(parts, axis=1)) * jnp.concatenate(inv, axis=1))

    gates = gate_ref[0]
    outs = []
    for j in range(Q_PER_GROUP):
        for g in range(KV_GROUPS):
            row = (g * Q_PER_GROUP + j) * 3
            cs = slice(g * gw + j * tq, g * gw + (j + 1) * tq)
            o_s = acc_ref[g][:, jcols(j)] * (1.0 / l_ref[:, cs])
            outs.append(gates[row:row + 1, :] * o_c[g][:, jcols(j)] + gates[row + 1:row + 2, :] * o_s
                        + gates[row + 2:row + 3, :] * o_w[g][:, jcols(j)])
    o_ref[0] = jnp.concatenate(outs, axis=0).T.astype(o_ref.dtype)


def _nsa_attn(q_raw, q_rot, k_cmp, v_cmp_t, ks, vs_t, kw, vw_t, gates_t):
    bn, s, _ = ks.shape
    n_cmp = k_cmp.shape[1]
    qspec = pl.BlockSpec((1, LANES, QT_COLS), lambda b, i: (b, 0, i))
    kspec = pl.BlockSpec((1, s, LANES), lambda b, i: (b, 0, 0))
    whole = lambda a: pl.BlockSpec((1,) + a.shape[1:], lambda b, i: (b,) + (0,) * (a.ndim - 1))
    return pl.pallas_call(
        _nsa_attn_kernel,
        grid=(bn, s // Q_BLOCK),
        in_specs=[qspec, qspec, whole(k_cmp), whole(v_cmp_t), kspec, whole(vs_t), kspec, whole(vw_t),
                  pl.BlockSpec((1, GATE_ROWS, Q_BLOCK), lambda b, i: (b, 0, i))],
        out_specs=pl.BlockSpec((1, Q_BLOCK, C_WIDTH), lambda b, i: (b, i, 0)),
        out_shape=jax.ShapeDtypeStruct((bn, s, C_WIDTH), BF16),
        scratch_shapes=[pltpu.VMEM((1, QT_COLS), F32), pltpu.VMEM((1, QT_COLS), F32),
                        pltpu.VMEM((KV_GROUPS, HEAD_DIM, Q_PER_GROUP * Q_BLOCK), F32)],
        compiler_params=_params("parallel", "arbitrary"),
        name="nsa_attn",
    )(q_raw, q_rot, k_cmp, v_cmp_t, ks, vs_t, kw, vw_t, gates_t)


def _out_proj_kernel(h_ref, ya_ref, yb_ref, yc_ref, w_ref, o_ref):
    acc = jnp.dot(ya_ref[...], w_ref[0:A_WIDTH, :], preferred_element_type=F32)
    acc = acc + jnp.dot(yb_ref[...], w_ref[A_WIDTH:A_WIDTH + B_WIDTH, :], preferred_element_type=F32)
    acc = acc + jnp.dot(yc_ref[...], w_ref[A_WIDTH + B_WIDTH:, :], preferred_element_type=F32)
    o_ref[...] = h_ref[...] + acc


def _out_proj(h, ya, yb, yc, w):
    n = h.shape[0]
    tm = min(TM_PROJ, n)
    tok = lambda width: pl.BlockSpec((tm, width), lambda i: (i, 0))
    return pl.pallas_call(
        _out_proj_kernel,
        grid=(n // tm,),
        in_specs=[tok(D_MODEL), tok(A_WIDTH), tok(B_WIDTH), tok(C_WIDTH),
                  pl.BlockSpec((D_MODEL, D_MODEL), lambda i: (0, 0))],
        out_specs=tok(D_MODEL),
        out_shape=jax.ShapeDtypeStruct((n, D_MODEL), F32),
        compiler_params=_params("parallel"),
        name="out_proj",
    )(h, ya, yb, yc, w)


def _swiglu_step(x, w1_ref, w3_ref, w2_ref):
    h1 = jnp.dot(x, w1_ref[0], preferred_element_type=F32)
    h3 = jnp.dot(x, w3_ref[0], preferred_element_type=F32)
    hid = h1 * _sigmoid(h1) * h3
    return jnp.dot(hid.astype(BF16), w2_ref[0], preferred_element_type=F32)


def _ffn_kernel(h_ref, g_ref, w1_ref, w3_ref, w2_ref, o_ref, hn_ref, acc_ref):
    f = pl.program_id(1)

    @pl.when(f == 0)
    def _():
        hn_ref[...] = _rms(h_ref[...], g_ref[...]).astype(BF16)
        acc_ref[...] = jnp.zeros_like(acc_ref)

    acc_ref[...] += _swiglu_step(hn_ref[...], w1_ref, w3_ref, w2_ref)

    @pl.when(f == pl.num_programs(1) - 1)
    def _():
        o_ref[...] = h_ref[...] + acc_ref[...]


def _ffn(h, g, w1, w3, w2):
    n = h.shape[0]
    tm = min(TM_FFN, n)
    return pl.pallas_call(
        _ffn_kernel,
        grid=(n // tm, D_FF // TF_FFN),
        in_specs=[pl.BlockSpec((tm, D_MODEL), lambda i, f: (i, 0)),
                  pl.BlockSpec((1, D_MODEL), lambda i, f: (0, 0)),
                  pl.BlockSpec((1, D_MODEL, TF_FFN), lambda i, f: (0, 0, f)),
                  pl.BlockSpec((1, D_MODEL, TF_FFN), lambda i, f: (0, 0, f)),
                  pl.BlockSpec((1, TF_FFN, D_MODEL), lambda i, f: (0, f, 0))],
        out_specs=pl.BlockSpec((tm, D_MODEL), lambda i, f: (i, 0)),
        out_shape=jax.ShapeDtypeStruct((n, D_MODEL), F32),
        scratch_shapes=[pltpu.VMEM((tm, D_MODEL), BF16), pltpu.VMEM((tm, D_MODEL), F32)],
        compiler_params=_params("parallel", "arbitrary"),
        name="ffn_swiglu",
    )(h, g, w1, w3, w2)


META_E1, META_E2, META_R1, META_R2, META_P1, META_P2 = range(6)


def _router_kernel(h_ref, g_ref, rw_ref, rb_ref, hn_ref, meta_ref, cnt_ref, carry_ref):
    @pl.when(pl.program_id(0) == 0)
    def _():
        carry_ref[...] = jnp.zeros_like(carry_ref)

    hn = _rms(h_ref[...], g_ref[...])
    hn_ref[...] = hn
    tm = hn.shape[0]
    lane = lax.broadcasted_iota(jnp.int32, (tm, LANES), 1).astype(F32)
    logits = jnp.where(lane < N_EXPERTS, _dot(hn, rw_ref[...], passes=3) + rb_ref[...], NEG_INF)
    top1 = jnp.max(logits, axis=-1, keepdims=True)
    idx1 = jnp.min(jnp.where(logits == top1, lane, float(LANES)), axis=-1, keepdims=True)
    rest = jnp.where(lane == idx1, NEG_INF, logits)
    top2 = jnp.max(rest, axis=-1, keepdims=True)
    idx2 = jnp.min(jnp.where(rest == top2, lane, float(LANES)), axis=-1, keepdims=True)
    ex = jnp.exp(top2 - top1)
    picked = jnp.where((lane == idx1) | (lane == idx2), 1.0, 0.0)
    r = lax.broadcasted_iota(jnp.int32, (tm, tm), 0)
    c = lax.broadcasted_iota(jnp.int32, (tm, tm), 1)
    before = _dot(jnp.where(c < r, 1.0, 0.0), picked) + carry_ref[...]
    rank1 = jnp.sum(jnp.where(lane == idx1, before, 0.0), axis=-1, keepdims=True)
    rank2 = jnp.sum(jnp.where(lane == idx2, before, 0.0), axis=-1, keepdims=True)
    carry_ref[...] += jnp.sum(picked, axis=0, keepdims=True)
    cnt_ref[...] = carry_ref[...]
    meta = jnp.zeros((tm, LANES), F32)
    for pos, val in ((META_E1, idx1), (META_E2, idx2), (META_R1, rank1), (META_R2, rank2),
                     (META_P1, 1.0 / (1.0 + ex)), (META_P2, ex / (1.0 + ex))):
        meta = jnp.where(lane == pos, val, meta)
    meta_ref[...] = meta


def _dispatch_kernel(d1_ref, d2_ref, hn_ref, xs_in_ref, xs_ref, sem):
    del xs_in_ref
    tm = hn_ref.shape[0]

    def row_copy(r, dest):
        return pltpu.make_async_copy(hn_ref.at[pl.ds(r, 1)], xs_ref.at[pl.ds(dest, 1)], sem)

    def issue(r, carry):
        row_copy(r, d1_ref[0, 0, r]).start()
        row_copy(r, d2_ref[0, 0, r]).start()
        return carry

    lax.fori_loop(0, tm, issue, 0, unroll=8)
    for _ in range(2):
        pltpu.make_async_copy(hn_ref, xs_ref.at[pl.ds(0, tm)], sem).wait()


def _grouped_kernel(te_ref, nu_ref, xs_ref, w1_ref, w3_ref, w2_ref, ys_ref, acc_ref):
    i = pl.program_id(0)
    f = pl.program_id(1)
    last = f == pl.num_programs(1) - 1
    used = i < nu_ref[0]

    @pl.when(used & (f == 0))
    def _():
        acc_ref[...] = jnp.zeros_like(acc_ref)

    @pl.when(used)
    def _():
        acc_ref[...] += _swiglu_step(xs_ref[...].astype(BF16), w1_ref, w3_ref, w2_ref)

    @pl.when(used & last)
    def _():
        ys_ref[...] = acc_ref[...]

    @pl.when(jnp.logical_not(used) & last)
    def _():
        ys_ref[...] = jnp.zeros_like(ys_ref)


def _combine_kernel(d1_ref, d2_ref, h_ref, meta_ref, ys_ref, o_ref, buf1, buf2, sem):
    tm = h_ref.shape[0]

    def row_copy(dest, buf, r):
        return pltpu.make_async_copy(ys_ref.at[pl.ds(dest, 1)], buf.at[pl.ds(r, 1)], sem)

    def issue(r, carry):
        row_copy(d1_ref[0, 0, r], buf1, r).start()
        row_copy(d2_ref[0, 0, r], buf2, r).start()
        return carry

    lax.fori_loop(0, tm, issue, 0, unroll=8)
    for buf in (buf1, buf2):
        pltpu.make_async_copy(ys_ref.at[pl.ds(0, tm)], buf, sem).wait()
    meta = meta_ref[...]
    o_ref[...] = (h_ref[...] + meta[:, META_P1:META_P1 + 1] * buf1[...]
                  + meta[:, META_P2:META_P2 + 1] * buf2[...])


def _moe(h, g, router_w, router_b, w1, w3, w2):
    n = h.shape[0]
    tm = min(TM_FFN, n)
    tg = min(TM_GROUP, n)
    nt = n // tm
    rw = jnp.pad(router_w, ((0, 0), (0, LANES - N_EXPERTS)))
    rb = jnp.pad(router_b.reshape(1, -1), ((0, 0), (0, LANES - N_EXPERTS)))
    tok = lambda width: pl.BlockSpec((tm, width), lambda i: (i, 0))
    hn, meta, cnt = pl.pallas_call(
        _router_kernel,
        grid=(nt,),
        in_specs=[tok(D_MODEL), pl.BlockSpec((1, D_MODEL), lambda i: (0, 0)),
                  pl.BlockSpec((D_MODEL, LANES), lambda i: (0, 0)), pl.BlockSpec((1, LANES), lambda i: (0, 0))],
        out_specs=[tok(D_MODEL), tok(LANES), pl.BlockSpec((1, LANES), lambda i: (0, 0))],
        out_shape=[jax.ShapeDtypeStruct((n, D_MODEL), F32), jax.ShapeDtypeStruct((n, LANES), F32),
                   jax.ShapeDtypeStruct((1, LANES), F32)],
        scratch_shapes=[pltpu.VMEM((1, LANES), F32)],
        compiler_params=_params("arbitrary"),
        name="moe_router",
    )(h, g, rw, rb)

    counts = cnt[0, :N_EXPERTS].astype(jnp.int32)
    padded = ((counts + tg - 1) // tg) * tg
    ends = jnp.cumsum(padded)
    offs = ends - padded
    rows = TOP_K * n + N_EXPERTS * tg
    n_tiles = rows // tg
    tile_start = jnp.arange(n_tiles, dtype=jnp.int32) * tg
    n_used = (ends[-1:] // tg).astype(jnp.int32)
    tile_expert = jnp.sum(tile_start[:, None] >= ends[None, :], axis=1).astype(jnp.int32)
    tile_expert = jnp.minimum(tile_expert, tile_expert[n_used[0] - 1])
    onehot = lambda e: (e[:, None] == jnp.arange(N_EXPERTS, dtype=jnp.int32)[None, :]).astype(jnp.int32)
    dest = lambda e, r: (jnp.sum(onehot(e.astype(jnp.int32)) * offs[None, :], axis=1)
                         + r.astype(jnp.int32)).reshape(nt, 1, tm)
    d1 = dest(meta[:, META_E1], meta[:, META_R1])
    d2 = dest(meta[:, META_E2], meta[:, META_R2])
    dspec = pl.BlockSpec((1, 1, tm), lambda i: (i, 0, 0), memory_space=pltpu.SMEM)

    xs = pl.pallas_call(
        _dispatch_kernel,
        grid=(nt,),
        in_specs=[dspec, dspec, tok(D_MODEL), pl.BlockSpec(memory_space=pl.ANY)],
        out_specs=pl.BlockSpec(memory_space=pl.ANY),
        out_shape=jax.ShapeDtypeStruct((rows, D_MODEL), F32),
        scratch_shapes=[pltpu.SemaphoreType.DMA(())],
        input_output_aliases={3: 0},
        compiler_params=_params("arbitrary"),
        name="moe_dispatch",
    )(d1, d2, hn, jnp.zeros((rows, D_MODEL), F32))

    nf = D_FF // TF_FFN
    fidx = lambda i, f, nu: jnp.where(i < nu[0], f, nf - 1)
    ys = pl.pallas_call(
        _grouped_kernel,
        grid_spec=pltpu.PrefetchScalarGridSpec(
            num_scalar_prefetch=2,
            grid=(n_tiles, D_FF // TF_FFN),
            in_specs=[pl.BlockSpec((tg, D_MODEL), lambda i, f, te, nu: (jnp.minimum(i, nu[0] - 1), 0)),
                      pl.BlockSpec((1, D_MODEL, TF_FFN), lambda i, f, te, nu: (te[i], 0, fidx(i, f, nu))),
                      pl.BlockSpec((1, D_MODEL, TF_FFN), lambda i, f, te, nu: (te[i], 0, fidx(i, f, nu))),
                      pl.BlockSpec((1, TF_FFN, D_MODEL), lambda i, f, te, nu: (te[i], fidx(i, f, nu), 0))],
            out_specs=pl.BlockSpec((tg, D_MODEL), lambda i, f, te, nu: (i, 0)),
            scratch_shapes=[pltpu.VMEM((tg, D_MODEL), F32)]),
        out_shape=jax.ShapeDtypeStruct((rows, D_MODEL), F32),
        compiler_params=_params("arbitrary", "arbitrary"),
        name="moe_grouped",
    )(tile_expert, n_used, xs, w1, w3, w2)

    return pl.pallas_call(
        _combine_kernel,
        grid=(nt,),
        in_specs=[dspec, dspec, tok(D_MODEL), tok(LANES), pl.BlockSpec(memory_space=pl.ANY)],
        out_specs=tok(D_MODEL),
        out_shape=jax.ShapeDtypeStruct((n, D_MODEL), F32),
        scratch_shapes=[pltpu.VMEM((tm, D_MODEL), F32), pltpu.VMEM((tm, D_MODEL), F32),
                        pltpu.SemaphoreType.DMA(())],
        compiler_params=_params("arbitrary"),
        name="moe_combine",
    )(d1, d2, h, meta, ys)


def _ple_kernel(h_ref, p_ref, g_ref, wg_ref, wp_ref, gf_ref, o_ref, *, final):
    h = h_ref[...]
    gate = _sigmoid(jnp.dot(_rms(h, g_ref[...]).astype(BF16), wg_ref[...], preferred_element_type=F32))
    out = h + jnp.dot(p_ref[...].astype(BF16), wp_ref[...], preferred_element_type=F32) * gate
    if final:
        out = _rms(out, gf_ref[...])
    o_ref[...] = out


def _ple(h, p, g, wg, wp, g_final, final):
    n = h.shape[0]
    tm = min(TM_PROJ, n)
    return pl.pallas_call(
        functools.partial(_ple_kernel, final=final),
        grid=(n // tm,),
        in_specs=[pl.BlockSpec((tm, D_MODEL), lambda i: (i, 0)),
                  pl.BlockSpec((tm, PLE_DIM), lambda i: (i, 0)),
                  pl.BlockSpec((1, D_MODEL), lambda i: (0, 0)),
                  pl.BlockSpec((D_MODEL, D_MODEL), lambda i: (0, 0)),
                  pl.BlockSpec((PLE_DIM, D_MODEL), lambda i: (0, 0)),
                  pl.BlockSpec((1, D_MODEL), lambda i: (0, 0))],
        out_specs=pl.BlockSpec((tm, D_MODEL), lambda i: (i, 0)),
        out_shape=jax.ShapeDtypeStruct((n, D_MODEL), F32),
        compiler_params=_params("parallel"),
        name="ple",
    )(h, p, g, wg, wp, g_final)


def _q_perm():
    idx = []
    for j in range(Q_PER_GROUP):
        for g in range(KV_GROUPS):
            base = (g * Q_PER_GROUP + j) * HEAD_DIM
            idx.extend(range(base, base + HEAD_DIM))
    return jnp.asarray(idx, dtype=jnp.int32)


def _layout_w_in(w):
    a1 = 2 * A_WIDTH
    b1 = a1 + 4 * B_WIDTH
    q1 = b1 + C_WIDTH
    w = jnp.concatenate([w[:, a1:b1], w[:, :a1], w[:, b1:q1][:, _q_perm()], w[:, q1:]], axis=1)
    return jnp.pad(w, ((0, 0), (0, Z_W - w.shape[1]))).astype(BF16)


def _layout_w_out(w):
    c0 = A_WIDTH + B_WIDTH
    return jnp.concatenate([w[:c0], w[c0:][_q_perm()]], axis=0).astype(BF16)


def _mix_layer(h, bn, s, tabs, g_mix, w_in, w_out, gm_ln_g, gm_ln_b, gm_ws, gm_bs,
               rw_mu, rw_w0, rw_w_up, rw_a0, rw_a_up, rw_g_up, rw_k_k, rw_k_a, rw_r_k, rw_gn_g, rw_gn_b,
               nsa_cmp_pos, nsa_kc_w1, nsa_kc_w2, nsa_vc_w1, nsa_vc_w2):
    n = bn * s
    z = _proj_in(h, g_mix.reshape(1, -1), _layout_w_in(w_in))
    z3 = z.reshape(bn, s, Z_W)
    y_a = _gmlp(z3, gm_ln_g, gm_ln_b, gm_ws, gm_bs)
    y_b = _rwkv(z3, rw_mu, rw_w0, rw_w_up, rw_a0, rw_a_up, rw_g_up, rw_k_k, rw_k_a, rw_r_k, rw_gn_g, rw_gn_b)
    q_raw, q_rot, kc, vc, ks, vs, kw, vw, gates = _nsa_prep(z3, tabs)
    k_cmp, v_cmp = _nsa_compress(kc, vc, nsa_cmp_pos, nsa_kc_w1, nsa_kc_w2, nsa_vc_w1, nsa_vc_w2)
    y_c = _nsa_attn(q_raw, q_rot, k_cmp, v_cmp, ks, vs, kw, vw, gates)
    return _out_proj(h, y_a.reshape(n, -1), y_b.reshape(n, -1), y_c.reshape(n, -1), _layout_w_out(w_out))


def kernel(x, p, positions, g_mix, w_in, w_out, gm_ln_g, gm_ln_b, gm_ws, gm_bs, rw_mu, rw_w0, rw_w_up, rw_a0,
           rw_a_up, rw_g_up, rw_k_k, rw_k_a, rw_r_k, rw_gn_g, rw_gn_b, nsa_cmp_pos, nsa_kc_w1, nsa_kc_w2,
           nsa_vc_w1, nsa_vc_w2, g_ffn, ffn_w1, ffn_w3, ffn_w2, router_w, router_b, moe_w1, moe_w3, moe_w2,
           g_ple, ple_gate_w, ple_proj_w, g_final):
    bn, s, _ = x.shape
    n = bn * s
    depth = g_mix.shape[0]
    tabs = _rope_tables(positions)
    h = x.reshape(n, D_MODEL)
    for i in range(depth):
        h = _mix_layer(h, bn, s, tabs, g_mix[i], w_in[i], w_out[i], gm_ln_g[i], gm_ln_b[i], gm_ws[i], gm_bs[i],
                       rw_mu[i], rw_w0[i], rw_w_up[i], rw_a0[i], rw_a_up[i], rw_g_up[i], rw_k_k[i], rw_k_a[i],
                       rw_r_k[i], rw_gn_g[i], rw_gn_b[i], nsa_cmp_pos[i], nsa_kc_w1[i], nsa_kc_w2[i],
                       nsa_vc_w1[i], nsa_vc_w2[i])
        j = i // 2
        if i % 2 == 0:
            h = _ffn(h, g_ffn[i].reshape(1, -1), ffn_w1[j:j + 1].astype(BF16), ffn_w3[j:j + 1].astype(BF16),
                     ffn_w2[j:j + 1].astype(BF16))
        else:
            h = _moe(h, g_ffn[i].reshape(1, -1), router_w[j], router_b[j], moe_w1[j].astype(BF16),
                     moe_w3[j].astype(BF16), moe_w2[j].astype(BF16))
        h = _ple(h, p[i].reshape(n, PLE_DIM), g_ple[i].reshape(1, -1), ple_gate_w[i].astype(BF16),
                 ple_proj_w[i].astype(BF16), g_final.reshape(1, -1), final=(i == depth - 1))
    return h.reshape(bn, s, D_MODEL)
```

```python
import functools
import math

import jax
import jax.numpy as jnp
from jax import lax
from jax.experimental import pallas as pl
from jax.experimental.pallas import tpu as pltpu

F32 = jnp.float32
BF16 = jnp.bfloat16

D_MODEL = 1024
HEAD_DIM = 64
A_HEADS = 4
B_HEADS = 4
A_WIDTH = 256
B_WIDTH = 256
C_WIDTH = 512
CHUNK = 128
LN_EPS = 1e-5
GN_EPS = 64e-5
RMS_EPS = 1e-6
RW_LORA = 64
KV_GROUPS = 2
Q_PER_GROUP = 4
CMP_BLOCK = 32
CMP_STRIDE = 16
CMP_HIDDEN = 128
SLC_BLOCK = 64
SLC_TOPK = 16
WINDOW = 512
Q_BLOCK = 128
NEG_INF = -1e30
FORCE = 1e4
ROPE_THETA = 500000.0
ROPE_DIM = 16
D_FF = 2816
N_EXPERTS = 8
TOP_K = 2
PLE_DIM = 256

LANES = 128
Z_B = 0
Z_A = 1024
Z_Q = 1536
Z_KV = 2048
Z_G = 2816
Z_W = 2944
N_GATE = 24

VMEM_LIMIT = 56 * 1024 * 1024
TM_PROJ = 512
TM_FFN = 512
TF_FFN = 1408
TM_GROUP = 512
RW_T = 512
RW_C = 64
TK_SEL = 512

_NN = (((1,), (0,)), ((), ()))
_NT = (((1,), (1,)), ((), ()))


def _params(*sem):
    return pltpu.CompilerParams(dimension_semantics=sem, vmem_limit_bytes=VMEM_LIMIT)


def _dot(a, b, dn=_NN, passes=1):
    if passes == 6:
        return lax.dot_general(a.astype(F32), b.astype(F32), dn, precision=lax.Precision.HIGHEST,
                               preferred_element_type=F32)
    a_hi = a.astype(BF16)
    b_hi = b.astype(BF16)
    out = lax.dot_general(a_hi, b_hi, dn, preferred_element_type=F32)
    if passes == 1:
        return out
    a_lo = (a - a_hi.astype(F32)).astype(BF16)
    out = out + lax.dot_general(a_lo, b_hi, dn, preferred_element_type=F32)
    if passes == 2:
        return out
    b_lo = (b - b_hi.astype(F32)).astype(BF16)
    return out + lax.dot_general(a_hi, b_lo, dn, preferred_element_type=F32)


def _rms(x, g):
    return x * lax.rsqrt(jnp.mean(x * x, axis=-1, keepdims=True) + RMS_EPS) * g


def _sigmoid(x):
    return 1.0 / (1.0 + jnp.exp(-x))


def _div(x, d):
    return lax.shift_right_logical(x, jnp.int32(int(math.log2(d))))


def _head_block_ones(n):
    r = lax.broadcasted_iota(jnp.int32, (n, n), 0)
    c = lax.broadcasted_iota(jnp.int32, (n, n), 1)
    return jnp.where(_div(r, HEAD_DIM) == _div(c, HEAD_DIM), 1.0, 0.0).astype(F32)


def _rope_kernel(inv_ref, pos_ref, cos_ref, sin_ref):
    p = pos_ref[0].astype(F32)
    for j in range(ROPE_DIM // 2):
        ang = p * inv_ref[j]
        cos_ref[0, j] = jnp.cos(ang)
        sin_ref[0, j] = jnp.sin(ang)


def _rope_tables(positions):
    bn, s = positions.shape
    half = ROPE_DIM // 2
    inv = 1.0 / (ROPE_THETA ** (jnp.arange(0, ROPE_DIM, 2, dtype=F32) / ROPE_DIM))
    pos3 = positions.reshape(bn, s // LANES, LANES)
    cos, sin = pl.pallas_call(
        _rope_kernel,
        grid=(bn,),
        in_specs=[pl.BlockSpec(memory_space=pltpu.SMEM),
                  pl.BlockSpec((1, s // LANES, LANES), lambda b: (b, 0, 0))],
        out_specs=[pl.BlockSpec((1, half, s // LANES, LANES), lambda b: (b, 0, 0, 0))] * 2,
        out_shape=[jax.ShapeDtypeStruct((bn, half, s // LANES, LANES), F32)] * 2,
        compiler_params=_params("parallel"),
        name="rope_tables",
    )(inv, pos3)
    cos = cos.reshape(bn, half, s).transpose(0, 2, 1)
    sin = sin.reshape(bn, half, s).transpose(0, 2, 1)
    one = jnp.ones((bn, s, HEAD_DIM - ROPE_DIM), F32)
    zero = jnp.zeros((bn, s, HEAD_DIM - half), F32)
    tab_c = jnp.concatenate([cos, cos, one], axis=-1)
    tab_lo = jnp.concatenate([sin, zero], axis=-1)
    tab_hi = jnp.concatenate([zero[..., :half], sin, zero[..., :HEAD_DIM - ROPE_DIM]], axis=-1)
    rep = lambda t: jnp.tile(t, (1, 1, LANES // HEAD_DIM))
    return rep(tab_c), rep(tab_lo), rep(tab_hi)


def _proj_in_kernel(h_ref, g_ref, w_ref, z_ref):
    y = _rms(h_ref[...], g_ref[...])
    z_ref[...] = jnp.dot(y.astype(BF16), w_ref[...], preferred_element_type=F32)


def _proj_in(h, g, w):
    n = h.shape[0]
    tm = min(TM_PROJ, n)
    return pl.pallas_call(
        _proj_in_kernel,
        grid=(n // tm,),
        in_specs=[pl.BlockSpec((tm, D_MODEL), lambda i: (i, 0)),
                  pl.BlockSpec((1, D_MODEL), lambda i: (0, 0)),
                  pl.BlockSpec((D_MODEL, Z_W), lambda i: (0, 0))],
        out_specs=pl.BlockSpec((tm, Z_W), lambda i: (i, 0)),
        out_shape=jax.ShapeDtypeStruct((n, Z_W), F32),
        compiler_params=_params("parallel"),
        name="proj_in",
    )(h, g, w)


def _gmlp_kernel(z_ref, lng_ref, lnb_ref, ws_ref, bias_ref, o_ref):
    gz = jax.nn.gelu(z_ref[0])
    u = gz[:, :A_WIDTH]
    v = gz[:, A_WIDTH:]
    ones = _head_block_ones(A_WIDTH)
    mu = _dot(v, ones, passes=2) * (1.0 / HEAD_DIM)
    d = v - mu
    var = _dot(d * d, ones, passes=2) * (1.0 / HEAD_DIM)
    vn = d * lax.rsqrt(var + LN_EPS) * lng_ref[...] + lnb_ref[...]
    r = lax.broadcasted_iota(jnp.int32, (CHUNK, CHUNK), 0)
    c = lax.broadcasted_iota(jnp.int32, (CHUNK, CHUNK), 1)
    lane_head = _div(lax.broadcasted_iota(jnp.int32, (CHUNK, A_WIDTH), 1), HEAD_DIM)
    mixed = bias_ref[...]
    for hd in range(A_HEADS):
        w = jnp.where(c <= r, ws_ref[hd], 0.0)
        vh = jnp.where(lane_head == hd, vn, 0.0)
        mixed = mixed + _dot(w, vh)
    o_ref[0] = (u * mixed).astype(o_ref.dtype)


def _gmlp(z3, ln_g, ln_b, w_s, b_s):
    bn, s, _ = z3.shape
    bias = jnp.repeat(b_s.T, HEAD_DIM, axis=1)
    return pl.pallas_call(
        _gmlp_kernel,
        grid=(bn, s // CHUNK),
        in_specs=[pl.BlockSpec((1, CHUNK, 2 * A_WIDTH), lambda b, i: (b, i, Z_A // (2 * A_WIDTH))),
                  pl.BlockSpec((1, A_WIDTH), lambda b, i: (0, 0)),
                  pl.BlockSpec((1, A_WIDTH), lambda b, i: (0, 0)),
                  pl.BlockSpec((A_HEADS, CHUNK, CHUNK), lambda b, i: (0, 0, 0)),
                  pl.BlockSpec((CHUNK, A_WIDTH), lambda b, i: (0, 0))],
        out_specs=pl.BlockSpec((1, CHUNK, A_WIDTH), lambda b, i: (b, i, 0)),
        out_shape=jax.ShapeDtypeStruct((bn, s, A_WIDTH), BF16),
        compiler_params=_params("parallel", "parallel"),
        name="gmlp",
    )(z3, ln_g.reshape(1, A_WIDTH), ln_b.reshape(1, A_WIDTH), w_s, bias)


RW_PASSES = 1


def _rwkv_chunks(r, lw, k, v, kk, a, s0, c_len):
    nh = B_HEADS
    m = nh * c_len
    chunks = range(r.shape[0] // c_len)
    dot = functools.partial(_dot, passes=RW_PASSES)
    ti = lax.broadcasted_iota(jnp.int32, (c_len, c_len), 0)
    tj = lax.broadcasted_iota(jnp.int32, (c_len, c_len), 1)
    tril = jnp.where(tj <= ti, 1.0, 0.0).astype(F32)
    row_head = _div(lax.broadcasted_iota(jnp.int32, (m, B_WIDTH), 0), c_len)
    lane_head = _div(lax.broadcasted_iota(jnp.int32, (m, B_WIDTH), 1), HEAD_DIM)
    head_mask = row_head == lane_head
    rep = lambda x: jnp.concatenate([x] * nh, axis=0)
    stack = lambda x: jnp.where(head_mask, rep(x), 0.0)
    ri = lax.broadcasted_iota(jnp.int32, (m, m), 0)
    ci = lax.broadcasted_iota(jnp.int32, (m, m), 1)
    same = _div(ri, c_len) == _div(ci, c_len)
    strict = same & (ci < ri)
    incl = same & (ci <= ri)
    eye = ri == ci
    assert 2 * c_len == LANES
    low = lax.broadcasted_iota(jnp.int32, (2 * m, LANES), 1) < c_len

    a_st, r_st, v_st, bw_st, kw_st, w_c, l_ab, l_ak, l_rb, l_rk = ([] for _ in range(10))
    for c in chunks:
        sl = slice(c * c_len, (c + 1) * c_len)
        cum = _dot(tril, lw[sl], passes=3)
        w_t = jnp.exp(cum)
        w_i = jnp.exp(-cum)
        a_t = -kk[sl] * jnp.exp(cum - lw[sl])
        b_t = kk[sl] * a[sl] * w_i
        k_t = k[sl] * w_i
        w_c.append(w_t[c_len - 1:c_len, :])
        a_st.append(stack(a_t))
        r_st.append(stack(r[sl] * w_t))
        v_st.append(stack(v[sl]))
        bw_st.append(stack(b_t * w_c[c]))
        kw_st.append(stack(k_t * w_c[c]))
        cross = dot(jnp.concatenate([a_st[c], r_st[c]], axis=0), jnp.concatenate([b_t, k_t], axis=0), _NT)
        swapped = pltpu.roll(cross, c_len, axis=1)
        vs_b = jnp.concatenate([jnp.where(low, cross, swapped)] * (nh // 2), axis=1)
        vs_k = jnp.concatenate([jnp.where(low, swapped, cross)] * (nh // 2), axis=1)
        l_ab.append(jnp.where(strict, vs_b[:m], 0.0))
        l_ak.append(jnp.where(strict, vs_k[:m], 0.0))
        l_rb.append(jnp.where(incl, vs_b[m:], 0.0))
        l_rk.append(jnp.where(incl, vs_k[m:], 0.0))

    p_inv = [jnp.where(eye, 1.0, 0.0) + l_ab[c] for c in chunks]
    pw = l_ab
    for _ in range(int(math.log2(c_len)) - 1):
        pw = [dot(pw[c], pw[c]) for c in chunks]
        p_inv = [p_inv[c] + dot(p_inv[c], pw[c]) for c in chunks]
    t_m = [dot(l_ak[c], v_st[c]) for c in chunks]
    q_m = [dot(p_inv[c], t_m[c]) for c in chunks]
    p_m = [dot(p_inv[c], a_st[c]) for c in chunks]
    g_m = [r_st[c] + dot(l_rb[c], p_m[c]) for c in chunks]
    h_m = [dot(l_rb[c], q_m[c]) + dot(l_rk[c], v_st[c]) for c in chunks]
    m_m = [jnp.where(eye, w_c[c], 0.0) + dot(p_m[c].T, bw_st[c]) for c in chunks]
    n_m = [dot(q_m[c].T, bw_st[c]) + dot(v_st[c].T, kw_st[c]) for c in chunks]

    states = [s0]
    for c in chunks:
        states.append(dot(states[c], m_m[c]) + n_m[c])
    ys = []
    for c in chunks:
        y_st = dot(g_m[c], states[c], _NT) + h_m[c]
        y = y_st[0:c_len]
        for hd in range(1, nh):
            y = y + y_st[hd * c_len:(hd + 1) * c_len]
        ys.append(y)
    return jnp.concatenate(ys, axis=0), states[-1]


def _rwkv_kernel(z_ref, mu_ref, w0_ref, a0_ref, wwa_ref, gup_ref, kk_ref, ka_ref, rk_ref, gng_ref, gnb_ref,
                 o_ref, carry_ref, state_ref, *, c_len):
    @pl.when(pl.program_id(1) == 0)
    def _():
        carry_ref[...] = jnp.zeros_like(carry_ref)
        state_ref[...] = jnp.zeros_like(state_ref)

    zb = z_ref[0]
    t_len = zb.shape[0]
    row = lax.broadcasted_iota(jnp.int32, zb.shape, 0)
    z_prev = jnp.where(row == 0, carry_ref[...], pltpu.roll(zb, 1, axis=0))
    carry_ref[...] = zb[t_len - 1:t_len, :]
    zz = zb + (z_prev - zb) * mu_ref[...]
    r = zz[:, 0:B_WIDTH]
    k = zz[:, B_WIDTH:2 * B_WIDTH]
    v = zz[:, 2 * B_WIDTH:3 * B_WIDTH]
    wa = zz[:, 3 * B_WIDTH:3 * B_WIDTH + 2 * RW_LORA]
    gd = zz[:, 3 * B_WIDTH + 2 * RW_LORA:]
    lane = lax.broadcasted_iota(jnp.int32, wa.shape, 1)
    proj = _dot(jnp.where(lane < RW_LORA, jnp.tanh(wa), wa), wwa_ref[...])
    x = -(w0_ref[...] + proj[:, :B_WIDTH])
    softplus = jnp.maximum(x, 0.0) + jnp.log(1.0 + jnp.exp(-jnp.abs(x)))
    lw = -jnp.exp(-softplus - 0.5)
    a = _sigmoid(a0_ref[...] + proj[:, B_WIDTH:])
    g = _dot(_sigmoid(gd), gup_ref[...])
    ones = _head_block_ones(B_WIDTH)
    kk = k * kk_ref[...]
    kk = kk * lax.rsqrt(jnp.maximum(_dot(kk * kk, ones, passes=2), 1e-24))
    k2 = k * (1.0 + (a - 1.0) * ka_ref[...])

    y, state_ref[...] = _rwkv_chunks(r, lw, k2, v, kk, a, state_ref[...], c_len)

    mu_y = _dot(y, ones, passes=2) * (1.0 / HEAD_DIM)
    d = y - mu_y
    var = _dot(d * d, ones, passes=2) * (1.0 / HEAD_DIM)
    yn = d * lax.rsqrt(var + GN_EPS) * gng_ref[...] + gnb_ref[...]
    bonus = _dot(r * k2 * rk_ref[...], ones, passes=2) * v
    o_ref[0] = ((yn + bonus) * g).astype(o_ref.dtype)


def _rwkv(z3, mu, w0, w_up, a0, a_up, g_up, k_k, k_a, r_k, gn_g, gn_b):
    bn, s, _ = z3.shape
    t_len = min(RW_T, s)
    zero = jnp.zeros((RW_LORA, B_WIDTH), F32)
    wwa = jnp.concatenate([jnp.concatenate([w_up, zero], axis=1),
                           jnp.concatenate([zero, a_up], axis=1)], axis=0).astype(BF16)
    row = lambda t: t.reshape(1, -1)
    vec = pl.BlockSpec((1, B_WIDTH), lambda b, i: (0, 0))
    return pl.pallas_call(
        functools.partial(_rwkv_kernel, c_len=RW_C),
        grid=(bn, s // t_len),
        in_specs=[pl.BlockSpec((1, t_len, 4 * B_WIDTH), lambda b, i: (b, i, Z_B // (4 * B_WIDTH))),
                  pl.BlockSpec((1, 4 * B_WIDTH), lambda b, i: (0, 0)),
                  vec, vec,
                  pl.BlockSpec((2 * RW_LORA, 2 * B_WIDTH), lambda b, i: (0, 0)),
                  pl.BlockSpec((2 * RW_LORA, B_WIDTH), lambda b, i: (0, 0)),
                  vec, vec, vec, vec, vec],
        out_specs=pl.BlockSpec((1, t_len, B_WIDTH), lambda b, i: (b, i, 0)),
        out_shape=jax.ShapeDtypeStruct((bn, s, B_WIDTH), BF16),
        scratch_shapes=[pltpu.VMEM((1, 4 * B_WIDTH), F32), pltpu.VMEM((B_WIDTH, B_WIDTH), F32)],
        compiler_params=_params("parallel", "arbitrary"),
        name="rwkv7",
    )(z3, row(mu), row(w0), row(a0), wwa, g_up.astype(BF16), row(k_k), row(k_a), row(r_k), row(gn_g), row(gn_b))


def _rope(x, tab_c, tab_lo, tab_hi):
    n = x.shape[-1]
    half = ROPE_DIM // 2
    return x * tab_c - pltpu.roll(x, n - half, axis=1) * tab_lo + pltpu.roll(x, half, axis=1) * tab_hi


N_HEADS_C = KV_GROUPS * Q_PER_GROUP
QT_COLS = N_HEADS_C * Q_BLOCK
GATE_ROWS = 32
LOG2E = 1.4426950408889634


def _nsa_prep_kernel(zq_ref, zkc_ref, zks_ref, zkw_ref, zg_ref, tc_ref, tl_ref, th_ref,
                     qraw_ref, qrot_ref, kc_ref, vc_ref, ks_ref, vst_ref, kw_ref, vwt_ref, gate_ref):
    tab_c, tab_lo, tab_hi = tc_ref[0], tl_ref[0], th_ref[0]
    nrep = C_WIDTH // LANES
    wide = lambda t: jnp.concatenate([t] * nrep, axis=1)
    q = zq_ref[0] * (HEAD_DIM ** -0.5 * LOG2E)
    q_rot = _rope(q, wide(tab_c), wide(tab_lo), wide(tab_hi))
    ts = q.shape[0]
    row_group0 = lax.broadcasted_iota(jnp.int32, (LANES, ts), 0) < HEAD_DIM

    def put_queries(ref, x):
        for j in range(Q_PER_GROUP):
            xt = x[:, j * LANES:(j + 1) * LANES].T
            for g in range(KV_GROUPS):
                keep = row_group0 if g == 0 else jnp.logical_not(row_group0)
                xm = jnp.where(keep, xt, 0.0).astype(BF16)
                for qq in range(ts // Q_BLOCK):
                    col = qq * QT_COLS + (g * Q_PER_GROUP + j) * Q_BLOCK
                    ref[0, :, col:col + Q_BLOCK] = xm[:, qq * Q_BLOCK:(qq + 1) * Q_BLOCK]

    put_queries(qraw_ref, q)
    put_queries(qrot_ref, q_rot)
    kc_ref[0] = zkc_ref[0][:, :LANES].astype(BF16)
    vc_ref[0] = zkc_ref[0][:, LANES:].astype(BF16)
    ks_ref[0] = _rope(zks_ref[0][:, :LANES], tab_c, tab_lo, tab_hi).astype(BF16)
    vst_ref[0, 0] = zks_ref[0][:, LANES:].T.astype(BF16)
    kw_ref[0] = _rope(zkw_ref[0][:, :LANES], tab_c, tab_lo, tab_hi).astype(BF16)
    vwt = zkw_ref[0][:, LANES:].T.astype(BF16)
    for qq in range(ts // Q_BLOCK):
        vwt_ref[0, qq] = vwt[:, qq * Q_BLOCK:(qq + 1) * Q_BLOCK]
    gate_ref[0] = _sigmoid(zg_ref[0]).T[:GATE_ROWS, :]


def _nsa_prep(z3, tabs):
    bn, s, _ = z3.shape
    ts = TK_SEL
    zspec = lambda width, off: pl.BlockSpec((1, ts, width), lambda b, i: (b, i, off // width))
    tspec = pl.BlockSpec((1, ts, LANES), lambda b, i: (b, i, 0))
    rowmajor = pl.BlockSpec((1, ts, LANES), lambda b, i: (b, i, 0))
    qspec = pl.BlockSpec((1, LANES, N_HEADS_C * ts), lambda b, i: (b, 0, i))
    rm_sds = jax.ShapeDtypeStruct((bn, s, LANES), BF16)
    q_sds = jax.ShapeDtypeStruct((bn, LANES, N_HEADS_C * s), BF16)
    return pl.pallas_call(
        _nsa_prep_kernel,
        grid=(bn, s // ts),
        in_specs=[zspec(C_WIDTH, Z_Q), zspec(2 * LANES, Z_KV), zspec(2 * LANES, Z_KV + 2 * LANES),
                  zspec(2 * LANES, Z_KV + 4 * LANES), zspec(LANES, Z_G), tspec, tspec, tspec],
        out_specs=[qspec, qspec, rowmajor, rowmajor, rowmajor,
                   pl.BlockSpec((1, 1, LANES, ts), lambda b, i: (b, i, 0, 0)), rowmajor,
                   pl.BlockSpec((1, ts // Q_BLOCK, LANES, Q_BLOCK), lambda b, i: (b, i, 0, 0)),
                   pl.BlockSpec((1, GATE_ROWS, ts), lambda b, i: (b, 0, i))],
        out_shape=[q_sds, q_sds, rm_sds, rm_sds, rm_sds,
                   jax.ShapeDtypeStruct((bn, s // ts, LANES, ts), BF16), rm_sds,
                   jax.ShapeDtypeStruct((bn, s // Q_BLOCK, LANES, Q_BLOCK), BF16),
                   jax.ShapeDtypeStruct((bn, GATE_ROWS, s), F32)],
        compiler_params=_params("parallel", "parallel"),
        name="nsa_prep",
    )(z3, z3, z3, z3, z3, *tabs)


def _nsa_compress_kernel(xk_ref, xv_ref, pos_ref, kw1a_ref, kw1b_ref, kw2_ref, vw1a_ref, vw1b_ref, vw2_ref,
                         ko_ref, vo_ref):
    n = xk_ref.shape[1]
    pos_a = pos_ref[0]
    pos_b = pos_ref[1]
    for x_ref, w1a, w1b, w2, o_ref, transposed in ((xk_ref, kw1a_ref, kw1b_ref, kw2_ref, ko_ref, False),
                                                    (xv_ref, vw1a_ref, vw1b_ref, vw2_ref, vo_ref, True)):
        x = x_ref[0]
        first = _dot(x, w1a[...])
        second = _dot(x, w1b[...])
        pc = _dot(pos_a, w1a[...]) + _dot(pos_b, w1b[...])
        hid = jax.nn.gelu(first + pltpu.roll(second, n - 1, axis=0) + pc[0:1, :])
        out = _dot(hid, w2[...])
        o_ref[0] = (out.T if transposed else out).astype(o_ref.dtype)


def _expand_cmp_weights(w1, w2):
    half = CMP_BLOCK // 2
    w1r = w1.reshape(CMP_BLOCK, HEAD_DIM, CMP_HIDDEN)
    eye = jnp.eye(KV_GROUPS, dtype=F32)
    w1e = jnp.einsum('ldh,gk->lgdkh', w1r, eye).reshape(CMP_BLOCK * LANES, KV_GROUPS * CMP_HIDDEN)
    w2e = jnp.einsum('hd,gk->ghkd', w2, eye).reshape(KV_GROUPS * CMP_HIDDEN, LANES)
    return (w1e[:half * LANES].astype(BF16), w1e[half * LANES:].astype(BF16), w2e.astype(BF16))


def _nsa_compress(kc, vc, cmp_pos, kc_w1, kc_w2, vc_w1, vc_w2):
    bn, s, _ = kc.shape
    n = s // CMP_STRIDE
    xw = CMP_STRIDE * LANES
    xk = kc.reshape(bn, n, xw)
    xv = vc.reshape(bn, n, xw)
    pos = jnp.tile(cmp_pos[:, None, :], (1, KV_GROUPS, 1)).reshape(2, 1, xw)
    pos = jnp.broadcast_to(pos, (2, 8, xw)).astype(BF16)
    kw = _expand_cmp_weights(kc_w1, kc_w2)
    vw = _expand_cmp_weights(vc_w1, vc_w2)
    full = lambda a: pl.BlockSpec(a.shape, lambda b: (0,) * a.ndim)
    xspec = pl.BlockSpec((1, n, xw), lambda b: (b, 0, 0))
    return pl.pallas_call(
        _nsa_compress_kernel,
        grid=(bn,),
        in_specs=[xspec, xspec, full(pos)] + [full(a) for a in kw + vw],
        out_specs=[pl.BlockSpec((1, n, LANES), lambda b: (b, 0, 0)), pl.BlockSpec((1, LANES, n), lambda b: (b, 0, 0))],
        out_shape=[jax.ShapeDtypeStruct((bn, n, LANES), BF16), jax.ShapeDtypeStruct((bn, LANES, n), BF16)],
        compiler_params=_params("parallel"),
        name="nsa_compress",
    )(xk, xv, pos, *kw, *vw)


def _nsa_attn_kernel(qraw_ref, qrot_ref, kcmp_ref, vcmpt_ref, ks_ref, vst_ref, kw_ref, vwt_ref, gate_ref, o_ref,
                     m_ref, l_ref, acc_ref, sa_ref, sb_ref):
    tq = Q_BLOCK
    gw = Q_PER_GROUP * tq
    seq = ks_ref.shape[1]
    n_cmp = kcmp_ref.shape[1]
    n_slc = seq // SLC_BLOCK
    n_sel = min(SLC_TOPK, n_slc)
    qb = pl.program_id(1)
    t0 = qb * tq
    gcols = lambda g: slice(g * gw, (g + 1) * gw)
    grows = lambda g: slice(g * HEAD_DIM, (g + 1) * HEAD_DIM)
    jcols = lambda j: slice(j * tq, (j + 1) * tq)

    t_c = t0 + lax.broadcasted_iota(jnp.int32, (n_cmp, tq), 1)
    n_c = lax.broadcasted_iota(jnp.int32, (n_cmp, tq), 0)
    cmask = (n_c * CMP_STRIDE + (CMP_BLOCK - 1)) <= t_c
    cbias = jnp.where(cmask, 0.0, NEG_INF)
    o_c, p_sum = [], []
    for g in range(KV_GROUPS):
        s = _dot(kcmp_ref[0], qraw_ref[0, :, gcols(g)])
        parts, total = [], None
        for j in range(Q_PER_GROUP):
            sj = s[:, jcols(j)] + cbias
            e = jnp.exp2(sj - jnp.max(sj, axis=0, keepdims=True))
            p = jnp.where(cmask, e * (1.0 / jnp.sum(e, axis=0, keepdims=True)), 0.0)
            total = p if total is None else total + p
            parts.append(p.astype(BF16))
        p_sum.append(total)
        o_c.append(_dot(vcmpt_ref[0, grows(g), :], jnp.concatenate(parts, axis=1)))

    n_band = WINDOW // tq + 1
    band = n_band * tq
    wb = jnp.maximum(qb - WINDOW // tq, 0)
    w0 = pl.multiple_of(wb * tq, tq)
    t_w = t0 + lax.broadcasted_iota(jnp.int32, (band, tq), 1)
    k_w = w0 + lax.broadcasted_iota(jnp.int32, (band, tq), 0)
    wbias = jnp.where((k_w <= t_w) & (k_w > t_w - WINDOW), 0.0, NEG_INF)
    kband = kw_ref[0, pl.ds(w0, band), :]
    o_w = []
    for g in range(KV_GROUPS):
        s = _dot(kband, qrot_ref[0, :, gcols(g)])
        parts, inv = [], []
        for j in range(Q_PER_GROUP):
            sj = s[:, jcols(j)] + wbias
            e = jnp.exp2(sj - jnp.max(sj, axis=0, keepdims=True))
            inv.append(1.0 / jnp.sum(e, axis=0, keepdims=True))
            parts.append(e.astype(BF16))
        vband = jnp.concatenate([vwt_ref[0, wb + i, grows(g), :] for i in range(n_band)], axis=1)
        o_w.append(_dot(vband, jnp.concatenate(parts, axis=1)) * jnp.concatenate(inv, axis=1))

    m_o = lax.broadcasted_iota(jnp.int32, (n_slc, n_cmp), 0) * SLC_BLOCK
    n_o = lax.broadcasted_iota(jnp.int32, (n_slc, n_cmp), 1) * CMP_STRIDE
    overlap_t = jnp.where((n_o < m_o + SLC_BLOCK) & (n_o + (CMP_BLOCK - 1) >= m_o), 1.0, 0.0).astype(F32)
    m_i = lax.broadcasted_iota(jnp.int32, (n_slc, tq), 0)
    blk_t = _div(t0 + lax.broadcasted_iota(jnp.int32, (n_slc, tq), 1), SLC_BLOCK)
    valid = m_i <= blk_t
    forced = (m_i == 0) | (m_i == blk_t) | (m_i == blk_t - 1)
    sel = []
    for g in range(KV_GROUPS):
        imp = _dot(overlap_t, p_sum[g], passes=3)
        imp = jnp.where(valid, imp + jnp.where(forced, FORCE, 0.0), -FORCE)
        rank = jnp.zeros((n_slc, tq), F32)
        for mp in range(n_slc):
            other = imp[mp:mp + 1, :]
            rank = rank + jnp.where(m_i > mp, jnp.where(other >= imp, 1.0, 0.0), jnp.where(other > imp, 1.0, 0.0))
        sel.append(jnp.where((rank < n_sel) & valid, 1.0, 0.0).astype(BF16))

    m_ref[...] = jnp.full(m_ref.shape, NEG_INF, F32)
    l_ref[...] = jnp.zeros(l_ref.shape, F32)
    acc_ref[...] = jnp.zeros(acc_ref.shape, F32)
    t_s = t0 + lax.broadcasted_iota(jnp.int32, (TK_SEL, tq), 1)
    k_s = lax.broadcasted_iota(jnp.int32, (TK_SEL, tq), 0)
    k_e = lax.broadcasted_iota(jnp.int32, (TK_SEL, n_slc), 0)
    m_e = lax.broadcasted_iota(jnp.int32, (TK_SEL, n_slc), 1)

    n_tiles = (t0 + tq + TK_SEL - 1) // TK_SEL

    def scores(kt, s_ref):
        k0 = pl.multiple_of(jnp.minimum(kt, n_tiles - 1) * TK_SEL, TK_SEL)
        keys = ks_ref[0, pl.ds(k0, TK_SEL), :]
        causal = (k_s + k0) <= t_s
        expand = jnp.where(_div(k_e + k0, SLC_BLOCK) == m_e, 1.0, 0.0).astype(BF16)
        for g in range(KV_GROUPS):
            bias = jnp.where((_dot(expand, sel[g]) > 0.5) & causal, 0.0, NEG_INF)
            s = _dot(keys, qrot_ref[0, :, gcols(g)])
            for j in range(Q_PER_GROUP):
                s_ref[g, :, jcols(j)] = s[:, jcols(j)] + bias

    def attend(kt, s_ref):
        for g in range(KV_GROUPS):
            parts, alphas = [], []
            for j in range(Q_PER_GROUP):
                cs = slice(g * gw + j * tq, g * gw + (j + 1) * tq)
                sj = s_ref[g, :, jcols(j)]
                m_old = m_ref[:, cs]
                m_new = jnp.maximum(m_old, jnp.max(sj, axis=0, keepdims=True))
                alpha = jnp.exp2(m_old - m_new)
                p = jnp.exp2(sj - m_new)
                l_ref[:, cs] = alpha * l_ref[:, cs] + jnp.sum(p, axis=0, keepdims=True)
                m_ref[:, cs] = m_new
                alphas.append(alpha)
                parts.append(p.astype(BF16))
            pv = _dot(vst_ref[0, kt, grows(g), :], jnp.concatenate(parts, axis=1))
            acc_ref[g] = acc_ref[g] * jnp.concatenate(alphas, axis=1) + pv

    odd = lax.rem(n_tiles, 2)
    scores(0, sa_ref)

    @pl.when(odd == 1)
    def _():
        attend(0, sa_ref)
        scores(1, sa_ref)

    def sel_pair(i, carry):
        kt = odd + 2 * i
        scores(kt + 1, sb_ref)
        attend(kt, sa_ref)
        scores(kt + 2, sa_ref)
        attend(kt + 1, sb_ref)
        return carry

    lax.fori_loop(0, (n_tiles - odd) // 2, sel_pair, 0)

    gates = gate_ref[0]
    outs = []
    for j in range(Q_PER_GROUP):
        for g in range(KV_GROUPS):
            row = (g * Q_PER_GROUP + j) * 3
            cs = slice(g * gw + j * tq, g * gw + (j + 1) * tq)
            o_s = acc_ref[g][:, jcols(j)] * (1.0 / l_ref[:, cs])
            outs.append(gates[row:row + 1, :] * o_c[g][:, jcols(j)] + gates[row + 1:row + 2, :] * o_s
                        + gates[row + 2:row + 3, :] * o_w[g][:, jcols(j)])
    o_ref[0] = jnp.concatenate(outs, axis=0).T.astype(o_ref.dtype)


def _nsa_attn(q_raw, q_rot, k_cmp, v_cmp_t, ks, vs_t, kw, vw_t, gates_t):
    bn, s, _ = ks.shape
    n_cmp = k_cmp.shape[1]
    qspec = pl.BlockSpec((1, LANES, QT_COLS), lambda b, i: (b, 0, i))
    kspec = pl.BlockSpec((1, s, LANES), lambda b, i: (b, 0, 0))
    whole = lambda a: pl.BlockSpec((1,) + a.shape[1:], lambda b, i: (b,) + (0,) * (a.ndim - 1))
    return pl.pallas_call(
        _nsa_attn_kernel,
        grid=(bn, s // Q_BLOCK),
        in_specs=[qspec, qspec, whole(k_cmp), whole(v_cmp_t), kspec, whole(vs_t), kspec, whole(vw_t),
                  pl.BlockSpec((1, GATE_ROWS, Q_BLOCK), lambda b, i: (b, 0, i))],
        out_specs=pl.BlockSpec((1, Q_BLOCK, C_WIDTH), lambda b, i: (b, i, 0)),
        out_shape=jax.ShapeDtypeStruct((bn, s, C_WIDTH), BF16),
        scratch_shapes=[pltpu.VMEM((1, QT_COLS), F32), pltpu.VMEM((1, QT_COLS), F32),
                        pltpu.VMEM((KV_GROUPS, HEAD_DIM, Q_PER_GROUP * Q_BLOCK), F32),
                        pltpu.VMEM((KV_GROUPS, TK_SEL, Q_PER_GROUP * Q_BLOCK), F32),
                        pltpu.VMEM((KV_GROUPS, TK_SEL, Q_PER_GROUP * Q_BLOCK), F32)],
        compiler_params=_params("parallel", "arbitrary"),
        name="nsa_attn",
    )(q_raw, q_rot, k_cmp, v_cmp_t, ks, vs_t, kw, vw_t, gates_t)


def _out_proj_kernel(h_ref, ya_ref, yb_ref, yc_ref, w_ref, o_ref):
    acc = jnp.dot(ya_ref[...], w_ref[0:A_WIDTH, :], preferred_element_type=F32)
    acc = acc + jnp.dot(yb_ref[...], w_ref[A_WIDTH:A_WIDTH + B_WIDTH, :], preferred_element_type=F32)
    acc = acc + jnp.dot(yc_ref[...], w_ref[A_WIDTH + B_WIDTH:, :], preferred_element_type=F32)
    o_ref[...] = h_ref[...] + acc


def _out_proj(h, ya, yb, yc, w):
    n = h.shape[0]
    tm = min(TM_PROJ, n)
    tok = lambda width: pl.BlockSpec((tm, width), lambda i: (i, 0))
    return pl.pallas_call(
        _out_proj_kernel,
        grid=(n // tm,),
        in_specs=[tok(D_MODEL), tok(A_WIDTH), tok(B_WIDTH), tok(C_WIDTH),
                  pl.BlockSpec((D_MODEL, D_MODEL), lambda i: (0, 0))],
        out_specs=tok(D_MODEL),
        out_shape=jax.ShapeDtypeStruct((n, D_MODEL), F32),
        compiler_params=_params("parallel"),
        name="out_proj",
    )(h, ya, yb, yc, w)


def _swiglu_step(x, w1_ref, w3_ref, w2_ref):
    h1 = jnp.dot(x, w1_ref[0], preferred_element_type=F32)
    h3 = jnp.dot(x, w3_ref[0], preferred_element_type=F32)
    hid = h1 * _sigmoid(h1) * h3
    return jnp.dot(hid.astype(BF16), w2_ref[0], preferred_element_type=F32)


def _ffn_kernel(h_ref, g_ref, w1_ref, w3_ref, w2_ref, o_ref, hn_ref, acc_ref):
    f = pl.program_id(1)

    @pl.when(f == 0)
    def _():
        hn_ref[...] = _rms(h_ref[...], g_ref[...]).astype(BF16)
        acc_ref[...] = jnp.zeros_like(acc_ref)

    acc_ref[...] += _swiglu_step(hn_ref[...], w1_ref, w3_ref, w2_ref)

    @pl.when(f == pl.num_programs(1) - 1)
    def _():
        o_ref[...] = h_ref[...] + acc_ref[...]


def _ffn(h, g, w1, w3, w2):
    n = h.shape[0]
    tm = min(TM_FFN, n)
    return pl.pallas_call(
        _ffn_kernel,
        grid=(n // tm, D_FF // TF_FFN),
        in_specs=[pl.BlockSpec((tm, D_MODEL), lambda i, f: (i, 0)),
                  pl.BlockSpec((1, D_MODEL), lambda i, f: (0, 0)),
                  pl.BlockSpec((1, D_MODEL, TF_FFN), lambda i, f: (0, 0, f)),
                  pl.BlockSpec((1, D_MODEL, TF_FFN), lambda i, f: (0, 0, f)),
                  pl.BlockSpec((1, TF_FFN, D_MODEL), lambda i, f: (0, f, 0))],
        out_specs=pl.BlockSpec((tm, D_MODEL), lambda i, f: (i, 0)),
        out_shape=jax.ShapeDtypeStruct((n, D_MODEL), F32),
        scratch_shapes=[pltpu.VMEM((tm, D_MODEL), BF16), pltpu.VMEM((tm, D_MODEL), F32)],
        compiler_params=_params("parallel", "arbitrary"),
        name="ffn_swiglu",
    )(h, g, w1, w3, w2)


META_E1, META_E2, META_R1, META_R2, META_P1, META_P2 = range(6)


def _router_kernel(h_ref, g_ref, rw_ref, rb_ref, hn_ref, meta_ref, cnt_ref, carry_ref):
    @pl.when(pl.program_id(0) == 0)
    def _():
        carry_ref[...] = jnp.zeros_like(carry_ref)

    hn = _rms(h_ref[...], g_ref[...])
    hn_ref[...] = hn
    tm = hn.shape[0]
    lane = lax.broadcasted_iota(jnp.int32, (tm, LANES), 1).astype(F32)
    logits = jnp.where(lane < N_EXPERTS, _dot(hn, rw_ref[...], passes=3) + rb_ref[...], NEG_INF)
    top1 = jnp.max(logits, axis=-1, keepdims=True)
    idx1 = jnp.min(jnp.where(logits == top1, lane, float(LANES)), axis=-1, keepdims=True)
    rest = jnp.where(lane == idx1, NEG_INF, logits)
    top2 = jnp.max(rest, axis=-1, keepdims=True)
    idx2 = jnp.min(jnp.where(rest == top2, lane, float(LANES)), axis=-1, keepdims=True)
    ex = jnp.exp(top2 - top1)
    picked = jnp.where((lane == idx1) | (lane == idx2), 1.0, 0.0)
    r = lax.broadcasted_iota(jnp.int32, (tm, tm), 0)
    c = lax.broadcasted_iota(jnp.int32, (tm, tm), 1)
    before = _dot(jnp.where(c < r, 1.0, 0.0), picked) + carry_ref[...]
    rank1 = jnp.sum(jnp.where(lane == idx1, before, 0.0), axis=-1, keepdims=True)
    rank2 = jnp.sum(jnp.where(lane == idx2, before, 0.0), axis=-1, keepdims=True)
    carry_ref[...] += jnp.sum(picked, axis=0, keepdims=True)
    cnt_ref[...] = carry_ref[...]
    meta = jnp.zeros((tm, LANES), F32)
    for pos, val in ((META_E1, idx1), (META_E2, idx2), (META_R1, rank1), (META_R2, rank2),
                     (META_P1, 1.0 / (1.0 + ex)), (META_P2, ex / (1.0 + ex))):
        meta = jnp.where(lane == pos, val, meta)
    meta_ref[...] = meta


def _dispatch_kernel(d1_ref, d2_ref, hn_ref, xs_in_ref, xs_ref, sem):
    del xs_in_ref
    tm = hn_ref.shape[0]

    def row_copy(r, dest):
        return pltpu.make_async_copy(hn_ref.at[pl.ds(r, 1)], xs_ref.at[pl.ds(dest, 1)], sem)

    def issue(r, carry):
        row_copy(r, d1_ref[0, 0, r]).start()
        row_copy(r, d2_ref[0, 0, r]).start()
        return carry

    lax.fori_loop(0, tm, issue, 0, unroll=8)
    for _ in range(2):
        pltpu.make_async_copy(hn_ref, xs_ref.at[pl.ds(0, tm)], sem).wait()


def _grouped_kernel(te_ref, nu_ref, xs_ref, w1_ref, w3_ref, w2_ref, ys_ref, acc_ref):
    i = pl.program_id(0)
    f = pl.program_id(1)
    last = f == pl.num_programs(1) - 1
    used = i < nu_ref[0]

    @pl.when(used & (f == 0))
    def _():
        acc_ref[...] = jnp.zeros_like(acc_ref)

    @pl.when(used)
    def _():
        acc_ref[...] += _swiglu_step(xs_ref[...].astype(BF16), w1_ref, w3_ref, w2_ref)

    @pl.when(used & last)
    def _():
        ys_ref[...] = acc_ref[...]

    @pl.when(jnp.logical_not(used) & last)
    def _():
        ys_ref[...] = jnp.zeros_like(ys_ref)


def _combine_kernel(d1_ref, d2_ref, h_ref, meta_ref, ys_ref, o_ref, buf1, buf2, sem):
    tm = h_ref.shape[0]

    def row_copy(dest, buf, r):
        return pltpu.make_async_copy(ys_ref.at[pl.ds(dest, 1)], buf.at[pl.ds(r, 1)], sem)

    def issue(r, carry):
        row_copy(d1_ref[0, 0, r], buf1, r).start()
        row_copy(d2_ref[0, 0, r], buf2, r).start()
        return carry

    lax.fori_loop(0, tm, issue, 0, unroll=8)
    for buf in (buf1, buf2):
        pltpu.make_async_copy(ys_ref.at[pl.ds(0, tm)], buf, sem).wait()
    meta = meta_ref[...]
    o_ref[...] = (h_ref[...] + meta[:, META_P1:META_P1 + 1] * buf1[...]
                  + meta[:, META_P2:META_P2 + 1] * buf2[...])


def _moe(h, g, router_w, router_b, w1, w3, w2):
    n = h.shape[0]
    tm = min(TM_FFN, n)
    tg = min(TM_GROUP, n)
    nt = n // tm
    rw = jnp.pad(router_w, ((0, 0), (0, LANES - N_EXPERTS)))
    rb = jnp.pad(router_b.reshape(1, -1), ((0, 0), (0, LANES - N_EXPERTS)))
    tok = lambda width: pl.BlockSpec((tm, width), lambda i: (i, 0))
    hn, meta, cnt = pl.pallas_call(
        _router_kernel,
        grid=(nt,),
        in_specs=[tok(D_MODEL), pl.BlockSpec((1, D_MODEL), lambda i: (0, 0)),
                  pl.BlockSpec((D_MODEL, LANES), lambda i: (0, 0)), pl.BlockSpec((1, LANES), lambda i: (0, 0))],
        out_specs=[tok(D_MODEL), tok(LANES), pl.BlockSpec((1, LANES), lambda i: (0, 0))],
        out_shape=[jax.ShapeDtypeStruct((n, D_MODEL), F32), jax.ShapeDtypeStruct((n, LANES), F32),
                   jax.ShapeDtypeStruct((1, LANES), F32)],
        scratch_shapes=[pltpu.VMEM((1, LANES), F32)],
        compiler_params=_params("arbitrary"),
        name="moe_router",
    )(h, g, rw, rb)

    counts = cnt[0, :N_EXPERTS].astype(jnp.int32)
    padded = ((counts + tg - 1) // tg) * tg
    ends = jnp.cumsum(padded)
    offs = ends - padded
    rows = TOP_K * n + N_EXPERTS * tg
    n_tiles = rows // tg
    tile_start = jnp.arange(n_tiles, dtype=jnp.int32) * tg
    n_used = (ends[-1:] // tg).astype(jnp.int32)
    tile_expert = jnp.sum(tile_start[:, None] >= ends[None, :], axis=1).astype(jnp.int32)
    tile_expert = jnp.minimum(tile_expert, tile_expert[n_used[0] - 1])
    onehot = lambda e: (e[:, None] == jnp.arange(N_EXPERTS, dtype=jnp.int32)[None, :]).astype(jnp.int32)
    dest = lambda e, r: (jnp.sum(onehot(e.astype(jnp.int32)) * offs[None, :], axis=1)
                         + r.astype(jnp.int32)).reshape(nt, 1, tm)
    d1 = dest(meta[:, META_E1], meta[:, META_R1])
    d2 = dest(meta[:, META_E2], meta[:, META_R2])
    dspec = pl.BlockSpec((1, 1, tm), lambda i: (i, 0, 0), memory_space=pltpu.SMEM)

    xs = pl.pallas_call(
        _dispatch_kernel,
        grid=(nt,),
        in_specs=[dspec, dspec, tok(D_MODEL), pl.BlockSpec(memory_space=pl.ANY)],
        out_specs=pl.BlockSpec(memory_space=pl.ANY),
        out_shape=jax.ShapeDtypeStruct((rows, D_MODEL), F32),
        scratch_shapes=[pltpu.SemaphoreType.DMA(())],
        input_output_aliases={3: 0},
        compiler_params=_params("arbitrary"),
        name="moe_dispatch",
    )(d1, d2, hn, jnp.zeros((rows, D_MODEL), F32))

    nf = D_FF // TF_FFN
    fidx = lambda i, f, nu: jnp.where(i < nu[0], f, nf - 1)
    ys = pl.pallas_call(
        _grouped_kernel,
        grid_spec=pltpu.PrefetchScalarGridSpec(
            num_scalar_prefetch=2,
            grid=(n_tiles, D_FF // TF_FFN),
            in_specs=[pl.BlockSpec((tg, D_MODEL), lambda i, f, te, nu: (jnp.minimum(i, nu[0] - 1), 0)),
                      pl.BlockSpec((1, D_MODEL, TF_FFN), lambda i, f, te, nu: (te[i], 0, fidx(i, f, nu))),
                      pl.BlockSpec((1, D_MODEL, TF_FFN), lambda i, f, te, nu: (te[i], 0, fidx(i, f, nu))),
                      pl.BlockSpec((1, TF_FFN, D_MODEL), lambda i, f, te, nu: (te[i], fidx(i, f, nu), 0))],
            out_specs=pl.BlockSpec((tg, D_MODEL), lambda i, f, te, nu: (i, 0)),
            scratch_shapes=[pltpu.VMEM((tg, D_MODEL), F32)]),
        out_shape=jax.ShapeDtypeStruct((rows, D_MODEL), F32),
        compiler_params=_params("arbitrary", "arbitrary"),
        name="moe_grouped",
    )(tile_expert, n_used, xs, w1, w3, w2)

    return pl.pallas_call(
        _combine_kernel,
        grid=(nt,),
        in_specs=[dspec, dspec, tok(D_MODEL), tok(LANES), pl.BlockSpec(memory_space=pl.ANY)],
        out_specs=tok(D_MODEL),
        out_shape=jax.ShapeDtypeStruct((n, D_MODEL), F32),
        scratch_shapes=[pltpu.VMEM((tm, D_MODEL), F32), pltpu.VMEM((tm, D_MODEL), F32),
                        pltpu.SemaphoreType.DMA(())],
        compiler_params=_params("arbitrary"),
        name="moe_combine",
    )(d1, d2, h, meta, ys)


def _ple_kernel(h_ref, p_ref, g_ref, wg_ref, wp_ref, gf_ref, o_ref, *, final):
    h = h_ref[...]
    gate = _sigmoid(jnp.dot(_rms(h, g_ref[...]).astype(BF16), wg_ref[...], preferred_element_type=F32))
    out = h + jnp.dot(p_ref[...].astype(BF16), wp_ref[...], preferred_element_type=F32) * gate
    if final:
        out = _rms(out, gf_ref[...])
    o_ref[...] = out


def _ple(h, p, g, wg, wp, g_final, final):
    n = h.shape[0]
    tm = min(TM_PROJ, n)
    return pl.pallas_call(
        functools.partial(_ple_kernel, final=final),
        grid=(n // tm,),
        in_specs=[pl.BlockSpec((tm, D_MODEL), lambda i: (i, 0)),
                  pl.BlockSpec((tm, PLE_DIM), lambda i: (i, 0)),
                  pl.BlockSpec((1, D_MODEL), lambda i: (0, 0)),
                  pl.BlockSpec((D_MODEL, D_MODEL), lambda i: (0, 0)),
                  pl.BlockSpec((PLE_DIM, D_MODEL), lambda i: (0, 0)),
                  pl.BlockSpec((1, D_MODEL), lambda i: (0, 0))],
        out_specs=pl.BlockSpec((tm, D_MODEL), lambda i: (i, 0)),
        out_shape=jax.ShapeDtypeStruct((n, D_MODEL), F32),
        compiler_params=_params("parallel"),
        name="ple",
    )(h, p, g, wg, wp, g_final)


def _q_perm():
    idx = []
    for j in range(Q_PER_GROUP):
        for g in range(KV_GROUPS):
            base = (g * Q_PER_GROUP + j) * HEAD_DIM
            idx.extend(range(base, base + HEAD_DIM))
    return jnp.asarray(idx, dtype=jnp.int32)


def _layout_w_in(w):
    a1 = 2 * A_WIDTH
    b1 = a1 + 4 * B_WIDTH
    q1 = b1 + C_WIDTH
    w = jnp.concatenate([w[:, a1:b1], w[:, :a1], w[:, b1:q1][:, _q_perm()], w[:, q1:]], axis=1)
    return jnp.pad(w, ((0, 0), (0, Z_W - w.shape[1]))).astype(BF16)


def _layout_w_out(w):
    c0 = A_WIDTH + B_WIDTH
    return jnp.concatenate([w[:c0], w[c0:][_q_perm()]], axis=0).astype(BF16)


def _mix_layer(h, bn, s, tabs, g_mix, w_in, w_out, gm_ln_g, gm_ln_b, gm_ws, gm_bs,
               rw_mu, rw_w0, rw_w_up, rw_a0, rw_a_up, rw_g_up, rw_k_k, rw_k_a, rw_r_k, rw_gn_g, rw_gn_b,
               nsa_cmp_pos, nsa_kc_w1, nsa_kc_w2, nsa_vc_w1, nsa_vc_w2):
    n = bn * s
    z = _proj_in(h, g_mix.reshape(1, -1), _layout_w_in(w_in))
    z3 = z.reshape(bn, s, Z_W)
    y_a = _gmlp(z3, gm_ln_g, gm_ln_b, gm_ws, gm_bs)
    y_b = _rwkv(z3, rw_mu, rw_w0, rw_w_up, rw_a0, rw_a_up, rw_g_up, rw_k_k, rw_k_a, rw_r_k, rw_gn_g, rw_gn_b)
    q_raw, q_rot, kc, vc, ks, vs, kw, vw, gates = _nsa_prep(z3, tabs)
    k_cmp, v_cmp = _nsa_compress(kc, vc, nsa_cmp_pos, nsa_kc_w1, nsa_kc_w2, nsa_vc_w1, nsa_vc_w2)
    y_c = _nsa_attn(q_raw, q_rot, k_cmp, v_cmp, ks, vs, kw, vw, gates)
    return _out_proj(h, y_a.reshape(n, -1), y_b.reshape(n, -1), y_c.reshape(n, -1), _layout_w_out(w_out))


def kernel(x, p, positions, g_mix, w_in, w_out, gm_ln_g, gm_ln_b, gm_ws, gm_bs, rw_mu, rw_w0, rw_w_up, rw_a0,
           rw_a_up, rw_g_up, rw_k_k, rw_k_a, rw_r_k, rw_gn_g, rw_gn_b, nsa_cmp_pos, nsa_kc_w1, nsa_kc_w2,
           nsa_vc_w1, nsa_vc_w2, g_ffn, ffn_w1, ffn_w3, ffn_w2, router_w, router_b, moe_w1, moe_w3, moe_w2,
           g_ple, ple_gate_w, ple_proj_w, g_final):
    bn, s, _ = x.shape
    n = bn * s
    depth = g_mix.shape[0]
    tabs = _rope_tables(positions)
    h = x.reshape(n, D_MODEL)
    for i in range(depth):
        h = _mix_layer(h, bn, s, tabs, g_mix[i], w_in[i], w_out[i], gm_ln_g[i], gm_ln_b[i], gm_ws[i], gm_bs[i],
                       rw_mu[i], rw_w0[i], rw_w_up[i], rw_a0[i], rw_a_up[i], rw_g_up[i], rw_k_k[i], rw_k_a[i],
                       rw_r_k[i], rw_gn_g[i], rw_gn_b[i], nsa_cmp_pos[i], nsa_kc_w1[i], nsa_kc_w2[i],
                       nsa_vc_w1[i], nsa_vc_w2[i])
        j = i // 2
        if i % 2 == 0:
            h = _ffn(h, g_ffn[i].reshape(1, -1), ffn_w1[j:j + 1].astype(BF16), ffn_w3[j:j + 1].astype(BF16),
                     ffn_w2[j:j + 1].astype(BF16))
        else:
            h = _moe(h, g_ffn[i].reshape(1, -1), router_w[j], router_b[j], moe_w1[j].astype(BF16),
                     moe_w3[j].astype(BF16), moe_w2[j].astype(BF16))
        h = _ple(h, p[i].reshape(n, PLE_DIM), g_ple[i].reshape(1, -1), ple_gate_w[i].astype(BF16),
                 ple_proj_w[i].astype(BF16), g_final.reshape(1, -1), final=(i == depth - 1))
    return h.reshape(bn, s, D_MODEL)
```

```python
import functools
import math

import jax
import jax.numpy as jnp
from jax import lax
from jax.experimental import pallas as pl
from jax.experimental.pallas import tpu as pltpu

F32 = jnp.float32
BF16 = jnp.bfloat16

D_MODEL = 1024
HEAD_DIM = 64
A_HEADS = 4
B_HEADS = 4
A_WIDTH = 256
B_WIDTH = 256
C_WIDTH = 512
CHUNK = 128
LN_EPS = 1e-5
GN_EPS = 64e-5
RMS_EPS = 1e-6
RW_LORA = 64
KV_GROUPS = 2
Q_PER_GROUP = 4
CMP_BLOCK = 32
CMP_STRIDE = 16
CMP_HIDDEN = 128
SLC_BLOCK = 64
SLC_TOPK = 16
WINDOW = 512
Q_BLOCK = 128
NEG_INF = -1e30
FORCE = 1e4
ROPE_THETA = 500000.0
ROPE_DIM = 16
D_FF = 2816
N_EXPERTS = 8
TOP_K = 2
PLE_DIM = 256

LANES = 128
Z_B = 0
Z_A = 1024
Z_Q = 1536
Z_KV = 2048
Z_G = 2816
Z_W = 2944
N_GATE = 24

VMEM_LIMIT = 56 * 1024 * 1024
TM_PROJ = 512
TM_FFN = 512
TM_GROUP = 256
RW_T = 512
RW_C = 64
TK_SEL = 512

_NN = (((1,), (0,)), ((), ()))
_NT = (((1,), (1,)), ((), ()))


def _params(*sem):
    return pltpu.CompilerParams(dimension_semantics=sem, vmem_limit_bytes=VMEM_LIMIT)


def _dot(a, b, dn=_NN, passes=1):
    if passes == 6:
        return lax.dot_general(a.astype(F32), b.astype(F32), dn, precision=lax.Precision.HIGHEST,
                               preferred_element_type=F32)
    a_hi = a.astype(BF16)
    b_hi = b.astype(BF16)
    out = lax.dot_general(a_hi, b_hi, dn, preferred_element_type=F32)
    if passes == 1:
        return out
    a_lo = (a - a_hi.astype(F32)).astype(BF16)
    out = out + lax.dot_general(a_lo, b_hi, dn, preferred_element_type=F32)
    if passes == 2:
        return out
    b_lo = (b - b_hi.astype(F32)).astype(BF16)
    return out + lax.dot_general(a_hi, b_lo, dn, preferred_element_type=F32)


def _rms(x, g):
    return x * lax.rsqrt(jnp.mean(x * x, axis=-1, keepdims=True) + RMS_EPS) * g


def _sigmoid(x):
    return 1.0 / (1.0 + jnp.exp(-x))


def _div(x, d):
    return lax.shift_right_logical(x, jnp.int32(int(math.log2(d))))


def _head_block_ones(n):
    r = lax.broadcasted_iota(jnp.int32, (n, n), 0)
    c = lax.broadcasted_iota(jnp.int32, (n, n), 1)
    return jnp.where(_div(r, HEAD_DIM) == _div(c, HEAD_DIM), 1.0, 0.0).astype(F32)


def _rope_kernel(inv_ref, pos_ref, cos_ref, sin_ref):
    p = pos_ref[0].astype(F32)
    for j in range(ROPE_DIM // 2):
        ang = p * inv_ref[j]
        cos_ref[0, j] = jnp.cos(ang)
        sin_ref[0, j] = jnp.sin(ang)


def _rope_tables(positions):
    bn, s = positions.shape
    half = ROPE_DIM // 2
    inv = 1.0 / (ROPE_THETA ** (jnp.arange(0, ROPE_DIM, 2, dtype=F32) / ROPE_DIM))
    pos3 = positions.reshape(bn, s // LANES, LANES)
    cos, sin = pl.pallas_call(
        _rope_kernel,
        grid=(bn,),
        in_specs=[pl.BlockSpec(memory_space=pltpu.SMEM),
                  pl.BlockSpec((1, s // LANES, LANES), lambda b: (b, 0, 0))],
        out_specs=[pl.BlockSpec((1, half, s // LANES, LANES), lambda b: (b, 0, 0, 0))] * 2,
        out_shape=[jax.ShapeDtypeStruct((bn, half, s // LANES, LANES), F32)] * 2,
        compiler_params=_params("parallel"),
        name="rope_tables",
    )(inv, pos3)
    cos = cos.reshape(bn, half, s).transpose(0, 2, 1)
    sin = sin.reshape(bn, half, s).transpose(0, 2, 1)
    one = jnp.ones((bn, s, HEAD_DIM - ROPE_DIM), F32)
    zero = jnp.zeros((bn, s, HEAD_DIM - half), F32)
    tab_c = jnp.concatenate([cos, cos, one], axis=-1)
    tab_lo = jnp.concatenate([sin, zero], axis=-1)
    tab_hi = jnp.concatenate([zero[..., :half], sin, zero[..., :HEAD_DIM - ROPE_DIM]], axis=-1)
    rep = lambda t: jnp.tile(t, (1, 1, LANES // HEAD_DIM))
    return rep(tab_c), rep(tab_lo), rep(tab_hi)


def _proj_in_kernel(h_ref, g_ref, w_ref, z_ref):
    y = _rms(h_ref[...], g_ref[...])
    z_ref[...] = jnp.dot(y.astype(BF16), w_ref[...], preferred_element_type=F32)


def _proj_in(h, g, w):
    n = h.shape[0]
    tm = min(TM_PROJ, n)
    return pl.pallas_call(
        _proj_in_kernel,
        grid=(n // tm,),
        in_specs=[pl.BlockSpec((tm, D_MODEL), lambda i: (i, 0)),
                  pl.BlockSpec((1, D_MODEL), lambda i: (0, 0)),
                  pl.BlockSpec((D_MODEL, Z_W), lambda i: (0, 0))],
        out_specs=pl.BlockSpec((tm, Z_W), lambda i: (i, 0)),
        out_shape=jax.ShapeDtypeStruct((n, Z_W), F32),
        compiler_params=_params("parallel"),
        name="proj_in",
    )(h, g, w)


def _gmlp_kernel(z_ref, lng_ref, lnb_ref, ws_ref, bias_ref, o_ref):
    gz = jax.nn.gelu(z_ref[0])
    u = gz[:, :A_WIDTH]
    v = gz[:, A_WIDTH:]
    ones = _head_block_ones(A_WIDTH)
    mu = _dot(v, ones, passes=2) * (1.0 / HEAD_DIM)
    d = v - mu
    var = _dot(d * d, ones, passes=2) * (1.0 / HEAD_DIM)
    vn = d * lax.rsqrt(var + LN_EPS) * lng_ref[...] + lnb_ref[...]
    r = lax.broadcasted_iota(jnp.int32, (CHUNK, CHUNK), 0)
    c = lax.broadcasted_iota(jnp.int32, (CHUNK, CHUNK), 1)
    lane_head = _div(lax.broadcasted_iota(jnp.int32, (CHUNK, A_WIDTH), 1), HEAD_DIM)
    mixed = bias_ref[...]
    for hd in range(A_HEADS):
        w = jnp.where(c <= r, ws_ref[hd], 0.0)
        vh = jnp.where(lane_head == hd, vn, 0.0)
        mixed = mixed + _dot(w, vh)
    o_ref[0] = (u * mixed).astype(o_ref.dtype)


def _gmlp(z3, ln_g, ln_b, w_s, b_s):
    bn, s, _ = z3.shape
    bias = jnp.repeat(b_s.T, HEAD_DIM, axis=1)
    return pl.pallas_call(
        _gmlp_kernel,
        grid=(bn, s // CHUNK),
        in_specs=[pl.BlockSpec((1, CHUNK, 2 * A_WIDTH), lambda b, i: (b, i, Z_A // (2 * A_WIDTH))),
                  pl.BlockSpec((1, A_WIDTH), lambda b, i: (0, 0)),
                  pl.BlockSpec((1, A_WIDTH), lambda b, i: (0, 0)),
                  pl.BlockSpec((A_HEADS, CHUNK, CHUNK), lambda b, i: (0, 0, 0)),
                  pl.BlockSpec((CHUNK, A_WIDTH), lambda b, i: (0, 0))],
        out_specs=pl.BlockSpec((1, CHUNK, A_WIDTH), lambda b, i: (b, i, 0)),
        out_shape=jax.ShapeDtypeStruct((bn, s, A_WIDTH), BF16),
        compiler_params=_params("parallel", "parallel"),
        name="gmlp",
    )(z3, ln_g.reshape(1, A_WIDTH), ln_b.reshape(1, A_WIDTH), w_s, bias)


RW_PASSES = 1


def _rwkv_chunks(r, lw, k, v, kk, a, s0, c_len):
    nh = B_HEADS
    m = nh * c_len
    chunks = range(r.shape[0] // c_len)
    dot = functools.partial(_dot, passes=RW_PASSES)
    ti = lax.broadcasted_iota(jnp.int32, (c_len, c_len), 0)
    tj = lax.broadcasted_iota(jnp.int32, (c_len, c_len), 1)
    tril = jnp.where(tj <= ti, 1.0, 0.0).astype(F32)
    row_head = _div(lax.broadcasted_iota(jnp.int32, (m, B_WIDTH), 0), c_len)
    lane_head = _div(lax.broadcasted_iota(jnp.int32, (m, B_WIDTH), 1), HEAD_DIM)
    head_mask = row_head == lane_head
    rep = lambda x: jnp.concatenate([x] * nh, axis=0)
    stack = lambda x: jnp.where(head_mask, rep(x), 0.0)
    ri = lax.broadcasted_iota(jnp.int32, (m, m), 0)
    ci = lax.broadcasted_iota(jnp.int32, (m, m), 1)
    same = _div(ri, c_len) == _div(ci, c_len)
    strict = same & (ci < ri)
    incl = same & (ci <= ri)
    eye = ri == ci
    assert 2 * c_len == LANES
    low = lax.broadcasted_iota(jnp.int32, (2 * m, LANES), 1) < c_len

    a_st, r_st, v_st, bw_st, kw_st, w_c, l_ab, l_ak, l_rb, l_rk = ([] for _ in range(10))
    for c in chunks:
        sl = slice(c * c_len, (c + 1) * c_len)
        cum = _dot(tril, lw[sl], passes=3)
        w_t = jnp.exp(cum)
        w_i = jnp.exp(-cum)
        a_t = -kk[sl] * jnp.exp(cum - lw[sl])
        b_t = kk[sl] * a[sl] * w_i
        k_t = k[sl] * w_i
        w_c.append(w_t[c_len - 1:c_len, :])
        a_st.append(stack(a_t))
        r_st.append(stack(r[sl] * w_t))
        v_st.append(stack(v[sl]))
        bw_st.append(stack(b_t * w_c[c]))
        kw_st.append(stack(k_t * w_c[c]))
        cross = dot(jnp.concatenate([a_st[c], r_st[c]], axis=0), jnp.concatenate([b_t, k_t], axis=0), _NT)
        swapped = pltpu.roll(cross, c_len, axis=1)
        vs_b = jnp.concatenate([jnp.where(low, cross, swapped)] * (nh // 2), axis=1)
        vs_k = jnp.concatenate([jnp.where(low, swapped, cross)] * (nh // 2), axis=1)
        l_ab.append(jnp.where(strict, vs_b[:m], 0.0))
        l_ak.append(jnp.where(strict, vs_k[:m], 0.0))
        l_rb.append(jnp.where(incl, vs_b[m:], 0.0))
        l_rk.append(jnp.where(incl, vs_k[m:], 0.0))

    p_inv = [jnp.where(eye, 1.0, 0.0) + l_ab[c] for c in chunks]
    pw = l_ab
    for _ in range(int(math.log2(c_len)) - 1):
        pw = [dot(pw[c], pw[c]) for c in chunks]
        p_inv = [p_inv[c] + dot(p_inv[c], pw[c]) for c in chunks]
    t_m = [dot(l_ak[c], v_st[c]) for c in chunks]
    q_m = [dot(p_inv[c], t_m[c]) for c in chunks]
    p_m = [dot(p_inv[c], a_st[c]) for c in chunks]
    g_m = [r_st[c] + dot(l_rb[c], p_m[c]) for c in chunks]
    h_m = [dot(l_rb[c], q_m[c]) + dot(l_rk[c], v_st[c]) for c in chunks]
    m_m = [jnp.where(eye, w_c[c], 0.0) + dot(p_m[c].T, bw_st[c]) for c in chunks]
    n_m = [dot(q_m[c].T, bw_st[c]) + dot(v_st[c].T, kw_st[c]) for c in chunks]

    states = [s0]
    for c in chunks:
        states.append(dot(states[c], m_m[c]) + n_m[c])
    ys = []
    for c in chunks:
        y_st = dot(g_m[c], states[c], _NT) + h_m[c]
        y = y_st[0:c_len]
        for hd in range(1, nh):
            y = y + y_st[hd * c_len:(hd + 1) * c_len]
        ys.append(y)
    return jnp.concatenate(ys, axis=0), states[-1]


def _rwkv_kernel(z_ref, mu_ref, w0_ref, a0_ref, wwa_ref, gup_ref, kk_ref, ka_ref, rk_ref, gng_ref, gnb_ref,
                 o_ref, carry_ref, state_ref, *, c_len):
    @pl.when(pl.program_id(1) == 0)
    def _():
        carry_ref[...] = jnp.zeros_like(carry_ref)
        state_ref[...] = jnp.zeros_like(state_ref)

    zb = z_ref[0]
    t_len = zb.shape[0]
    row = lax.broadcasted_iota(jnp.int32, zb.shape, 0)
    z_prev = jnp.where(row == 0, carry_ref[...], pltpu.roll(zb, 1, axis=0))
    carry_ref[...] = zb[t_len - 1:t_len, :]
    zz = zb + (z_prev - zb) * mu_ref[...]
    r = zz[:, 0:B_WIDTH]
    k = zz[:, B_WIDTH:2 * B_WIDTH]
    v = zz[:, 2 * B_WIDTH:3 * B_WIDTH]
    wa = zz[:, 3 * B_WIDTH:3 * B_WIDTH + 2 * RW_LORA]
    gd = zz[:, 3 * B_WIDTH + 2 * RW_LORA:]
    lane = lax.broadcasted_iota(jnp.int32, wa.shape, 1)
    proj = _dot(jnp.where(lane < RW_LORA, jnp.tanh(wa), wa), wwa_ref[...])
    x = -(w0_ref[...] + proj[:, :B_WIDTH])
    softplus = jnp.maximum(x, 0.0) + jnp.log(1.0 + jnp.exp(-jnp.abs(x)))
    lw = -jnp.exp(-softplus - 0.5)
    a = _sigmoid(a0_ref[...] + proj[:, B_WIDTH:])
    g = _dot(_sigmoid(gd), gup_ref[...])
    ones = _head_block_ones(B_WIDTH)
    kk = k * kk_ref[...]
    kk = kk * lax.rsqrt(jnp.maximum(_dot(kk * kk, ones, passes=2), 1e-24))
    k2 = k * (1.0 + (a - 1.0) * ka_ref[...])

    y, state_ref[...] = _rwkv_chunks(r, lw, k2, v, kk, a, state_ref[...], c_len)

    mu_y = _dot(y, ones, passes=2) * (1.0 / HEAD_DIM)
    d = y - mu_y
    var = _dot(d * d, ones, passes=2) * (1.0 / HEAD_DIM)
    yn = d * lax.rsqrt(var + GN_EPS) * gng_ref[...] + gnb_ref[...]
    bonus = _dot(r * k2 * rk_ref[...], ones, passes=2) * v
    o_ref[0] = ((yn + bonus) * g).astype(o_ref.dtype)


def _rwkv(z3, mu, w0, w_up, a0, a_up, g_up, k_k, k_a, r_k, gn_g, gn_b):
    bn, s, _ = z3.shape
    t_len = min(RW_T, s)
    zero = jnp.zeros((RW_LORA, B_WIDTH), F32)
    wwa = jnp.concatenate([jnp.concatenate([w_up, zero], axis=1),
                           jnp.concatenate([zero, a_up], axis=1)], axis=0).astype(BF16)
    row = lambda t: t.reshape(1, -1)
    vec = pl.BlockSpec((1, B_WIDTH), lambda b, i: (0, 0))
    return pl.pallas_call(
        functools.partial(_rwkv_kernel, c_len=RW_C),
        grid=(bn, s // t_len),
        in_specs=[pl.BlockSpec((1, t_len, 4 * B_WIDTH), lambda b, i: (b, i, Z_B // (4 * B_WIDTH))),
                  pl.BlockSpec((1, 4 * B_WIDTH), lambda b, i: (0, 0)),
                  vec, vec,
                  pl.BlockSpec((2 * RW_LORA, 2 * B_WIDTH), lambda b, i: (0, 0)),
                  pl.BlockSpec((2 * RW_LORA, B_WIDTH), lambda b, i: (0, 0)),
                  vec, vec, vec, vec, vec],
        out_specs=pl.BlockSpec((1, t_len, B_WIDTH), lambda b, i: (b, i, 0)),
        out_shape=jax.ShapeDtypeStruct((bn, s, B_WIDTH), BF16),
        scratch_shapes=[pltpu.VMEM((1, 4 * B_WIDTH), F32), pltpu.VMEM((B_WIDTH, B_WIDTH), F32)],
        compiler_params=_params("parallel", "arbitrary"),
        name="rwkv7",
    )(z3, row(mu), row(w0), row(a0), wwa, g_up.astype(BF16), row(k_k), row(k_a), row(r_k), row(gn_g), row(gn_b))


def _rope(x, tab_c, tab_lo, tab_hi):
    n = x.shape[-1]
    half = ROPE_DIM // 2
    return x * tab_c - pltpu.roll(x, n - half, axis=1) * tab_lo + pltpu.roll(x, half, axis=1) * tab_hi


N_HEADS_C = KV_GROUPS * Q_PER_GROUP
QT_COLS = N_HEADS_C * Q_BLOCK
GATE_ROWS = 32
LOG2E = 1.4426950408889634


def _nsa_prep_kernel(zq_ref, zkc_ref, zks_ref, zkw_ref, zg_ref, tc_ref, tl_ref, th_ref,
                     qraw_ref, qrot_ref, kc_ref, vc_ref, ks_ref, vst_ref, kw_ref, vwt_ref, gate_ref):
    tab_c, tab_lo, tab_hi = tc_ref[0], tl_ref[0], th_ref[0]
    nrep = C_WIDTH // LANES
    wide = lambda t: jnp.concatenate([t] * nrep, axis=1)
    q = zq_ref[0] * (HEAD_DIM ** -0.5 * LOG2E)
    q_rot = _rope(q, wide(tab_c), wide(tab_lo), wide(tab_hi))
    ts = q.shape[0]
    row_group0 = lax.broadcasted_iota(jnp.int32, (LANES, ts), 0) < HEAD_DIM

    def put_queries(ref, x):
        for j in range(Q_PER_GROUP):
            xt = x[:, j * LANES:(j + 1) * LANES].T
            for g in range(KV_GROUPS):
                keep = row_group0 if g == 0 else jnp.logical_not(row_group0)
                xm = jnp.where(keep, xt, 0.0).astype(BF16)
                for qq in range(ts // Q_BLOCK):
                    col = qq * QT_COLS + (g * Q_PER_GROUP + j) * Q_BLOCK
                    ref[0, :, col:col + Q_BLOCK] = xm[:, qq * Q_BLOCK:(qq + 1) * Q_BLOCK]

    put_queries(qraw_ref, q)
    put_queries(qrot_ref, q_rot)
    kc_ref[0] = zkc_ref[0][:, :LANES].astype(BF16)
    vc_ref[0] = zkc_ref[0][:, LANES:].astype(BF16)
    ks_ref[0, :, :LANES] = _rope(zks_ref[0][:, :LANES], tab_c, tab_lo, tab_hi).astype(BF16)
    key_block = _div(pl.program_id(1) * ts + lax.broadcasted_iota(jnp.int32, (ts, LANES), 0), SLC_BLOCK)
    ks_ref[0, :, LANES:] = jnp.where(key_block == lax.broadcasted_iota(jnp.int32, (ts, LANES), 1), 1.0, 0.0).astype(BF16)
    vst_ref[0, 0] = zks_ref[0][:, LANES:].T.astype(BF16)
    kw_ref[0] = _rope(zkw_ref[0][:, :LANES], tab_c, tab_lo, tab_hi).astype(BF16)
    vwt = zkw_ref[0][:, LANES:].T.astype(BF16)
    for qq in range(ts // Q_BLOCK):
        vwt_ref[0, qq] = vwt[:, qq * Q_BLOCK:(qq + 1) * Q_BLOCK]
    gate_ref[0] = _sigmoid(zg_ref[0]).T[:GATE_ROWS, :]


def _nsa_prep(z3, tabs):
    bn, s, _ = z3.shape
    ts = TK_SEL
    zspec = lambda width, off: pl.BlockSpec((1, ts, width), lambda b, i: (b, i, off // width))
    tspec = pl.BlockSpec((1, ts, LANES), lambda b, i: (b, i, 0))
    rowmajor = pl.BlockSpec((1, ts, LANES), lambda b, i: (b, i, 0))
    qspec = pl.BlockSpec((1, LANES, N_HEADS_C * ts), lambda b, i: (b, 0, i))
    rm_sds = jax.ShapeDtypeStruct((bn, s, LANES), BF16)
    q_sds = jax.ShapeDtypeStruct((bn, LANES, N_HEADS_C * s), BF16)
    return pl.pallas_call(
        _nsa_prep_kernel,
        grid=(bn, s // ts),
        in_specs=[zspec(C_WIDTH, Z_Q), zspec(2 * LANES, Z_KV), zspec(2 * LANES, Z_KV + 2 * LANES),
                  zspec(2 * LANES, Z_KV + 4 * LANES), zspec(LANES, Z_G), tspec, tspec, tspec],
        out_specs=[qspec, qspec, rowmajor, rowmajor, pl.BlockSpec((1, ts, 2 * LANES), lambda b, i: (b, i, 0)),
                   pl.BlockSpec((1, 1, LANES, ts), lambda b, i: (b, i, 0, 0)), rowmajor,
                   pl.BlockSpec((1, ts // Q_BLOCK, LANES, Q_BLOCK), lambda b, i: (b, i, 0, 0)),
                   pl.BlockSpec((1, GATE_ROWS, ts), lambda b, i: (b, 0, i))],
        out_shape=[q_sds, q_sds, rm_sds, rm_sds, jax.ShapeDtypeStruct((bn, s, 2 * LANES), BF16),
                   jax.ShapeDtypeStruct((bn, s // ts, LANES, ts), BF16), rm_sds,
                   jax.ShapeDtypeStruct((bn, s // Q_BLOCK, LANES, Q_BLOCK), BF16),
                   jax.ShapeDtypeStruct((bn, GATE_ROWS, s), F32)],
        compiler_params=_params("parallel", "parallel"),
        name="nsa_prep",
    )(z3, z3, z3, z3, z3, *tabs)


def _nsa_compress_kernel(xk_ref, xv_ref, pos_ref, kw1a_ref, kw1b_ref, kw2_ref, vw1a_ref, vw1b_ref, vw2_ref,
                         ko_ref, vo_ref):
    n = xk_ref.shape[1]
    pos_a = pos_ref[0]
    pos_b = pos_ref[1]
    for x_ref, w1a, w1b, w2, o_ref, transposed in ((xk_ref, kw1a_ref, kw1b_ref, kw2_ref, ko_ref, False),
                                                    (xv_ref, vw1a_ref, vw1b_ref, vw2_ref, vo_ref, True)):
        x = x_ref[0]
        first = _dot(x, w1a[...])
        second = _dot(x, w1b[...])
        pc = _dot(pos_a, w1a[...]) + _dot(pos_b, w1b[...])
        hid = jax.nn.gelu(first + pltpu.roll(second, n - 1, axis=0) + pc[0:1, :])
        out = _dot(hid, w2[...])
        o_ref[0] = (out.T if transposed else out).astype(o_ref.dtype)


def _expand_cmp_weights(w1, w2):
    half = CMP_BLOCK // 2
    w1r = w1.reshape(CMP_BLOCK, HEAD_DIM, CMP_HIDDEN)
    eye = jnp.eye(KV_GROUPS, dtype=F32)
    w1e = jnp.einsum('ldh,gk->lgdkh', w1r, eye).reshape(CMP_BLOCK * LANES, KV_GROUPS * CMP_HIDDEN)
    w2e = jnp.einsum('hd,gk->ghkd', w2, eye).reshape(KV_GROUPS * CMP_HIDDEN, LANES)
    return (w1e[:half * LANES].astype(BF16), w1e[half * LANES:].astype(BF16), w2e.astype(BF16))


def _nsa_compress(kc, vc, cmp_pos, kc_w1, kc_w2, vc_w1, vc_w2):
    bn, s, _ = kc.shape
    n = s // CMP_STRIDE
    xw = CMP_STRIDE * LANES
    xk = kc.reshape(bn, n, xw)
    xv = vc.reshape(bn, n, xw)
    pos = jnp.tile(cmp_pos[:, None, :], (1, KV_GROUPS, 1)).reshape(2, 1, xw)
    pos = jnp.broadcast_to(pos, (2, 8, xw)).astype(BF16)
    kw = _expand_cmp_weights(kc_w1, kc_w2)
    vw = _expand_cmp_weights(vc_w1, vc_w2)
    full = lambda a: pl.BlockSpec(a.shape, lambda b: (0,) * a.ndim)
    xspec = pl.BlockSpec((1, n, xw), lambda b: (b, 0, 0))
    return pl.pallas_call(
        _nsa_compress_kernel,
        grid=(bn,),
        in_specs=[xspec, xspec, full(pos)] + [full(a) for a in kw + vw],
        out_specs=[pl.BlockSpec((1, n, LANES), lambda b: (b, 0, 0)), pl.BlockSpec((1, LANES, n), lambda b: (b, 0, 0))],
        out_shape=[jax.ShapeDtypeStruct((bn, n, LANES), BF16), jax.ShapeDtypeStruct((bn, LANES, n), BF16)],
        compiler_params=_params("parallel"),
        name="nsa_compress",
    )(xk, xv, pos, *kw, *vw)


def _nsa_attn_kernel(qraw_ref, qrot_ref, kcmp_ref, vcmpt_ref, ks_ref, vst_ref, kw_ref, vwt_ref, gate_ref, o_ref,
                     m_ref, l_ref, acc_ref, sa_ref, sb_ref, qa_ref):
    tq = Q_BLOCK
    gw = Q_PER_GROUP * tq
    seq = ks_ref.shape[1]
    n_cmp = kcmp_ref.shape[1]
    n_slc = seq // SLC_BLOCK
    n_sel = min(SLC_TOPK, n_slc)
    qb = pl.program_id(1)
    t0 = qb * tq
    gcols = lambda g: slice(g * gw, (g + 1) * gw)
    grows = lambda g: slice(g * HEAD_DIM, (g + 1) * HEAD_DIM)
    jcols = lambda j: slice(j * tq, (j + 1) * tq)

    t_c = t0 + lax.broadcasted_iota(jnp.int32, (n_cmp, tq), 1)
    n_c = lax.broadcasted_iota(jnp.int32, (n_cmp, tq), 0)
    cmask = (n_c * CMP_STRIDE + (CMP_BLOCK - 1)) <= t_c
    cbias = jnp.where(cmask, 0.0, NEG_INF)
    o_c, p_sum = [], []
    for g in range(KV_GROUPS):
        s = _dot(kcmp_ref[0], qraw_ref[0, :, gcols(g)])
        parts, total = [], None
        for j in range(Q_PER_GROUP):
            sj = s[:, jcols(j)] + cbias
            e = jnp.exp2(sj - jnp.max(sj, axis=0, keepdims=True))
            p = jnp.where(cmask, e * (1.0 / jnp.sum(e, axis=0, keepdims=True)), 0.0)
            total = p if total is None else total + p
            parts.append(p.astype(BF16))
        p_sum.append(total)
        o_c.append(_dot(vcmpt_ref[0, grows(g), :], jnp.concatenate(parts, axis=1)))

    n_band = WINDOW // tq + 1
    band = n_band * tq
    wb = jnp.maximum(qb - WINDOW // tq, 0)
    w0 = pl.multiple_of(wb * tq, tq)
    t_w = t0 + lax.broadcasted_iota(jnp.int32, (band, tq), 1)
    k_w = w0 + lax.broadcasted_iota(jnp.int32, (band, tq), 0)
    wbias = jnp.where((k_w <= t_w) & (k_w > t_w - WINDOW), 0.0, NEG_INF)
    kband = kw_ref[0, pl.ds(w0, band), :]
    o_w = []
    for g in range(KV_GROUPS):
        s = _dot(kband, qrot_ref[0, :, gcols(g)])
        parts, inv = [], []
        for j in range(Q_PER_GROUP):
            sj = s[:, jcols(j)] + wbias
            e = jnp.exp2(sj - jnp.max(sj, axis=0, keepdims=True))
            inv.append(1.0 / jnp.sum(e, axis=0, keepdims=True))
            parts.append(e.astype(BF16))
        vband = jnp.concatenate([vwt_ref[0, wb + i, grows(g), :] for i in range(n_band)], axis=1)
        o_w.append(_dot(vband, jnp.concatenate(parts, axis=1)) * jnp.concatenate(inv, axis=1))

    m_o = lax.broadcasted_iota(jnp.int32, (n_slc, n_cmp), 0) * SLC_BLOCK
    n_o = lax.broadcasted_iota(jnp.int32, (n_slc, n_cmp), 1) * CMP_STRIDE
    overlap_t = jnp.where((n_o < m_o + SLC_BLOCK) & (n_o + (CMP_BLOCK - 1) >= m_o), 1.0, 0.0).astype(F32)
    m_i = lax.broadcasted_iota(jnp.int32, (n_slc, tq), 0)
    blk_t = _div(t0 + lax.broadcasted_iota(jnp.int32, (n_slc, tq), 1), SLC_BLOCK)
    valid = m_i <= blk_t
    forced = (m_i == 0) | (m_i == blk_t) | (m_i == blk_t - 1)
    sel = []
    for g in range(KV_GROUPS):
        imp = _dot(overlap_t, p_sum[g], passes=3)
        imp = jnp.where(valid, imp + jnp.where(forced, FORCE, 0.0), -FORCE)
        rank = jnp.zeros((n_slc, tq), F32)
        for mp in range(n_slc):
            other = imp[mp:mp + 1, :]
            rank = rank + jnp.where(m_i > mp, jnp.where(other >= imp, 1.0, 0.0), jnp.where(other > imp, 1.0, 0.0))
        sel.append(jnp.where((rank < n_sel) & valid, 0.0, NEG_INF).astype(BF16))

    m_ref[...] = jnp.full(m_ref.shape, NEG_INF, F32)
    l_ref[...] = jnp.zeros(l_ref.shape, F32)
    acc_ref[...] = jnp.zeros(acc_ref.shape, F32)
    n_tiles = (t0 + tq + TK_SEL - 1) // TK_SEL
    pad = jnp.zeros((LANES - n_slc, gw), BF16)
    for g in range(KV_GROUPS):
        qa_ref[g] = jnp.concatenate([qrot_ref[0, :, gcols(g)], jnp.concatenate([sel[g]] * Q_PER_GROUP, axis=1), pad],
                                    axis=0)
    r0 = pl.multiple_of(t0 - (n_tiles - 1) * TK_SEL, tq)
    tri = jnp.where(lax.broadcasted_iota(jnp.int32, (tq, tq), 0) > lax.broadcasted_iota(jnp.int32, (tq, tq), 1),
                    NEG_INF, 0.0)

    def scores(kt, s_ref):
        k0 = pl.multiple_of(jnp.minimum(kt, n_tiles - 1) * TK_SEL, TK_SEL)
        keys = ks_ref[0, pl.ds(k0, TK_SEL), :]
        own = jnp.where(kt >= n_tiles - 1, tri, 0.0)
        for g in range(KV_GROUPS):
            s_ref[g] = _dot(keys, qa_ref[g])
            for j in range(Q_PER_GROUP):
                s_ref[g, pl.ds(r0, tq), jcols(j)] += own

    def attend(kt, s_ref):
        for g in range(KV_GROUPS):
            parts, alphas = [], []
            for j in range(Q_PER_GROUP):
                cs = slice(g * gw + j * tq, g * gw + (j + 1) * tq)
                sj = s_ref[g, :, jcols(j)]
                m_old = m_ref[:, cs]
                m_new = jnp.maximum(m_old, jnp.max(sj, axis=0, keepdims=True))
                alpha = jnp.exp2(m_old - m_new)
                p = jnp.exp2(sj - m_new)
                l_ref[:, cs] = alpha * l_ref[:, cs] + jnp.sum(p, axis=0, keepdims=True)
                m_ref[:, cs] = m_new
                alphas.append(alpha)
                parts.append(p.astype(BF16))
            pv = _dot(vst_ref[0, kt, grows(g), :], jnp.concatenate(parts, axis=1))
            acc_ref[g] = acc_ref[g] * jnp.concatenate(alphas, axis=1) + pv

    odd = lax.rem(n_tiles, 2)
    scores(0, sa_ref)

    @pl.when(odd == 1)
    def _():
        attend(0, sa_ref)
        scores(1, sa_ref)

    def sel_pair(i, carry):
        kt = odd + 2 * i
        scores(kt + 1, sb_ref)
        attend(kt, sa_ref)
        scores(kt + 2, sa_ref)
        attend(kt + 1, sb_ref)
        return carry

    lax.fori_loop(0, (n_tiles - odd) // 2, sel_pair, 0)

    gates = gate_ref[0]
    outs = []
    for j in range(Q_PER_GROUP):
        for g in range(KV_GROUPS):
            row = (g * Q_PER_GROUP + j) * 3
            cs = slice(g * gw + j * tq, g * gw + (j + 1) * tq)
            o_s = acc_ref[g][:, jcols(j)] * (1.0 / l_ref[:, cs])
            outs.append(gates[row:row + 1, :] * o_c[g][:, jcols(j)] + gates[row + 1:row + 2, :] * o_s
                        + gates[row + 2:row + 3, :] * o_w[g][:, jcols(j)])
    o_ref[0] = jnp.concatenate(outs, axis=0).T.astype(o_ref.dtype)


def _nsa_attn(q_raw, q_rot, k_cmp, v_cmp_t, ks, vs_t, kw, vw_t, gates_t):
    bn, s, _ = ks.shape
    n_cmp = k_cmp.shape[1]
    qspec = pl.BlockSpec((1, LANES, QT_COLS), lambda b, i: (b, 0, i))
    kspec = pl.BlockSpec((1, s, LANES), lambda b, i: (b, 0, 0))
    whole = lambda a: pl.BlockSpec((1,) + a.shape[1:], lambda b, i: (b,) + (0,) * (a.ndim - 1))
    return pl.pallas_call(
        _nsa_attn_kernel,
        grid=(bn, s // Q_BLOCK),
        in_specs=[qspec, qspec, whole(k_cmp), whole(v_cmp_t), whole(ks), whole(vs_t), kspec, whole(vw_t),
                  pl.BlockSpec((1, GATE_ROWS, Q_BLOCK), lambda b, i: (b, 0, i))],
        out_specs=pl.BlockSpec((1, Q_BLOCK, C_WIDTH), lambda b, i: (b, i, 0)),
        out_shape=jax.ShapeDtypeStruct((bn, s, C_WIDTH), BF16),
        scratch_shapes=[pltpu.VMEM((1, QT_COLS), F32), pltpu.VMEM((1, QT_COLS), F32),
                        pltpu.VMEM((KV_GROUPS, HEAD_DIM, Q_PER_GROUP * Q_BLOCK), F32),
                        pltpu.VMEM((KV_GROUPS, TK_SEL, Q_PER_GROUP * Q_BLOCK), F32),
                        pltpu.VMEM((KV_GROUPS, TK_SEL, Q_PER_GROUP * Q_BLOCK), F32),
                        pltpu.VMEM((KV_GROUPS, 2 * LANES, Q_PER_GROUP * Q_BLOCK), BF16)],
        compiler_params=_params("parallel", "arbitrary"),
        name="nsa_attn",
    )(q_raw, q_rot, k_cmp, v_cmp_t, ks, vs_t, kw, vw_t, gates_t)


def _out_proj_kernel(h_ref, ya_ref, yb_ref, yc_ref, w_ref, o_ref):
    acc = jnp.dot(ya_ref[...], w_ref[0:A_WIDTH, :], preferred_element_type=F32)
    acc = acc + jnp.dot(yb_ref[...], w_ref[A_WIDTH:A_WIDTH + B_WIDTH, :], preferred_element_type=F32)
    acc = acc + jnp.dot(yc_ref[...], w_ref[A_WIDTH + B_WIDTH:, :], preferred_element_type=F32)
    o_ref[...] = h_ref[...] + acc


def _out_proj(h, ya, yb, yc, w):
    n = h.shape[0]
    tm = min(TM_PROJ, n)
    tok = lambda width: pl.BlockSpec((tm, width), lambda i: (i, 0))
    return pl.pallas_call(
        _out_proj_kernel,
        grid=(n // tm,),
        in_specs=[tok(D_MODEL), tok(A_WIDTH), tok(B_WIDTH), tok(C_WIDTH),
                  pl.BlockSpec((D_MODEL, D_MODEL), lambda i: (0, 0))],
        out_specs=tok(D_MODEL),
        out_shape=jax.ShapeDtypeStruct((n, D_MODEL), F32),
        compiler_params=_params("parallel"),
        name="out_proj",
    )(h, ya, yb, yc, w)


def _swiglu_step(x, w1_ref, w3_ref, w2_ref):
    h1 = jnp.dot(x, w1_ref[0], preferred_element_type=F32)
    h3 = jnp.dot(x, w3_ref[0], preferred_element_type=F32)
    hid = h1 * _sigmoid(h1) * h3
    return jnp.dot(hid.astype(BF16), w2_ref[0], preferred_element_type=F32)


def _ffn_kernel(h_ref, g_ref, w1_ref, w3_ref, w2_ref, o_ref):
    x = _rms(h_ref[...], g_ref[...]).astype(BF16)
    o_ref[...] = h_ref[...] + _swiglu_step(x, w1_ref, w3_ref, w2_ref)


def _ffn(h, g, w1, w3, w2):
    n = h.shape[0]
    tm = min(TM_FFN, n)
    resident = lambda shape: pl.BlockSpec(shape, lambda i: (0, 0, 0), pipeline_mode=pl.Buffered(1))
    return pl.pallas_call(
        _ffn_kernel,
        grid=(n // tm,),
        in_specs=[pl.BlockSpec((tm, D_MODEL), lambda i: (i, 0)),
                  pl.BlockSpec((1, D_MODEL), lambda i: (0, 0)),
                  resident((1, D_MODEL, D_FF)), resident((1, D_MODEL, D_FF)), resident((1, D_FF, D_MODEL))],
        out_specs=pl.BlockSpec((tm, D_MODEL), lambda i: (i, 0)),
        out_shape=jax.ShapeDtypeStruct((n, D_MODEL), F32),
        compiler_params=_params("parallel"),
        name="ffn_swiglu",
    )(h, g, w1, w3, w2)


META_E1, META_E2, META_R1, META_R2, META_P1, META_P2 = range(6)


def _router_kernel(h_ref, g_ref, rw_ref, rb_ref, hn_ref, meta_ref, cnt_ref, carry_ref):
    @pl.when(pl.program_id(0) == 0)
    def _():
        carry_ref[...] = jnp.zeros_like(carry_ref)

    hn = _rms(h_ref[...], g_ref[...])
    hn_ref[...] = hn
    tm = hn.shape[0]
    lane = lax.broadcasted_iota(jnp.int32, (tm, LANES), 1).astype(F32)
    logits = jnp.where(lane < N_EXPERTS, _dot(hn, rw_ref[...], passes=3) + rb_ref[...], NEG_INF)
    top1 = jnp.max(logits, axis=-1, keepdims=True)
    idx1 = jnp.min(jnp.where(logits == top1, lane, float(LANES)), axis=-1, keepdims=True)
    rest = jnp.where(lane == idx1, NEG_INF, logits)
    top2 = jnp.max(rest, axis=-1, keepdims=True)
    idx2 = jnp.min(jnp.where(rest == top2, lane, float(LANES)), axis=-1, keepdims=True)
    ex = jnp.exp(top2 - top1)
    picked = jnp.where((lane == idx1) | (lane == idx2), 1.0, 0.0)
    r = lax.broadcasted_iota(jnp.int32, (tm, tm), 0)
    c = lax.broadcasted_iota(jnp.int32, (tm, tm), 1)
    before = _dot(jnp.where(c < r, 1.0, 0.0), picked) + carry_ref[...]
    rank1 = jnp.sum(jnp.where(lane == idx1, before, 0.0), axis=-1, keepdims=True)
    rank2 = jnp.sum(jnp.where(lane == idx2, before, 0.0), axis=-1, keepdims=True)
    carry_ref[...] += jnp.sum(picked, axis=0, keepdims=True)
    cnt_ref[...] = carry_ref[...]
    meta = jnp.zeros((tm, LANES), F32)
    for pos, val in ((META_E1, idx1), (META_E2, idx2), (META_R1, rank1), (META_R2, rank2),
                     (META_P1, 1.0 / (1.0 + ex)), (META_P2, ex / (1.0 + ex))):
        meta = jnp.where(lane == pos, val, meta)
    meta_ref[...] = meta


def _dispatch_kernel(d1_ref, d2_ref, hn_ref, xs_in_ref, xs_ref, sem):
    del xs_in_ref
    tm = hn_ref.shape[0]

    def row_copy(r, dest):
        return pltpu.make_async_copy(hn_ref.at[pl.ds(r, 1)], xs_ref.at[pl.ds(dest, 1)], sem)

    def issue(r, carry):
        row_copy(r, d1_ref[0, 0, r]).start()
        row_copy(r, d2_ref[0, 0, r]).start()
        return carry

    lax.fori_loop(0, tm, issue, 0, unroll=8)
    for _ in range(2):
        pltpu.make_async_copy(hn_ref, xs_ref.at[pl.ds(0, tm)], sem).wait()


def _grouped_kernel(te_ref, nu_ref, xs_ref, w1_ref, w3_ref, w2_ref, ys_ref):
    del te_ref
    used = pl.program_id(0) < nu_ref[0]

    @pl.when(used)
    def _():
        ys_ref[...] = _swiglu_step(xs_ref[...].astype(BF16), w1_ref, w3_ref, w2_ref)

    @pl.when(jnp.logical_not(used))
    def _():
        ys_ref[...] = jnp.zeros_like(ys_ref)


def _combine_kernel(d1_ref, d2_ref, h_ref, meta_ref, ys_ref, o_ref, buf1, buf2, sem):
    tm = h_ref.shape[0]

    def row_copy(dest, buf, r):
        return pltpu.make_async_copy(ys_ref.at[pl.ds(dest, 1)], buf.at[pl.ds(r, 1)], sem)

    def issue(r, carry):
        row_copy(d1_ref[0, 0, r], buf1, r).start()
        row_copy(d2_ref[0, 0, r], buf2, r).start()
        return carry

    lax.fori_loop(0, tm, issue, 0, unroll=8)
    for buf in (buf1, buf2):
        pltpu.make_async_copy(ys_ref.at[pl.ds(0, tm)], buf, sem).wait()
    meta = meta_ref[...]
    o_ref[...] = (h_ref[...] + meta[:, META_P1:META_P1 + 1] * buf1[...]
                  + meta[:, META_P2:META_P2 + 1] * buf2[...])


def _moe(h, g, router_w, router_b, w1, w3, w2):
    n = h.shape[0]
    tm = min(TM_FFN, n)
    tg = min(TM_GROUP, n)
    nt = n // tm
    rw = jnp.pad(router_w, ((0, 0), (0, LANES - N_EXPERTS)))
    rb = jnp.pad(router_b.reshape(1, -1), ((0, 0), (0, LANES - N_EXPERTS)))
    tok = lambda width: pl.BlockSpec((tm, width), lambda i: (i, 0))
    hn, meta, cnt = pl.pallas_call(
        _router_kernel,
        grid=(nt,),
        in_specs=[tok(D_MODEL), pl.BlockSpec((1, D_MODEL), lambda i: (0, 0)),
                  pl.BlockSpec((D_MODEL, LANES), lambda i: (0, 0)), pl.BlockSpec((1, LANES), lambda i: (0, 0))],
        out_specs=[tok(D_MODEL), tok(LANES), pl.BlockSpec((1, LANES), lambda i: (0, 0))],
        out_shape=[jax.ShapeDtypeStruct((n, D_MODEL), F32), jax.ShapeDtypeStruct((n, LANES), F32),
                   jax.ShapeDtypeStruct((1, LANES), F32)],
        scratch_shapes=[pltpu.VMEM((1, LANES), F32)],
        compiler_params=_params("arbitrary"),
        name="moe_router",
    )(h, g, rw, rb)

    counts = cnt[0, :N_EXPERTS].astype(jnp.int32)
    padded = ((counts + tg - 1) // tg) * tg
    ends = jnp.cumsum(padded)
    offs = ends - padded
    rows = TOP_K * n + N_EXPERTS * tg
    n_tiles = rows // tg
    tile_start = jnp.arange(n_tiles, dtype=jnp.int32) * tg
    n_used = (ends[-1:] // tg).astype(jnp.int32)
    tile_expert = jnp.sum(tile_start[:, None] >= ends[None, :], axis=1).astype(jnp.int32)
    tile_expert = jnp.minimum(tile_expert, tile_expert[n_used[0] - 1])
    onehot = lambda e: (e[:, None] == jnp.arange(N_EXPERTS, dtype=jnp.int32)[None, :]).astype(jnp.int32)
    dest = lambda e, r: (jnp.sum(onehot(e.astype(jnp.int32)) * offs[None, :], axis=1)
                         + r.astype(jnp.int32)).reshape(nt, 1, tm)
    d1 = dest(meta[:, META_E1], meta[:, META_R1])
    d2 = dest(meta[:, META_E2], meta[:, META_R2])
    dspec = pl.BlockSpec((1, 1, tm), lambda i: (i, 0, 0), memory_space=pltpu.SMEM)

    xs = pl.pallas_call(
        _dispatch_kernel,
        grid=(nt,),
        in_specs=[dspec, dspec, tok(D_MODEL), pl.BlockSpec(memory_space=pl.ANY)],
        out_specs=pl.BlockSpec(memory_space=pl.ANY),
        out_shape=jax.ShapeDtypeStruct((rows, D_MODEL), F32),
        scratch_shapes=[pltpu.SemaphoreType.DMA(())],
        input_output_aliases={3: 0},
        compiler_params=_params("arbitrary"),
        name="moe_dispatch",
    )(d1, d2, hn, jnp.zeros((rows, D_MODEL), F32))

    wspec = lambda shape: pl.BlockSpec(shape, lambda i, te, nu: (te[i], 0, 0))
    ys = pl.pallas_call(
        _grouped_kernel,
        grid_spec=pltpu.PrefetchScalarGridSpec(
            num_scalar_prefetch=2,
            grid=(n_tiles,),
            in_specs=[pl.BlockSpec((tg, D_MODEL), lambda i, te, nu: (jnp.minimum(i, nu[0] - 1), 0)),
                      wspec((1, D_MODEL, D_FF)), wspec((1, D_MODEL, D_FF)), wspec((1, D_FF, D_MODEL))],
            out_specs=pl.BlockSpec((tg, D_MODEL), lambda i, te, nu: (i, 0))),
        out_shape=jax.ShapeDtypeStruct((rows, D_MODEL), F32),
        compiler_params=_params("arbitrary"),
        name="moe_grouped",
    )(tile_expert, n_used, xs, w1, w3, w2)

    return pl.pallas_call(
        _combine_kernel,
        grid=(nt,),
        in_specs=[dspec, dspec, tok(D_MODEL), tok(LANES), pl.BlockSpec(memory_space=pl.ANY)],
        out_specs=tok(D_MODEL),
        out_shape=jax.ShapeDtypeStruct((n, D_MODEL), F32),
        scratch_shapes=[pltpu.VMEM((tm, D_MODEL), F32), pltpu.VMEM((tm, D_MODEL), F32),
                        pltpu.SemaphoreType.DMA(())],
        compiler_params=_params("arbitrary"),
        name="moe_combine",
    )(d1, d2, h, meta, ys)


def _ple_kernel(h_ref, p_ref, g_ref, wg_ref, wp_ref, gf_ref, o_ref, *, final):
    h = h_ref[...]
    gate = _sigmoid(jnp.dot(_rms(h, g_ref[...]).astype(BF16), wg_ref[...], preferred_element_type=F32))
    out = h + jnp.dot(p_ref[...].astype(BF16), wp_ref[...], preferred_element_type=F32) * gate
    if final:
        out = _rms(out, gf_ref[...])
    o_ref[...] = out


def _ple(h, p, g, wg, wp, g_final, final):
    n = h.shape[0]
    tm = min(TM_PROJ, n)
    return pl.pallas_call(
        functools.partial(_ple_kernel, final=final),
        grid=(n // tm,),
        in_specs=[pl.BlockSpec((tm, D_MODEL), lambda i: (i, 0)),
                  pl.BlockSpec((tm, PLE_DIM), lambda i: (i, 0)),
                  pl.BlockSpec((1, D_MODEL), lambda i: (0, 0)),
                  pl.BlockSpec((D_MODEL, D_MODEL), lambda i: (0, 0)),
                  pl.BlockSpec((PLE_DIM, D_MODEL), lambda i: (0, 0)),
                  pl.BlockSpec((1, D_MODEL), lambda i: (0, 0))],
        out_specs=pl.BlockSpec((tm, D_MODEL), lambda i: (i, 0)),
        out_shape=jax.ShapeDtypeStruct((n, D_MODEL), F32),
        compiler_params=_params("parallel"),
        name="ple",
    )(h, p, g, wg, wp, g_final)


def _q_perm():
    idx = []
    for j in range(Q_PER_GROUP):
        for g in range(KV_GROUPS):
            base = (g * Q_PER_GROUP + j) * HEAD_DIM
            idx.extend(range(base, base + HEAD_DIM))
    return jnp.asarray(idx, dtype=jnp.int32)


def _layout_w_in(w):
    a1 = 2 * A_WIDTH
    b1 = a1 + 4 * B_WIDTH
    q1 = b1 + C_WIDTH
    w = jnp.concatenate([w[:, a1:b1], w[:, :a1], w[:, b1:q1][:, _q_perm()], w[:, q1:]], axis=1)
    return jnp.pad(w, ((0, 0), (0, Z_W - w.shape[1]))).astype(BF16)


def _layout_w_out(w):
    c0 = A_WIDTH + B_WIDTH
    return jnp.concatenate([w[:c0], w[c0:][_q_perm()]], axis=0).astype(BF16)


def _mix_layer(h, bn, s, tabs, g_mix, w_in, w_out, gm_ln_g, gm_ln_b, gm_ws, gm_bs,
               rw_mu, rw_w0, rw_w_up, rw_a0, rw_a_up, rw_g_up, rw_k_k, rw_k_a, rw_r_k, rw_gn_g, rw_gn_b,
               nsa_cmp_pos, nsa_kc_w1, nsa_kc_w2, nsa_vc_w1, nsa_vc_w2):
    n = bn * s
    z = _proj_in(h, g_mix.reshape(1, -1), _layout_w_in(w_in))
    z3 = z.reshape(bn, s, Z_W)
    y_a = _gmlp(z3, gm_ln_g, gm_ln_b, gm_ws, gm_bs)
    y_b = _rwkv(z3, rw_mu, rw_w0, rw_w_up, rw_a0, rw_a_up, rw_g_up, rw_k_k, rw_k_a, rw_r_k, rw_gn_g, rw_gn_b)
    q_raw, q_rot, kc, vc, ks, vs, kw, vw, gates = _nsa_prep(z3, tabs)
    k_cmp, v_cmp = _nsa_compress(kc, vc, nsa_cmp_pos, nsa_kc_w1, nsa_kc_w2, nsa_vc_w1, nsa_vc_w2)
    y_c = _nsa_attn(q_raw, q_rot, k_cmp, v_cmp, ks, vs, kw, vw, gates)
    return _out_proj(h, y_a.reshape(n, -1), y_b.reshape(n, -1), y_c.reshape(n, -1), _layout_w_out(w_out))


def kernel(x, p, positions, g_mix, w_in, w_out, gm_ln_g, gm_ln_b, gm_ws, gm_bs, rw_mu, rw_w0, rw_w_up, rw_a0,
           rw_a_up, rw_g_up, rw_k_k, rw_k_a, rw_r_k, rw_gn_g, rw_gn_b, nsa_cmp_pos, nsa_kc_w1, nsa_kc_w2,
           nsa_vc_w1, nsa_vc_w2, g_ffn, ffn_w1, ffn_w3, ffn_w2, router_w, router_b, moe_w1, moe_w3, moe_w2,
           g_ple, ple_gate_w, ple_proj_w, g_final):
    bn, s, _ = x.shape
    n = bn * s
    depth = g_mix.shape[0]
    tabs = _rope_tables(positions)
    h = x.reshape(n, D_MODEL)
    for i in range(depth):
        h = _mix_layer(h, bn, s, tabs, g_mix[i], w_in[i], w_out[i], gm_ln_g[i], gm_ln_b[i], gm_ws[i], gm_bs[i],
                       rw_mu[i], rw_w0[i], rw_w_up[i], rw_a0[i], rw_a_up[i], rw_g_up[i], rw_k_k[i], rw_k_a[i],
                       rw_r_k[i], rw_gn_g[i], rw_gn_b[i], nsa_cmp_pos[i], nsa_kc_w1[i], nsa_kc_w2[i],
                       nsa_vc_w1[i], nsa_vc_w2[i])
        j = i // 2
        if i % 2 == 0:
            h = _ffn(h, g_ffn[i].reshape(1, -1), ffn_w1[j:j + 1].astype(BF16), ffn_w3[j:j + 1].astype(BF16),
                     ffn_w2[j:j + 1].astype(BF16))
        else:
            h = _moe(h, g_ffn[i].reshape(1, -1), router_w[j], router_b[j], moe_w1[j].astype(BF16),
                     moe_w3[j].astype(BF16), moe_w2[j].astype(BF16))
        h = _ple(h, p[i].reshape(n, PLE_DIM), g_ple[i].reshape(1, -1), ple_gate_w[i].astype(BF16),
                 ple_proj_w[i].astype(BF16), g_final.reshape(1, -1), final=(i == depth - 1))
    return h.reshape(bn, s, D_MODEL)
```

```python
import functools
import math

import jax
import jax.numpy as jnp
from jax import lax
from jax.experimental import pallas as pl
from jax.experimental.pallas import tpu as pltpu

F32 = jnp.float32
BF16 = jnp.bfloat16

D_MODEL = 1024
HEAD_DIM = 64
A_HEADS = 4
B_HEADS = 4
A_WIDTH = 256
B_WIDTH = 256
C_WIDTH = 512
CHUNK = 128
LN_EPS = 1e-5
GN_EPS = 64e-5
RMS_EPS = 1e-6
RW_LORA = 64
KV_GROUPS = 2
Q_PER_GROUP = 4
CMP_BLOCK = 32
CMP_STRIDE = 16
CMP_HIDDEN = 128
SLC_BLOCK = 64
SLC_TOPK = 16
WINDOW = 512
Q_BLOCK = 128
NEG_INF = -1e30
FORCE = 1e4
ROPE_THETA = 500000.0
ROPE_DIM = 16
D_FF = 2816
N_EXPERTS = 8
TOP_K = 2
PLE_DIM = 256

LANES = 128
Z_B = 0
Z_A = 1024
Z_Q = 1536
Z_KV = 2048
Z_G = 2816
Z_W = 2944
N_GATE = 24

VMEM_LIMIT = 56 * 1024 * 1024
TM_PROJ = 512
TM_FFN = 512
TM_GROUP = 256
GM_T = 512
RW_T = 512
RW_C = 64
TK_SEL = 512

_NN = (((1,), (0,)), ((), ()))
_NT = (((1,), (1,)), ((), ()))


def _params(*sem):
    return pltpu.CompilerParams(dimension_semantics=sem, vmem_limit_bytes=VMEM_LIMIT)


def _dot(a, b, dn=_NN, passes=1):
    if passes == 6:
        return lax.dot_general(a.astype(F32), b.astype(F32), dn, precision=lax.Precision.HIGHEST,
                               preferred_element_type=F32)
    a_hi = a.astype(BF16)
    b_hi = b.astype(BF16)
    out = lax.dot_general(a_hi, b_hi, dn, preferred_element_type=F32)
    if passes == 1:
        return out
    a_lo = (a - a_hi.astype(F32)).astype(BF16)
    out = out + lax.dot_general(a_lo, b_hi, dn, preferred_element_type=F32)
    if passes == 2:
        return out
    b_lo = (b - b_hi.astype(F32)).astype(BF16)
    return out + lax.dot_general(a_hi, b_lo, dn, preferred_element_type=F32)


def _rms(x, g):
    return x * lax.rsqrt(jnp.mean(x * x, axis=-1, keepdims=True) + RMS_EPS) * g


def _sigmoid(x):
    return 1.0 / (1.0 + jnp.exp(-x))


def _div(x, d):
    return lax.shift_right_logical(x, jnp.int32(int(math.log2(d))))


def _head_block_ones(n):
    r = lax.broadcasted_iota(jnp.int32, (n, n), 0)
    c = lax.broadcasted_iota(jnp.int32, (n, n), 1)
    return jnp.where(_div(r, HEAD_DIM) == _div(c, HEAD_DIM), 1.0, 0.0).astype(F32)


def _rope_kernel(inv_ref, pos_ref, cos_ref, sin_ref):
    p = pos_ref[0].astype(F32)
    for j in range(ROPE_DIM // 2):
        ang = p * inv_ref[j]
        cos_ref[0, j] = jnp.cos(ang)
        sin_ref[0, j] = jnp.sin(ang)


def _rope_tables(positions):
    bn, s = positions.shape
    half = ROPE_DIM // 2
    inv = 1.0 / (ROPE_THETA ** (jnp.arange(0, ROPE_DIM, 2, dtype=F32) / ROPE_DIM))
    pos3 = positions.reshape(bn, s // LANES, LANES)
    cos, sin = pl.pallas_call(
        _rope_kernel,
        grid=(bn,),
        in_specs=[pl.BlockSpec(memory_space=pltpu.SMEM),
                  pl.BlockSpec((1, s // LANES, LANES), lambda b: (b, 0, 0))],
        out_specs=[pl.BlockSpec((1, half, s // LANES, LANES), lambda b: (b, 0, 0, 0))] * 2,
        out_shape=[jax.ShapeDtypeStruct((bn, half, s // LANES, LANES), F32)] * 2,
        compiler_params=_params("parallel"),
        name="rope_tables",
    )(inv, pos3)
    cos = cos.reshape(bn, half, s).transpose(0, 2, 1)
    sin = sin.reshape(bn, half, s).transpose(0, 2, 1)
    one = jnp.ones((bn, s, HEAD_DIM - ROPE_DIM), F32)
    zero = jnp.zeros((bn, s, HEAD_DIM - half), F32)
    tab_c = jnp.concatenate([cos, cos, one], axis=-1)
    tab_lo = jnp.concatenate([sin, zero], axis=-1)
    tab_hi = jnp.concatenate([zero[..., :half], sin, zero[..., :HEAD_DIM - ROPE_DIM]], axis=-1)
    rep = lambda t: jnp.tile(t, (1, 1, LANES // HEAD_DIM))
    return rep(tab_c), rep(tab_lo), rep(tab_hi)


def _proj_in_kernel(h_ref, g_ref, w_ref, z_ref):
    y = _rms(h_ref[...], g_ref[...])
    z_ref[...] = jnp.dot(y.astype(BF16), w_ref[...], preferred_element_type=F32)


def _proj_in(h, g, w):
    n = h.shape[0]
    tm = min(TM_PROJ, n)
    return pl.pallas_call(
        _proj_in_kernel,
        grid=(n // tm,),
        in_specs=[pl.BlockSpec((tm, D_MODEL), lambda i: (i, 0)),
                  pl.BlockSpec((1, D_MODEL), lambda i: (0, 0)),
                  pl.BlockSpec((D_MODEL, Z_W), lambda i: (0, 0))],
        out_specs=pl.BlockSpec((tm, Z_W), lambda i: (i, 0)),
        out_shape=jax.ShapeDtypeStruct((n, Z_W), F32),
        compiler_params=_params("parallel"),
        name="proj_in",
    )(h, g, w)


def _gmlp_kernel(z_ref, lng_ref, lnb_ref, ws_ref, bias_ref, o_ref):
    gz = jax.nn.gelu(z_ref[0])
    u = gz[:, :A_WIDTH]
    v = gz[:, A_WIDTH:]
    ones = _head_block_ones(A_WIDTH)
    mu = _dot(v, ones, passes=2) * (1.0 / HEAD_DIM)
    d = v - mu
    var = _dot(d * d, ones, passes=2) * (1.0 / HEAD_DIM)
    vn = d * lax.rsqrt(var + LN_EPS) * lng_ref[...] + lnb_ref[...]
    r = lax.broadcasted_iota(jnp.int32, (CHUNK, CHUNK), 0)
    c = lax.broadcasted_iota(jnp.int32, (CHUNK, CHUNK), 1)
    lane_head = _div(lax.broadcasted_iota(jnp.int32, (CHUNK, A_WIDTH), 1), HEAD_DIM)
    w_causal = [jnp.where(c <= r, ws_ref[hd], 0.0).astype(BF16) for hd in range(A_HEADS)]
    for ck in range(vn.shape[0] // CHUNK):
        rows = slice(ck * CHUNK, (ck + 1) * CHUNK)
        mixed = bias_ref[...]
        for hd in range(A_HEADS):
            mixed = mixed + _dot(w_causal[hd], jnp.where(lane_head == hd, vn[rows], 0.0))
        o_ref[0, rows, :] = (u[rows] * mixed).astype(o_ref.dtype)


def _gmlp(z3, ln_g, ln_b, w_s, b_s):
    bn, s, _ = z3.shape
    t_len = min(GM_T, s)
    bias = jnp.repeat(b_s.T, HEAD_DIM, axis=1)
    return pl.pallas_call(
        _gmlp_kernel,
        grid=(bn, s // t_len),
        in_specs=[pl.BlockSpec((1, t_len, 2 * A_WIDTH), lambda b, i: (b, i, Z_A // (2 * A_WIDTH))),
                  pl.BlockSpec((1, A_WIDTH), lambda b, i: (0, 0)),
                  pl.BlockSpec((1, A_WIDTH), lambda b, i: (0, 0)),
                  pl.BlockSpec((A_HEADS, CHUNK, CHUNK), lambda b, i: (0, 0, 0)),
                  pl.BlockSpec((CHUNK, A_WIDTH), lambda b, i: (0, 0))],
        out_specs=pl.BlockSpec((1, t_len, A_WIDTH), lambda b, i: (b, i, 0)),
        out_shape=jax.ShapeDtypeStruct((bn, s, A_WIDTH), BF16),
        compiler_params=_params("parallel", "parallel"),
        name="gmlp",
    )(z3, ln_g.reshape(1, A_WIDTH), ln_b.reshape(1, A_WIDTH), w_s, bias)


RW_PASSES = 1


def _rwkv_chunks(r, lw, k, v, kk, a, s0, c_len):
    nh = B_HEADS
    m = nh * c_len
    chunks = range(r.shape[0] // c_len)
    dot = functools.partial(_dot, passes=RW_PASSES)
    ti = lax.broadcasted_iota(jnp.int32, (c_len, c_len), 0)
    tj = lax.broadcasted_iota(jnp.int32, (c_len, c_len), 1)
    tril = jnp.where(tj <= ti, 1.0, 0.0).astype(F32)
    row_head = _div(lax.broadcasted_iota(jnp.int32, (m, B_WIDTH), 0), c_len)
    lane_head = _div(lax.broadcasted_iota(jnp.int32, (m, B_WIDTH), 1), HEAD_DIM)
    head_mask = row_head == lane_head
    rep = lambda x: jnp.concatenate([x] * nh, axis=0)
    stack = lambda x: jnp.where(head_mask, rep(x), 0.0)
    ri = lax.broadcasted_iota(jnp.int32, (m, m), 0)
    ci = lax.broadcasted_iota(jnp.int32, (m, m), 1)
    same = _div(ri, c_len) == _div(ci, c_len)
    strict = same & (ci < ri)
    incl = same & (ci <= ri)
    eye = ri == ci
    assert 2 * c_len == LANES
    low = lax.broadcasted_iota(jnp.int32, (2 * m, LANES), 1) < c_len

    a_st, r_st, v_st, bw_st, kw_st, w_c, l_ab, l_ak, l_rb, l_rk = ([] for _ in range(10))
    for c in chunks:
        sl = slice(c * c_len, (c + 1) * c_len)
        cum = _dot(tril, lw[sl], passes=3)
        w_t = jnp.exp(cum)
        w_i = jnp.exp(-cum)
        a_t = -kk[sl] * jnp.exp(cum - lw[sl])
        b_t = kk[sl] * a[sl] * w_i
        k_t = k[sl] * w_i
        w_c.append(w_t[c_len - 1:c_len, :])
        a_st.append(stack(a_t))
        r_st.append(stack(r[sl] * w_t))
        v_st.append(stack(v[sl]))
        bw_st.append(stack(b_t * w_c[c]))
        kw_st.append(stack(k_t * w_c[c]))
        cross = dot(jnp.concatenate([a_st[c], r_st[c]], axis=0), jnp.concatenate([b_t, k_t], axis=0), _NT)
        swapped = pltpu.roll(cross, c_len, axis=1)
        vs_b = jnp.concatenate([jnp.where(low, cross, swapped)] * (nh // 2), axis=1)
        vs_k = jnp.concatenate([jnp.where(low, swapped, cross)] * (nh // 2), axis=1)
        l_ab.append(jnp.where(strict, vs_b[:m], 0.0))
        l_ak.append(jnp.where(strict, vs_k[:m], 0.0))
        l_rb.append(jnp.where(incl, vs_b[m:], 0.0))
        l_rk.append(jnp.where(incl, vs_k[m:], 0.0))

    p_inv = [jnp.where(eye, 1.0, 0.0) + l_ab[c] for c in chunks]
    pw = l_ab
    for _ in range(int(math.log2(c_len)) - 1):
        pw = [dot(pw[c], pw[c]) for c in chunks]
        p_inv = [p_inv[c] + dot(p_inv[c], pw[c]) for c in chunks]
    t_m = [dot(l_ak[c], v_st[c]) for c in chunks]
    q_m = [dot(p_inv[c], t_m[c]) for c in chunks]
    p_m = [dot(p_inv[c], a_st[c]) for c in chunks]
    g_m = [r_st[c] + dot(l_rb[c], p_m[c]) for c in chunks]
    h_m = [dot(l_rb[c], q_m[c]) + dot(l_rk[c], v_st[c]) for c in chunks]
    m_m = [jnp.where(eye, w_c[c], 0.0) + dot(p_m[c].T, bw_st[c]) for c in chunks]
    n_m = [dot(q_m[c].T, bw_st[c]) + dot(v_st[c].T, kw_st[c]) for c in chunks]

    states = [s0]
    for c in chunks:
        states.append(dot(states[c], m_m[c]) + n_m[c])
    ys = []
    for c in chunks:
        y_st = dot(g_m[c], states[c], _NT) + h_m[c]
        y = y_st[0:c_len]
        for hd in range(1, nh):
            y = y + y_st[hd * c_len:(hd + 1) * c_len]
        ys.append(y)
    return jnp.concatenate(ys, axis=0), states[-1]


def _rwkv_kernel(z_ref, mu_ref, w0_ref, a0_ref, wwa_ref, gup_ref, kk_ref, ka_ref, rk_ref, gng_ref, gnb_ref,
                 o_ref, carry_ref, state_ref, *, c_len):
    @pl.when(pl.program_id(1) == 0)
    def _():
        carry_ref[...] = jnp.zeros_like(carry_ref)
        state_ref[...] = jnp.zeros_like(state_ref)

    zb = z_ref[0]
    t_len = zb.shape[0]
    row = lax.broadcasted_iota(jnp.int32, zb.shape, 0)
    z_prev = jnp.where(row == 0, carry_ref[...], pltpu.roll(zb, 1, axis=0))
    carry_ref[...] = zb[t_len - 1:t_len, :]
    zz = zb + (z_prev - zb) * mu_ref[...]
    r = zz[:, 0:B_WIDTH]
    k = zz[:, B_WIDTH:2 * B_WIDTH]
    v = zz[:, 2 * B_WIDTH:3 * B_WIDTH]
    wa = zz[:, 3 * B_WIDTH:3 * B_WIDTH + 2 * RW_LORA]
    gd = zz[:, 3 * B_WIDTH + 2 * RW_LORA:]
    lane = lax.broadcasted_iota(jnp.int32, wa.shape, 1)
    proj = _dot(jnp.where(lane < RW_LORA, jnp.tanh(wa), wa), wwa_ref[...])
    x = -(w0_ref[...] + proj[:, :B_WIDTH])
    softplus = jnp.maximum(x, 0.0) + jnp.log(1.0 + jnp.exp(-jnp.abs(x)))
    lw = -jnp.exp(-softplus - 0.5)
    a = _sigmoid(a0_ref[...] + proj[:, B_WIDTH:])
    g = _dot(_sigmoid(gd), gup_ref[...])
    ones = _head_block_ones(B_WIDTH)
    kk = k * kk_ref[...]
    kk = kk * lax.rsqrt(jnp.maximum(_dot(kk * kk, ones, passes=2), 1e-24))
    k2 = k * (1.0 + (a - 1.0) * ka_ref[...])

    y, state_ref[...] = _rwkv_chunks(r, lw, k2, v, kk, a, state_ref[...], c_len)

    mu_y = _dot(y, ones, passes=2) * (1.0 / HEAD_DIM)
    d = y - mu_y
    var = _dot(d * d, ones, passes=2) * (1.0 / HEAD_DIM)
    yn = d * lax.rsqrt(var + GN_EPS) * gng_ref[...] + gnb_ref[...]
    bonus = _dot(r * k2 * rk_ref[...], ones, passes=2) * v
    o_ref[0] = ((yn + bonus) * g).astype(o_ref.dtype)


def _rwkv(z3, mu, w0, w_up, a0, a_up, g_up, k_k, k_a, r_k, gn_g, gn_b):
    bn, s, _ = z3.shape
    t_len = min(RW_T, s)
    zero = jnp.zeros((RW_LORA, B_WIDTH), F32)
    wwa = jnp.concatenate([jnp.concatenate([w_up, zero], axis=1),
                           jnp.concatenate([zero, a_up], axis=1)], axis=0).astype(BF16)
    row = lambda t: t.reshape(1, -1)
    vec = pl.BlockSpec((1, B_WIDTH), lambda b, i: (0, 0))
    return pl.pallas_call(
        functools.partial(_rwkv_kernel, c_len=RW_C),
        grid=(bn, s // t_len),
        in_specs=[pl.BlockSpec((1, t_len, 4 * B_WIDTH), lambda b, i: (b, i, Z_B // (4 * B_WIDTH))),
                  pl.BlockSpec((1, 4 * B_WIDTH), lambda b, i: (0, 0)),
                  vec, vec,
                  pl.BlockSpec((2 * RW_LORA, 2 * B_WIDTH), lambda b, i: (0, 0)),
                  pl.BlockSpec((2 * RW_LORA, B_WIDTH), lambda b, i: (0, 0)),
                  vec, vec, vec, vec, vec],
        out_specs=pl.BlockSpec((1, t_len, B_WIDTH), lambda b, i: (b, i, 0)),
        out_shape=jax.ShapeDtypeStruct((bn, s, B_WIDTH), BF16),
        scratch_shapes=[pltpu.VMEM((1, 4 * B_WIDTH), F32), pltpu.VMEM((B_WIDTH, B_WIDTH), F32)],
        compiler_params=_params("parallel", "arbitrary"),
        name="rwkv7",
    )(z3, row(mu), row(w0), row(a0), wwa, g_up.astype(BF16), row(k_k), row(k_a), row(r_k), row(gn_g), row(gn_b))


def _rope(x, tab_c, tab_lo, tab_hi):
    n = x.shape[-1]
    half = ROPE_DIM // 2
    return x * tab_c - pltpu.roll(x, n - half, axis=1) * tab_lo + pltpu.roll(x, half, axis=1) * tab_hi


N_HEADS_C = KV_GROUPS * Q_PER_GROUP
QT_COLS = N_HEADS_C * Q_BLOCK
GATE_ROWS = 32
LOG2E = 1.4426950408889634


def _nsa_prep_kernel(zq_ref, zkc_ref, zks_ref, zkw_ref, zg_ref, tc_ref, tl_ref, th_ref,
                     qraw_ref, qrot_ref, kc_ref, vc_ref, ks_ref, vst_ref, kw_ref, vwt_ref, gate_ref):
    tab_c, tab_lo, tab_hi = tc_ref[0], tl_ref[0], th_ref[0]
    nrep = C_WIDTH // LANES
    wide = lambda t: jnp.concatenate([t] * nrep, axis=1)
    q = zq_ref[0] * (HEAD_DIM ** -0.5 * LOG2E)
    q_rot = _rope(q, wide(tab_c), wide(tab_lo), wide(tab_hi))
    ts = q.shape[0]
    row_group0 = lax.broadcasted_iota(jnp.int32, (LANES, ts), 0) < HEAD_DIM

    def put_queries(ref, x):
        for j in range(Q_PER_GROUP):
            xt = x[:, j * LANES:(j + 1) * LANES].T
            for g in range(KV_GROUPS):
                keep = row_group0 if g == 0 else jnp.logical_not(row_group0)
                xm = jnp.where(keep, xt, 0.0).astype(BF16)
                for qq in range(ts // Q_BLOCK):
                    col = qq * QT_COLS + (g * Q_PER_GROUP + j) * Q_BLOCK
                    ref[0, :, col:col + Q_BLOCK] = xm[:, qq * Q_BLOCK:(qq + 1) * Q_BLOCK]

    put_queries(qraw_ref, q)
    put_queries(qrot_ref, q_rot)
    kc_ref[0] = zkc_ref[0][:, :LANES].astype(BF16)
    vc_ref[0] = zkc_ref[0][:, LANES:].astype(BF16)
    ks_ref[0, :, :LANES] = _rope(zks_ref[0][:, :LANES], tab_c, tab_lo, tab_hi).astype(BF16)
    key_block = _div(pl.program_id(1) * ts + lax.broadcasted_iota(jnp.int32, (ts, LANES), 0), SLC_BLOCK)
    ks_ref[0, :, LANES:] = jnp.where(key_block == lax.broadcasted_iota(jnp.int32, (ts, LANES), 1), 1.0, 0.0).astype(BF16)
    vst_ref[0, 0] = zks_ref[0][:, LANES:].T.astype(BF16)
    kw_ref[0] = _rope(zkw_ref[0][:, :LANES], tab_c, tab_lo, tab_hi).astype(BF16)
    vwt = zkw_ref[0][:, LANES:].T.astype(BF16)
    for qq in range(ts // Q_BLOCK):
        vwt_ref[0, qq] = vwt[:, qq * Q_BLOCK:(qq + 1) * Q_BLOCK]
    gate_ref[0] = _sigmoid(zg_ref[0]).T[:GATE_ROWS, :]


def _nsa_prep(z3, tabs):
    bn, s, _ = z3.shape
    ts = TK_SEL
    zspec = lambda width, off: pl.BlockSpec((1, ts, width), lambda b, i: (b, i, off // width))
    tspec = pl.BlockSpec((1, ts, LANES), lambda b, i: (b, i, 0))
    rowmajor = pl.BlockSpec((1, ts, LANES), lambda b, i: (b, i, 0))
    qspec = pl.BlockSpec((1, LANES, N_HEADS_C * ts), lambda b, i: (b, 0, i))
    rm_sds = jax.ShapeDtypeStruct((bn, s, LANES), BF16)
    q_sds = jax.ShapeDtypeStruct((bn, LANES, N_HEADS_C * s), BF16)
    return pl.pallas_call(
        _nsa_prep_kernel,
        grid=(bn, s // ts),
        in_specs=[zspec(C_WIDTH, Z_Q), zspec(2 * LANES, Z_KV), zspec(2 * LANES, Z_KV + 2 * LANES),
                  zspec(2 * LANES, Z_KV + 4 * LANES), zspec(LANES, Z_G), tspec, tspec, tspec],
        out_specs=[qspec, qspec, rowmajor, rowmajor, pl.BlockSpec((1, ts, 2 * LANES), lambda b, i: (b, i, 0)),
                   pl.BlockSpec((1, 1, LANES, ts), lambda b, i: (b, i, 0, 0)), rowmajor,
                   pl.BlockSpec((1, ts // Q_BLOCK, LANES, Q_BLOCK), lambda b, i: (b, i, 0, 0)),
                   pl.BlockSpec((1, GATE_ROWS, ts), lambda b, i: (b, 0, i))],
        out_shape=[q_sds, q_sds, rm_sds, rm_sds, jax.ShapeDtypeStruct((bn, s, 2 * LANES), BF16),
                   jax.ShapeDtypeStruct((bn, s // ts, LANES, ts), BF16), rm_sds,
                   jax.ShapeDtypeStruct((bn, s // Q_BLOCK, LANES, Q_BLOCK), BF16),
                   jax.ShapeDtypeStruct((bn, GATE_ROWS, s), F32)],
        compiler_params=_params("parallel", "parallel"),
        name="nsa_prep",
    )(z3, z3, z3, z3, z3, *tabs)


def _nsa_compress_kernel(xk_ref, xv_ref, pos_ref, kw1a_ref, kw1b_ref, kw2_ref, vw1a_ref, vw1b_ref, vw2_ref,
                         ko_ref, vo_ref):
    n = xk_ref.shape[1]
    pos_a = pos_ref[0]
    pos_b = pos_ref[1]
    for x_ref, w1a, w1b, w2, o_ref, transposed in ((xk_ref, kw1a_ref, kw1b_ref, kw2_ref, ko_ref, False),
                                                    (xv_ref, vw1a_ref, vw1b_ref, vw2_ref, vo_ref, True)):
        x = x_ref[0]
        first = _dot(x, w1a[...])
        second = _dot(x, w1b[...])
        pc = _dot(pos_a, w1a[...]) + _dot(pos_b, w1b[...])
        hid = jax.nn.gelu(first + pltpu.roll(second, n - 1, axis=0) + pc[0:1, :])
        out = _dot(hid, w2[...])
        o_ref[0] = (out.T if transposed else out).astype(o_ref.dtype)


def _expand_cmp_weights(w1, w2):
    half = CMP_BLOCK // 2
    w1r = w1.reshape(CMP_BLOCK, HEAD_DIM, CMP_HIDDEN)
    eye = jnp.eye(KV_GROUPS, dtype=F32)
    w1e = jnp.einsum('ldh,gk->lgdkh', w1r, eye).reshape(CMP_BLOCK * LANES, KV_GROUPS * CMP_HIDDEN)
    w2e = jnp.einsum('hd,gk->ghkd', w2, eye).reshape(KV_GROUPS * CMP_HIDDEN, LANES)
    return (w1e[:half * LANES].astype(BF16), w1e[half * LANES:].astype(BF16), w2e.astype(BF16))


def _nsa_compress(kc, vc, cmp_pos, kc_w1, kc_w2, vc_w1, vc_w2):
    bn, s, _ = kc.shape
    n = s // CMP_STRIDE
    xw = CMP_STRIDE * LANES
    xk = kc.reshape(bn, n, xw)
    xv = vc.reshape(bn, n, xw)
    pos = jnp.tile(cmp_pos[:, None, :], (1, KV_GROUPS, 1)).reshape(2, 1, xw)
    pos = jnp.broadcast_to(pos, (2, 8, xw)).astype(BF16)
    kw = _expand_cmp_weights(kc_w1, kc_w2)
    vw = _expand_cmp_weights(vc_w1, vc_w2)
    full = lambda a: pl.BlockSpec(a.shape, lambda b: (0,) * a.ndim)
    xspec = pl.BlockSpec((1, n, xw), lambda b: (b, 0, 0))
    return pl.pallas_call(
        _nsa_compress_kernel,
        grid=(bn,),
        in_specs=[xspec, xspec, full(pos)] + [full(a) for a in kw + vw],
        out_specs=[pl.BlockSpec((1, n, LANES), lambda b: (b, 0, 0)), pl.BlockSpec((1, LANES, n), lambda b: (b, 0, 0))],
        out_shape=[jax.ShapeDtypeStruct((bn, n, LANES), BF16), jax.ShapeDtypeStruct((bn, LANES, n), BF16)],
        compiler_params=_params("parallel"),
        name="nsa_compress",
    )(xk, xv, pos, *kw, *vw)


def _nsa_attn_kernel(qraw_ref, qrot_ref, kcmp_ref, vcmpt_ref, ks_ref, vst_ref, kw_ref, vwt_ref, gate_ref, o_ref,
                     m_ref, l_ref, acc_ref, sa_ref, sb_ref, qa_ref):
    tq = Q_BLOCK
    gw = Q_PER_GROUP * tq
    seq = ks_ref.shape[1]
    n_cmp = kcmp_ref.shape[1]
    n_slc = seq // SLC_BLOCK
    n_sel = min(SLC_TOPK, n_slc)
    qb = pl.program_id(1)
    t0 = qb * tq
    gcols = lambda g: slice(g * gw, (g + 1) * gw)
    grows = lambda g: slice(g * HEAD_DIM, (g + 1) * HEAD_DIM)
    jcols = lambda j: slice(j * tq, (j + 1) * tq)

    t_c = t0 + lax.broadcasted_iota(jnp.int32, (n_cmp, tq), 1)
    n_c = lax.broadcasted_iota(jnp.int32, (n_cmp, tq), 0)
    cmask = (n_c * CMP_STRIDE + (CMP_BLOCK - 1)) <= t_c
    cbias = jnp.where(cmask, 0.0, NEG_INF)
    o_c, p_sum = [], []
    for g in range(KV_GROUPS):
        s = _dot(kcmp_ref[0], qraw_ref[0, :, gcols(g)])
        parts, total = [], None
        for j in range(Q_PER_GROUP):
            sj = s[:, jcols(j)] + cbias
            e = jnp.exp2(sj - jnp.max(sj, axis=0, keepdims=True))
            p = jnp.where(cmask, e * (1.0 / jnp.sum(e, axis=0, keepdims=True)), 0.0)
            total = p if total is None else total + p
            parts.append(p.astype(BF16))
        p_sum.append(total)
        o_c.append(_dot(vcmpt_ref[0, grows(g), :], jnp.concatenate(parts, axis=1)))

    n_band = WINDOW // tq + 1
    band = n_band * tq
    wb = jnp.maximum(qb - WINDOW // tq, 0)
    w0 = pl.multiple_of(wb * tq, tq)
    t_w = t0 + lax.broadcasted_iota(jnp.int32, (band, tq), 1)
    k_w = w0 + lax.broadcasted_iota(jnp.int32, (band, tq), 0)
    wbias = jnp.where((k_w <= t_w) & (k_w > t_w - WINDOW), 0.0, NEG_INF)
    kband = kw_ref[0, pl.ds(w0, band), :]
    o_w = []
    for g in range(KV_GROUPS):
        s = _dot(kband, qrot_ref[0, :, gcols(g)])
        parts, inv = [], []
        for j in range(Q_PER_GROUP):
            sj = s[:, jcols(j)] + wbias
            e = jnp.exp2(sj - jnp.max(sj, axis=0, keepdims=True))
            inv.append(1.0 / jnp.sum(e, axis=0, keepdims=True))
            parts.append(e.astype(BF16))
        vband = jnp.concatenate([vwt_ref[0, wb + i, grows(g), :] for i in range(n_band)], axis=1)
        o_w.append(_dot(vband, jnp.concatenate(parts, axis=1)) * jnp.concatenate(inv, axis=1))

    m_o = lax.broadcasted_iota(jnp.int32, (n_slc, n_cmp), 0) * SLC_BLOCK
    n_o = lax.broadcasted_iota(jnp.int32, (n_slc, n_cmp), 1) * CMP_STRIDE
    overlap_t = jnp.where((n_o < m_o + SLC_BLOCK) & (n_o + (CMP_BLOCK - 1) >= m_o), 1.0, 0.0).astype(F32)
    m_i = lax.broadcasted_iota(jnp.int32, (n_slc, tq), 0)
    blk_t = _div(t0 + lax.broadcasted_iota(jnp.int32, (n_slc, tq), 1), SLC_BLOCK)
    valid = m_i <= blk_t
    forced = (m_i == 0) | (m_i == blk_t) | (m_i == blk_t - 1)
    sel = []
    for g in range(KV_GROUPS):
        imp = _dot(overlap_t, p_sum[g], passes=3)
        imp = jnp.where(valid, imp + jnp.where(forced, FORCE, 0.0), -FORCE)
        rank = jnp.zeros((n_slc, tq), F32)
        for mp in range(n_slc):
            other = imp[mp:mp + 1, :]
            rank = rank + jnp.where(m_i > mp, jnp.where(other >= imp, 1.0, 0.0), jnp.where(other > imp, 1.0, 0.0))
        sel.append(jnp.where((rank < n_sel) & valid, 0.0, NEG_INF).astype(BF16))

    m_ref[...] = jnp.full(m_ref.shape, NEG_INF, F32)
    l_ref[...] = jnp.zeros(l_ref.shape, F32)
    acc_ref[...] = jnp.zeros(acc_ref.shape, F32)
    n_tiles = (t0 + tq + TK_SEL - 1) // TK_SEL
    pad = jnp.zeros((LANES - n_slc, gw), BF16)
    for g in range(KV_GROUPS):
        qa_ref[g] = jnp.concatenate([qrot_ref[0, :, gcols(g)], jnp.concatenate([sel[g]] * Q_PER_GROUP, axis=1), pad],
                                    axis=0)
    r0 = pl.multiple_of(t0 - (n_tiles - 1) * TK_SEL, tq)
    tri = jnp.where(lax.broadcasted_iota(jnp.int32, (tq, tq), 0) > lax.broadcasted_iota(jnp.int32, (tq, tq), 1),
                    NEG_INF, 0.0)

    def scores(kt, s_ref):
        k0 = pl.multiple_of(jnp.minimum(kt, n_tiles - 1) * TK_SEL, TK_SEL)
        keys = ks_ref[0, pl.ds(k0, TK_SEL), :]
        own = jnp.where(kt >= n_tiles - 1, tri, 0.0)
        for g in range(KV_GROUPS):
            s_ref[g] = _dot(keys, qa_ref[g])
            for j in range(Q_PER_GROUP):
                s_ref[g, pl.ds(r0, tq), jcols(j)] += own

    def attend(kt, s_ref):
        for g in range(KV_GROUPS):
            parts, alphas = [], []
            for j in range(Q_PER_GROUP):
                cs = slice(g * gw + j * tq, g * gw + (j + 1) * tq)
                sj = s_ref[g, :, jcols(j)]
                m_old = m_ref[:, cs]
                m_new = jnp.maximum(m_old, jnp.max(sj, axis=0, keepdims=True))
                alpha = jnp.exp2(m_old - m_new)
                p = jnp.exp2(sj - m_new)
                l_ref[:, cs] = alpha * l_ref[:, cs] + jnp.sum(p, axis=0, keepdims=True)
                m_ref[:, cs] = m_new
                alphas.append(alpha)
                parts.append(p.astype(BF16))
            pv = _dot(vst_ref[0, kt, grows(g), :], jnp.concatenate(parts, axis=1))
            acc_ref[g] = acc_ref[g] * jnp.concatenate(alphas, axis=1) + pv

    odd = lax.rem(n_tiles, 2)
    scores(0, sa_ref)

    @pl.when(odd == 1)
    def _():
        attend(0, sa_ref)
        scores(1, sa_ref)

    def sel_pair(i, carry):
        kt = odd + 2 * i
        scores(kt + 1, sb_ref)
        attend(kt, sa_ref)
        scores(kt + 2, sa_ref)
        attend(kt + 1, sb_ref)
        return carry

    lax.fori_loop(0, (n_tiles - odd) // 2, sel_pair, 0)

    gates = gate_ref[0]
    outs = []
    for j in range(Q_PER_GROUP):
        for g in range(KV_GROUPS):
            row = (g * Q_PER_GROUP + j) * 3
            cs = slice(g * gw + j * tq, g * gw + (j + 1) * tq)
            o_s = acc_ref[g][:, jcols(j)] * (1.0 / l_ref[:, cs])
            outs.append(gates[row:row + 1, :] * o_c[g][:, jcols(j)] + gates[row + 1:row + 2, :] * o_s
                        + gates[row + 2:row + 3, :] * o_w[g][:, jcols(j)])
    o_ref[0] = jnp.concatenate(outs, axis=0).T.astype(o_ref.dtype)


def _nsa_attn(q_raw, q_rot, k_cmp, v_cmp_t, ks, vs_t, kw, vw_t, gates_t):
    bn, s, _ = ks.shape
    n_cmp = k_cmp.shape[1]
    qspec = pl.BlockSpec((1, LANES, QT_COLS), lambda b, i: (b, 0, i))
    kspec = pl.BlockSpec((1, s, LANES), lambda b, i: (b, 0, 0))
    whole = lambda a: pl.BlockSpec((1,) + a.shape[1:], lambda b, i: (b,) + (0,) * (a.ndim - 1))
    return pl.pallas_call(
        _nsa_attn_kernel,
        grid=(bn, s // Q_BLOCK),
        in_specs=[qspec, qspec, whole(k_cmp), whole(v_cmp_t), whole(ks), whole(vs_t), kspec, whole(vw_t),
                  pl.BlockSpec((1, GATE_ROWS, Q_BLOCK), lambda b, i: (b, 0, i))],
        out_specs=pl.BlockSpec((1, Q_BLOCK, C_WIDTH), lambda b, i: (b, i, 0)),
        out_shape=jax.ShapeDtypeStruct((bn, s, C_WIDTH), BF16),
        scratch_shapes=[pltpu.VMEM((1, QT_COLS), F32), pltpu.VMEM((1, QT_COLS), F32),
                        pltpu.VMEM((KV_GROUPS, HEAD_DIM, Q_PER_GROUP * Q_BLOCK), F32),
                        pltpu.VMEM((KV_GROUPS, TK_SEL, Q_PER_GROUP * Q_BLOCK), F32),
                        pltpu.VMEM((KV_GROUPS, TK_SEL, Q_PER_GROUP * Q_BLOCK), F32),
                        pltpu.VMEM((KV_GROUPS, 2 * LANES, Q_PER_GROUP * Q_BLOCK), BF16)],
        compiler_params=_params("parallel", "arbitrary"),
        name="nsa_attn",
    )(q_raw, q_rot, k_cmp, v_cmp_t, ks, vs_t, kw, vw_t, gates_t)


def _mixed_residual(h_ref, ya_ref, yb_ref, yc_ref, w_ref):
    acc = jnp.dot(ya_ref[...], w_ref[0:A_WIDTH, :], preferred_element_type=F32)
    acc = acc + jnp.dot(yb_ref[...], w_ref[A_WIDTH:A_WIDTH + B_WIDTH, :], preferred_element_type=F32)
    acc = acc + jnp.dot(yc_ref[...], w_ref[A_WIDTH + B_WIDTH:, :], preferred_element_type=F32)
    return h_ref[...] + acc


def _mixed_specs(tm):
    tok = lambda width: pl.BlockSpec((tm, width), lambda i: (i, 0))
    return [tok(D_MODEL), tok(A_WIDTH), tok(B_WIDTH), tok(C_WIDTH), pl.BlockSpec((D_MODEL, D_MODEL), lambda i: (0, 0))]


def _swiglu_step(x, w1_ref, w3_ref, w2_ref):
    h1 = jnp.dot(x, w1_ref[0], preferred_element_type=F32)
    h3 = jnp.dot(x, w3_ref[0], preferred_element_type=F32)
    hid = h1 * _sigmoid(h1) * h3
    return jnp.dot(hid.astype(BF16), w2_ref[0], preferred_element_type=F32)


def _ffn_kernel(h_ref, ya_ref, yb_ref, yc_ref, wo_ref, g_ref, w1_ref, w3_ref, w2_ref, o_ref):
    h = _mixed_residual(h_ref, ya_ref, yb_ref, yc_ref, wo_ref)
    x = _rms(h, g_ref[...]).astype(BF16)
    o_ref[...] = h + _swiglu_step(x, w1_ref, w3_ref, w2_ref)


def _ffn(h, ya, yb, yc, w_out, g, w1, w3, w2):
    n = h.shape[0]
    tm = min(TM_FFN, n)
    resident = lambda shape: pl.BlockSpec(shape, lambda i: (0, 0, 0), pipeline_mode=pl.Buffered(1))
    return pl.pallas_call(
        _ffn_kernel,
        grid=(n // tm,),
        in_specs=_mixed_specs(tm) + [
            pl.BlockSpec((1, D_MODEL), lambda i: (0, 0)),
            resident((1, D_MODEL, D_FF)), resident((1, D_MODEL, D_FF)), resident((1, D_FF, D_MODEL))],
        out_specs=pl.BlockSpec((tm, D_MODEL), lambda i: (i, 0)),
        out_shape=jax.ShapeDtypeStruct((n, D_MODEL), F32),
        compiler_params=_params("parallel"),
        name="ffn_swiglu",
    )(h, ya, yb, yc, w_out, g, w1, w3, w2)


META_E1, META_E2, META_R1, META_R2, META_P1, META_P2 = range(6)


def _router_kernel(h_ref, ya_ref, yb_ref, yc_ref, wo_ref, g_ref, rw_ref, rb_ref,
                   hmid_ref, hn_ref, meta_ref, cnt_ref, carry_ref):
    @pl.when(pl.program_id(0) == 0)
    def _():
        carry_ref[...] = jnp.zeros_like(carry_ref)

    h = _mixed_residual(h_ref, ya_ref, yb_ref, yc_ref, wo_ref)
    hmid_ref[...] = h
    hn = _rms(h, g_ref[...])
    hn_ref[...] = hn
    tm = hn.shape[0]
    lane = lax.broadcasted_iota(jnp.int32, (tm, LANES), 1).astype(F32)
    logits = jnp.where(lane < N_EXPERTS, _dot(hn, rw_ref[...], passes=3) + rb_ref[...], NEG_INF)
    top1 = jnp.max(logits, axis=-1, keepdims=True)
    idx1 = jnp.min(jnp.where(logits == top1, lane, float(LANES)), axis=-1, keepdims=True)
    rest = jnp.where(lane == idx1, NEG_INF, logits)
    top2 = jnp.max(rest, axis=-1, keepdims=True)
    idx2 = jnp.min(jnp.where(rest == top2, lane, float(LANES)), axis=-1, keepdims=True)
    ex = jnp.exp(top2 - top1)
    picked = jnp.where((lane == idx1) | (lane == idx2), 1.0, 0.0)
    r = lax.broadcasted_iota(jnp.int32, (tm, tm), 0)
    c = lax.broadcasted_iota(jnp.int32, (tm, tm), 1)
    before = _dot(jnp.where(c < r, 1.0, 0.0), picked) + carry_ref[...]
    rank1 = jnp.sum(jnp.where(lane == idx1, before, 0.0), axis=-1, keepdims=True)
    rank2 = jnp.sum(jnp.where(lane == idx2, before, 0.0), axis=-1, keepdims=True)
    carry_ref[...] += jnp.sum(picked, axis=0, keepdims=True)
    cnt_ref[...] = carry_ref[...]
    meta = jnp.zeros((tm, LANES), F32)
    for pos, val in ((META_E1, idx1), (META_E2, idx2), (META_R1, rank1), (META_R2, rank2),
                     (META_P1, 1.0 / (1.0 + ex)), (META_P2, ex / (1.0 + ex))):
        meta = jnp.where(lane == pos, val, meta)
    meta_ref[...] = meta


def _dispatch_kernel(d1_ref, d2_ref, hn_ref, xs_in_ref, xs_ref, sem):
    del xs_in_ref
    tm = hn_ref.shape[0]

    def row_copy(r, dest):
        return pltpu.make_async_copy(hn_ref.at[pl.ds(r, 1)], xs_ref.at[pl.ds(dest, 1)], sem)

    def issue(r, carry):
        row_copy(r, d1_ref[0, 0, r]).start()
        row_copy(r, d2_ref[0, 0, r]).start()
        return carry

    lax.fori_loop(0, tm, issue, 0, unroll=8)
    for _ in range(2):
        pltpu.make_async_copy(hn_ref, xs_ref.at[pl.ds(0, tm)], sem).wait()


def _grouped_kernel(te_ref, nu_ref, xs_ref, w1_ref, w3_ref, w2_ref, ys_ref):
    del te_ref
    used = pl.program_id(0) < nu_ref[0]

    @pl.when(used)
    def _():
        ys_ref[...] = _swiglu_step(xs_ref[...].astype(BF16), w1_ref, w3_ref, w2_ref)

    @pl.when(jnp.logical_not(used))
    def _():
        ys_ref[...] = jnp.zeros_like(ys_ref)


def _combine_kernel(d1_ref, d2_ref, h_ref, meta_ref, ys_ref, o_ref, buf1, buf2, sem):
    tm = h_ref.shape[0]

    def row_copy(dest, buf, r):
        return pltpu.make_async_copy(ys_ref.at[pl.ds(dest, 1)], buf.at[pl.ds(r, 1)], sem)

    def issue(r, carry):
        row_copy(d1_ref[0, 0, r], buf1, r).start()
        row_copy(d2_ref[0, 0, r], buf2, r).start()
        return carry

    lax.fori_loop(0, tm, issue, 0, unroll=8)
    for buf in (buf1, buf2):
        pltpu.make_async_copy(ys_ref.at[pl.ds(0, tm)], buf, sem).wait()
    meta = meta_ref[...]
    o_ref[...] = (h_ref[...] + meta[:, META_P1:META_P1 + 1] * buf1[...]
                  + meta[:, META_P2:META_P2 + 1] * buf2[...])


def _moe(h, ya, yb, yc, w_out, g, router_w, router_b, w1, w3, w2):
    n = h.shape[0]
    tm = min(TM_FFN, n)
    tg = min(TM_GROUP, n)
    nt = n // tm
    rw = jnp.pad(router_w, ((0, 0), (0, LANES - N_EXPERTS)))
    rb = jnp.pad(router_b.reshape(1, -1), ((0, 0), (0, LANES - N_EXPERTS)))
    tok = lambda width: pl.BlockSpec((tm, width), lambda i: (i, 0))
    h, hn, meta, cnt = pl.pallas_call(
        _router_kernel,
        grid=(nt,),
        in_specs=_mixed_specs(tm) + [
            pl.BlockSpec((1, D_MODEL), lambda i: (0, 0)),
            pl.BlockSpec((D_MODEL, LANES), lambda i: (0, 0)), pl.BlockSpec((1, LANES), lambda i: (0, 0))],
        out_specs=[tok(D_MODEL), tok(D_MODEL), tok(LANES), pl.BlockSpec((1, LANES), lambda i: (0, 0))],
        out_shape=[jax.ShapeDtypeStruct((n, D_MODEL), F32), jax.ShapeDtypeStruct((n, D_MODEL), F32),
                   jax.ShapeDtypeStruct((n, LANES), F32), jax.ShapeDtypeStruct((1, LANES), F32)],
        scratch_shapes=[pltpu.VMEM((1, LANES), F32)],
        compiler_params=_params("arbitrary"),
        name="moe_router",
    )(h, ya, yb, yc, w_out, g, rw, rb)

    counts = cnt[0, :N_EXPERTS].astype(jnp.int32)
    padded = ((counts + tg - 1) // tg) * tg
    ends = jnp.cumsum(padded)
    offs = ends - padded
    rows = TOP_K * n + N_EXPERTS * tg
    n_tiles = rows // tg
    tile_start = jnp.arange(n_tiles, dtype=jnp.int32) * tg
    n_used = (ends[-1:] // tg).astype(jnp.int32)
    tile_expert = jnp.sum(tile_start[:, None] >= ends[None, :], axis=1).astype(jnp.int32)
    tile_expert = jnp.minimum(tile_expert, tile_expert[n_used[0] - 1])
    onehot = lambda e: (e[:, None] == jnp.arange(N_EXPERTS, dtype=jnp.int32)[None, :]).astype(jnp.int32)
    dest = lambda e, r: (jnp.sum(onehot(e.astype(jnp.int32)) * offs[None, :], axis=1)
                         + r.astype(jnp.int32)).reshape(nt, 1, tm)
    d1 = dest(meta[:, META_E1], meta[:, META_R1])
    d2 = dest(meta[:, META_E2], meta[:, META_R2])
    dspec = pl.BlockSpec((1, 1, tm), lambda i: (i, 0, 0), memory_space=pltpu.SMEM)

    xs = pl.pallas_call(
        _dispatch_kernel,
        grid=(nt,),
        in_specs=[dspec, dspec, tok(D_MODEL), pl.BlockSpec(memory_space=pl.ANY)],
        out_specs=pl.BlockSpec(memory_space=pl.ANY),
        out_shape=jax.ShapeDtypeStruct((rows, D_MODEL), F32),
        scratch_shapes=[pltpu.SemaphoreType.DMA(())],
        input_output_aliases={3: 0},
        compiler_params=_params("arbitrary"),
        name="moe_dispatch",
    )(d1, d2, hn, jnp.zeros((rows, D_MODEL), F32))

    wspec = lambda shape: pl.BlockSpec(shape, lambda i, te, nu: (te[i], 0, 0))
    ys = pl.pallas_call(
        _grouped_kernel,
        grid_spec=pltpu.PrefetchScalarGridSpec(
            num_scalar_prefetch=2,
            grid=(n_tiles,),
            in_specs=[pl.BlockSpec((tg, D_MODEL), lambda i, te, nu: (jnp.minimum(i, nu[0] - 1), 0)),
                      wspec((1, D_MODEL, D_FF)), wspec((1, D_MODEL, D_FF)), wspec((1, D_FF, D_MODEL))],
            out_specs=pl.BlockSpec((tg, D_MODEL), lambda i, te, nu: (i, 0))),
        out_shape=jax.ShapeDtypeStruct((rows, D_MODEL), F32),
        compiler_params=_params("arbitrary"),
        name="moe_grouped",
    )(tile_expert, n_used, xs, w1, w3, w2)

    return pl.pallas_call(
        _combine_kernel,
        grid=(nt,),
        in_specs=[dspec, dspec, tok(D_MODEL), tok(LANES), pl.BlockSpec(memory_space=pl.ANY)],
        out_specs=tok(D_MODEL),
        out_shape=jax.ShapeDtypeStruct((n, D_MODEL), F32),
        scratch_shapes=[pltpu.VMEM((tm, D_MODEL), F32), pltpu.VMEM((tm, D_MODEL), F32),
                        pltpu.SemaphoreType.DMA(())],
        compiler_params=_params("arbitrary"),
        name="moe_combine",
    )(d1, d2, h, meta, ys)


def _ple_update(h_ref, p_ref, g_ref, wg_ref, wp_ref):
    h = h_ref[...]
    gate = _sigmoid(jnp.dot(_rms(h, g_ref[...]).astype(BF16), wg_ref[...], preferred_element_type=F32))
    return h + jnp.dot(p_ref[...].astype(BF16), wp_ref[...], preferred_element_type=F32) * gate


def _ple_final_kernel(h_ref, p_ref, g_ref, wg_ref, wp_ref, gf_ref, o_ref):
    o_ref[...] = _rms(_ple_update(h_ref, p_ref, g_ref, wg_ref, wp_ref), gf_ref[...])


def _ple_proj_kernel(h_ref, p_ref, g_ref, wg_ref, wp_ref, gn_ref, wn_ref, o_ref, z_ref):
    h = _ple_update(h_ref, p_ref, g_ref, wg_ref, wp_ref)
    o_ref[...] = h
    z_ref[...] = jnp.dot(_rms(h, gn_ref[...]).astype(BF16), wn_ref[...], preferred_element_type=F32)


def _ple(h, p, g, wg, wp, g_tail, w_next=None):
    n = h.shape[0]
    tm = min(TM_PROJ, n)
    tok = lambda width: pl.BlockSpec((tm, width), lambda i: (i, 0))
    const = lambda a: pl.BlockSpec(a.shape, lambda i: (0, 0))
    common = dict(grid=(n // tm,), compiler_params=_params("parallel"))
    in_specs = [tok(D_MODEL), tok(PLE_DIM), const(g), const(wg), const(wp), const(g_tail)]
    h_sds = jax.ShapeDtypeStruct((n, D_MODEL), F32)
    if w_next is None:
        return pl.pallas_call(_ple_final_kernel, in_specs=in_specs, out_specs=tok(D_MODEL), out_shape=h_sds,
                              name="ple_final", **common)(h, p, g, wg, wp, g_tail)
    return pl.pallas_call(_ple_proj_kernel, in_specs=in_specs + [const(w_next)],
                          out_specs=[tok(D_MODEL), tok(Z_W)],
                          out_shape=[h_sds, jax.ShapeDtypeStruct((n, Z_W), F32)],
                          name="ple_proj_in", **common)(h, p, g, wg, wp, g_tail, w_next)


def _q_perm():
    idx = []
    for j in range(Q_PER_GROUP):
        for g in range(KV_GROUPS):
            base = (g * Q_PER_GROUP + j) * HEAD_DIM
            idx.extend(range(base, base + HEAD_DIM))
    return jnp.asarray(idx, dtype=jnp.int32)


def _layout_w_in(w):
    a1 = 2 * A_WIDTH
    b1 = a1 + 4 * B_WIDTH
    q1 = b1 + C_WIDTH
    w = jnp.concatenate([w[:, a1:b1], w[:, :a1], w[:, b1:q1][:, _q_perm()], w[:, q1:]], axis=1)
    return jnp.pad(w, ((0, 0), (0, Z_W - w.shape[1]))).astype(BF16)


def _layout_w_out(w):
    c0 = A_WIDTH + B_WIDTH
    return jnp.concatenate([w[:c0], w[c0:][_q_perm()]], axis=0).astype(BF16)


def _mixers(z, bn, s, tabs, gm_ln_g, gm_ln_b, gm_ws, gm_bs,
            rw_mu, rw_w0, rw_w_up, rw_a0, rw_a_up, rw_g_up, rw_k_k, rw_k_a, rw_r_k, rw_gn_g, rw_gn_b,
            nsa_cmp_pos, nsa_kc_w1, nsa_kc_w2, nsa_vc_w1, nsa_vc_w2):
    n = bn * s
    z3 = z.reshape(bn, s, Z_W)
    y_a = _gmlp(z3, gm_ln_g, gm_ln_b, gm_ws, gm_bs)
    y_b = _rwkv(z3, rw_mu, rw_w0, rw_w_up, rw_a0, rw_a_up, rw_g_up, rw_k_k, rw_k_a, rw_r_k, rw_gn_g, rw_gn_b)
    q_raw, q_rot, kc, vc, ks, vs, kw, vw, gates = _nsa_prep(z3, tabs)
    k_cmp, v_cmp = _nsa_compress(kc, vc, nsa_cmp_pos, nsa_kc_w1, nsa_kc_w2, nsa_vc_w1, nsa_vc_w2)
    y_c = _nsa_attn(q_raw, q_rot, k_cmp, v_cmp, ks, vs, kw, vw, gates)
    return y_a.reshape(n, -1), y_b.reshape(n, -1), y_c.reshape(n, -1)


def kernel(x, p, positions, g_mix, w_in, w_out, gm_ln_g, gm_ln_b, gm_ws, gm_bs, rw_mu, rw_w0, rw_w_up, rw_a0,
           rw_a_up, rw_g_up, rw_k_k, rw_k_a, rw_r_k, rw_gn_g, rw_gn_b, nsa_cmp_pos, nsa_kc_w1, nsa_kc_w2,
           nsa_vc_w1, nsa_vc_w2, g_ffn, ffn_w1, ffn_w3, ffn_w2, router_w, router_b, moe_w1, moe_w3, moe_w2,
           g_ple, ple_gate_w, ple_proj_w, g_final):
    bn, s, _ = x.shape
    n = bn * s
    depth = g_mix.shape[0]
    tabs = _rope_tables(positions)
    h = x.reshape(n, D_MODEL)
    z = _proj_in(h, g_mix[0].reshape(1, -1), _layout_w_in(w_in[0]))
    for i in range(depth):
        ys = _mixers(z, bn, s, tabs, gm_ln_g[i], gm_ln_b[i], gm_ws[i], gm_bs[i],
                     rw_mu[i], rw_w0[i], rw_w_up[i], rw_a0[i], rw_a_up[i], rw_g_up[i], rw_k_k[i], rw_k_a[i],
                     rw_r_k[i], rw_gn_g[i], rw_gn_b[i], nsa_cmp_pos[i], nsa_kc_w1[i], nsa_kc_w2[i],
                     nsa_vc_w1[i], nsa_vc_w2[i])
        wo = _layout_w_out(w_out[i])
        j = i // 2
        if i % 2 == 0:
            h = _ffn(h, *ys, wo, g_ffn[i].reshape(1, -1), ffn_w1[j:j + 1].astype(BF16),
                     ffn_w3[j:j + 1].astype(BF16), ffn_w2[j:j + 1].astype(BF16))
        else:
            h = _moe(h, *ys, wo, g_ffn[i].reshape(1, -1), router_w[j], router_b[j], moe_w1[j].astype(BF16),
                     moe_w3[j].astype(BF16), moe_w2[j].astype(BF16))
        ple_args = (p[i].reshape(n, PLE_DIM), g_ple[i].reshape(1, -1), ple_gate_w[i].astype(BF16),
                    ple_proj_w[i].astype(BF16))
        if i == depth - 1:
            h = _ple(h, *ple_args, g_final.reshape(1, -1))
        else:
            h, z = _ple(h, *ple_args, g_mix[i + 1].reshape(1, -1), _layout_w_in(w_in[i + 1]))
    return h.reshape(bn, s, D_MODEL)
```

```python
import functools
import math

import jax
import jax.numpy as jnp
from jax import lax
from jax.experimental import pallas as pl
from jax.experimental.pallas import tpu as pltpu

F32 = jnp.float32
BF16 = jnp.bfloat16

D_MODEL = 1024
HEAD_DIM = 64
A_HEADS = 4
B_HEADS = 4
A_WIDTH = 256
B_WIDTH = 256
C_WIDTH = 512
CHUNK = 128
LN_EPS = 1e-5
GN_EPS = 64e-5
RMS_EPS = 1e-6
RW_LORA = 64
KV_GROUPS = 2
Q_PER_GROUP = 4
CMP_BLOCK = 32
CMP_STRIDE = 16
CMP_HIDDEN = 128
SLC_BLOCK = 64
SLC_TOPK = 16
WINDOW = 512
Q_BLOCK = 128
NEG_INF = -1e30
FORCE = 1e4
ROPE_THETA = 500000.0
ROPE_DIM = 16
D_FF = 2816
N_EXPERTS = 8
TOP_K = 2
PLE_DIM = 256

LANES = 128
Z_B = 0
Z_A = 1024
Z_Q = 1536
Z_KV = 2048
Z_G = 2816
Z_W = 2944
Z_DTYPE = BF16

VMEM_LIMIT = 56 * 1024 * 1024
TM_PROJ = 512
TM_FFN = 512
TM_GROUP = 256
GM_T = 512
RW_T = 512
RW_C = 64
TK_SEL = 512

_NN = (((1,), (0,)), ((), ()))
_NT = (((1,), (1,)), ((), ()))


def _params(*sem):
    return pltpu.CompilerParams(dimension_semantics=sem, vmem_limit_bytes=VMEM_LIMIT)


def _dot(a, b, dn=_NN, passes=1):
    if passes == 6:
        return lax.dot_general(a.astype(F32), b.astype(F32), dn, precision=lax.Precision.HIGHEST,
                               preferred_element_type=F32)
    a_hi = a.astype(BF16)
    b_hi = b.astype(BF16)
    out = lax.dot_general(a_hi, b_hi, dn, preferred_element_type=F32)
    if passes == 1:
        return out
    a_lo = (a - a_hi.astype(F32)).astype(BF16)
    out = out + lax.dot_general(a_lo, b_hi, dn, preferred_element_type=F32)
    if passes == 2:
        return out
    b_lo = (b - b_hi.astype(F32)).astype(BF16)
    return out + lax.dot_general(a_hi, b_lo, dn, preferred_element_type=F32)


def _rms(x, g):
    return x * lax.rsqrt(jnp.mean(x * x, axis=-1, keepdims=True) + RMS_EPS) * g


def _sigmoid(x):
    return 1.0 / (1.0 + jnp.exp(-x))


def _div(x, d):
    return lax.shift_right_logical(x, jnp.int32(int(math.log2(d))))


def _head_block_ones(n):
    r = lax.broadcasted_iota(jnp.int32, (n, n), 0)
    c = lax.broadcasted_iota(jnp.int32, (n, n), 1)
    return jnp.where(_div(r, HEAD_DIM) == _div(c, HEAD_DIM), 1.0, 0.0).astype(F32)


def _rope_kernel(inv_ref, pos_ref, cos_ref, sin_ref):
    p = pos_ref[0].astype(F32)
    for j in range(ROPE_DIM // 2):
        ang = p * inv_ref[j]
        cos_ref[0, j] = jnp.cos(ang)
        sin_ref[0, j] = jnp.sin(ang)


def _rope_tables(positions):
    bn, s = positions.shape
    half = ROPE_DIM // 2
    inv = 1.0 / (ROPE_THETA ** (jnp.arange(0, ROPE_DIM, 2, dtype=F32) / ROPE_DIM))
    pos3 = positions.reshape(bn, s // LANES, LANES)
    cos, sin = pl.pallas_call(
        _rope_kernel,
        grid=(bn,),
        in_specs=[pl.BlockSpec(memory_space=pltpu.SMEM),
                  pl.BlockSpec((1, s // LANES, LANES), lambda b: (b, 0, 0))],
        out_specs=[pl.BlockSpec((1, half, s // LANES, LANES), lambda b: (b, 0, 0, 0))] * 2,
        out_shape=[jax.ShapeDtypeStruct((bn, half, s // LANES, LANES), F32)] * 2,
        compiler_params=_params("parallel"),
        name="rope_tables",
    )(inv, pos3)
    cos = cos.reshape(bn, half, s).transpose(0, 2, 1)
    sin = sin.reshape(bn, half, s).transpose(0, 2, 1)
    one = jnp.ones((bn, s, HEAD_DIM - ROPE_DIM), F32)
    zero = jnp.zeros((bn, s, HEAD_DIM - half), F32)
    tab_c = jnp.concatenate([cos, cos, one], axis=-1)
    tab_lo = jnp.concatenate([sin, zero], axis=-1)
    tab_hi = jnp.concatenate([zero[..., :half], sin, zero[..., :HEAD_DIM - ROPE_DIM]], axis=-1)
    rep = lambda t: jnp.tile(t, (1, 1, LANES // HEAD_DIM))
    return rep(tab_c), rep(tab_lo), rep(tab_hi)


def _proj_in_kernel(h_ref, g_ref, w_ref, z_ref):
    y = _rms(h_ref[...], g_ref[...])
    z_ref[...] = jnp.dot(y.astype(BF16), w_ref[...], preferred_element_type=F32).astype(z_ref.dtype)


def _proj_in(h, g, w):
    n = h.shape[0]
    tm = min(TM_PROJ, n)
    return pl.pallas_call(
        _proj_in_kernel,
        grid=(n // tm,),
        in_specs=[pl.BlockSpec((tm, D_MODEL), lambda i: (i, 0)),
                  pl.BlockSpec((1, D_MODEL), lambda i: (0, 0)),
                  pl.BlockSpec((D_MODEL, Z_W), lambda i: (0, 0))],
        out_specs=pl.BlockSpec((tm, Z_W), lambda i: (i, 0)),
        out_shape=jax.ShapeDtypeStruct((n, Z_W), Z_DTYPE),
        compiler_params=_params("parallel"),
        name="proj_in",
    )(h, g, w)


def _gmlp_kernel(z_ref, lng_ref, lnb_ref, ws_ref, bias_ref, o_ref):
    gz = jax.nn.gelu(z_ref[0].astype(F32))
    u = gz[:, :A_WIDTH]
    v = gz[:, A_WIDTH:]
    ones = _head_block_ones(A_WIDTH)
    mu = _dot(v, ones, passes=2) * (1.0 / HEAD_DIM)
    d = v - mu
    var = _dot(d * d, ones, passes=2) * (1.0 / HEAD_DIM)
    vn = d * lax.rsqrt(var + LN_EPS) * lng_ref[...] + lnb_ref[...]
    r = lax.broadcasted_iota(jnp.int32, (CHUNK, CHUNK), 0)
    c = lax.broadcasted_iota(jnp.int32, (CHUNK, CHUNK), 1)
    lane_head = _div(lax.broadcasted_iota(jnp.int32, (CHUNK, A_WIDTH), 1), HEAD_DIM)
    w_causal = [jnp.where(c <= r, ws_ref[hd], 0.0).astype(BF16) for hd in range(A_HEADS)]
    for ck in range(vn.shape[0] // CHUNK):
        rows = slice(ck * CHUNK, (ck + 1) * CHUNK)
        mixed = bias_ref[...]
        for hd in range(A_HEADS):
            mixed = mixed + _dot(w_causal[hd], jnp.where(lane_head == hd, vn[rows], 0.0))
        o_ref[0, rows, :] = (u[rows] * mixed).astype(o_ref.dtype)


def _gmlp(z3, ln_g, ln_b, w_s, b_s):
    bn, s, _ = z3.shape
    t_len = min(GM_T, s)
    bias = jnp.repeat(b_s.T, HEAD_DIM, axis=1)
    return pl.pallas_call(
        _gmlp_kernel,
        grid=(bn, s // t_len),
        in_specs=[pl.BlockSpec((1, t_len, 2 * A_WIDTH), lambda b, i: (b, i, Z_A // (2 * A_WIDTH))),
                  pl.BlockSpec((1, A_WIDTH), lambda b, i: (0, 0)),
                  pl.BlockSpec((1, A_WIDTH), lambda b, i: (0, 0)),
                  pl.BlockSpec((A_HEADS, CHUNK, CHUNK), lambda b, i: (0, 0, 0)),
                  pl.BlockSpec((CHUNK, A_WIDTH), lambda b, i: (0, 0))],
        out_specs=pl.BlockSpec((1, t_len, A_WIDTH), lambda b, i: (b, i, 0)),
        out_shape=jax.ShapeDtypeStruct((bn, s, A_WIDTH), BF16),
        compiler_params=_params("parallel", "parallel"),
        name="gmlp",
    )(z3, ln_g.reshape(1, A_WIDTH), ln_b.reshape(1, A_WIDTH), w_s, bias)


RW_PASSES = 1


def _rwkv_chunks(r, lw, k, v, kk, a, s0, c_len):
    nh = B_HEADS
    m = nh * c_len
    chunks = range(r.shape[0] // c_len)
    dot = functools.partial(_dot, passes=RW_PASSES)
    ti = lax.broadcasted_iota(jnp.int32, (c_len, c_len), 0)
    tj = lax.broadcasted_iota(jnp.int32, (c_len, c_len), 1)
    tril = jnp.where(tj <= ti, 1.0, 0.0).astype(F32)
    row_head = _div(lax.broadcasted_iota(jnp.int32, (m, B_WIDTH), 0), c_len)
    lane_head = _div(lax.broadcasted_iota(jnp.int32, (m, B_WIDTH), 1), HEAD_DIM)
    head_mask = row_head == lane_head
    rep = lambda x: jnp.concatenate([x] * nh, axis=0)
    stack = lambda x: jnp.where(head_mask, rep(x), 0.0)
    ri = lax.broadcasted_iota(jnp.int32, (m, m), 0)
    ci = lax.broadcasted_iota(jnp.int32, (m, m), 1)
    same = _div(ri, c_len) == _div(ci, c_len)
    strict = same & (ci < ri)
    incl = same & (ci <= ri)
    eye = ri == ci
    assert 2 * c_len == LANES
    low = lax.broadcasted_iota(jnp.int32, (2 * m, LANES), 1) < c_len

    a_st, r_st, v_st, bw_st, kw_st, w_c, l_ab, l_ak, l_rb, l_rk = ([] for _ in range(10))
    for c in chunks:
        sl = slice(c * c_len, (c + 1) * c_len)
        cum = _dot(tril, lw[sl], passes=3)
        w_t = jnp.exp(cum)
        w_i = jnp.exp(-cum)
        a_t = -kk[sl] * jnp.exp(cum - lw[sl])
        b_t = kk[sl] * a[sl] * w_i
        k_t = k[sl] * w_i
        w_c.append(w_t[c_len - 1:c_len, :])
        a_st.append(stack(a_t))
        r_st.append(stack(r[sl] * w_t))
        v_st.append(stack(v[sl]))
        bw_st.append(stack(b_t * w_c[c]))
        kw_st.append(stack(k_t * w_c[c]))
        cross = dot(jnp.concatenate([a_st[c], r_st[c]], axis=0), jnp.concatenate([b_t, k_t], axis=0), _NT)
        swapped = pltpu.roll(cross, c_len, axis=1)
        vs_b = jnp.concatenate([jnp.where(low, cross, swapped)] * (nh // 2), axis=1)
        vs_k = jnp.concatenate([jnp.where(low, swapped, cross)] * (nh // 2), axis=1)
        l_ab.append(jnp.where(strict, vs_b[:m], 0.0))
        l_ak.append(jnp.where(strict, vs_k[:m], 0.0))
        l_rb.append(jnp.where(incl, vs_b[m:], 0.0))
        l_rk.append(jnp.where(incl, vs_k[m:], 0.0))

    p_inv = [jnp.where(eye, 1.0, 0.0) + l_ab[c] for c in chunks]
    pw = l_ab
    for _ in range(int(math.log2(c_len)) - 1):
        pw = [dot(pw[c], pw[c]) for c in chunks]
        p_inv = [p_inv[c] + dot(p_inv[c], pw[c]) for c in chunks]
    t_m = [dot(l_ak[c], v_st[c]) for c in chunks]
    q_m = [dot(p_inv[c], t_m[c]) for c in chunks]
    p_m = [dot(p_inv[c], a_st[c]) for c in chunks]
    g_m = [r_st[c] + dot(l_rb[c], p_m[c]) for c in chunks]
    h_m = [dot(l_rb[c], q_m[c]) + dot(l_rk[c], v_st[c]) for c in chunks]
    m_m = [jnp.where(eye, w_c[c], 0.0) + dot(p_m[c].T, bw_st[c]) for c in chunks]
    n_m = [dot(q_m[c].T, bw_st[c]) + dot(v_st[c].T, kw_st[c]) for c in chunks]

    states = [s0]
    for c in chunks:
        states.append(dot(states[c], m_m[c]) + n_m[c])
    ys = []
    for c in chunks:
        y_st = dot(g_m[c], states[c], _NT) + h_m[c]
        y = y_st[0:c_len]
        for hd in range(1, nh):
            y = y + y_st[hd * c_len:(hd + 1) * c_len]
        ys.append(y)
    return jnp.concatenate(ys, axis=0), states[-1]


def _rwkv_kernel(z_ref, mu_ref, w0_ref, a0_ref, wwa_ref, gup_ref, kk_ref, ka_ref, rk_ref, gng_ref, gnb_ref,
                 o_ref, carry_ref, state_ref, *, c_len):
    @pl.when(pl.program_id(1) == 0)
    def _():
        carry_ref[...] = jnp.zeros_like(carry_ref)
        state_ref[...] = jnp.zeros_like(state_ref)

    zb = z_ref[0].astype(F32)
    t_len = zb.shape[0]
    row = lax.broadcasted_iota(jnp.int32, zb.shape, 0)
    z_prev = jnp.where(row == 0, carry_ref[...], pltpu.roll(zb, 1, axis=0))
    carry_ref[...] = zb[t_len - 1:t_len, :]
    zz = zb + (z_prev - zb) * mu_ref[...]
    r = zz[:, 0:B_WIDTH]
    k = zz[:, B_WIDTH:2 * B_WIDTH]
    v = zz[:, 2 * B_WIDTH:3 * B_WIDTH]
    wa = zz[:, 3 * B_WIDTH:3 * B_WIDTH + 2 * RW_LORA]
    gd = zz[:, 3 * B_WIDTH + 2 * RW_LORA:]
    lane = lax.broadcasted_iota(jnp.int32, wa.shape, 1)
    proj = _dot(jnp.where(lane < RW_LORA, jnp.tanh(wa), wa), wwa_ref[...])
    x = -(w0_ref[...] + proj[:, :B_WIDTH])
    softplus = jnp.maximum(x, 0.0) + jnp.log(1.0 + jnp.exp(-jnp.abs(x)))
    lw = -jnp.exp(-softplus - 0.5)
    a = _sigmoid(a0_ref[...] + proj[:, B_WIDTH:])
    g = _dot(_sigmoid(gd), gup_ref[...])
    ones = _head_block_ones(B_WIDTH)
    kk = k * kk_ref[...]
    kk = kk * lax.rsqrt(jnp.maximum(_dot(kk * kk, ones, passes=2), 1e-24))
    k2 = k * (1.0 + (a - 1.0) * ka_ref[...])

    y, state_ref[...] = _rwkv_chunks(r, lw, k2, v, kk, a, state_ref[...], c_len)

    mu_y = _dot(y, ones, passes=2) * (1.0 / HEAD_DIM)
    d = y - mu_y
    var = _dot(d * d, ones, passes=2) * (1.0 / HEAD_DIM)
    yn = d * lax.rsqrt(var + GN_EPS) * gng_ref[...] + gnb_ref[...]
    bonus = _dot(r * k2 * rk_ref[...], ones, passes=2) * v
    o_ref[0] = ((yn + bonus) * g).astype(o_ref.dtype)


def _rwkv(z3, mu, w0, w_up, a0, a_up, g_up, k_k, k_a, r_k, gn_g, gn_b):
    bn, s, _ = z3.shape
    t_len = min(RW_T, s)
    zero = jnp.zeros((RW_LORA, B_WIDTH), F32)
    wwa = jnp.concatenate([jnp.concatenate([w_up, zero], axis=1),
                           jnp.concatenate([zero, a_up], axis=1)], axis=0).astype(BF16)
    row = lambda t: t.reshape(1, -1)
    vec = pl.BlockSpec((1, B_WIDTH), lambda b, i: (0, 0))
    return pl.pallas_call(
        functools.partial(_rwkv_kernel, c_len=RW_C),
        grid=(bn, s // t_len),
        in_specs=[pl.BlockSpec((1, t_len, 4 * B_WIDTH), lambda b, i: (b, i, Z_B // (4 * B_WIDTH))),
                  pl.BlockSpec((1, 4 * B_WIDTH), lambda b, i: (0, 0)),
                  vec, vec,
                  pl.BlockSpec((2 * RW_LORA, 2 * B_WIDTH), lambda b, i: (0, 0)),
                  pl.BlockSpec((2 * RW_LORA, B_WIDTH), lambda b, i: (0, 0)),
                  vec, vec, vec, vec, vec],
        out_specs=pl.BlockSpec((1, t_len, B_WIDTH), lambda b, i: (b, i, 0)),
        out_shape=jax.ShapeDtypeStruct((bn, s, B_WIDTH), BF16),
        scratch_shapes=[pltpu.VMEM((1, 4 * B_WIDTH), F32), pltpu.VMEM((B_WIDTH, B_WIDTH), F32)],
        compiler_params=_params("parallel", "arbitrary"),
        name="rwkv7",
    )(z3, row(mu), row(w0), row(a0), wwa, g_up.astype(BF16), row(k_k), row(k_a), row(r_k), row(gn_g), row(gn_b))


def _rope(x, tab_c, tab_lo, tab_hi):
    n = x.shape[-1]
    half = ROPE_DIM // 2
    return x * tab_c - pltpu.roll(x, n - half, axis=1) * tab_lo + pltpu.roll(x, half, axis=1) * tab_hi


N_HEADS_C = KV_GROUPS * Q_PER_GROUP
QT_COLS = N_HEADS_C * Q_BLOCK
GATE_ROWS = 32
ONES_ROWS = 16
LOG2E = 1.4426950408889634


def _nsa_prep_kernel(zq_ref, zkc_ref, zks_ref, zkw_ref, zg_ref, tc_ref, tl_ref, th_ref,
                     qraw_ref, qrot_ref, kc_ref, vc_ref, ks_ref, vst_ref, kw_ref, vwt_ref, gate_ref):
    tab_c, tab_lo, tab_hi = tc_ref[0], tl_ref[0], th_ref[0]
    nrep = C_WIDTH // LANES
    wide = lambda t: jnp.concatenate([t] * nrep, axis=1)
    zkc, zks, zkw = (ref[0].astype(F32) for ref in (zkc_ref, zks_ref, zkw_ref))
    q = zq_ref[0].astype(F32) * (HEAD_DIM ** -0.5 * LOG2E)
    q_rot = _rope(q, wide(tab_c), wide(tab_lo), wide(tab_hi))
    ts = q.shape[0]
    row_group0 = lax.broadcasted_iota(jnp.int32, (LANES, ts), 0) < HEAD_DIM

    def put_queries(ref, x):
        for j in range(Q_PER_GROUP):
            xt = x[:, j * LANES:(j + 1) * LANES].T
            for g in range(KV_GROUPS):
                keep = row_group0 if g == 0 else jnp.logical_not(row_group0)
                xm = jnp.where(keep, xt, 0.0).astype(BF16)
                for qq in range(ts // Q_BLOCK):
                    col = qq * QT_COLS + (g * Q_PER_GROUP + j) * Q_BLOCK
                    ref[0, :, col:col + Q_BLOCK] = xm[:, qq * Q_BLOCK:(qq + 1) * Q_BLOCK]

    put_queries(qraw_ref, q)
    put_queries(qrot_ref, q_rot)
    kc_ref[0] = zkc[:, :LANES].astype(BF16)
    vc_ref[0] = zkc[:, LANES:].astype(BF16)
    ks_ref[0, :, :LANES] = _rope(zks[:, :LANES], tab_c, tab_lo, tab_hi).astype(BF16)
    key_block = _div(pl.program_id(1) * ts + lax.broadcasted_iota(jnp.int32, (ts, LANES), 0), SLC_BLOCK)
    ks_ref[0, :, LANES:] = jnp.where(key_block == lax.broadcasted_iota(jnp.int32, (ts, LANES), 1), 1.0, 0.0).astype(BF16)
    vst_ref[0, 0] = zks[:, LANES:].T.astype(BF16)
    kw_ref[0] = _rope(zkw[:, :LANES], tab_c, tab_lo, tab_hi).astype(BF16)
    vwt = zkw[:, LANES:].T.astype(BF16)
    for qq in range(ts // Q_BLOCK):
        vwt_ref[0, qq] = vwt[:, qq * Q_BLOCK:(qq + 1) * Q_BLOCK]
    gate_ref[0] = _sigmoid(zg_ref[0].astype(F32)).T[:GATE_ROWS, :]


def _nsa_prep(z3, tabs):
    bn, s, _ = z3.shape
    ts = TK_SEL
    zspec = lambda width, off: pl.BlockSpec((1, ts, width), lambda b, i: (b, i, off // width))
    tspec = pl.BlockSpec((1, ts, LANES), lambda b, i: (b, i, 0))
    rowmajor = pl.BlockSpec((1, ts, LANES), lambda b, i: (b, i, 0))
    qspec = pl.BlockSpec((1, LANES, N_HEADS_C * ts), lambda b, i: (b, 0, i))
    rm_sds = jax.ShapeDtypeStruct((bn, s, LANES), BF16)
    q_sds = jax.ShapeDtypeStruct((bn, LANES, N_HEADS_C * s), BF16)
    return pl.pallas_call(
        _nsa_prep_kernel,
        grid=(bn, s // ts),
        in_specs=[zspec(C_WIDTH, Z_Q), zspec(2 * LANES, Z_KV), zspec(2 * LANES, Z_KV + 2 * LANES),
                  zspec(2 * LANES, Z_KV + 4 * LANES), zspec(LANES, Z_G), tspec, tspec, tspec],
        out_specs=[qspec, qspec, rowmajor, rowmajor, pl.BlockSpec((1, ts, 2 * LANES), lambda b, i: (b, i, 0)),
                   pl.BlockSpec((1, 1, LANES, ts), lambda b, i: (b, i, 0, 0)), rowmajor,
                   pl.BlockSpec((1, ts // Q_BLOCK, LANES, Q_BLOCK), lambda b, i: (b, i, 0, 0)),
                   pl.BlockSpec((1, GATE_ROWS, ts), lambda b, i: (b, 0, i))],
        out_shape=[q_sds, q_sds, rm_sds, rm_sds, jax.ShapeDtypeStruct((bn, s, 2 * LANES), BF16),
                   jax.ShapeDtypeStruct((bn, s // ts, LANES, ts), BF16), rm_sds,
                   jax.ShapeDtypeStruct((bn, s // Q_BLOCK, LANES, Q_BLOCK), BF16),
                   jax.ShapeDtypeStruct((bn, GATE_ROWS, s), F32)],
        compiler_params=_params("parallel", "parallel"),
        name="nsa_prep",
    )(z3, z3, z3, z3, z3, *tabs)


def _nsa_compress_kernel(xk_ref, xv_ref, pos_ref, kw1a_ref, kw1b_ref, kw2_ref, vw1a_ref, vw1b_ref, vw2_ref,
                         ko_ref, vo_ref):
    n = xk_ref.shape[1]
    pos_a = pos_ref[0]
    pos_b = pos_ref[1]
    for x_ref, w1a, w1b, w2, o_ref, transposed in ((xk_ref, kw1a_ref, kw1b_ref, kw2_ref, ko_ref, False),
                                                    (xv_ref, vw1a_ref, vw1b_ref, vw2_ref, vo_ref, True)):
        x = x_ref[0]
        first = _dot(x, w1a[...])
        second = _dot(x, w1b[...])
        pc = _dot(pos_a, w1a[...]) + _dot(pos_b, w1b[...])
        hid = jax.nn.gelu(first + pltpu.roll(second, n - 1, axis=0) + pc[0:1, :])
        out = _dot(hid, w2[...])
        o_ref[0] = (out.T if transposed else out).astype(o_ref.dtype)


def _expand_cmp_weights(w1, w2):
    half = CMP_BLOCK // 2
    w1r = w1.reshape(CMP_BLOCK, HEAD_DIM, CMP_HIDDEN)
    eye = jnp.eye(KV_GROUPS, dtype=F32)
    w1e = jnp.einsum('ldh,gk->lgdkh', w1r, eye).reshape(CMP_BLOCK * LANES, KV_GROUPS * CMP_HIDDEN)
    w2e = jnp.einsum('hd,gk->ghkd', w2, eye).reshape(KV_GROUPS * CMP_HIDDEN, LANES)
    return (w1e[:half * LANES].astype(BF16), w1e[half * LANES:].astype(BF16), w2e.astype(BF16))


def _nsa_compress(kc, vc, cmp_pos, kc_w1, kc_w2, vc_w1, vc_w2):
    bn, s, _ = kc.shape
    n = s // CMP_STRIDE
    xw = CMP_STRIDE * LANES
    xk = kc.reshape(bn, n, xw)
    xv = vc.reshape(bn, n, xw)
    pos = jnp.tile(cmp_pos[:, None, :], (1, KV_GROUPS, 1)).reshape(2, 1, xw)
    pos = jnp.broadcast_to(pos, (2, 8, xw)).astype(BF16)
    kw = _expand_cmp_weights(kc_w1, kc_w2)
    vw = _expand_cmp_weights(vc_w1, vc_w2)
    full = lambda a: pl.BlockSpec(a.shape, lambda b: (0,) * a.ndim)
    xspec = pl.BlockSpec((1, n, xw), lambda b: (b, 0, 0))
    return pl.pallas_call(
        _nsa_compress_kernel,
        grid=(bn,),
        in_specs=[xspec, xspec, full(pos)] + [full(a) for a in kw + vw],
        out_specs=[pl.BlockSpec((1, n, LANES), lambda b: (b, 0, 0)), pl.BlockSpec((1, LANES, n), lambda b: (b, 0, 0))],
        out_shape=[jax.ShapeDtypeStruct((bn, n, LANES), BF16), jax.ShapeDtypeStruct((bn, LANES, n), BF16)],
        compiler_params=_params("parallel"),
        name="nsa_compress",
    )(xk, xv, pos, *kw, *vw)


def _nsa_attn_kernel(qraw_ref, qrot_ref, kcmp_ref, vcmpt_ref, ks_ref, vst_ref, kw_ref, vwt_ref, gate_ref, o_ref,
                     m_ref, acc_ref, sa_ref, sb_ref, qa_ref):
    tq = Q_BLOCK
    gw = Q_PER_GROUP * tq
    seq = ks_ref.shape[1]
    n_cmp = kcmp_ref.shape[1]
    n_slc = seq // SLC_BLOCK
    n_sel = min(SLC_TOPK, n_slc)
    qb = pl.program_id(1)
    t0 = qb * tq
    gcols = lambda g: slice(g * gw, (g + 1) * gw)
    grows = lambda g: slice(g * HEAD_DIM, (g + 1) * HEAD_DIM)
    jcols = lambda j: slice(j * tq, (j + 1) * tq)

    t_c = t0 + lax.broadcasted_iota(jnp.int32, (n_cmp, tq), 1)
    n_c = lax.broadcasted_iota(jnp.int32, (n_cmp, tq), 0)
    cmask = (n_c * CMP_STRIDE + (CMP_BLOCK - 1)) <= t_c
    cbias = jnp.where(cmask, 0.0, NEG_INF)
    o_c, p_sum = [], []
    for g in range(KV_GROUPS):
        s = _dot(kcmp_ref[0], qraw_ref[0, :, gcols(g)])
        parts, total = [], None
        for j in range(Q_PER_GROUP):
            sj = s[:, jcols(j)] + cbias
            e = jnp.exp2(sj - jnp.max(sj, axis=0, keepdims=True))
            p = jnp.where(cmask, e * (1.0 / jnp.sum(e, axis=0, keepdims=True)), 0.0)
            total = p if total is None else total + p
            parts.append(p.astype(BF16))
        p_sum.append(total)
        o_c.append(_dot(vcmpt_ref[0, grows(g), :], jnp.concatenate(parts, axis=1)))

    n_band = WINDOW // tq + 1
    band = n_band * tq
    wb = jnp.maximum(qb - WINDOW // tq, 0)
    w0 = pl.multiple_of(wb * tq, tq)
    t_w = t0 + lax.broadcasted_iota(jnp.int32, (band, tq), 1)
    k_w = w0 + lax.broadcasted_iota(jnp.int32, (band, tq), 0)
    wbias = jnp.where((k_w <= t_w) & (k_w > t_w - WINDOW), 0.0, NEG_INF)
    kband = kw_ref[0, pl.ds(w0, band), :]
    o_w = []
    for g in range(KV_GROUPS):
        s = _dot(kband, qrot_ref[0, :, gcols(g)])
        parts = []
        for j in range(Q_PER_GROUP):
            sj = s[:, jcols(j)] + wbias
            parts.append(jnp.exp2(sj - jnp.max(sj, axis=0, keepdims=True)).astype(BF16))
        vband = jnp.concatenate([vwt_ref[0, wb + i, grows(g), :] for i in range(n_band)], axis=1)
        vband = jnp.concatenate([vband, jnp.ones((ONES_ROWS, band), BF16)], axis=0)
        ow = _dot(vband, jnp.concatenate(parts, axis=1))
        o_w.append(ow[:HEAD_DIM] * (1.0 / ow[HEAD_DIM:HEAD_DIM + 1]))

    m_o = lax.broadcasted_iota(jnp.int32, (n_slc, n_cmp), 0) * SLC_BLOCK
    n_o = lax.broadcasted_iota(jnp.int32, (n_slc, n_cmp), 1) * CMP_STRIDE
    overlap_t = jnp.where((n_o < m_o + SLC_BLOCK) & (n_o + (CMP_BLOCK - 1) >= m_o), 1.0, 0.0).astype(F32)
    m_i = lax.broadcasted_iota(jnp.int32, (n_slc, tq), 0)
    blk_t = _div(t0 + lax.broadcasted_iota(jnp.int32, (n_slc, tq), 1), SLC_BLOCK)
    valid = m_i <= blk_t
    forced = (m_i == 0) | (m_i == blk_t) | (m_i == blk_t - 1)
    sel = []
    for g in range(KV_GROUPS):
        imp = _dot(overlap_t, p_sum[g], passes=3)
        imp = jnp.where(valid, imp + jnp.where(forced, FORCE, 0.0), -FORCE)
        rank = jnp.zeros((n_slc, tq), F32)
        for mp in range(n_slc):
            other = imp[mp:mp + 1, :]
            rank = rank + jnp.where(m_i > mp, jnp.where(other >= imp, 1.0, 0.0), jnp.where(other > imp, 1.0, 0.0))
        sel.append(jnp.where((rank < n_sel) & valid, 0.0, NEG_INF).astype(BF16))

    m_ref[...] = jnp.full(m_ref.shape, NEG_INF, F32)
    acc_ref[...] = jnp.zeros(acc_ref.shape, F32)
    n_tiles = (t0 + tq + TK_SEL - 1) // TK_SEL
    pad = jnp.zeros((LANES - n_slc, gw), BF16)
    for g in range(KV_GROUPS):
        qa_ref[g] = jnp.concatenate([qrot_ref[0, :, gcols(g)], jnp.concatenate([sel[g]] * Q_PER_GROUP, axis=1), pad],
                                    axis=0)
    r0 = pl.multiple_of(t0 - (n_tiles - 1) * TK_SEL, tq)
    tri = jnp.where(lax.broadcasted_iota(jnp.int32, (tq, tq), 0) > lax.broadcasted_iota(jnp.int32, (tq, tq), 1),
                    NEG_INF, 0.0)

    def scores(kt, s_ref):
        k0 = pl.multiple_of(jnp.minimum(kt, n_tiles - 1) * TK_SEL, TK_SEL)
        keys = ks_ref[0, pl.ds(k0, TK_SEL), :]
        own = jnp.where(kt >= n_tiles - 1, tri, 0.0)
        for g in range(KV_GROUPS):
            s_ref[g] = _dot(keys, qa_ref[g])
            for j in range(Q_PER_GROUP):
                s_ref[g, pl.ds(r0, tq), jcols(j)] += own

    def attend(kt, s_ref):
        for g in range(KV_GROUPS):
            parts, alphas = [], []
            for j in range(Q_PER_GROUP):
                cs = slice(g * gw + j * tq, g * gw + (j + 1) * tq)
                sj = s_ref[g, :, jcols(j)]
                m_old = m_ref[:, cs]
                m_new = jnp.maximum(m_old, jnp.max(sj, axis=0, keepdims=True))
                m_ref[:, cs] = m_new
                alphas.append(jnp.exp2(m_old - m_new))
                parts.append(jnp.exp2(sj - m_new).astype(BF16))
            vals = jnp.concatenate([vst_ref[0, kt, grows(g), :], jnp.ones((ONES_ROWS, TK_SEL), BF16)], axis=0)
            acc_ref[g] = acc_ref[g] * jnp.concatenate(alphas, axis=1) + _dot(vals, jnp.concatenate(parts, axis=1))

    odd = lax.rem(n_tiles, 2)
    scores(0, sa_ref)

    @pl.when(odd == 1)
    def _():
        attend(0, sa_ref)
        scores(1, sa_ref)

    def sel_pair(i, carry):
        kt = odd + 2 * i
        scores(kt + 1, sb_ref)
        attend(kt, sa_ref)
        scores(kt + 2, sa_ref)
        attend(kt + 1, sb_ref)
        return carry

    lax.fori_loop(0, (n_tiles - odd) // 2, sel_pair, 0)

    gates = gate_ref[0]
    outs = []
    for j in range(Q_PER_GROUP):
        for g in range(KV_GROUPS):
            row = (g * Q_PER_GROUP + j) * 3
            acc = acc_ref[g][:, jcols(j)]
            o_s = acc[:HEAD_DIM] * (1.0 / acc[HEAD_DIM:HEAD_DIM + 1])
            outs.append(gates[row:row + 1, :] * o_c[g][:, jcols(j)] + gates[row + 1:row + 2, :] * o_s
                        + gates[row + 2:row + 3, :] * o_w[g][:, jcols(j)])
    o_ref[0] = jnp.concatenate(outs, axis=0).T.astype(o_ref.dtype)


def _nsa_attn(q_raw, q_rot, k_cmp, v_cmp_t, ks, vs_t, kw, vw_t, gates_t):
    bn, s, _ = ks.shape
    n_cmp = k_cmp.shape[1]
    qspec = pl.BlockSpec((1, LANES, QT_COLS), lambda b, i: (b, 0, i))
    kspec = pl.BlockSpec((1, s, LANES), lambda b, i: (b, 0, 0))
    whole = lambda a: pl.BlockSpec((1,) + a.shape[1:], lambda b, i: (b,) + (0,) * (a.ndim - 1))
    return pl.pallas_call(
        _nsa_attn_kernel,
        grid=(bn, s // Q_BLOCK),
        in_specs=[qspec, qspec, whole(k_cmp), whole(v_cmp_t), whole(ks), whole(vs_t), kspec, whole(vw_t),
                  pl.BlockSpec((1, GATE_ROWS, Q_BLOCK), lambda b, i: (b, 0, i))],
        out_specs=pl.BlockSpec((1, Q_BLOCK, C_WIDTH), lambda b, i: (b, i, 0)),
        out_shape=jax.ShapeDtypeStruct((bn, s, C_WIDTH), BF16),
        scratch_shapes=[pltpu.VMEM((1, QT_COLS), F32),
                        pltpu.VMEM((KV_GROUPS, HEAD_DIM + ONES_ROWS, Q_PER_GROUP * Q_BLOCK), F32),
                        pltpu.VMEM((KV_GROUPS, TK_SEL, Q_PER_GROUP * Q_BLOCK), F32),
                        pltpu.VMEM((KV_GROUPS, TK_SEL, Q_PER_GROUP * Q_BLOCK), F32),
                        pltpu.VMEM((KV_GROUPS, 2 * LANES, Q_PER_GROUP * Q_BLOCK), BF16)],
        compiler_params=_params("parallel", "arbitrary"),
        name="nsa_attn",
    )(q_raw, q_rot, k_cmp, v_cmp_t, ks, vs_t, kw, vw_t, gates_t)


def _mixed_residual(h_ref, ya_ref, yb_ref, yc_ref, w_ref):
    acc = jnp.dot(ya_ref[...], w_ref[0:A_WIDTH, :], preferred_element_type=F32)
    acc = acc + jnp.dot(yb_ref[...], w_ref[A_WIDTH:A_WIDTH + B_WIDTH, :], preferred_element_type=F32)
    acc = acc + jnp.dot(yc_ref[...], w_ref[A_WIDTH + B_WIDTH:, :], preferred_element_type=F32)
    return h_ref[...] + acc


def _mixed_specs(tm):
    tok = lambda width: pl.BlockSpec((tm, width), lambda i: (i, 0))
    return [tok(D_MODEL), tok(A_WIDTH), tok(B_WIDTH), tok(C_WIDTH), pl.BlockSpec((D_MODEL, D_MODEL), lambda i: (0, 0))]


def _swiglu_step(x, w1_ref, w3_ref, w2_ref):
    h1 = jnp.dot(x, w1_ref[0], preferred_element_type=F32)
    h3 = jnp.dot(x, w3_ref[0], preferred_element_type=F32)
    hid = h1 * _sigmoid(h1) * h3
    return jnp.dot(hid.astype(BF16), w2_ref[0], preferred_element_type=F32)


def _ffn_kernel(h_ref, ya_ref, yb_ref, yc_ref, wo_ref, g_ref, w1_ref, w3_ref, w2_ref, o_ref):
    h = _mixed_residual(h_ref, ya_ref, yb_ref, yc_ref, wo_ref)
    x = _rms(h, g_ref[...]).astype(BF16)
    o_ref[...] = h + _swiglu_step(x, w1_ref, w3_ref, w2_ref)


def _ffn(h, ya, yb, yc, w_out, g, w1, w3, w2):
    n = h.shape[0]
    tm = min(TM_FFN, n)
    resident = lambda shape: pl.BlockSpec(shape, lambda i: (0, 0, 0), pipeline_mode=pl.Buffered(1))
    return pl.pallas_call(
        _ffn_kernel,
        grid=(n // tm,),
        in_specs=_mixed_specs(tm) + [
            pl.BlockSpec((1, D_MODEL), lambda i: (0, 0)),
            resident((1, D_MODEL, D_FF)), resident((1, D_MODEL, D_FF)), resident((1, D_FF, D_MODEL))],
        out_specs=pl.BlockSpec((tm, D_MODEL), lambda i: (i, 0)),
        out_shape=jax.ShapeDtypeStruct((n, D_MODEL), F32),
        compiler_params=_params("parallel"),
        name="ffn_swiglu",
    )(h, ya, yb, yc, w_out, g, w1, w3, w2)


META_E1, META_E2, META_R1, META_R2, META_P1, META_P2 = range(6)


def _router_kernel(h_ref, ya_ref, yb_ref, yc_ref, wo_ref, g_ref, rw_ref, rb_ref,
                   hmid_ref, hn_ref, meta_ref, cnt_ref, carry_ref):
    @pl.when(pl.program_id(0) == 0)
    def _():
        carry_ref[...] = jnp.zeros_like(carry_ref)

    h = _mixed_residual(h_ref, ya_ref, yb_ref, yc_ref, wo_ref)
    hmid_ref[...] = h
    hn = _rms(h, g_ref[...])
    hn_ref[...] = hn
    tm = hn.shape[0]
    lane = lax.broadcasted_iota(jnp.int32, (tm, LANES), 1).astype(F32)
    logits = jnp.where(lane < N_EXPERTS, _dot(hn, rw_ref[...], passes=3) + rb_ref[...], NEG_INF)
    top1 = jnp.max(logits, axis=-1, keepdims=True)
    idx1 = jnp.min(jnp.where(logits == top1, lane, float(LANES)), axis=-1, keepdims=True)
    rest = jnp.where(lane == idx1, NEG_INF, logits)
    top2 = jnp.max(rest, axis=-1, keepdims=True)
    idx2 = jnp.min(jnp.where(rest == top2, lane, float(LANES)), axis=-1, keepdims=True)
    ex = jnp.exp(top2 - top1)
    picked = jnp.where((lane == idx1) | (lane == idx2), 1.0, 0.0)
    r = lax.broadcasted_iota(jnp.int32, (tm, tm), 0)
    c = lax.broadcasted_iota(jnp.int32, (tm, tm), 1)
    before = _dot(jnp.where(c < r, 1.0, 0.0), picked) + carry_ref[...]
    rank1 = jnp.sum(jnp.where(lane == idx1, before, 0.0), axis=-1, keepdims=True)
    rank2 = jnp.sum(jnp.where(lane == idx2, before, 0.0), axis=-1, keepdims=True)
    carry_ref[...] += jnp.sum(picked, axis=0, keepdims=True)
    cnt_ref[...] = carry_ref[...]
    meta = jnp.zeros((tm, LANES), F32)
    for pos, val in ((META_E1, idx1), (META_E2, idx2), (META_R1, rank1), (META_R2, rank2),
                     (META_P1, 1.0 / (1.0 + ex)), (META_P2, ex / (1.0 + ex))):
        meta = jnp.where(lane == pos, val, meta)
    meta_ref[...] = meta


def _dispatch_kernel(d1_ref, d2_ref, hn_ref, xs_in_ref, xs_ref, sem):
    del xs_in_ref
    tm = hn_ref.shape[0]

    def row_copy(r, dest):
        return pltpu.make_async_copy(hn_ref.at[pl.ds(r, 1)], xs_ref.at[pl.ds(dest, 1)], sem)

    def issue(r, carry):
        row_copy(r, d1_ref[0, 0, r]).start()
        row_copy(r, d2_ref[0, 0, r]).start()
        return carry

    lax.fori_loop(0, tm, issue, 0, unroll=8)
    for _ in range(2):
        pltpu.make_async_copy(hn_ref, xs_ref.at[pl.ds(0, tm)], sem).wait()


def _grouped_kernel(te_ref, nu_ref, xs_ref, w1_ref, w3_ref, w2_ref, ys_ref):
    del te_ref
    used = pl.program_id(0) < nu_ref[0]

    @pl.when(used)
    def _():
        ys_ref[...] = _swiglu_step(xs_ref[...].astype(BF16), w1_ref, w3_ref, w2_ref)

    @pl.when(jnp.logical_not(used))
    def _():
        ys_ref[...] = jnp.zeros_like(ys_ref)


def _combine_kernel(d1_ref, d2_ref, h_ref, meta_ref, ys_ref, o_ref, buf1, buf2, sem):
    tm = h_ref.shape[0]

    def row_copy(dest, buf, r):
        return pltpu.make_async_copy(ys_ref.at[pl.ds(dest, 1)], buf.at[pl.ds(r, 1)], sem)

    def issue(r, carry):
        row_copy(d1_ref[0, 0, r], buf1, r).start()
        row_copy(d2_ref[0, 0, r], buf2, r).start()
        return carry

    lax.fori_loop(0, tm, issue, 0, unroll=8)
    for buf in (buf1, buf2):
        pltpu.make_async_copy(ys_ref.at[pl.ds(0, tm)], buf, sem).wait()
    meta = meta_ref[...]
    o_ref[...] = (h_ref[...] + meta[:, META_P1:META_P1 + 1] * buf1[...]
                  + meta[:, META_P2:META_P2 + 1] * buf2[...])


def _moe(h, ya, yb, yc, w_out, g, router_w, router_b, w1, w3, w2):
    n = h.shape[0]
    tm = min(TM_FFN, n)
    tg = min(TM_GROUP, n)
    nt = n // tm
    rw = jnp.pad(router_w, ((0, 0), (0, LANES - N_EXPERTS)))
    rb = jnp.pad(router_b.reshape(1, -1), ((0, 0), (0, LANES - N_EXPERTS)))
    tok = lambda width: pl.BlockSpec((tm, width), lambda i: (i, 0))
    h, hn, meta, cnt = pl.pallas_call(
        _router_kernel,
        grid=(nt,),
        in_specs=_mixed_specs(tm) + [
            pl.BlockSpec((1, D_MODEL), lambda i: (0, 0)),
            pl.BlockSpec((D_MODEL, LANES), lambda i: (0, 0)), pl.BlockSpec((1, LANES), lambda i: (0, 0))],
        out_specs=[tok(D_MODEL), tok(D_MODEL), tok(LANES), pl.BlockSpec((1, LANES), lambda i: (0, 0))],
        out_shape=[jax.ShapeDtypeStruct((n, D_MODEL), F32), jax.ShapeDtypeStruct((n, D_MODEL), F32),
                   jax.ShapeDtypeStruct((n, LANES), F32), jax.ShapeDtypeStruct((1, LANES), F32)],
        scratch_shapes=[pltpu.VMEM((1, LANES), F32)],
        compiler_params=_params("arbitrary"),
        name="moe_router",
    )(h, ya, yb, yc, w_out, g, rw, rb)

    counts = cnt[0, :N_EXPERTS].astype(jnp.int32)
    padded = ((counts + tg - 1) // tg) * tg
    ends = jnp.cumsum(padded)
    offs = ends - padded
    rows = TOP_K * n + N_EXPERTS * tg
    n_tiles = rows // tg
    tile_start = jnp.arange(n_tiles, dtype=jnp.int32) * tg
    n_used = (ends[-1:] // tg).astype(jnp.int32)
    tile_expert = jnp.sum(tile_start[:, None] >= ends[None, :], axis=1).astype(jnp.int32)
    tile_expert = jnp.minimum(tile_expert, tile_expert[n_used[0] - 1])
    onehot = lambda e: (e[:, None] == jnp.arange(N_EXPERTS, dtype=jnp.int32)[None, :]).astype(jnp.int32)
    dest = lambda e, r: (jnp.sum(onehot(e.astype(jnp.int32)) * offs[None, :], axis=1)
                         + r.astype(jnp.int32)).reshape(nt, 1, tm)
    d1 = dest(meta[:, META_E1], meta[:, META_R1])
    d2 = dest(meta[:, META_E2], meta[:, META_R2])
    dspec = pl.BlockSpec((1, 1, tm), lambda i: (i, 0, 0), memory_space=pltpu.SMEM)

    xs = pl.pallas_call(
        _dispatch_kernel,
        grid=(nt,),
        in_specs=[dspec, dspec, tok(D_MODEL), pl.BlockSpec(memory_space=pl.ANY)],
        out_specs=pl.BlockSpec(memory_space=pl.ANY),
        out_shape=jax.ShapeDtypeStruct((rows, D_MODEL), F32),
        scratch_shapes=[pltpu.SemaphoreType.DMA(())],
        input_output_aliases={3: 0},
        compiler_params=_params("arbitrary"),
        name="moe_dispatch",
    )(d1, d2, hn, jnp.zeros((rows, D_MODEL), F32))

    wspec = lambda shape: pl.BlockSpec(shape, lambda i, te, nu: (te[i], 0, 0))
    ys = pl.pallas_call(
        _grouped_kernel,
        grid_spec=pltpu.PrefetchScalarGridSpec(
            num_scalar_prefetch=2,
            grid=(n_tiles,),
            in_specs=[pl.BlockSpec((tg, D_MODEL), lambda i, te, nu: (jnp.minimum(i, nu[0] - 1), 0)),
                      wspec((1, D_MODEL, D_FF)), wspec((1, D_MODEL, D_FF)), wspec((1, D_FF, D_MODEL))],
            out_specs=pl.BlockSpec((tg, D_MODEL), lambda i, te, nu: (i, 0))),
        out_shape=jax.ShapeDtypeStruct((rows, D_MODEL), F32),
        compiler_params=_params("arbitrary"),
        name="moe_grouped",
    )(tile_expert, n_used, xs, w1, w3, w2)

    return pl.pallas_call(
        _combine_kernel,
        grid=(nt,),
        in_specs=[dspec, dspec, tok(D_MODEL), tok(LANES), pl.BlockSpec(memory_space=pl.ANY)],
        out_specs=tok(D_MODEL),
        out_shape=jax.ShapeDtypeStruct((n, D_MODEL), F32),
        scratch_shapes=[pltpu.VMEM((tm, D_MODEL), F32), pltpu.VMEM((tm, D_MODEL), F32),
                        pltpu.SemaphoreType.DMA(())],
        compiler_params=_params("arbitrary"),
        name="moe_combine",
    )(d1, d2, h, meta, ys)


def _ple_update(h_ref, p_ref, g_ref, wg_ref, wp_ref):
    h = h_ref[...]
    gate = _sigmoid(jnp.dot(_rms(h, g_ref[...]).astype(BF16), wg_ref[...], preferred_element_type=F32))
    return h + jnp.dot(p_ref[...].astype(BF16), wp_ref[...], preferred_element_type=F32) * gate


def _ple_final_kernel(h_ref, p_ref, g_ref, wg_ref, wp_ref, gf_ref, o_ref):
    o_ref[...] = _rms(_ple_update(h_ref, p_ref, g_ref, wg_ref, wp_ref), gf_ref[...])


def _ple_proj_kernel(h_ref, p_ref, g_ref, wg_ref, wp_ref, gn_ref, wn_ref, o_ref, z_ref):
    h = _ple_update(h_ref, p_ref, g_ref, wg_ref, wp_ref)
    o_ref[...] = h
    z_ref[...] = jnp.dot(_rms(h, gn_ref[...]).astype(BF16), wn_ref[...],
                         preferred_element_type=F32).astype(z_ref.dtype)


def _ple(h, p, g, wg, wp, g_tail, w_next=None):
    n = h.shape[0]
    tm = min(TM_PROJ, n)
    tok = lambda width: pl.BlockSpec((tm, width), lambda i: (i, 0))
    const = lambda a: pl.BlockSpec(a.shape, lambda i: (0, 0))
    common = dict(grid=(n // tm,), compiler_params=_params("parallel"))
    in_specs = [tok(D_MODEL), tok(PLE_DIM), const(g), const(wg), const(wp), const(g_tail)]
    h_sds = jax.ShapeDtypeStruct((n, D_MODEL), F32)
    if w_next is None:
        return pl.pallas_call(_ple_final_kernel, in_specs=in_specs, out_specs=tok(D_MODEL), out_shape=h_sds,
                              name="ple_final", **common)(h, p, g, wg, wp, g_tail)
    return pl.pallas_call(_ple_proj_kernel, in_specs=in_specs + [const(w_next)],
                          out_specs=[tok(D_MODEL), tok(Z_W)],
                          out_shape=[h_sds, jax.ShapeDtypeStruct((n, Z_W), Z_DTYPE)],
                          name="ple_proj_in", **common)(h, p, g, wg, wp, g_tail, w_next)


def _q_perm():
    idx = []
    for j in range(Q_PER_GROUP):
        for g in range(KV_GROUPS):
            base = (g * Q_PER_GROUP + j) * HEAD_DIM
            idx.extend(range(base, base + HEAD_DIM))
    return jnp.asarray(idx, dtype=jnp.int32)


def _layout_w_in(w):
    a1 = 2 * A_WIDTH
    b1 = a1 + 4 * B_WIDTH
    q1 = b1 + C_WIDTH
    w = jnp.concatenate([w[:, a1:b1], w[:, :a1], w[:, b1:q1][:, _q_perm()], w[:, q1:]], axis=1)
    return jnp.pad(w, ((0, 0), (0, Z_W - w.shape[1]))).astype(BF16)


def _layout_w_out(w):
    c0 = A_WIDTH + B_WIDTH
    return jnp.concatenate([w[:c0], w[c0:][_q_perm()]], axis=0).astype(BF16)


def _mixers(z, bn, s, tabs, gm_ln_g, gm_ln_b, gm_ws, gm_bs,
            rw_mu, rw_w0, rw_w_up, rw_a0, rw_a_up, rw_g_up, rw_k_k, rw_k_a, rw_r_k, rw_gn_g, rw_gn_b,
            nsa_cmp_pos, nsa_kc_w1, nsa_kc_w2, nsa_vc_w1, nsa_vc_w2):
    n = bn * s
    z3 = z.reshape(bn, s, Z_W)
    y_a = _gmlp(z3, gm_ln_g, gm_ln_b, gm_ws, gm_bs)
    y_b = _rwkv(z3, rw_mu, rw_w0, rw_w_up, rw_a0, rw_a_up, rw_g_up, rw_k_k, rw_k_a, rw_r_k, rw_gn_g, rw_gn_b)
    q_raw, q_rot, kc, vc, ks, vs, kw, vw, gates = _nsa_prep(z3, tabs)
    k_cmp, v_cmp = _nsa_compress(kc, vc, nsa_cmp_pos, nsa_kc_w1, nsa_kc_w2, nsa_vc_w1, nsa_vc_w2)
    y_c = _nsa_attn(q_raw, q_rot, k_cmp, v_cmp, ks, vs, kw, vw, gates)
    return y_a.reshape(n, -1), y_b.reshape(n, -1), y_c.reshape(n, -1)


def kernel(x, p, positions, g_mix, w_in, w_out, gm_ln_g, gm_ln_b, gm_ws, gm_bs, rw_mu, rw_w0, rw_w_up, rw_a0,
           rw_a_up, rw_g_up, rw_k_k, rw_k_a, rw_r_k, rw_gn_g, rw_gn_b, nsa_cmp_pos, nsa_kc_w1, nsa_kc_w2,
           nsa_vc_w1, nsa_vc_w2, g_ffn, ffn_w1, ffn_w3, ffn_w2, router_w, router_b, moe_w1, moe_w3, moe_w2,
           g_ple, ple_gate_w, ple_proj_w, g_final):
    bn, s, _ = x.shape
    n = bn * s
    depth = g_mix.shape[0]
    tabs = _rope_tables(positions)
    h = x.reshape(n, D_MODEL)
    z = _proj_in(h, g_mix[0].reshape(1, -1), _layout_w_in(w_in[0]))
    for i in range(depth):
        ys = _mixers(z, bn, s, tabs, gm_ln_g[i], gm_ln_b[i], gm_ws[i], gm_bs[i],
                     rw_mu[i], rw_w0[i], rw_w_up[i], rw_a0[i], rw_a_up[i], rw_g_up[i], rw_k_k[i], rw_k_a[i],
                     rw_r_k[i], rw_gn_g[i], rw_gn_b[i], nsa_cmp_pos[i], nsa_kc_w1[i], nsa_kc_w2[i],
                     nsa_vc_w1[i], nsa_vc_w2[i])
        wo = _layout_w_out(w_out[i])
        j = i // 2
        if i % 2 == 0:
            h = _ffn(h, *ys, wo, g_ffn[i].reshape(1, -1), ffn_w1[j:j + 1].astype(BF16),
                     ffn_w3[j:j + 1].astype(BF16), ffn_w2[j:j + 1].astype(BF16))
        else:
            h = _moe(h, *ys, wo, g_ffn[i].reshape(1, -1), router_w[j], router_b[j], moe_w1[j].astype(BF16),
                     moe_w3[j].astype(BF16), moe_w2[j].astype(BF16))
        ple_args = (p[i].reshape(n, PLE_DIM), g_ple[i].reshape(1, -1), ple_gate_w[i].astype(BF16),
                    ple_proj_w[i].astype(BF16))
        if i == depth - 1:
            h = _ple(h, *ple_args, g_final.reshape(1, -1))
        else:
            h, z = _ple(h, *ple_args, g_mix[i + 1].reshape(1, -1), _layout_w_in(w_in[i + 1]))
    return h.reshape(bn, s, D_MODEL)
```

```python
import functools
import math

import jax
import jax.numpy as jnp
from jax import lax
from jax.experimental import pallas as pl
from jax.experimental.pallas import tpu as pltpu

F32 = jnp.float32
BF16 = jnp.bfloat16

D_MODEL = 1024
HEAD_DIM = 64
A_HEADS = 4
B_HEADS = 4
A_WIDTH = 256
B_WIDTH = 256
C_WIDTH = 512
CHUNK = 128
LN_EPS = 1e-5
GN_EPS = 64e-5
RMS_EPS = 1e-6
RW_LORA = 64
KV_GROUPS = 2
Q_PER_GROUP = 4
CMP_BLOCK = 32
CMP_STRIDE = 16
CMP_HIDDEN = 128
SLC_BLOCK = 64
SLC_TOPK = 16
WINDOW = 512
Q_BLOCK = 128
NEG_INF = -1e30
FORCE = 1e4
ROPE_THETA = 500000.0
ROPE_DIM = 16
D_FF = 2816
N_EXPERTS = 8
TOP_K = 2
PLE_DIM = 256

LANES = 128
Z_B = 0
Z_A = 1024
Z_Q = 1536
Z_KV = 2048
Z_G = 2816
Z_W = 2944
Z_DTYPE = BF16

VMEM_LIMIT = 56 * 1024 * 1024
TM_PROJ = 512
TM_FFN = 512
TM_GROUP = 256
GM_T = 512
RW_T = 512
RW_C = 64
TK_SEL = 512

_NN = (((1,), (0,)), ((), ()))
_NT = (((1,), (1,)), ((), ()))


def _params(*sem):
    return pltpu.CompilerParams(dimension_semantics=sem, vmem_limit_bytes=VMEM_LIMIT)


def _dot(a, b, dn=_NN, passes=1):
    if passes == 6:
        return lax.dot_general(a.astype(F32), b.astype(F32), dn, precision=lax.Precision.HIGHEST,
                               preferred_element_type=F32)
    a_hi = a.astype(BF16)
    b_hi = b.astype(BF16)
    out = lax.dot_general(a_hi, b_hi, dn, preferred_element_type=F32)
    if passes == 1:
        return out
    a_lo = (a - a_hi.astype(F32)).astype(BF16)
    out = out + lax.dot_general(a_lo, b_hi, dn, preferred_element_type=F32)
    if passes == 2:
        return out
    b_lo = (b - b_hi.astype(F32)).astype(BF16)
    return out + lax.dot_general(a_hi, b_lo, dn, preferred_element_type=F32)


def _rms(x, g):
    return x * lax.rsqrt(jnp.mean(x * x, axis=-1, keepdims=True) + RMS_EPS) * g


def _sigmoid(x):
    return 1.0 / (1.0 + jnp.exp(-x))


def _div(x, d):
    return lax.shift_right_logical(x, jnp.int32(int(math.log2(d))))


def _head_block_ones(n):
    r = lax.broadcasted_iota(jnp.int32, (n, n), 0)
    c = lax.broadcasted_iota(jnp.int32, (n, n), 1)
    return jnp.where(_div(r, HEAD_DIM) == _div(c, HEAD_DIM), 1.0, 0.0).astype(F32)


def _rope_kernel(inv_ref, pos_ref, cos_ref, sin_ref):
    p = pos_ref[0].astype(F32)
    for j in range(ROPE_DIM // 2):
        ang = p * inv_ref[j]
        cos_ref[0, j] = jnp.cos(ang)
        sin_ref[0, j] = jnp.sin(ang)


def _rope_tables(positions):
    bn, s = positions.shape
    half = ROPE_DIM // 2
    inv = 1.0 / (ROPE_THETA ** (jnp.arange(0, ROPE_DIM, 2, dtype=F32) / ROPE_DIM))
    pos3 = positions.reshape(bn, s // LANES, LANES)
    cos, sin = pl.pallas_call(
        _rope_kernel,
        grid=(bn,),
        in_specs=[pl.BlockSpec(memory_space=pltpu.SMEM),
                  pl.BlockSpec((1, s // LANES, LANES), lambda b: (b, 0, 0))],
        out_specs=[pl.BlockSpec((1, half, s // LANES, LANES), lambda b: (b, 0, 0, 0))] * 2,
        out_shape=[jax.ShapeDtypeStruct((bn, half, s // LANES, LANES), F32)] * 2,
        compiler_params=_params("parallel"),
        name="rope_tables",
    )(inv, pos3)
    cos = cos.reshape(bn, half, s).transpose(0, 2, 1)
    sin = sin.reshape(bn, half, s).transpose(0, 2, 1)
    one = jnp.ones((bn, s, HEAD_DIM - ROPE_DIM), F32)
    zero = jnp.zeros((bn, s, HEAD_DIM - half), F32)
    tab_c = jnp.concatenate([cos, cos, one], axis=-1)
    tab_lo = jnp.concatenate([sin, zero], axis=-1)
    tab_hi = jnp.concatenate([zero[..., :half], sin, zero[..., :HEAD_DIM - ROPE_DIM]], axis=-1)
    rep = lambda t: jnp.tile(t, (1, 1, LANES // HEAD_DIM))
    return rep(tab_c), rep(tab_lo), rep(tab_hi)


def _proj_in_kernel(h_ref, g_ref, w_ref, z_ref):
    y = _rms(h_ref[...], g_ref[...])
    z_ref[...] = jnp.dot(y.astype(BF16), w_ref[...], preferred_element_type=F32).astype(z_ref.dtype)


def _proj_in(h, g, w):
    n = h.shape[0]
    tm = min(TM_PROJ, n)
    return pl.pallas_call(
        _proj_in_kernel,
        grid=(n // tm,),
        in_specs=[pl.BlockSpec((tm, D_MODEL), lambda i: (i, 0)),
                  pl.BlockSpec((1, D_MODEL), lambda i: (0, 0)),
                  pl.BlockSpec((D_MODEL, Z_W), lambda i: (0, 0))],
        out_specs=pl.BlockSpec((tm, Z_W), lambda i: (i, 0)),
        out_shape=jax.ShapeDtypeStruct((n, Z_W), Z_DTYPE),
        compiler_params=_params("parallel"),
        name="proj_in",
    )(h, g, w)


def _gmlp_kernel(z_ref, lng_ref, lnb_ref, ws_ref, bias_ref, o_ref):
    gz = jax.nn.gelu(z_ref[0].astype(F32))
    u = gz[:, :A_WIDTH]
    v = gz[:, A_WIDTH:]
    ones = _head_block_ones(A_WIDTH)
    mu = _dot(v, ones, passes=2) * (1.0 / HEAD_DIM)
    d = v - mu
    var = _dot(d * d, ones, passes=2) * (1.0 / HEAD_DIM)
    vn = d * lax.rsqrt(var + LN_EPS) * lng_ref[...] + lnb_ref[...]
    r = lax.broadcasted_iota(jnp.int32, (CHUNK, CHUNK), 0)
    c = lax.broadcasted_iota(jnp.int32, (CHUNK, CHUNK), 1)
    lane_head = _div(lax.broadcasted_iota(jnp.int32, (CHUNK, A_WIDTH), 1), HEAD_DIM)
    w_causal = [jnp.where(c <= r, ws_ref[hd], 0.0).astype(BF16) for hd in range(A_HEADS)]
    for ck in range(vn.shape[0] // CHUNK):
        rows = slice(ck * CHUNK, (ck + 1) * CHUNK)
        mixed = bias_ref[...]
        for hd in range(A_HEADS):
            mixed = mixed + _dot(w_causal[hd], jnp.where(lane_head == hd, vn[rows], 0.0))
        o_ref[0, rows, :] = (u[rows] * mixed).astype(o_ref.dtype)


def _gmlp(z3, ln_g, ln_b, w_s, b_s):
    bn, s, _ = z3.shape
    t_len = min(GM_T, s)
    bias = jnp.repeat(b_s.T, HEAD_DIM, axis=1)
    return pl.pallas_call(
        _gmlp_kernel,
        grid=(bn, s // t_len),
        in_specs=[pl.BlockSpec((1, t_len, 2 * A_WIDTH), lambda b, i: (b, i, Z_A // (2 * A_WIDTH))),
                  pl.BlockSpec((1, A_WIDTH), lambda b, i: (0, 0)),
                  pl.BlockSpec((1, A_WIDTH), lambda b, i: (0, 0)),
                  pl.BlockSpec((A_HEADS, CHUNK, CHUNK), lambda b, i: (0, 0, 0)),
                  pl.BlockSpec((CHUNK, A_WIDTH), lambda b, i: (0, 0))],
        out_specs=pl.BlockSpec((1, t_len, A_WIDTH), lambda b, i: (b, i, 0)),
        out_shape=jax.ShapeDtypeStruct((bn, s, A_WIDTH), BF16),
        compiler_params=_params("parallel", "parallel"),
        name="gmlp",
    )(z3, ln_g.reshape(1, A_WIDTH), ln_b.reshape(1, A_WIDTH), w_s, bias)


RW_PASSES = 1


def _rwkv_chunks(r, lw, k, v, kk, a, s0, c_len):
    nh = B_HEADS
    m = nh * c_len
    chunks = range(r.shape[0] // c_len)
    dot = functools.partial(_dot, passes=RW_PASSES)
    ti = lax.broadcasted_iota(jnp.int32, (c_len, c_len), 0)
    tj = lax.broadcasted_iota(jnp.int32, (c_len, c_len), 1)
    tril = jnp.where(tj <= ti, 1.0, 0.0).astype(F32)
    row_head = _div(lax.broadcasted_iota(jnp.int32, (m, B_WIDTH), 0), c_len)
    lane_head = _div(lax.broadcasted_iota(jnp.int32, (m, B_WIDTH), 1), HEAD_DIM)
    head_mask = row_head == lane_head
    rep = lambda x: jnp.concatenate([x] * nh, axis=0)
    stack = lambda x: jnp.where(head_mask, rep(x), 0.0)
    ri = lax.broadcasted_iota(jnp.int32, (m, m), 0)
    ci = lax.broadcasted_iota(jnp.int32, (m, m), 1)
    same = _div(ri, c_len) == _div(ci, c_len)
    strict = same & (ci < ri)
    incl = same & (ci <= ri)
    eye = ri == ci
    assert 2 * c_len == LANES
    low = lax.broadcasted_iota(jnp.int32, (2 * m, LANES), 1) < c_len

    a_st, r_st, v_st, bw_st, kw_st, w_c, l_ab, l_ak, l_rb, l_rk = ([] for _ in range(10))
    for c in chunks:
        sl = slice(c * c_len, (c + 1) * c_len)
        cum = _dot(tril, lw[sl], passes=3)
        w_t = jnp.exp(cum)
        w_i = jnp.exp(-cum)
        a_t = -kk[sl] * jnp.exp(cum - lw[sl])
        b_t = kk[sl] * a[sl] * w_i
        k_t = k[sl] * w_i
        w_c.append(w_t[c_len - 1:c_len, :])
        a_st.append(stack(a_t))
        r_st.append(stack(r[sl] * w_t))
        v_st.append(stack(v[sl]))
        bw_st.append(stack(b_t * w_c[c]))
        kw_st.append(stack(k_t * w_c[c]))
        cross = dot(jnp.concatenate([a_st[c], r_st[c]], axis=0), jnp.concatenate([b_t, k_t], axis=0), _NT)
        swapped = pltpu.roll(cross, c_len, axis=1)
        vs_b = jnp.concatenate([jnp.where(low, cross, swapped)] * (nh // 2), axis=1)
        vs_k = jnp.concatenate([jnp.where(low, swapped, cross)] * (nh // 2), axis=1)
        l_ab.append(jnp.where(strict, vs_b[:m], 0.0))
        l_ak.append(jnp.where(strict, vs_k[:m], 0.0))
        l_rb.append(jnp.where(incl, vs_b[m:], 0.0))
        l_rk.append(jnp.where(incl, vs_k[m:], 0.0))

    p_inv = [jnp.where(eye, 1.0, 0.0) + l_ab[c] for c in chunks]
    pw = l_ab
    for _ in range(int(math.log2(c_len)) - 1):
        pw = [dot(pw[c], pw[c]) for c in chunks]
        p_inv = [p_inv[c] + dot(p_inv[c], pw[c]) for c in chunks]
    t_m = [dot(l_ak[c], v_st[c]) for c in chunks]
    q_m = [dot(p_inv[c], t_m[c]) for c in chunks]
    p_m = [dot(p_inv[c], a_st[c]) for c in chunks]
    g_m = [r_st[c] + dot(l_rb[c], p_m[c]) for c in chunks]
    h_m = [dot(l_rb[c], q_m[c]) + dot(l_rk[c], v_st[c]) for c in chunks]
    m_m = [jnp.where(eye, w_c[c], 0.0) + dot(p_m[c].T, bw_st[c]) for c in chunks]
    n_m = [dot(q_m[c].T, bw_st[c]) + dot(v_st[c].T, kw_st[c]) for c in chunks]

    states = [s0]
    for c in chunks:
        states.append(dot(states[c], m_m[c]) + n_m[c])
    ys = []
    for c in chunks:
        y_st = dot(g_m[c], states[c], _NT) + h_m[c]
        y = y_st[0:c_len]
        for hd in range(1, nh):
            y = y + y_st[hd * c_len:(hd + 1) * c_len]
        ys.append(y)
    return jnp.concatenate(ys, axis=0), states[-1]


def _rwkv_kernel(z_ref, mu_ref, w0_ref, a0_ref, wwa_ref, gup_ref, kk_ref, ka_ref, rk_ref, gng_ref, gnb_ref,
                 o_ref, carry_ref, state_ref, *, c_len):
    @pl.when(pl.program_id(1) == 0)
    def _():
        carry_ref[...] = jnp.zeros_like(carry_ref)
        state_ref[...] = jnp.zeros_like(state_ref)

    zb = z_ref[0].astype(F32)
    t_len = zb.shape[0]
    row = lax.broadcasted_iota(jnp.int32, zb.shape, 0)
    z_prev = jnp.where(row == 0, carry_ref[...], pltpu.roll(zb, 1, axis=0))
    carry_ref[...] = zb[t_len - 1:t_len, :]
    zz = zb + (z_prev - zb) * mu_ref[...]
    r = zz[:, 0:B_WIDTH]
    k = zz[:, B_WIDTH:2 * B_WIDTH]
    v = zz[:, 2 * B_WIDTH:3 * B_WIDTH]
    wa = zz[:, 3 * B_WIDTH:3 * B_WIDTH + 2 * RW_LORA]
    gd = zz[:, 3 * B_WIDTH + 2 * RW_LORA:]
    lane = lax.broadcasted_iota(jnp.int32, wa.shape, 1)
    proj = _dot(jnp.where(lane < RW_LORA, jnp.tanh(wa), wa), wwa_ref[...])
    x = -(w0_ref[...] + proj[:, :B_WIDTH])
    softplus = jnp.maximum(x, 0.0) + jnp.log(1.0 + jnp.exp(-jnp.abs(x)))
    lw = -jnp.exp(-softplus - 0.5)
    a = _sigmoid(a0_ref[...] + proj[:, B_WIDTH:])
    g = _dot(_sigmoid(gd), gup_ref[...])
    ones = _head_block_ones(B_WIDTH)
    kk = k * kk_ref[...]
    kk = kk * lax.rsqrt(jnp.maximum(_dot(kk * kk, ones, passes=2), 1e-24))
    k2 = k * (1.0 + (a - 1.0) * ka_ref[...])

    y, state_ref[...] = _rwkv_chunks(r, lw, k2, v, kk, a, state_ref[...], c_len)

    mu_y = _dot(y, ones, passes=2) * (1.0 / HEAD_DIM)
    d = y - mu_y
    var = _dot(d * d, ones, passes=2) * (1.0 / HEAD_DIM)
    yn = d * lax.rsqrt(var + GN_EPS) * gng_ref[...] + gnb_ref[...]
    bonus = _dot(r * k2 * rk_ref[...], ones, passes=2) * v
    o_ref[0] = ((yn + bonus) * g).astype(o_ref.dtype)


def _rwkv(z3, mu, w0, w_up, a0, a_up, g_up, k_k, k_a, r_k, gn_g, gn_b):
    bn, s, _ = z3.shape
    t_len = min(RW_T, s)
    zero = jnp.zeros((RW_LORA, B_WIDTH), F32)
    wwa = jnp.concatenate([jnp.concatenate([w_up, zero], axis=1),
                           jnp.concatenate([zero, a_up], axis=1)], axis=0).astype(BF16)
    row = lambda t: t.reshape(1, -1)
    vec = pl.BlockSpec((1, B_WIDTH), lambda b, i: (0, 0))
    return pl.pallas_call(
        functools.partial(_rwkv_kernel, c_len=RW_C),
        grid=(bn, s // t_len),
        in_specs=[pl.BlockSpec((1, t_len, 4 * B_WIDTH), lambda b, i: (b, i, Z_B // (4 * B_WIDTH))),
                  pl.BlockSpec((1, 4 * B_WIDTH), lambda b, i: (0, 0)),
                  vec, vec,
                  pl.BlockSpec((2 * RW_LORA, 2 * B_WIDTH), lambda b, i: (0, 0)),
                  pl.BlockSpec((2 * RW_LORA, B_WIDTH), lambda b, i: (0, 0)),
                  vec, vec, vec, vec, vec],
        out_specs=pl.BlockSpec((1, t_len, B_WIDTH), lambda b, i: (b, i, 0)),
        out_shape=jax.ShapeDtypeStruct((bn, s, B_WIDTH), BF16),
        scratch_shapes=[pltpu.VMEM((1, 4 * B_WIDTH), F32), pltpu.VMEM((B_WIDTH, B_WIDTH), F32)],
        compiler_params=_params("parallel", "arbitrary"),
        name="rwkv7",
    )(z3, row(mu), row(w0), row(a0), wwa, g_up.astype(BF16), row(k_k), row(k_a), row(r_k), row(gn_g), row(gn_b))


def _rope(x, tab_c, tab_lo, tab_hi):
    n = x.shape[-1]
    half = ROPE_DIM // 2
    return x * tab_c - pltpu.roll(x, n - half, axis=1) * tab_lo + pltpu.roll(x, half, axis=1) * tab_hi


N_HEADS_C = KV_GROUPS * Q_PER_GROUP
QT_COLS = N_HEADS_C * Q_BLOCK
GATE_ROWS = 32
ONES_ROWS = 16
LOG2E = 1.4426950408889634


def _nsa_prep_kernel(zq_ref, zkc_ref, zks_ref, zkw_ref, zg_ref, tc_ref, tl_ref, th_ref,
                     qraw_ref, qrot_ref, kc_ref, vc_ref, ks_ref, vst_ref, kw_ref, vwt_ref, gate_ref):
    tab_c, tab_lo, tab_hi = tc_ref[0], tl_ref[0], th_ref[0]
    nrep = C_WIDTH // LANES
    wide = lambda t: jnp.concatenate([t] * nrep, axis=1)
    zkc, zks, zkw = (ref[0].astype(F32) for ref in (zkc_ref, zks_ref, zkw_ref))
    q = zq_ref[0].astype(F32) * (HEAD_DIM ** -0.5 * LOG2E)
    q_rot = _rope(q, wide(tab_c), wide(tab_lo), wide(tab_hi))
    ts = q.shape[0]
    row_group0 = lax.broadcasted_iota(jnp.int32, (LANES, ts), 0) < HEAD_DIM

    def put_queries(ref, x):
        for j in range(Q_PER_GROUP):
            xt = x[:, j * LANES:(j + 1) * LANES].T
            for g in range(KV_GROUPS):
                keep = row_group0 if g == 0 else jnp.logical_not(row_group0)
                xm = jnp.where(keep, xt, 0.0).astype(BF16)
                for qq in range(ts // Q_BLOCK):
                    col = qq * QT_COLS + (g * Q_PER_GROUP + j) * Q_BLOCK
                    ref[0, :, col:col + Q_BLOCK] = xm[:, qq * Q_BLOCK:(qq + 1) * Q_BLOCK]

    put_queries(qraw_ref, q)
    put_queries(qrot_ref, q_rot)
    kc_ref[0] = zkc[:, :LANES].astype(BF16)
    vc_ref[0] = zkc[:, LANES:].astype(BF16)
    ks_ref[0, :, :LANES] = _rope(zks[:, :LANES], tab_c, tab_lo, tab_hi).astype(BF16)
    key_block = _div(pl.program_id(1) * ts + lax.broadcasted_iota(jnp.int32, (ts, LANES), 0), SLC_BLOCK)
    ks_ref[0, :, LANES:] = jnp.where(key_block == lax.broadcasted_iota(jnp.int32, (ts, LANES), 1), 1.0, 0.0).astype(BF16)
    vst_ref[0, 0] = zks[:, LANES:].T.astype(BF16)
    kw_ref[0] = _rope(zkw[:, :LANES], tab_c, tab_lo, tab_hi).astype(BF16)
    vwt = zkw[:, LANES:].T.astype(BF16)
    for qq in range(ts // Q_BLOCK):
        vwt_ref[0, qq] = vwt[:, qq * Q_BLOCK:(qq + 1) * Q_BLOCK]
    gate_ref[0] = _sigmoid(zg_ref[0].astype(F32)).T[:GATE_ROWS, :]


def _nsa_prep(z3, tabs):
    bn, s, _ = z3.shape
    ts = TK_SEL
    zspec = lambda width, off: pl.BlockSpec((1, ts, width), lambda b, i: (b, i, off // width))
    tspec = pl.BlockSpec((1, ts, LANES), lambda b, i: (b, i, 0))
    rowmajor = pl.BlockSpec((1, ts, LANES), lambda b, i: (b, i, 0))
    qspec = pl.BlockSpec((1, LANES, N_HEADS_C * ts), lambda b, i: (b, 0, i))
    rm_sds = jax.ShapeDtypeStruct((bn, s, LANES), BF16)
    q_sds = jax.ShapeDtypeStruct((bn, LANES, N_HEADS_C * s), BF16)
    return pl.pallas_call(
        _nsa_prep_kernel,
        grid=(bn, s // ts),
        in_specs=[zspec(C_WIDTH, Z_Q), zspec(2 * LANES, Z_KV), zspec(2 * LANES, Z_KV + 2 * LANES),
                  zspec(2 * LANES, Z_KV + 4 * LANES), zspec(LANES, Z_G), tspec, tspec, tspec],
        out_specs=[qspec, qspec, rowmajor, rowmajor, pl.BlockSpec((1, ts, 2 * LANES), lambda b, i: (b, i, 0)),
                   pl.BlockSpec((1, 1, LANES, ts), lambda b, i: (b, i, 0, 0)), rowmajor,
                   pl.BlockSpec((1, ts // Q_BLOCK, LANES, Q_BLOCK), lambda b, i: (b, i, 0, 0)),
                   pl.BlockSpec((1, GATE_ROWS, ts), lambda b, i: (b, 0, i))],
        out_shape=[q_sds, q_sds, rm_sds, rm_sds, jax.ShapeDtypeStruct((bn, s, 2 * LANES), BF16),
                   jax.ShapeDtypeStruct((bn, s // ts, LANES, ts), BF16), rm_sds,
                   jax.ShapeDtypeStruct((bn, s // Q_BLOCK, LANES, Q_BLOCK), BF16),
                   jax.ShapeDtypeStruct((bn, GATE_ROWS, s), F32)],
        compiler_params=_params("parallel", "parallel"),
        name="nsa_prep",
    )(z3, z3, z3, z3, z3, *tabs)


def _nsa_compress_kernel(xk_ref, xv_ref, pos_ref, kw1a_ref, kw1b_ref, kw2_ref, vw1a_ref, vw1b_ref, vw2_ref,
                         ko_ref, vo_ref):
    n = xk_ref.shape[1]
    pos_a = pos_ref[0]
    pos_b = pos_ref[1]
    for x_ref, w1a, w1b, w2, o_ref, transposed in ((xk_ref, kw1a_ref, kw1b_ref, kw2_ref, ko_ref, False),
                                                    (xv_ref, vw1a_ref, vw1b_ref, vw2_ref, vo_ref, True)):
        x = x_ref[0]
        first = _dot(x, w1a[...])
        second = _dot(x, w1b[...])
        pc = _dot(pos_a, w1a[...]) + _dot(pos_b, w1b[...])
        hid = jax.nn.gelu(first + pltpu.roll(second, n - 1, axis=0) + pc[0:1, :])
        out = _dot(hid, w2[...])
        o_ref[0] = (out.T if transposed else out).astype(o_ref.dtype)


def _expand_cmp_weights(w1, w2):
    half = CMP_BLOCK // 2
    w1r = w1.reshape(CMP_BLOCK, HEAD_DIM, CMP_HIDDEN)
    eye = jnp.eye(KV_GROUPS, dtype=F32)
    w1e = jnp.einsum('ldh,gk->lgdkh', w1r, eye).reshape(CMP_BLOCK * LANES, KV_GROUPS * CMP_HIDDEN)
    w2e = jnp.einsum('hd,gk->ghkd', w2, eye).reshape(KV_GROUPS * CMP_HIDDEN, LANES)
    return (w1e[:half * LANES].astype(BF16), w1e[half * LANES:].astype(BF16), w2e.astype(BF16))


def _nsa_compress(kc, vc, cmp_pos, kc_w1, kc_w2, vc_w1, vc_w2):
    bn, s, _ = kc.shape
    n = s // CMP_STRIDE
    xw = CMP_STRIDE * LANES
    xk = kc.reshape(bn, n, xw)
    xv = vc.reshape(bn, n, xw)
    pos = jnp.tile(cmp_pos[:, None, :], (1, KV_GROUPS, 1)).reshape(2, 1, xw)
    pos = jnp.broadcast_to(pos, (2, 8, xw)).astype(BF16)
    kw = _expand_cmp_weights(kc_w1, kc_w2)
    vw = _expand_cmp_weights(vc_w1, vc_w2)
    full = lambda a: pl.BlockSpec(a.shape, lambda b: (0,) * a.ndim)
    xspec = pl.BlockSpec((1, n, xw), lambda b: (b, 0, 0))
    return pl.pallas_call(
        _nsa_compress_kernel,
        grid=(bn,),
        in_specs=[xspec, xspec, full(pos)] + [full(a) for a in kw + vw],
        out_specs=[pl.BlockSpec((1, n, LANES), lambda b: (b, 0, 0)), pl.BlockSpec((1, LANES, n), lambda b: (b, 0, 0))],
        out_shape=[jax.ShapeDtypeStruct((bn, n, LANES), BF16), jax.ShapeDtypeStruct((bn, LANES, n), BF16)],
        compiler_params=_params("parallel"),
        name="nsa_compress",
    )(xk, xv, pos, *kw, *vw)


def _nsa_attn_kernel(qraw_ref, qrot_ref, kcmp_ref, vcmpt_ref, ks_ref, vst_ref, kw_ref, vwt_ref, gate_ref, o_ref,
                     m_ref, acc_ref, sa_ref, sb_ref, qa_ref):
    tq = Q_BLOCK
    gw = Q_PER_GROUP * tq
    seq = ks_ref.shape[1]
    n_cmp = kcmp_ref.shape[1]
    n_slc = seq // SLC_BLOCK
    n_sel = min(SLC_TOPK, n_slc)
    qb = pl.program_id(1)
    t0 = qb * tq
    gcols = lambda g: slice(g * gw, (g + 1) * gw)
    grows = lambda g: slice(g * HEAD_DIM, (g + 1) * HEAD_DIM)
    jcols = lambda j: slice(j * tq, (j + 1) * tq)

    t_c = t0 + lax.broadcasted_iota(jnp.int32, (n_cmp, tq), 1)
    n_c = lax.broadcasted_iota(jnp.int32, (n_cmp, tq), 0)
    cmask = (n_c * CMP_STRIDE + (CMP_BLOCK - 1)) <= t_c
    cbias = jnp.where(cmask, 0.0, NEG_INF)
    o_c, p_sum = [], []
    for g in range(KV_GROUPS):
        s = _dot(kcmp_ref[0], qraw_ref[0, :, gcols(g)])
        parts, total = [], None
        for j in range(Q_PER_GROUP):
            sj = s[:, jcols(j)] + cbias
            e = jnp.exp2(sj - jnp.max(sj, axis=0, keepdims=True))
            p = jnp.where(cmask, e * (1.0 / jnp.sum(e, axis=0, keepdims=True)), 0.0)
            total = p if total is None else total + p
            parts.append(p.astype(BF16))
        p_sum.append(total)
        o_c.append(_dot(vcmpt_ref[0, grows(g), :], jnp.concatenate(parts, axis=1)))

    n_band = WINDOW // tq + 1
    band = n_band * tq
    wb = jnp.maximum(qb - WINDOW // tq, 0)
    w0 = pl.multiple_of(wb * tq, tq)
    t_w = t0 + lax.broadcasted_iota(jnp.int32, (band, tq), 1)
    k_w = w0 + lax.broadcasted_iota(jnp.int32, (band, tq), 0)
    wbias = jnp.where((k_w <= t_w) & (k_w > t_w - WINDOW), 0.0, NEG_INF).astype(BF16)
    kband = kw_ref[0, pl.ds(w0, band), :]
    o_w = []
    for g in range(KV_GROUPS):
        s = _dot(kband, qrot_ref[0, :, gcols(g)]).astype(BF16)
        parts = []
        for j in range(Q_PER_GROUP):
            sj = s[:, jcols(j)] + wbias
            parts.append(jnp.exp2(sj - jnp.max(sj, axis=0, keepdims=True)))
        vband = jnp.concatenate([vwt_ref[0, wb + i, grows(g), :] for i in range(n_band)], axis=1)
        vband = jnp.concatenate([vband, jnp.ones((ONES_ROWS, band), BF16)], axis=0)
        ow = _dot(vband, jnp.concatenate(parts, axis=1))
        o_w.append(ow[:HEAD_DIM] * (1.0 / ow[HEAD_DIM:HEAD_DIM + 1]))

    m_o = lax.broadcasted_iota(jnp.int32, (n_slc, n_cmp), 0) * SLC_BLOCK
    n_o = lax.broadcasted_iota(jnp.int32, (n_slc, n_cmp), 1) * CMP_STRIDE
    overlap_t = jnp.where((n_o < m_o + SLC_BLOCK) & (n_o + (CMP_BLOCK - 1) >= m_o), 1.0, 0.0).astype(F32)
    m_i = lax.broadcasted_iota(jnp.int32, (n_slc, tq), 0)
    blk_t = _div(t0 + lax.broadcasted_iota(jnp.int32, (n_slc, tq), 1), SLC_BLOCK)
    valid = m_i <= blk_t
    forced = (m_i == 0) | (m_i == blk_t) | (m_i == blk_t - 1)
    sel = []
    for g in range(KV_GROUPS):
        imp = _dot(overlap_t, p_sum[g], passes=3)
        imp = jnp.where(valid, imp + jnp.where(forced, FORCE, 0.0), -FORCE)
        rank = jnp.zeros((n_slc, tq), F32)
        for mp in range(n_slc):
            other = imp[mp:mp + 1, :]
            rank = rank + jnp.where(m_i > mp, jnp.where(other >= imp, 1.0, 0.0), jnp.where(other > imp, 1.0, 0.0))
        sel.append(jnp.where((rank < n_sel) & valid, 0.0, NEG_INF).astype(BF16))

    m_ref[...] = jnp.full(m_ref.shape, NEG_INF, BF16).astype(F32)
    acc_ref[...] = jnp.zeros(acc_ref.shape, F32)
    n_tiles = (t0 + tq + TK_SEL - 1) // TK_SEL
    pad = jnp.zeros((LANES - n_slc, gw), BF16)
    for g in range(KV_GROUPS):
        qa_ref[g] = jnp.concatenate([qrot_ref[0, :, gcols(g)], jnp.concatenate([sel[g]] * Q_PER_GROUP, axis=1), pad],
                                    axis=0)
    r0 = pl.multiple_of(t0 - (n_tiles - 1) * TK_SEL, tq)
    tri = jnp.where(lax.broadcasted_iota(jnp.int32, (tq, tq), 0) > lax.broadcasted_iota(jnp.int32, (tq, tq), 1),
                    NEG_INF, 0.0)

    def scores(kt, s_ref):
        k0 = pl.multiple_of(jnp.minimum(kt, n_tiles - 1) * TK_SEL, TK_SEL)
        keys = ks_ref[0, pl.ds(k0, TK_SEL), :]
        own = jnp.where(kt >= n_tiles - 1, tri, 0.0).astype(BF16)
        for g in range(KV_GROUPS):
            s_ref[g] = _dot(keys, qa_ref[g]).astype(BF16)
            for j in range(Q_PER_GROUP):
                s_ref[g, pl.ds(r0, tq), jcols(j)] += own

    def attend(kt, s_ref):
        for g in range(KV_GROUPS):
            parts, alphas = [], []
            for j in range(Q_PER_GROUP):
                cs = slice(g * gw + j * tq, g * gw + (j + 1) * tq)
                sj = s_ref[g, :, jcols(j)]
                m_old = m_ref[:, cs]
                m_new = jnp.maximum(m_old, jnp.max(sj, axis=0, keepdims=True).astype(F32))
                m_ref[:, cs] = m_new
                alphas.append(jnp.exp2(m_old - m_new))
                parts.append(jnp.exp2(sj - m_new.astype(BF16)))
            vals = jnp.concatenate([vst_ref[0, kt, grows(g), :], jnp.ones((ONES_ROWS, TK_SEL), BF16)], axis=0)
            acc_ref[g] = acc_ref[g] * jnp.concatenate(alphas, axis=1) + _dot(vals, jnp.concatenate(parts, axis=1))

    odd = lax.rem(n_tiles, 2)
    scores(0, sa_ref)

    @pl.when(odd == 1)
    def _():
        attend(0, sa_ref)
        scores(1, sa_ref)

    def sel_pair(i, carry):
        kt = odd + 2 * i
        scores(kt + 1, sb_ref)
        attend(kt, sa_ref)
        scores(kt + 2, sa_ref)
        attend(kt + 1, sb_ref)
        return carry

    lax.fori_loop(0, (n_tiles - odd) // 2, sel_pair, 0)

    gates = gate_ref[0]
    outs = []
    for j in range(Q_PER_GROUP):
        for g in range(KV_GROUPS):
            row = (g * Q_PER_GROUP + j) * 3
            acc = acc_ref[g][:, jcols(j)]
            o_s = acc[:HEAD_DIM] * (1.0 / acc[HEAD_DIM:HEAD_DIM + 1])
            outs.append(gates[row:row + 1, :] * o_c[g][:, jcols(j)] + gates[row + 1:row + 2, :] * o_s
                        + gates[row + 2:row + 3, :] * o_w[g][:, jcols(j)])
    o_ref[0] = jnp.concatenate(outs, axis=0).T.astype(o_ref.dtype)


def _nsa_attn(q_raw, q_rot, k_cmp, v_cmp_t, ks, vs_t, kw, vw_t, gates_t):
    bn, s, _ = ks.shape
    n_cmp = k_cmp.shape[1]
    qspec = pl.BlockSpec((1, LANES, QT_COLS), lambda b, i: (b, 0, i))
    kspec = pl.BlockSpec((1, s, LANES), lambda b, i: (b, 0, 0))
    whole = lambda a: pl.BlockSpec((1,) + a.shape[1:], lambda b, i: (b,) + (0,) * (a.ndim - 1))
    return pl.pallas_call(
        _nsa_attn_kernel,
        grid=(bn, s // Q_BLOCK),
        in_specs=[qspec, qspec, whole(k_cmp), whole(v_cmp_t), whole(ks), whole(vs_t), kspec, whole(vw_t),
                  pl.BlockSpec((1, GATE_ROWS, Q_BLOCK), lambda b, i: (b, 0, i))],
        out_specs=pl.BlockSpec((1, Q_BLOCK, C_WIDTH), lambda b, i: (b, i, 0)),
        out_shape=jax.ShapeDtypeStruct((bn, s, C_WIDTH), BF16),
        scratch_shapes=[pltpu.VMEM((1, QT_COLS), F32),
                        pltpu.VMEM((KV_GROUPS, HEAD_DIM + ONES_ROWS, Q_PER_GROUP * Q_BLOCK), F32),
                        pltpu.VMEM((KV_GROUPS, TK_SEL, Q_PER_GROUP * Q_BLOCK), BF16),
                        pltpu.VMEM((KV_GROUPS, TK_SEL, Q_PER_GROUP * Q_BLOCK), BF16),
                        pltpu.VMEM((KV_GROUPS, 2 * LANES, Q_PER_GROUP * Q_BLOCK), BF16)],
        compiler_params=_params("parallel", "arbitrary"),
        name="nsa_attn",
    )(q_raw, q_rot, k_cmp, v_cmp_t, ks, vs_t, kw, vw_t, gates_t)


def _mixed_residual(h_ref, ya_ref, yb_ref, yc_ref, w_ref):
    acc = jnp.dot(ya_ref[...], w_ref[0:A_WIDTH, :], preferred_element_type=F32)
    acc = acc + jnp.dot(yb_ref[...], w_ref[A_WIDTH:A_WIDTH + B_WIDTH, :], preferred_element_type=F32)
    acc = acc + jnp.dot(yc_ref[...], w_ref[A_WIDTH + B_WIDTH:, :], preferred_element_type=F32)
    return h_ref[...] + acc


def _mixed_specs(tm):
    tok = lambda width: pl.BlockSpec((tm, width), lambda i: (i, 0))
    return [tok(D_MODEL), tok(A_WIDTH), tok(B_WIDTH), tok(C_WIDTH), pl.BlockSpec((D_MODEL, D_MODEL), lambda i: (0, 0))]


def _swiglu_step(x, w1_ref, w3_ref, w2_ref):
    h1 = jnp.dot(x, w1_ref[0], preferred_element_type=F32)
    h3 = jnp.dot(x, w3_ref[0], preferred_element_type=F32)
    hid = h1 * _sigmoid(h1) * h3
    return jnp.dot(hid.astype(BF16), w2_ref[0], preferred_element_type=F32)


def _ffn_kernel(h_ref, ya_ref, yb_ref, yc_ref, wo_ref, g_ref, w1_ref, w3_ref, w2_ref, o_ref):
    h = _mixed_residual(h_ref, ya_ref, yb_ref, yc_ref, wo_ref)
    x = _rms(h, g_ref[...]).astype(BF16)
    o_ref[...] = h + _swiglu_step(x, w1_ref, w3_ref, w2_ref)


def _ffn(h, ya, yb, yc, w_out, g, w1, w3, w2):
    n = h.shape[0]
    tm = min(TM_FFN, n)
    resident = lambda shape: pl.BlockSpec(shape, lambda i: (0, 0, 0), pipeline_mode=pl.Buffered(1))
    return pl.pallas_call(
        _ffn_kernel,
        grid=(n // tm,),
        in_specs=_mixed_specs(tm) + [
            pl.BlockSpec((1, D_MODEL), lambda i: (0, 0)),
            resident((1, D_MODEL, D_FF)), resident((1, D_MODEL, D_FF)), resident((1, D_FF, D_MODEL))],
        out_specs=pl.BlockSpec((tm, D_MODEL), lambda i: (i, 0)),
        out_shape=jax.ShapeDtypeStruct((n, D_MODEL), F32),
        compiler_params=_params("parallel"),
        name="ffn_swiglu",
    )(h, ya, yb, yc, w_out, g, w1, w3, w2)


META_E1, META_E2, META_R1, META_R2, META_P1, META_P2 = range(6)


def _router_kernel(h_ref, ya_ref, yb_ref, yc_ref, wo_ref, g_ref, rw_ref, rb_ref,
                   hmid_ref, hn_ref, meta_ref, cnt_ref, xs_zero_ref, carry_ref):
    @pl.when(pl.program_id(0) == 0)
    def _():
        carry_ref[...] = jnp.zeros_like(carry_ref)

    xs_zero_ref[...] = jnp.zeros_like(xs_zero_ref)

    h = _mixed_residual(h_ref, ya_ref, yb_ref, yc_ref, wo_ref)
    hmid_ref[...] = h
    hn = _rms(h, g_ref[...])
    hn_ref[...] = hn
    tm = hn.shape[0]
    lane = lax.broadcasted_iota(jnp.int32, (tm, LANES), 1).astype(F32)
    logits = jnp.where(lane < N_EXPERTS, _dot(hn, rw_ref[...], passes=3) + rb_ref[...], NEG_INF)
    top1 = jnp.max(logits, axis=-1, keepdims=True)
    idx1 = jnp.min(jnp.where(logits == top1, lane, float(LANES)), axis=-1, keepdims=True)
    rest = jnp.where(lane == idx1, NEG_INF, logits)
    top2 = jnp.max(rest, axis=-1, keepdims=True)
    idx2 = jnp.min(jnp.where(rest == top2, lane, float(LANES)), axis=-1, keepdims=True)
    ex = jnp.exp(top2 - top1)
    picked = jnp.where((lane == idx1) | (lane == idx2), 1.0, 0.0)
    r = lax.broadcasted_iota(jnp.int32, (tm, tm), 0)
    c = lax.broadcasted_iota(jnp.int32, (tm, tm), 1)
    before = _dot(jnp.where(c < r, 1.0, 0.0), picked) + carry_ref[...]
    rank1 = jnp.sum(jnp.where(lane == idx1, before, 0.0), axis=-1, keepdims=True)
    rank2 = jnp.sum(jnp.where(lane == idx2, before, 0.0), axis=-1, keepdims=True)
    carry_ref[...] += jnp.sum(picked, axis=0, keepdims=True)
    cnt_ref[...] = carry_ref[...]
    meta = jnp.zeros((tm, LANES), F32)
    for pos, val in ((META_E1, idx1), (META_E2, idx2), (META_R1, rank1), (META_R2, rank2),
                     (META_P1, 1.0 / (1.0 + ex)), (META_P2, ex / (1.0 + ex))):
        meta = jnp.where(lane == pos, val, meta)
    meta_ref[...] = meta


def _dispatch_kernel(d1_ref, d2_ref, hn_ref, xs_in_ref, xs_ref, sem):
    del xs_in_ref
    tm = hn_ref.shape[0]

    def row_copy(r, dest):
        return pltpu.make_async_copy(hn_ref.at[pl.ds(r, 1)], xs_ref.at[pl.ds(dest, 1)], sem)

    def issue(r, carry):
        row_copy(r, d1_ref[0, 0, r]).start()
        row_copy(r, d2_ref[0, 0, r]).start()
        return carry

    lax.fori_loop(0, tm, issue, 0, unroll=8)
    for _ in range(2):
        pltpu.make_async_copy(hn_ref, xs_ref.at[pl.ds(0, tm)], sem).wait()


def _grouped_kernel(te_ref, nu_ref, xs_ref, w1_ref, w3_ref, w2_ref, ys_ref):
    del te_ref
    used = pl.program_id(0) < nu_ref[0]

    @pl.when(used)
    def _():
        ys_ref[...] = _swiglu_step(xs_ref[...].astype(BF16), w1_ref, w3_ref, w2_ref)

    @pl.when(jnp.logical_not(used))
    def _():
        ys_ref[...] = jnp.zeros_like(ys_ref)


def _combine_kernel(d1_ref, d2_ref, d1n_ref, d2n_ref, h_ref, meta_ref, p_ref, g_ref, wg_ref, wp_ref, gf_ref, ys_ref,
                    o_ref, buf1, buf2, sem, *, ple_final):
    i = pl.program_id(0)
    tm = h_ref.shape[0]
    slot = lax.rem(i, 2)

    def gather(da_ref, db_ref, s):
        def issue(r, carry):
            pltpu.make_async_copy(ys_ref.at[pl.ds(da_ref[0, 0, r], 1)], buf1.at[s, pl.ds(r, 1)], sem.at[s]).start()
            pltpu.make_async_copy(ys_ref.at[pl.ds(db_ref[0, 0, r], 1)], buf2.at[s, pl.ds(r, 1)], sem.at[s]).start()
            return carry
        lax.fori_loop(0, tm, issue, 0, unroll=8)

    @pl.when(i == 0)
    def _():
        gather(d1_ref, d2_ref, 0)

    @pl.when(i + 1 < pl.num_programs(0))
    def _():
        gather(d1n_ref, d2n_ref, 1 - slot)

    for buf in (buf1, buf2):
        pltpu.make_async_copy(ys_ref.at[pl.ds(0, tm)], buf.at[slot], sem.at[slot]).wait()
    meta = meta_ref[...]
    h = h_ref[...] + meta[:, META_P1:META_P1 + 1] * buf1[slot] + meta[:, META_P2:META_P2 + 1] * buf2[slot]
    if ple_final:
        h = _rms(_ple_update(h, p_ref, g_ref, wg_ref, wp_ref), gf_ref[...])
    o_ref[...] = h


def _moe(h, ya, yb, yc, w_out, g, router_w, router_b, w1, w3, w2, ple_final=None):
    n = h.shape[0]
    tm = min(TM_FFN, n)
    tg = min(TM_GROUP, n)
    nt = n // tm
    rw = jnp.pad(router_w, ((0, 0), (0, LANES - N_EXPERTS)))
    rb = jnp.pad(router_b.reshape(1, -1), ((0, 0), (0, LANES - N_EXPERTS)))
    tok = lambda width: pl.BlockSpec((tm, width), lambda i: (i, 0))
    rows = TOP_K * n + N_EXPERTS * tg
    assert rows % nt == 0
    h, hn, meta, cnt, xs_zero = pl.pallas_call(
        _router_kernel,
        grid=(nt,),
        in_specs=_mixed_specs(tm) + [
            pl.BlockSpec((1, D_MODEL), lambda i: (0, 0)),
            pl.BlockSpec((D_MODEL, LANES), lambda i: (0, 0)), pl.BlockSpec((1, LANES), lambda i: (0, 0))],
        out_specs=[tok(D_MODEL), tok(D_MODEL), tok(LANES), pl.BlockSpec((1, LANES), lambda i: (0, 0)),
                   pl.BlockSpec((rows // nt, D_MODEL), lambda i: (i, 0))],
        out_shape=[jax.ShapeDtypeStruct((n, D_MODEL), F32), jax.ShapeDtypeStruct((n, D_MODEL), F32),
                   jax.ShapeDtypeStruct((n, LANES), F32), jax.ShapeDtypeStruct((1, LANES), F32),
                   jax.ShapeDtypeStruct((rows, D_MODEL), F32)],
        scratch_shapes=[pltpu.VMEM((1, LANES), F32)],
        compiler_params=_params("arbitrary"),
        name="moe_router",
    )(h, ya, yb, yc, w_out, g, rw, rb)

    counts = cnt[0, :N_EXPERTS].astype(jnp.int32)
    padded = ((counts + tg - 1) // tg) * tg
    ends = jnp.cumsum(padded)
    offs = ends - padded
    n_tiles = rows // tg
    tile_start = jnp.arange(n_tiles, dtype=jnp.int32) * tg
    n_used = (ends[-1:] // tg).astype(jnp.int32)
    tile_expert = jnp.sum(tile_start[:, None] >= ends[None, :], axis=1).astype(jnp.int32)
    tile_expert = jnp.minimum(tile_expert, tile_expert[n_used[0] - 1])
    onehot = lambda e: (e[:, None] == jnp.arange(N_EXPERTS, dtype=jnp.int32)[None, :]).astype(jnp.int32)
    dest = lambda e, r: (jnp.sum(onehot(e.astype(jnp.int32)) * offs[None, :], axis=1)
                         + r.astype(jnp.int32)).reshape(nt, 1, tm)
    d1 = dest(meta[:, META_E1], meta[:, META_R1])
    d2 = dest(meta[:, META_E2], meta[:, META_R2])
    dspec = pl.BlockSpec((1, 1, tm), lambda i: (i, 0, 0), memory_space=pltpu.SMEM)

    xs = pl.pallas_call(
        _dispatch_kernel,
        grid=(nt,),
        in_specs=[dspec, dspec, tok(D_MODEL), pl.BlockSpec(memory_space=pl.ANY)],
        out_specs=pl.BlockSpec(memory_space=pl.ANY),
        out_shape=jax.ShapeDtypeStruct((rows, D_MODEL), F32),
        scratch_shapes=[pltpu.SemaphoreType.DMA(())],
        input_output_aliases={3: 0},
        compiler_params=_params("arbitrary"),
        name="moe_dispatch",
    )(d1, d2, hn, xs_zero)

    wspec = lambda shape: pl.BlockSpec(shape, lambda i, te, nu: (te[i], 0, 0))
    ys = pl.pallas_call(
        _grouped_kernel,
        grid_spec=pltpu.PrefetchScalarGridSpec(
            num_scalar_prefetch=2,
            grid=(n_tiles,),
            in_specs=[pl.BlockSpec((tg, D_MODEL), lambda i, te, nu: (jnp.minimum(i, nu[0] - 1), 0)),
                      wspec((1, D_MODEL, D_FF)), wspec((1, D_MODEL, D_FF)), wspec((1, D_FF, D_MODEL))],
            out_specs=pl.BlockSpec((tg, D_MODEL), lambda i, te, nu: (i, 0))),
        out_shape=jax.ShapeDtypeStruct((rows, D_MODEL), F32),
        compiler_params=_params("arbitrary"),
        name="moe_grouped",
    )(tile_expert, n_used, xs, w1, w3, w2)

    dnext = pl.BlockSpec((1, 1, tm), lambda i: (jnp.minimum(i + 1, nt - 1), 0, 0), memory_space=pltpu.SMEM)
    const = lambda a: pl.BlockSpec(a.shape, lambda i: (0, 0))
    if ple_final is None:
        ple_args = tuple(jnp.zeros((8, LANES), F32) for _ in range(5))
        ple_specs = [const(a) for a in ple_args]
    else:
        ple_args = ple_final
        ple_specs = [tok(PLE_DIM)] + [const(a) for a in ple_final[1:]]
    return pl.pallas_call(
        functools.partial(_combine_kernel, ple_final=ple_final is not None),
        grid=(nt,),
        in_specs=[dspec, dspec, dnext, dnext, tok(D_MODEL), tok(LANES)] + ple_specs + [pl.BlockSpec(memory_space=pl.ANY)],
        out_specs=tok(D_MODEL),
        out_shape=jax.ShapeDtypeStruct((n, D_MODEL), F32),
        scratch_shapes=[pltpu.VMEM((2, tm, D_MODEL), F32), pltpu.VMEM((2, tm, D_MODEL), F32),
                        pltpu.SemaphoreType.DMA((2,))],
        compiler_params=_params("arbitrary"),
        name="moe_combine",
    )(d1, d2, d1, d2, h, meta, *ple_args, ys)


def _ple_update(h, p_ref, g_ref, wg_ref, wp_ref):
    gate = _sigmoid(jnp.dot(_rms(h, g_ref[...]).astype(BF16), wg_ref[...], preferred_element_type=F32))
    return h + jnp.dot(p_ref[...].astype(BF16), wp_ref[...], preferred_element_type=F32) * gate


def _ple_final_kernel(h_ref, p_ref, g_ref, wg_ref, wp_ref, gf_ref, o_ref):
    o_ref[...] = _rms(_ple_update(h_ref[...], p_ref, g_ref, wg_ref, wp_ref), gf_ref[...])


def _ple_proj_kernel(h_ref, p_ref, g_ref, wg_ref, wp_ref, gn_ref, wn_ref, o_ref, z_ref):
    h = _ple_update(h_ref[...], p_ref, g_ref, wg_ref, wp_ref)
    o_ref[...] = h
    z_ref[...] = jnp.dot(_rms(h, gn_ref[...]).astype(BF16), wn_ref[...],
                         preferred_element_type=F32).astype(z_ref.dtype)


def _ple(h, p, g, wg, wp, g_tail, w_next=None):
    n = h.shape[0]
    tm = min(TM_PROJ, n)
    tok = lambda width: pl.BlockSpec((tm, width), lambda i: (i, 0))
    const = lambda a: pl.BlockSpec(a.shape, lambda i: (0, 0))
    common = dict(grid=(n // tm,), compiler_params=_params("parallel"))
    in_specs = [tok(D_MODEL), tok(PLE_DIM), const(g), const(wg), const(wp), const(g_tail)]
    h_sds = jax.ShapeDtypeStruct((n, D_MODEL), F32)
    if w_next is None:
        return pl.pallas_call(_ple_final_kernel, in_specs=in_specs, out_specs=tok(D_MODEL), out_shape=h_sds,
                              name="ple_final", **common)(h, p, g, wg, wp, g_tail)
    return pl.pallas_call(_ple_proj_kernel, in_specs=in_specs + [const(w_next)],
                          out_specs=[tok(D_MODEL), tok(Z_W)],
                          out_shape=[h_sds, jax.ShapeDtypeStruct((n, Z_W), Z_DTYPE)],
                          name="ple_proj_in", **common)(h, p, g, wg, wp, g_tail, w_next)


def _q_perm():
    idx = []
    for j in range(Q_PER_GROUP):
        for g in range(KV_GROUPS):
            base = (g * Q_PER_GROUP + j) * HEAD_DIM
            idx.extend(range(base, base + HEAD_DIM))
    return jnp.asarray(idx, dtype=jnp.int32)


def _layout_w_in(w):
    a1 = 2 * A_WIDTH
    b1 = a1 + 4 * B_WIDTH
    q1 = b1 + C_WIDTH
    w = jnp.concatenate([w[:, a1:b1], w[:, :a1], w[:, b1:q1][:, _q_perm()], w[:, q1:]], axis=1)
    return jnp.pad(w, ((0, 0), (0, Z_W - w.shape[1]))).astype(BF16)


def _layout_w_out(w):
    c0 = A_WIDTH + B_WIDTH
    return jnp.concatenate([w[:c0], w[c0:][_q_perm()]], axis=0).astype(BF16)


def _mixers(z, bn, s, tabs, gm_ln_g, gm_ln_b, gm_ws, gm_bs,
            rw_mu, rw_w0, rw_w_up, rw_a0, rw_a_up, rw_g_up, rw_k_k, rw_k_a, rw_r_k, rw_gn_g, rw_gn_b,
            nsa_cmp_pos, nsa_kc_w1, nsa_kc_w2, nsa_vc_w1, nsa_vc_w2):
    n = bn * s
    z3 = z.reshape(bn, s, Z_W)
    y_a = _gmlp(z3, gm_ln_g, gm_ln_b, gm_ws, gm_bs)
    y_b = _rwkv(z3, rw_mu, rw_w0, rw_w_up, rw_a0, rw_a_up, rw_g_up, rw_k_k, rw_k_a, rw_r_k, rw_gn_g, rw_gn_b)
    q_raw, q_rot, kc, vc, ks, vs, kw, vw, gates = _nsa_prep(z3, tabs)
    k_cmp, v_cmp = _nsa_compress(kc, vc, nsa_cmp_pos, nsa_kc_w1, nsa_kc_w2, nsa_vc_w1, nsa_vc_w2)
    y_c = _nsa_attn(q_raw, q_rot, k_cmp, v_cmp, ks, vs, kw, vw, gates)
    return y_a.reshape(n, -1), y_b.reshape(n, -1), y_c.reshape(n, -1)


def kernel(x, p, positions, g_mix, w_in, w_out, gm_ln_g, gm_ln_b, gm_ws, gm_bs, rw_mu, rw_w0, rw_w_up, rw_a0,
           rw_a_up, rw_g_up, rw_k_k, rw_k_a, rw_r_k, rw_gn_g, rw_gn_b, nsa_cmp_pos, nsa_kc_w1, nsa_kc_w2,
           nsa_vc_w1, nsa_vc_w2, g_ffn, ffn_w1, ffn_w3, ffn_w2, router_w, router_b, moe_w1, moe_w3, moe_w2,
           g_ple, ple_gate_w, ple_proj_w, g_final):
    bn, s, _ = x.shape
    n = bn * s
    depth = g_mix.shape[0]
    tabs = _rope_tables(positions)
    h = x.reshape(n, D_MODEL)
    z = _proj_in(h, g_mix[0].reshape(1, -1), _layout_w_in(w_in[0]))
    for i in range(depth):
        ys = _mixers(z, bn, s, tabs, gm_ln_g[i], gm_ln_b[i], gm_ws[i], gm_bs[i],
                     rw_mu[i], rw_w0[i], rw_w_up[i], rw_a0[i], rw_a_up[i], rw_g_up[i], rw_k_k[i], rw_k_a[i],
                     rw_r_k[i], rw_gn_g[i], rw_gn_b[i], nsa_cmp_pos[i], nsa_kc_w1[i], nsa_kc_w2[i],
                     nsa_vc_w1[i], nsa_vc_w2[i])
        wo = _layout_w_out(w_out[i])
        j = i // 2
        ple_args = (p[i].reshape(n, PLE_DIM), g_ple[i].reshape(1, -1), ple_gate_w[i].astype(BF16),
                    ple_proj_w[i].astype(BF16))
        last = i == depth - 1
        fused_tail = last and i % 2 == 1
        if i % 2 == 0:
            h = _ffn(h, *ys, wo, g_ffn[i].reshape(1, -1), ffn_w1[j:j + 1].astype(BF16),
                     ffn_w3[j:j + 1].astype(BF16), ffn_w2[j:j + 1].astype(BF16))
        else:
            h = _moe(h, *ys, wo, g_ffn[i].reshape(1, -1), router_w[j], router_b[j], moe_w1[j].astype(BF16),
                     moe_w3[j].astype(BF16), moe_w2[j].astype(BF16),
                     ple_final=ple_args + (g_final.reshape(1, -1),) if fused_tail else None)
        if fused_tail:
            pass
        elif last:
            h = _ple(h, *ple_args, g_final.reshape(1, -1))
        else:
            h, z = _ple(h, *ple_args, g_mix[i + 1].reshape(1, -1), _layout_w_in(w_in[i + 1]))
    return h.reshape(bn, s, D_MODEL)
```

```python
import functools
import math

import jax
import jax.numpy as jnp
from jax import lax
from jax.experimental import pallas as pl
from jax.experimental.pallas import tpu as pltpu

F32 = jnp.float32
BF16 = jnp.bfloat16

D_MODEL = 1024
HEAD_DIM = 64
A_HEADS = 4
B_HEADS = 4
A_WIDTH = 256
B_WIDTH = 256
C_WIDTH = 512
CHUNK = 128
LN_EPS = 1e-5
GN_EPS = 64e-5
RMS_EPS = 1e-6
RW_LORA = 64
KV_GROUPS = 2
Q_PER_GROUP = 4
CMP_BLOCK = 32
CMP_STRIDE = 16
CMP_HIDDEN = 128
SLC_BLOCK = 64
SLC_TOPK = 16
WINDOW = 512
Q_BLOCK = 128
NEG_INF = -1e30
FORCE = 1e4
ROPE_THETA = 500000.0
ROPE_DIM = 16
D_FF = 2816
N_EXPERTS = 8
TOP_K = 2
PLE_DIM = 256

LANES = 128
Z_B = 0
Z_A = 1024
Z_Q = 1536
Z_KV = 2048
Z_G = 2816
Z_W = 2944
Z_DTYPE = BF16

VMEM_LIMIT = 56 * 1024 * 1024
TM_PROJ = 512
TM_FFN = 512
TM_GROUP = 256
GM_T = 512
RW_T = 512
RW_C = 64
TK_SEL = 512

_NN = (((1,), (0,)), ((), ()))
_NT = (((1,), (1,)), ((), ()))


def _params(*sem):
    return pltpu.CompilerParams(dimension_semantics=sem, vmem_limit_bytes=VMEM_LIMIT)


def _dot(a, b, dn=_NN, passes=1):
    if passes == 6:
        return lax.dot_general(a.astype(F32), b.astype(F32), dn, precision=lax.Precision.HIGHEST,
                               preferred_element_type=F32)
    a_hi = a.astype(BF16)
    b_hi = b.astype(BF16)
    out = lax.dot_general(a_hi, b_hi, dn, preferred_element_type=F32)
    if passes == 1:
        return out
    a_lo = (a - a_hi.astype(F32)).astype(BF16)
    out = out + lax.dot_general(a_lo, b_hi, dn, preferred_element_type=F32)
    if passes == 2:
        return out
    b_lo = (b - b_hi.astype(F32)).astype(BF16)
    return out + lax.dot_general(a_hi, b_lo, dn, preferred_element_type=F32)


def _rms(x, g):
    return x * lax.rsqrt(jnp.mean(x * x, axis=-1, keepdims=True) + RMS_EPS) * g


def _sigmoid(x):
    return 1.0 / (1.0 + jnp.exp(-x))


def _div(x, d):
    return lax.shift_right_logical(x, jnp.int32(int(math.log2(d))))


def _head_block_ones(n):
    r = lax.broadcasted_iota(jnp.int32, (n, n), 0)
    c = lax.broadcasted_iota(jnp.int32, (n, n), 1)
    return jnp.where(_div(r, HEAD_DIM) == _div(c, HEAD_DIM), 1.0, 0.0).astype(F32)


def _rope_kernel(inv_ref, pos_ref, cos_ref, sin_ref):
    p = pos_ref[0].astype(F32)
    for j in range(ROPE_DIM // 2):
        ang = p * inv_ref[j]
        cos_ref[0, j] = jnp.cos(ang)
        sin_ref[0, j] = jnp.sin(ang)


def _rope_tables(positions):
    bn, s = positions.shape
    half = ROPE_DIM // 2
    inv = 1.0 / (ROPE_THETA ** (jnp.arange(0, ROPE_DIM, 2, dtype=F32) / ROPE_DIM))
    pos3 = positions.reshape(bn, s // LANES, LANES)
    cos, sin = pl.pallas_call(
        _rope_kernel,
        grid=(bn,),
        in_specs=[pl.BlockSpec(memory_space=pltpu.SMEM),
                  pl.BlockSpec((1, s // LANES, LANES), lambda b: (b, 0, 0))],
        out_specs=[pl.BlockSpec((1, half, s // LANES, LANES), lambda b: (b, 0, 0, 0))] * 2,
        out_shape=[jax.ShapeDtypeStruct((bn, half, s // LANES, LANES), F32)] * 2,
        compiler_params=_params("parallel"),
        name="rope_tables",
    )(inv, pos3)
    cos = cos.reshape(bn, half, s).transpose(0, 2, 1)
    sin = sin.reshape(bn, half, s).transpose(0, 2, 1)
    one = jnp.ones((bn, s, HEAD_DIM - ROPE_DIM), F32)
    zero = jnp.zeros((bn, s, HEAD_DIM - half), F32)
    tab_c = jnp.concatenate([cos, cos, one], axis=-1)
    tab_lo = jnp.concatenate([sin, zero], axis=-1)
    tab_hi = jnp.concatenate([zero[..., :half], sin, zero[..., :HEAD_DIM - ROPE_DIM]], axis=-1)
    rep = lambda t: jnp.tile(t, (1, 1, LANES // HEAD_DIM))
    return rep(tab_c), rep(tab_lo), rep(tab_hi)


def _proj_in_kernel(h_ref, g_ref, w_ref, z_ref):
    y = _rms(h_ref[...], g_ref[...])
    z_ref[...] = jnp.dot(y.astype(BF16), w_ref[...], preferred_element_type=F32).astype(z_ref.dtype)


def _proj_in(h, g, w):
    n = h.shape[0]
    tm = min(TM_PROJ, n)
    return pl.pallas_call(
        _proj_in_kernel,
        grid=(n // tm,),
        in_specs=[pl.BlockSpec((tm, D_MODEL), lambda i: (i, 0)),
                  pl.BlockSpec((1, D_MODEL), lambda i: (0, 0)),
                  pl.BlockSpec((D_MODEL, Z_W), lambda i: (0, 0))],
        out_specs=pl.BlockSpec((tm, Z_W), lambda i: (i, 0)),
        out_shape=jax.ShapeDtypeStruct((n, Z_W), Z_DTYPE),
        compiler_params=_params("parallel"),
        name="proj_in",
    )(h, g, w)


def _gmlp_kernel(z_ref, lng_ref, lnb_ref, ws_ref, bias_ref, o_ref):
    gz = jax.nn.gelu(z_ref[0].astype(F32))
    u = gz[:, :A_WIDTH]
    v = gz[:, A_WIDTH:]
    ones = _head_block_ones(A_WIDTH)
    mu = _dot(v, ones, passes=2) * (1.0 / HEAD_DIM)
    d = v - mu
    var = _dot(d * d, ones, passes=2) * (1.0 / HEAD_DIM)
    vn = d * lax.rsqrt(var + LN_EPS) * lng_ref[...] + lnb_ref[...]
    r = lax.broadcasted_iota(jnp.int32, (CHUNK, CHUNK), 0)
    c = lax.broadcasted_iota(jnp.int32, (CHUNK, CHUNK), 1)
    lane_head = _div(lax.broadcasted_iota(jnp.int32, (CHUNK, A_WIDTH), 1), HEAD_DIM)
    w_causal = [jnp.where(c <= r, ws_ref[hd], 0.0).astype(BF16) for hd in range(A_HEADS)]
    for ck in range(vn.shape[0] // CHUNK):
        rows = slice(ck * CHUNK, (ck + 1) * CHUNK)
        mixed = bias_ref[...]
        for hd in range(A_HEADS):
            mixed = mixed + _dot(w_causal[hd], jnp.where(lane_head == hd, vn[rows], 0.0))
        o_ref[0, rows, :] = (u[rows] * mixed).astype(o_ref.dtype)


def _gmlp(z3, ln_g, ln_b, w_s, b_s):
    bn, s, _ = z3.shape
    t_len = min(GM_T, s)
    bias = jnp.repeat(b_s.T, HEAD_DIM, axis=1)
    return pl.pallas_call(
        _gmlp_kernel,
        grid=(bn, s // t_len),
        in_specs=[pl.BlockSpec((1, t_len, 2 * A_WIDTH), lambda b, i: (b, i, Z_A // (2 * A_WIDTH))),
                  pl.BlockSpec((1, A_WIDTH), lambda b, i: (0, 0)),
                  pl.BlockSpec((1, A_WIDTH), lambda b, i: (0, 0)),
                  pl.BlockSpec((A_HEADS, CHUNK, CHUNK), lambda b, i: (0, 0, 0)),
                  pl.BlockSpec((CHUNK, A_WIDTH), lambda b, i: (0, 0))],
        out_specs=pl.BlockSpec((1, t_len, A_WIDTH), lambda b, i: (b, i, 0)),
        out_shape=jax.ShapeDtypeStruct((bn, s, A_WIDTH), BF16),
        compiler_params=_params("parallel", "parallel"),
        name="gmlp",
    )(z3, ln_g.reshape(1, A_WIDTH), ln_b.reshape(1, A_WIDTH), w_s, bias)


RW_PASSES = 1


def _rwkv_chunks(r, lw, k, v, kk, a, s0, c_len):
    nh = B_HEADS
    m = nh * c_len
    chunks = range(r.shape[0] // c_len)
    dot = functools.partial(_dot, passes=RW_PASSES)
    ti = lax.broadcasted_iota(jnp.int32, (c_len, c_len), 0)
    tj = lax.broadcasted_iota(jnp.int32, (c_len, c_len), 1)
    tril = jnp.where(tj <= ti, 1.0, 0.0).astype(F32)
    row_head = _div(lax.broadcasted_iota(jnp.int32, (m, B_WIDTH), 0), c_len)
    lane_head = _div(lax.broadcasted_iota(jnp.int32, (m, B_WIDTH), 1), HEAD_DIM)
    head_mask = row_head == lane_head
    rep = lambda x: jnp.concatenate([x] * nh, axis=0)
    stack = lambda x: jnp.where(head_mask, rep(x), 0.0)
    ri = lax.broadcasted_iota(jnp.int32, (m, m), 0)
    ci = lax.broadcasted_iota(jnp.int32, (m, m), 1)
    same = _div(ri, c_len) == _div(ci, c_len)
    strict = same & (ci < ri)
    incl = same & (ci <= ri)
    eye = ri == ci
    assert 2 * c_len == LANES
    low = lax.broadcasted_iota(jnp.int32, (2 * m, LANES), 1) < c_len

    a_st, r_st, v_st, bw_st, kw_st, w_c, l_ab, l_ak, l_rb, l_rk = ([] for _ in range(10))
    for c in chunks:
        sl = slice(c * c_len, (c + 1) * c_len)
        cum = _dot(tril, lw[sl], passes=3)
        w_t = jnp.exp(cum)
        w_i = jnp.exp(-cum)
        a_t = -kk[sl] * jnp.exp(cum - lw[sl])
        b_t = kk[sl] * a[sl] * w_i
        k_t = k[sl] * w_i
        w_c.append(w_t[c_len - 1:c_len, :])
        a_st.append(stack(a_t))
        r_st.append(stack(r[sl] * w_t))
        v_st.append(stack(v[sl]))
        bw_st.append(stack(b_t * w_c[c]))
        kw_st.append(stack(k_t * w_c[c]))
        cross = dot(jnp.concatenate([a_st[c], r_st[c]], axis=0), jnp.concatenate([b_t, k_t], axis=0), _NT)
        swapped = pltpu.roll(cross, c_len, axis=1)
        vs_b = jnp.concatenate([jnp.where(low, cross, swapped)] * (nh // 2), axis=1)
        vs_k = jnp.concatenate([jnp.where(low, swapped, cross)] * (nh // 2), axis=1)
        l_ab.append(jnp.where(strict, vs_b[:m], 0.0))
        l_ak.append(jnp.where(strict, vs_k[:m], 0.0))
        l_rb.append(jnp.where(incl, vs_b[m:], 0.0))
        l_rk.append(jnp.where(incl, vs_k[m:], 0.0))

    p_inv = [jnp.where(eye, 1.0, 0.0) + l_ab[c] for c in chunks]
    pw = l_ab
    for _ in range(int(math.log2(c_len)) - 1):
        pw = [dot(pw[c], pw[c]) for c in chunks]
        p_inv = [p_inv[c] + dot(p_inv[c], pw[c]) for c in chunks]
    t_m = [dot(l_ak[c], v_st[c]) for c in chunks]
    q_m = [dot(p_inv[c], t_m[c]) for c in chunks]
    p_m = [dot(p_inv[c], a_st[c]) for c in chunks]
    g_m = [r_st[c] + dot(l_rb[c], p_m[c]) for c in chunks]
    h_m = [dot(l_rb[c], q_m[c]) + dot(l_rk[c], v_st[c]) for c in chunks]
    m_m = [jnp.where(eye, w_c[c], 0.0) + dot(p_m[c].T, bw_st[c]) for c in chunks]
    n_m = [dot(q_m[c].T, bw_st[c]) + dot(v_st[c].T, kw_st[c]) for c in chunks]

    states = [s0]
    for c in chunks:
        states.append(dot(states[c], m_m[c]) + n_m[c])
    ys = []
    for c in chunks:
        y_st = dot(g_m[c], states[c], _NT) + h_m[c]
        y = y_st[0:c_len]
        for hd in range(1, nh):
            y = y + y_st[hd * c_len:(hd + 1) * c_len]
        ys.append(y)
    return jnp.concatenate(ys, axis=0), states[-1]


def _rwkv_kernel(z_ref, mu_ref, w0_ref, a0_ref, wwa_ref, gup_ref, kk_ref, ka_ref, rk_ref, gng_ref, gnb_ref,
                 o_ref, carry_ref, state_ref, *, c_len):
    @pl.when(pl.program_id(1) == 0)
    def _():
        carry_ref[...] = jnp.zeros_like(carry_ref)
        state_ref[...] = jnp.zeros_like(state_ref)

    zb = z_ref[0].astype(F32)
    t_len = zb.shape[0]
    row = lax.broadcasted_iota(jnp.int32, zb.shape, 0)
    z_prev = jnp.where(row == 0, carry_ref[...], pltpu.roll(zb, 1, axis=0))
    carry_ref[...] = zb[t_len - 1:t_len, :]
    zz = zb + (z_prev - zb) * mu_ref[...]
    r = zz[:, 0:B_WIDTH]
    k = zz[:, B_WIDTH:2 * B_WIDTH]
    v = zz[:, 2 * B_WIDTH:3 * B_WIDTH]
    wa = zz[:, 3 * B_WIDTH:3 * B_WIDTH + 2 * RW_LORA]
    gd = zz[:, 3 * B_WIDTH + 2 * RW_LORA:]
    lane = lax.broadcasted_iota(jnp.int32, wa.shape, 1)
    proj = _dot(jnp.where(lane < RW_LORA, jnp.tanh(wa), wa), wwa_ref[...])
    x = -(w0_ref[...] + proj[:, :B_WIDTH])
    softplus = jnp.maximum(x, 0.0) + jnp.log(1.0 + jnp.exp(-jnp.abs(x)))
    lw = -jnp.exp(-softplus - 0.5)
    a = _sigmoid(a0_ref[...] + proj[:, B_WIDTH:])
    g = _dot(_sigmoid(gd), gup_ref[...])
    ones = _head_block_ones(B_WIDTH)
    kk = k * kk_ref[...]
    kk = kk * lax.rsqrt(jnp.maximum(_dot(kk * kk, ones, passes=2), 1e-24))
    k2 = k * (1.0 + (a - 1.0) * ka_ref[...])

    y, state_ref[...] = _rwkv_chunks(r, lw, k2, v, kk, a, state_ref[...], c_len)

    mu_y = _dot(y, ones, passes=2) * (1.0 / HEAD_DIM)
    d = y - mu_y
    var = _dot(d * d, ones, passes=2) * (1.0 / HEAD_DIM)
    yn = d * lax.rsqrt(var + GN_EPS) * gng_ref[...] + gnb_ref[...]
    bonus = _dot(r * k2 * rk_ref[...], ones, passes=2) * v
    o_ref[0] = ((yn + bonus) * g).astype(o_ref.dtype)


def _rwkv(z3, mu, w0, w_up, a0, a_up, g_up, k_k, k_a, r_k, gn_g, gn_b):
    bn, s, _ = z3.shape
    t_len = min(RW_T, s)
    zero = jnp.zeros((RW_LORA, B_WIDTH), F32)
    wwa = jnp.concatenate([jnp.concatenate([w_up, zero], axis=1),
                           jnp.concatenate([zero, a_up], axis=1)], axis=0).astype(BF16)
    row = lambda t: t.reshape(1, -1)
    vec = pl.BlockSpec((1, B_WIDTH), lambda b, i: (0, 0))
    return pl.pallas_call(
        functools.partial(_rwkv_kernel, c_len=RW_C),
        grid=(bn, s // t_len),
        in_specs=[pl.BlockSpec((1, t_len, 4 * B_WIDTH), lambda b, i: (b, i, Z_B // (4 * B_WIDTH))),
                  pl.BlockSpec((1, 4 * B_WIDTH), lambda b, i: (0, 0)),
                  vec, vec,
                  pl.BlockSpec((2 * RW_LORA, 2 * B_WIDTH), lambda b, i: (0, 0)),
                  pl.BlockSpec((2 * RW_LORA, B_WIDTH), lambda b, i: (0, 0)),
                  vec, vec, vec, vec, vec],
        out_specs=pl.BlockSpec((1, t_len, B_WIDTH), lambda b, i: (b, i, 0)),
        out_shape=jax.ShapeDtypeStruct((bn, s, B_WIDTH), BF16),
        scratch_shapes=[pltpu.VMEM((1, 4 * B_WIDTH), F32), pltpu.VMEM((B_WIDTH, B_WIDTH), F32)],
        compiler_params=_params("parallel", "arbitrary"),
        name="rwkv7",
    )(z3, row(mu), row(w0), row(a0), wwa, g_up.astype(BF16), row(k_k), row(k_a), row(r_k), row(gn_g), row(gn_b))


def _rope(x, tab_c, tab_lo, tab_hi):
    n = x.shape[-1]
    half = ROPE_DIM // 2
    return x * tab_c - pltpu.roll(x, n - half, axis=1) * tab_lo + pltpu.roll(x, half, axis=1) * tab_hi


N_HEADS_C = KV_GROUPS * Q_PER_GROUP
QT_COLS = N_HEADS_C * Q_BLOCK
GATE_ROWS = 32
ONES_ROWS = 16
LOG2E = 1.4426950408889634


def _nsa_prep_kernel(zq_ref, zkc_ref, zks_ref, zkw_ref, zg_ref, tc_ref, tl_ref, th_ref,
                     qraw_ref, qrot_ref, kc_ref, vc_ref, ks_ref, vst_ref, kw_ref, vwt_ref, gate_ref):
    tab_c, tab_lo, tab_hi = tc_ref[0], tl_ref[0], th_ref[0]
    nrep = C_WIDTH // LANES
    wide = lambda t: jnp.concatenate([t] * nrep, axis=1)
    zkc, zks, zkw = (ref[0].astype(F32) for ref in (zkc_ref, zks_ref, zkw_ref))
    q = zq_ref[0].astype(F32) * (HEAD_DIM ** -0.5 * LOG2E)
    q_rot = _rope(q, wide(tab_c), wide(tab_lo), wide(tab_hi))
    ts = q.shape[0]
    row_group0 = lax.broadcasted_iota(jnp.int32, (LANES, ts), 0) < HEAD_DIM

    def put_queries(ref, x):
        for j in range(Q_PER_GROUP):
            xt = x[:, j * LANES:(j + 1) * LANES].T
            for g in range(KV_GROUPS):
                keep = row_group0 if g == 0 else jnp.logical_not(row_group0)
                xm = jnp.where(keep, xt, 0.0).astype(BF16)
                for qq in range(ts // Q_BLOCK):
                    col = qq * QT_COLS + (g * Q_PER_GROUP + j) * Q_BLOCK
                    ref[0, :, col:col + Q_BLOCK] = xm[:, qq * Q_BLOCK:(qq + 1) * Q_BLOCK]

    put_queries(qraw_ref, q)
    put_queries(qrot_ref, q_rot)
    kc_ref[0] = zkc[:, :LANES].astype(BF16)
    vc_ref[0] = zkc[:, LANES:].astype(BF16)
    ks_ref[0, :, :LANES] = _rope(zks[:, :LANES], tab_c, tab_lo, tab_hi).astype(BF16)
    key_block = _div(pl.program_id(1) * ts + lax.broadcasted_iota(jnp.int32, (ts, LANES), 0), SLC_BLOCK)
    ks_ref[0, :, LANES:] = jnp.where(key_block == lax.broadcasted_iota(jnp.int32, (ts, LANES), 1), 1.0, 0.0).astype(BF16)
    vst_ref[0, 0] = zks[:, LANES:].T.astype(BF16)
    kw_ref[0] = _rope(zkw[:, :LANES], tab_c, tab_lo, tab_hi).astype(BF16)
    vwt = zkw[:, LANES:].T.astype(BF16)
    for qq in range(ts // Q_BLOCK):
        vwt_ref[0, qq] = vwt[:, qq * Q_BLOCK:(qq + 1) * Q_BLOCK]
    gate_ref[0] = _sigmoid(zg_ref[0].astype(F32)).T[:GATE_ROWS, :]


def _nsa_prep(z3, tabs):
    bn, s, _ = z3.shape
    ts = TK_SEL
    zspec = lambda width, off: pl.BlockSpec((1, ts, width), lambda b, i: (b, i, off // width))
    tspec = pl.BlockSpec((1, ts, LANES), lambda b, i: (b, i, 0))
    rowmajor = pl.BlockSpec((1, ts, LANES), lambda b, i: (b, i, 0))
    qspec = pl.BlockSpec((1, LANES, N_HEADS_C * ts), lambda b, i: (b, 0, i))
    rm_sds = jax.ShapeDtypeStruct((bn, s, LANES), BF16)
    q_sds = jax.ShapeDtypeStruct((bn, LANES, N_HEADS_C * s), BF16)
    return pl.pallas_call(
        _nsa_prep_kernel,
        grid=(bn, s // ts),
        in_specs=[zspec(C_WIDTH, Z_Q), zspec(2 * LANES, Z_KV), zspec(2 * LANES, Z_KV + 2 * LANES),
                  zspec(2 * LANES, Z_KV + 4 * LANES), zspec(LANES, Z_G), tspec, tspec, tspec],
        out_specs=[qspec, qspec, rowmajor, rowmajor, pl.BlockSpec((1, ts, 2 * LANES), lambda b, i: (b, i, 0)),
                   pl.BlockSpec((1, 1, LANES, ts), lambda b, i: (b, i, 0, 0)), rowmajor,
                   pl.BlockSpec((1, ts // Q_BLOCK, LANES, Q_BLOCK), lambda b, i: (b, i, 0, 0)),
                   pl.BlockSpec((1, GATE_ROWS, ts), lambda b, i: (b, 0, i))],
        out_shape=[q_sds, q_sds, rm_sds, rm_sds, jax.ShapeDtypeStruct((bn, s, 2 * LANES), BF16),
                   jax.ShapeDtypeStruct((bn, s // ts, LANES, ts), BF16), rm_sds,
                   jax.ShapeDtypeStruct((bn, s // Q_BLOCK, LANES, Q_BLOCK), BF16),
                   jax.ShapeDtypeStruct((bn, GATE_ROWS, s), F32)],
        compiler_params=_params("parallel", "parallel"),
        name="nsa_prep",
    )(z3, z3, z3, z3, z3, *tabs)


def _nsa_compress_kernel(xk_ref, xv_ref, pos_ref, kw1a_ref, kw1b_ref, kw2_ref, vw1a_ref, vw1b_ref, vw2_ref,
                         ko_ref, vo_ref):
    n = xk_ref.shape[1]
    pos_a = pos_ref[0]
    pos_b = pos_ref[1]
    for x_ref, w1a, w1b, w2, o_ref, transposed in ((xk_ref, kw1a_ref, kw1b_ref, kw2_ref, ko_ref, False),
                                                    (xv_ref, vw1a_ref, vw1b_ref, vw2_ref, vo_ref, True)):
        x = x_ref[0]
        first = _dot(x, w1a[...])
        second = _dot(x, w1b[...])
        pc = _dot(pos_a, w1a[...]) + _dot(pos_b, w1b[...])
        hid = jax.nn.gelu(first + pltpu.roll(second, n - 1, axis=0) + pc[0:1, :])
        out = _dot(hid, w2[...])
        o_ref[0] = (out.T if transposed else out).astype(o_ref.dtype)


def _expand_cmp_weights(w1, w2):
    half = CMP_BLOCK // 2
    w1r = w1.reshape(CMP_BLOCK, HEAD_DIM, CMP_HIDDEN)
    eye = jnp.eye(KV_GROUPS, dtype=F32)
    w1e = jnp.einsum('ldh,gk->lgdkh', w1r, eye).reshape(CMP_BLOCK * LANES, KV_GROUPS * CMP_HIDDEN)
    w2e = jnp.einsum('hd,gk->ghkd', w2, eye).reshape(KV_GROUPS * CMP_HIDDEN, LANES)
    return (w1e[:half * LANES].astype(BF16), w1e[half * LANES:].astype(BF16), w2e.astype(BF16))


def _nsa_compress(kc, vc, cmp_pos, kc_w1, kc_w2, vc_w1, vc_w2):
    bn, s, _ = kc.shape
    n = s // CMP_STRIDE
    xw = CMP_STRIDE * LANES
    xk = kc.reshape(bn, n, xw)
    xv = vc.reshape(bn, n, xw)
    pos = jnp.tile(cmp_pos[:, None, :], (1, KV_GROUPS, 1)).reshape(2, 1, xw)
    pos = jnp.broadcast_to(pos, (2, 8, xw)).astype(BF16)
    kw = _expand_cmp_weights(kc_w1, kc_w2)
    vw = _expand_cmp_weights(vc_w1, vc_w2)
    full = lambda a: pl.BlockSpec(a.shape, lambda b: (0,) * a.ndim)
    xspec = pl.BlockSpec((1, n, xw), lambda b: (b, 0, 0))
    return pl.pallas_call(
        _nsa_compress_kernel,
        grid=(bn,),
        in_specs=[xspec, xspec, full(pos)] + [full(a) for a in kw + vw],
        out_specs=[pl.BlockSpec((1, n, LANES), lambda b: (b, 0, 0)), pl.BlockSpec((1, LANES, n), lambda b: (b, 0, 0))],
        out_shape=[jax.ShapeDtypeStruct((bn, n, LANES), BF16), jax.ShapeDtypeStruct((bn, LANES, n), BF16)],
        compiler_params=_params("parallel"),
        name="nsa_compress",
    )(xk, xv, pos, *kw, *vw)


def _nsa_attn_kernel(qraw_ref, qrot_ref, kcmp_ref, vcmpt_ref, ks_ref, vst_ref, kw_ref, vwt_ref, gate_ref, o_ref,
                     m_ref, acc_ref, sa_ref, sb_ref, qa_ref):
    tq = Q_BLOCK
    gw = Q_PER_GROUP * tq
    seq = ks_ref.shape[1]
    n_cmp = kcmp_ref.shape[1]
    n_slc = seq // SLC_BLOCK
    n_sel = min(SLC_TOPK, n_slc)
    qb = pl.program_id(1)
    t0 = qb * tq
    gcols = lambda g: slice(g * gw, (g + 1) * gw)
    grows = lambda g: slice(g * HEAD_DIM, (g + 1) * HEAD_DIM)
    jcols = lambda j: slice(j * tq, (j + 1) * tq)

    t_c = t0 + lax.broadcasted_iota(jnp.int32, (n_cmp, tq), 1)
    n_c = lax.broadcasted_iota(jnp.int32, (n_cmp, tq), 0)
    cmask = (n_c * CMP_STRIDE + (CMP_BLOCK - 1)) <= t_c
    cbias = jnp.where(cmask, 0.0, NEG_INF)
    o_c, p_sum = [], []
    for g in range(KV_GROUPS):
        s = _dot(kcmp_ref[0], qraw_ref[0, :, gcols(g)])
        parts, total = [], None
        for j in range(Q_PER_GROUP):
            sj = s[:, jcols(j)] + cbias
            e = jnp.exp2(sj - jnp.max(sj, axis=0, keepdims=True))
            p = jnp.where(cmask, e * (1.0 / jnp.sum(e, axis=0, keepdims=True)), 0.0)
            total = p if total is None else total + p
            parts.append(p.astype(BF16))
        p_sum.append(total)
        o_c.append(_dot(vcmpt_ref[0, grows(g), :], jnp.concatenate(parts, axis=1)))

    n_band = WINDOW // tq + 1
    band = n_band * tq
    wb = jnp.maximum(qb - WINDOW // tq, 0)
    w0 = pl.multiple_of(wb * tq, tq)
    t_w = t0 + lax.broadcasted_iota(jnp.int32, (band, tq), 1)
    k_w = w0 + lax.broadcasted_iota(jnp.int32, (band, tq), 0)
    wbias = jnp.where((k_w <= t_w) & (k_w > t_w - WINDOW), 0.0, NEG_INF)
    kband = kw_ref[0, pl.ds(w0, band), :]
    o_w = []
    for g in range(KV_GROUPS):
        s = _dot(kband, qrot_ref[0, :, gcols(g)])
        parts = []
        for j in range(Q_PER_GROUP):
            sj = s[:, jcols(j)] + wbias
            parts.append(jnp.exp2(sj - jnp.max(sj, axis=0, keepdims=True)).astype(BF16))
        vband = jnp.concatenate([vwt_ref[0, wb + i, grows(g), :] for i in range(n_band)], axis=1)
        vband = jnp.concatenate([vband, jnp.ones((ONES_ROWS, band), BF16)], axis=0)
        ow = _dot(vband, jnp.concatenate(parts, axis=1))
        o_w.append(ow[:HEAD_DIM] * (1.0 / ow[HEAD_DIM:HEAD_DIM + 1]))

    m_o = lax.broadcasted_iota(jnp.int32, (n_slc, n_cmp), 0) * SLC_BLOCK
    n_o = lax.broadcasted_iota(jnp.int32, (n_slc, n_cmp), 1) * CMP_STRIDE
    overlap_t = jnp.where((n_o < m_o + SLC_BLOCK) & (n_o + (CMP_BLOCK - 1) >= m_o), 1.0, 0.0).astype(F32)
    m_i = lax.broadcasted_iota(jnp.int32, (n_slc, tq), 0)
    blk_t = _div(t0 + lax.broadcasted_iota(jnp.int32, (n_slc, tq), 1), SLC_BLOCK)
    valid = m_i <= blk_t
    forced = (m_i == 0) | (m_i == blk_t) | (m_i == blk_t - 1)
    sel = []
    for g in range(KV_GROUPS):
        imp = _dot(overlap_t, p_sum[g], passes=3)
        imp = jnp.where(valid, imp + jnp.where(forced, FORCE, 0.0), -FORCE)
        rank = jnp.zeros((n_slc, tq), F32)
        for mp in range(n_slc):
            other = imp[mp:mp + 1, :]
            rank = rank + jnp.where(m_i > mp, jnp.where(other >= imp, 1.0, 0.0), jnp.where(other > imp, 1.0, 0.0))
        sel.append(jnp.where((rank < n_sel) & valid, 0.0, NEG_INF).astype(BF16))

    m_ref[...] = jnp.full(m_ref.shape, NEG_INF, F32)
    acc_ref[...] = jnp.zeros(acc_ref.shape, F32)
    n_tiles = (t0 + tq + TK_SEL - 1) // TK_SEL
    pad = jnp.zeros((LANES - n_slc, gw), BF16)
    for g in range(KV_GROUPS):
        qa_ref[g] = jnp.concatenate([qrot_ref[0, :, gcols(g)], jnp.concatenate([sel[g]] * Q_PER_GROUP, axis=1), pad],
                                    axis=0)
    r0 = pl.multiple_of(t0 - (n_tiles - 1) * TK_SEL, tq)
    tri = jnp.where(lax.broadcasted_iota(jnp.int32, (tq, tq), 0) > lax.broadcasted_iota(jnp.int32, (tq, tq), 1),
                    NEG_INF, 0.0)

    def scores(kt, s_ref):
        k0 = pl.multiple_of(jnp.minimum(kt, n_tiles - 1) * TK_SEL, TK_SEL)
        keys = ks_ref[0, pl.ds(k0, TK_SEL), :]
        own = jnp.where(kt >= n_tiles - 1, tri, 0.0)
        for g in range(KV_GROUPS):
            s_ref[g] = _dot(keys, qa_ref[g])
            for j in range(Q_PER_GROUP):
                s_ref[g, pl.ds(r0, tq), jcols(j)] += own

    def attend(kt, s_ref):
        for g in range(KV_GROUPS):
            parts, alphas = [], []
            for j in range(Q_PER_GROUP):
                cs = slice(g * gw + j * tq, g * gw + (j + 1) * tq)
                sj = s_ref[g, :, jcols(j)]
                m_old = m_ref[:, cs]
                m_new = jnp.maximum(m_old, jnp.max(sj, axis=0, keepdims=True))
                m_ref[:, cs] = m_new
                alphas.append(jnp.exp2(m_old - m_new))
                parts.append(jnp.exp2(sj - m_new).astype(BF16))
            vals = jnp.concatenate([vst_ref[0, kt, grows(g), :], jnp.ones((ONES_ROWS, TK_SEL), BF16)], axis=0)
            acc_ref[g] = acc_ref[g] * jnp.concatenate(alphas, axis=1) + _dot(vals, jnp.concatenate(parts, axis=1))

    odd = lax.rem(n_tiles, 2)
    scores(0, sa_ref)

    @pl.when(odd == 1)
    def _():
        attend(0, sa_ref)
        scores(1, sa_ref)

    def sel_pair(i, carry):
        kt = odd + 2 * i
        scores(kt + 1, sb_ref)
        attend(kt, sa_ref)
        scores(kt + 2, sa_ref)
        attend(kt + 1, sb_ref)
        return carry

    lax.fori_loop(0, (n_tiles - odd) // 2, sel_pair, 0)

    gates = gate_ref[0]
    outs = []
    for j in range(Q_PER_GROUP):
        for g in range(KV_GROUPS):
            row = (g * Q_PER_GROUP + j) * 3
            acc = acc_ref[g][:, jcols(j)]
            o_s = acc[:HEAD_DIM] * (1.0 / acc[HEAD_DIM:HEAD_DIM + 1])
            outs.append(gates[row:row + 1, :] * o_c[g][:, jcols(j)] + gates[row + 1:row + 2, :] * o_s
                        + gates[row + 2:row + 3, :] * o_w[g][:, jcols(j)])
    o_ref[0] = jnp.concatenate(outs, axis=0).T.astype(o_ref.dtype)


def _nsa_attn(q_raw, q_rot, k_cmp, v_cmp_t, ks, vs_t, kw, vw_t, gates_t):
    bn, s, _ = ks.shape
    n_cmp = k_cmp.shape[1]
    qspec = pl.BlockSpec((1, LANES, QT_COLS), lambda b, i: (b, 0, i))
    kspec = pl.BlockSpec((1, s, LANES), lambda b, i: (b, 0, 0))
    whole = lambda a: pl.BlockSpec((1,) + a.shape[1:], lambda b, i: (b,) + (0,) * (a.ndim - 1))
    return pl.pallas_call(
        _nsa_attn_kernel,
        grid=(bn, s // Q_BLOCK),
        in_specs=[qspec, qspec, whole(k_cmp), whole(v_cmp_t), whole(ks), whole(vs_t), kspec, whole(vw_t),
                  pl.BlockSpec((1, GATE_ROWS, Q_BLOCK), lambda b, i: (b, 0, i))],
        out_specs=pl.BlockSpec((1, Q_BLOCK, C_WIDTH), lambda b, i: (b, i, 0)),
        out_shape=jax.ShapeDtypeStruct((bn, s, C_WIDTH), BF16),
        scratch_shapes=[pltpu.VMEM((1, QT_COLS), F32),
                        pltpu.VMEM((KV_GROUPS, HEAD_DIM + ONES_ROWS, Q_PER_GROUP * Q_BLOCK), F32),
                        pltpu.VMEM((KV_GROUPS, TK_SEL, Q_PER_GROUP * Q_BLOCK), F32),
                        pltpu.VMEM((KV_GROUPS, TK_SEL, Q_PER_GROUP * Q_BLOCK), F32),
                        pltpu.VMEM((KV_GROUPS, 2 * LANES, Q_PER_GROUP * Q_BLOCK), BF16)],
        compiler_params=_params("parallel", "arbitrary"),
        name="nsa_attn",
    )(q_raw, q_rot, k_cmp, v_cmp_t, ks, vs_t, kw, vw_t, gates_t)


def _mixed_residual(h_ref, ya_ref, yb_ref, yc_ref, w_ref):
    acc = jnp.dot(ya_ref[...], w_ref[0:A_WIDTH, :], preferred_element_type=F32)
    acc = acc + jnp.dot(yb_ref[...], w_ref[A_WIDTH:A_WIDTH + B_WIDTH, :], preferred_element_type=F32)
    acc = acc + jnp.dot(yc_ref[...], w_ref[A_WIDTH + B_WIDTH:, :], preferred_element_type=F32)
    return h_ref[...] + acc


def _mixed_specs(tm):
    tok = lambda width: pl.BlockSpec((tm, width), lambda i: (i, 0))
    return [tok(D_MODEL), tok(A_WIDTH), tok(B_WIDTH), tok(C_WIDTH), pl.BlockSpec((D_MODEL, D_MODEL), lambda i: (0, 0))]


def _swiglu_step(x, w1_ref, w3_ref, w2_ref):
    h1 = jnp.dot(x, w1_ref[0], preferred_element_type=F32)
    h3 = jnp.dot(x, w3_ref[0], preferred_element_type=F32)
    hid = h1 * _sigmoid(h1) * h3
    return jnp.dot(hid.astype(BF16), w2_ref[0], preferred_element_type=F32)


def _ffn_kernel(h_ref, ya_ref, yb_ref, yc_ref, wo_ref, g_ref, w1_ref, w3_ref, w2_ref, o_ref):
    h = _mixed_residual(h_ref, ya_ref, yb_ref, yc_ref, wo_ref)
    x = _rms(h, g_ref[...]).astype(BF16)
    o_ref[...] = h + _swiglu_step(x, w1_ref, w3_ref, w2_ref)


def _ffn(h, ya, yb, yc, w_out, g, w1, w3, w2):
    n = h.shape[0]
    tm = min(TM_FFN, n)
    resident = lambda shape: pl.BlockSpec(shape, lambda i: (0, 0, 0), pipeline_mode=pl.Buffered(1))
    return pl.pallas_call(
        _ffn_kernel,
        grid=(n // tm,),
        in_specs=_mixed_specs(tm) + [
            pl.BlockSpec((1, D_MODEL), lambda i: (0, 0)),
            resident((1, D_MODEL, D_FF)), resident((1, D_MODEL, D_FF)), resident((1, D_FF, D_MODEL))],
        out_specs=pl.BlockSpec((tm, D_MODEL), lambda i: (i, 0)),
        out_shape=jax.ShapeDtypeStruct((n, D_MODEL), F32),
        compiler_params=_params("parallel"),
        name="ffn_swiglu",
    )(h, ya, yb, yc, w_out, g, w1, w3, w2)


META_E1, META_E2, META_R1, META_R2, META_P1, META_P2 = range(6)


def _router_kernel(h_ref, ya_ref, yb_ref, yc_ref, wo_ref, g_ref, rw_ref, rb_ref,
                   hmid_ref, hn_ref, meta_ref, cnt_ref, xs_zero_ref, carry_ref):
    @pl.when(pl.program_id(0) == 0)
    def _():
        carry_ref[...] = jnp.zeros_like(carry_ref)

    xs_zero_ref[...] = jnp.zeros_like(xs_zero_ref)

    h = _mixed_residual(h_ref, ya_ref, yb_ref, yc_ref, wo_ref)
    hmid_ref[...] = h
    hn = _rms(h, g_ref[...])
    hn_ref[...] = hn
    tm = hn.shape[0]
    lane = lax.broadcasted_iota(jnp.int32, (tm, LANES), 1).astype(F32)
    logits = jnp.where(lane < N_EXPERTS, _dot(hn, rw_ref[...], passes=3) + rb_ref[...], NEG_INF)
    top1 = jnp.max(logits, axis=-1, keepdims=True)
    idx1 = jnp.min(jnp.where(logits == top1, lane, float(LANES)), axis=-1, keepdims=True)
    rest = jnp.where(lane == idx1, NEG_INF, logits)
    top2 = jnp.max(rest, axis=-1, keepdims=True)
    idx2 = jnp.min(jnp.where(rest == top2, lane, float(LANES)), axis=-1, keepdims=True)
    ex = jnp.exp(top2 - top1)
    picked = jnp.where((lane == idx1) | (lane == idx2), 1.0, 0.0)
    r = lax.broadcasted_iota(jnp.int32, (tm, tm), 0)
    c = lax.broadcasted_iota(jnp.int32, (tm, tm), 1)
    before = _dot(jnp.where(c < r, 1.0, 0.0), picked) + carry_ref[...]
    rank1 = jnp.sum(jnp.where(lane == idx1, before, 0.0), axis=-1, keepdims=True)
    rank2 = jnp.sum(jnp.where(lane == idx2, before, 0.0), axis=-1, keepdims=True)
    carry_ref[...] += jnp.sum(picked, axis=0, keepdims=True)
    cnt_ref[...] = carry_ref[...]
    meta = jnp.zeros((tm, LANES), F32)
    for pos, val in ((META_E1, idx1), (META_E2, idx2), (META_R1, rank1), (META_R2, rank2),
                     (META_P1, 1.0 / (1.0 + ex)), (META_P2, ex / (1.0 + ex))):
        meta = jnp.where(lane == pos, val, meta)
    meta_ref[...] = meta


def _dispatch_kernel(d1_ref, d2_ref, hn_ref, xs_in_ref, xs_ref, sem):
    del xs_in_ref
    tm = hn_ref.shape[0]

    def row_copy(r, dest):
        return pltpu.make_async_copy(hn_ref.at[pl.ds(r, 1)], xs_ref.at[pl.ds(dest, 1)], sem)

    def issue(r, carry):
        row_copy(r, d1_ref[0, 0, r]).start()
        row_copy(r, d2_ref[0, 0, r]).start()
        return carry

    lax.fori_loop(0, tm, issue, 0, unroll=8)
    for _ in range(2):
        pltpu.make_async_copy(hn_ref, xs_ref.at[pl.ds(0, tm)], sem).wait()


def _grouped_kernel(te_ref, nu_ref, xs_ref, w1_ref, w3_ref, w2_ref, ys_ref):
    del te_ref
    used = pl.program_id(0) < nu_ref[0]

    @pl.when(used)
    def _():
        ys_ref[...] = _swiglu_step(xs_ref[...].astype(BF16), w1_ref, w3_ref, w2_ref)

    @pl.when(jnp.logical_not(used))
    def _():
        ys_ref[...] = jnp.zeros_like(ys_ref)


def _combine_kernel(d1_ref, d2_ref, d1n_ref, d2n_ref, h_ref, meta_ref, p_ref, g_ref, wg_ref, wp_ref, gf_ref, ys_ref,
                    o_ref, buf1, buf2, sem, *, ple_final):
    i = pl.program_id(0)
    tm = h_ref.shape[0]
    slot = lax.rem(i, 2)

    def gather(da_ref, db_ref, s):
        def issue(r, carry):
            pltpu.make_async_copy(ys_ref.at[pl.ds(da_ref[0, 0, r], 1)], buf1.at[s, pl.ds(r, 1)], sem.at[s]).start()
            pltpu.make_async_copy(ys_ref.at[pl.ds(db_ref[0, 0, r], 1)], buf2.at[s, pl.ds(r, 1)], sem.at[s]).start()
            return carry
        lax.fori_loop(0, tm, issue, 0, unroll=8)

    @pl.when(i == 0)
    def _():
        gather(d1_ref, d2_ref, 0)

    @pl.when(i + 1 < pl.num_programs(0))
    def _():
        gather(d1n_ref, d2n_ref, 1 - slot)

    for buf in (buf1, buf2):
        pltpu.make_async_copy(ys_ref.at[pl.ds(0, tm)], buf.at[slot], sem.at[slot]).wait()
    meta = meta_ref[...]
    h = h_ref[...] + meta[:, META_P1:META_P1 + 1] * buf1[slot] + meta[:, META_P2:META_P2 + 1] * buf2[slot]
    if ple_final:
        h = _rms(_ple_update(h, p_ref, g_ref, wg_ref, wp_ref), gf_ref[...])
    o_ref[...] = h


def _moe(h, ya, yb, yc, w_out, g, router_w, router_b, w1, w3, w2, ple_final=None):
    n = h.shape[0]
    tm = min(TM_FFN, n)
    tg = min(TM_GROUP, n)
    nt = n // tm
    rw = jnp.pad(router_w, ((0, 0), (0, LANES - N_EXPERTS)))
    rb = jnp.pad(router_b.reshape(1, -1), ((0, 0), (0, LANES - N_EXPERTS)))
    tok = lambda width: pl.BlockSpec((tm, width), lambda i: (i, 0))
    rows = TOP_K * n + N_EXPERTS * tg
    assert rows % nt == 0
    h, hn, meta, cnt, xs_zero = pl.pallas_call(
        _router_kernel,
        grid=(nt,),
        in_specs=_mixed_specs(tm) + [
            pl.BlockSpec((1, D_MODEL), lambda i: (0, 0)),
            pl.BlockSpec((D_MODEL, LANES), lambda i: (0, 0)), pl.BlockSpec((1, LANES), lambda i: (0, 0))],
        out_specs=[tok(D_MODEL), tok(D_MODEL), tok(LANES), pl.BlockSpec((1, LANES), lambda i: (0, 0)),
                   pl.BlockSpec((rows // nt, D_MODEL), lambda i: (i, 0))],
        out_shape=[jax.ShapeDtypeStruct((n, D_MODEL), F32), jax.ShapeDtypeStruct((n, D_MODEL), F32),
                   jax.ShapeDtypeStruct((n, LANES), F32), jax.ShapeDtypeStruct((1, LANES), F32),
                   jax.ShapeDtypeStruct((rows, D_MODEL), F32)],
        scratch_shapes=[pltpu.VMEM((1, LANES), F32)],
        compiler_params=_params("arbitrary"),
        name="moe_router",
    )(h, ya, yb, yc, w_out, g, rw, rb)

    counts = cnt[0, :N_EXPERTS].astype(jnp.int32)
    padded = ((counts + tg - 1) // tg) * tg
    ends = jnp.cumsum(padded)
    offs = ends - padded
    n_tiles = rows // tg
    tile_start = jnp.arange(n_tiles, dtype=jnp.int32) * tg
    n_used = (ends[-1:] // tg).astype(jnp.int32)
    tile_expert = jnp.sum(tile_start[:, None] >= ends[None, :], axis=1).astype(jnp.int32)
    tile_expert = jnp.minimum(tile_expert, tile_expert[n_used[0] - 1])
    onehot = lambda e: (e[:, None] == jnp.arange(N_EXPERTS, dtype=jnp.int32)[None, :]).astype(jnp.int32)
    dest = lambda e, r: (jnp.sum(onehot(e.astype(jnp.int32)) * offs[None, :], axis=1)
                         + r.astype(jnp.int32)).reshape(nt, 1, tm)
    d1 = dest(meta[:, META_E1], meta[:, META_R1])
    d2 = dest(meta[:, META_E2], meta[:, META_R2])
    dspec = pl.BlockSpec((1, 1, tm), lambda i: (i, 0, 0), memory_space=pltpu.SMEM)

    xs = pl.pallas_call(
        _dispatch_kernel,
        grid=(nt,),
        in_specs=[dspec, dspec, tok(D_MODEL), pl.BlockSpec(memory_space=pl.ANY)],
        out_specs=pl.BlockSpec(memory_space=pl.ANY),
        out_shape=jax.ShapeDtypeStruct((rows, D_MODEL), F32),
        scratch_shapes=[pltpu.SemaphoreType.DMA(())],
        input_output_aliases={3: 0},
        compiler_params=_params("arbitrary"),
        name="moe_dispatch",
    )(d1, d2, hn, xs_zero)

    wspec = lambda shape: pl.BlockSpec(shape, lambda i, te, nu: (te[i], 0, 0))
    ys = pl.pallas_call(
        _grouped_kernel,
        grid_spec=pltpu.PrefetchScalarGridSpec(
            num_scalar_prefetch=2,
            grid=(n_tiles,),
            in_specs=[pl.BlockSpec((tg, D_MODEL), lambda i, te, nu: (jnp.minimum(i, nu[0] - 1), 0)),
                      wspec((1, D_MODEL, D_FF)), wspec((1, D_MODEL, D_FF)), wspec((1, D_FF, D_MODEL))],
            out_specs=pl.BlockSpec((tg, D_MODEL), lambda i, te, nu: (i, 0))),
        out_shape=jax.ShapeDtypeStruct((rows, D_MODEL), F32),
        compiler_params=_params("arbitrary"),
        name="moe_grouped",
    )(tile_expert, n_used, xs, w1, w3, w2)

    dnext = pl.BlockSpec((1, 1, tm), lambda i: (jnp.minimum(i + 1, nt - 1), 0, 0), memory_space=pltpu.SMEM)
    const = lambda a: pl.BlockSpec(a.shape, lambda i: (0, 0))
    if ple_final is None:
        ple_args = tuple(jnp.zeros((8, LANES), F32) for _ in range(5))
        ple_specs = [const(a) for a in ple_args]
    else:
        ple_args = ple_final
        ple_specs = [tok(PLE_DIM)] + [const(a) for a in ple_final[1:]]
    return pl.pallas_call(
        functools.partial(_combine_kernel, ple_final=ple_final is not None),
        grid=(nt,),
        in_specs=[dspec, dspec, dnext, dnext, tok(D_MODEL), tok(LANES)] + ple_specs + [pl.BlockSpec(memory_space=pl.ANY)],
        out_specs=tok(D_MODEL),
        out_shape=jax.ShapeDtypeStruct((n, D_MODEL), F32),
        scratch_shapes=[pltpu.VMEM((2, tm, D_MODEL), F32), pltpu.VMEM((2, tm, D_MODEL), F32),
                        pltpu.SemaphoreType.DMA((2,))],
        compiler_params=_params("arbitrary"),
        name="moe_combine",
    )(d1, d2, d1, d2, h, meta, *ple_args, ys)


def _ple_update(h, p_ref, g_ref, wg_ref, wp_ref):
    gate = _sigmoid(jnp.dot(_rms(h, g_ref[...]).astype(BF16), wg_ref[...], preferred_element_type=F32))
    return h + jnp.dot(p_ref[...].astype(BF16), wp_ref[...], preferred_element_type=F32) * gate


def _ple_final_kernel(h_ref, p_ref, g_ref, wg_ref, wp_ref, gf_ref, o_ref):
    o_ref[...] = _rms(_ple_update(h_ref[...], p_ref, g_ref, wg_ref, wp_ref), gf_ref[...])


def _ple_proj_kernel(h_ref, p_ref, g_ref, wg_ref, wp_ref, gn_ref, wn_ref, o_ref, z_ref):
    h = _ple_update(h_ref[...], p_ref, g_ref, wg_ref, wp_ref)
    o_ref[...] = h
    z_ref[...] = jnp.dot(_rms(h, gn_ref[...]).astype(BF16), wn_ref[...],
                         preferred_element_type=F32).astype(z_ref.dtype)


def _ple(h, p, g, wg, wp, g_tail, w_next=None):
    n = h.shape[0]
    tm = min(TM_PROJ, n)
    tok = lambda width: pl.BlockSpec((tm, width), lambda i: (i, 0))
    const = lambda a: pl.BlockSpec(a.shape, lambda i: (0, 0))
    common = dict(grid=(n // tm,), compiler_params=_params("parallel"))
    in_specs = [tok(D_MODEL), tok(PLE_DIM), const(g), const(wg), const(wp), const(g_tail)]
    h_sds = jax.ShapeDtypeStruct((n, D_MODEL), F32)
    if w_next is None:
        return pl.pallas_call(_ple_final_kernel, in_specs=in_specs, out_specs=tok(D_MODEL), out_shape=h_sds,
                              name="ple_final", **common)(h, p, g, wg, wp, g_tail)
    return pl.pallas_call(_ple_proj_kernel, in_specs=in_specs + [const(w_next)],
                          out_specs=[tok(D_MODEL), tok(Z_W)],
                          out_shape=[h_sds, jax.ShapeDtypeStruct((n, Z_W), Z_DTYPE)],
                          name="ple_proj_in", **common)(h, p, g, wg, wp, g_tail, w_next)


def _q_perm():
    idx = []
    for j in range(Q_PER_GROUP):
        for g in range(KV_GROUPS):
            base = (g * Q_PER_GROUP + j) * HEAD_DIM
            idx.extend(range(base, base + HEAD_DIM))
    return jnp.asarray(idx, dtype=jnp.int32)


def _layout_w_in(w):
    a1 = 2 * A_WIDTH
    b1 = a1 + 4 * B_WIDTH
    q1 = b1 + C_WIDTH
    w = jnp.concatenate([w[:, a1:b1], w[:, :a1], w[:, b1:q1][:, _q_perm()], w[:, q1:]], axis=1)
    return jnp.pad(w, ((0, 0), (0, Z_W - w.shape[1]))).astype(BF16)


def _layout_w_out(w):
    c0 = A_WIDTH + B_WIDTH
    return jnp.concatenate([w[:c0], w[c0:][_q_perm()]], axis=0).astype(BF16)


def _mixers(z, bn, s, tabs, gm_ln_g, gm_ln_b, gm_ws, gm_bs,
            rw_mu, rw_w0, rw_w_up, rw_a0, rw_a_up, rw_g_up, rw_k_k, rw_k_a, rw_r_k, rw_gn_g, rw_gn_b,
            nsa_cmp_pos, nsa_kc_w1, nsa_kc_w2, nsa_vc_w1, nsa_vc_w2):
    n = bn * s
    z3 = z.reshape(bn, s, Z_W)
    y_a = _gmlp(z3, gm_ln_g, gm_ln_b, gm_ws, gm_bs)
    y_b = _rwkv(z3, rw_mu, rw_w0, rw_w_up, rw_a0, rw_a_up, rw_g_up, rw_k_k, rw_k_a, rw_r_k, rw_gn_g, rw_gn_b)
    q_raw, q_rot, kc, vc, ks, vs, kw, vw, gates = _nsa_prep(z3, tabs)
    k_cmp, v_cmp = _nsa_compress(kc, vc, nsa_cmp_pos, nsa_kc_w1, nsa_kc_w2, nsa_vc_w1, nsa_vc_w2)
    y_c = _nsa_attn(q_raw, q_rot, k_cmp, v_cmp, ks, vs, kw, vw, gates)
    return y_a.reshape(n, -1), y_b.reshape(n, -1), y_c.reshape(n, -1)


def kernel(x, p, positions, g_mix, w_in, w_out, gm_ln_g, gm_ln_b, gm_ws, gm_bs, rw_mu, rw_w0, rw_w_up, rw_a0,
           rw_a_up, rw_g_up, rw_k_k, rw_k_a, rw_r_k, rw_gn_g, rw_gn_b, nsa_cmp_pos, nsa_kc_w1, nsa_kc_w2,
           nsa_vc_w1, nsa_vc_w2, g_ffn, ffn_w1, ffn_w3, ffn_w2, router_w, router_b, moe_w1, moe_w3, moe_w2,
           g_ple, ple_gate_w, ple_proj_w, g_final):
    bn, s, _ = x.shape
    n = bn * s
    depth = g_mix.shape[0]
    tabs = _rope_tables(positions)
    h = x.reshape(n, D_MODEL)
    z = _proj_in(h, g_mix[0].reshape(1, -1), _layout_w_in(w_in[0]))
    for i in range(depth):
        ys = _mixers(z, bn, s, tabs, gm_ln_g[i], gm_ln_b[i], gm_ws[i], gm_bs[i],
                     rw_mu[i], rw_w0[i], rw_w_up[i], rw_a0[i], rw_a_up[i], rw_g_up[i], rw_k_k[i], rw_k_a[i],
                     rw_r_k[i], rw_gn_g[i], rw_gn_b[i], nsa_cmp_pos[i], nsa_kc_w1[i], nsa_kc_w2[i],
                     nsa_vc_w1[i], nsa_vc_w2[i])
        wo = _layout_w_out(w_out[i])
        j = i // 2
        ple_args = (p[i].reshape(n, PLE_DIM), g_ple[i].reshape(1, -1), ple_gate_w[i].astype(BF16),
                    ple_proj_w[i].astype(BF16))
        last = i == depth - 1
        fused_tail = last and i % 2 == 1
        if i % 2 == 0:
            h = _ffn(h, *ys, wo, g_ffn[i].reshape(1, -1), ffn_w1[j:j + 1].astype(BF16),
                     ffn_w3[j:j + 1].astype(BF16), ffn_w2[j:j + 1].astype(BF16))
        else:
            h = _moe(h, *ys, wo, g_ffn[i].reshape(1, -1), router_w[j], router_b[j], moe_w1[j].astype(BF16),
                     moe_w3[j].astype(BF16), moe_w2[j].astype(BF16),
                     ple_final=ple_args + (g_final.reshape(1, -1),) if fused_tail else None)
        if fused_tail:
            pass
        elif last:
            h = _ple(h, *ple_args, g_final.reshape(1, -1))
        else:
            h, z = _ple(h, *ple_args, g_mix[i + 1].reshape(1, -1), _layout_w_in(w_in[i + 1]))
    return h.reshape(bn, s, D_MODEL)
```

```python
import functools
import math

import jax
import jax.numpy as jnp
from jax import lax
from jax.experimental import pallas as pl
from jax.experimental.pallas import tpu as pltpu

F32 = jnp.float32
BF16 = jnp.bfloat16

D_MODEL = 1024
HEAD_DIM = 64
A_HEADS = 4
B_HEADS = 4
A_WIDTH = 256
B_WIDTH = 256
C_WIDTH = 512
CHUNK = 128
LN_EPS = 1e-5
GN_EPS = 64e-5
RMS_EPS = 1e-6
RW_LORA = 64
KV_GROUPS = 2
Q_PER_GROUP = 4
CMP_BLOCK = 32
CMP_STRIDE = 16
CMP_HIDDEN = 128
SLC_BLOCK = 64
SLC_TOPK = 16
WINDOW = 512
Q_BLOCK = 128
NEG_INF = -1e30
FORCE = 1e4
ROPE_THETA = 500000.0
ROPE_DIM = 16
D_FF = 2816
N_EXPERTS = 8
TOP_K = 2
PLE_DIM = 256

LANES = 128
Z_B = 0
Z_A = 1024
Z_Q = 1536
Z_KV = 2048
Z_G = 2816
Z_W = 2944
Z_DTYPE = BF16

VMEM_LIMIT = 56 * 1024 * 1024
TM_PROJ = 512
TM_FFN = 512
TM_GROUP = 256
GM_T = 512
RW_T = 512
RW_C = 64
TK_SEL = 512

_NN = (((1,), (0,)), ((), ()))
_NT = (((1,), (1,)), ((), ()))


def _params(*sem):
    return pltpu.CompilerParams(dimension_semantics=sem, vmem_limit_bytes=VMEM_LIMIT)


def _dot(a, b, dn=_NN, passes=1):
    if passes == 6:
        return lax.dot_general(a.astype(F32), b.astype(F32), dn, precision=lax.Precision.HIGHEST,
                               preferred_element_type=F32)
    a_hi = a.astype(BF16)
    b_hi = b.astype(BF16)
    out = lax.dot_general(a_hi, b_hi, dn, preferred_element_type=F32)
    if passes == 1:
        return out
    a_lo = (a - a_hi.astype(F32)).astype(BF16)
    out = out + lax.dot_general(a_lo, b_hi, dn, preferred_element_type=F32)
    if passes == 2:
        return out
    b_lo = (b - b_hi.astype(F32)).astype(BF16)
    return out + lax.dot_general(a_hi, b_lo, dn, preferred_element_type=F32)


def _rms(x, g):
    return x * lax.rsqrt(jnp.mean(x * x, axis=-1, keepdims=True) + RMS_EPS) * g


def _sigmoid(x):
    return 1.0 / (1.0 + jnp.exp(-x))


def _div(x, d):
    return lax.shift_right_logical(x, jnp.int32(int(math.log2(d))))


def _head_block_ones(n):
    r = lax.broadcasted_iota(jnp.int32, (n, n), 0)
    c = lax.broadcasted_iota(jnp.int32, (n, n), 1)
    return jnp.where(_div(r, HEAD_DIM) == _div(c, HEAD_DIM), 1.0, 0.0).astype(F32)


def _rope_kernel(inv_ref, pos_ref, cos_ref, sin_ref):
    p = pos_ref[0].astype(F32)
    for j in range(ROPE_DIM // 2):
        ang = p * inv_ref[j]
        cos_ref[0, j] = jnp.cos(ang)
        sin_ref[0, j] = jnp.sin(ang)


def _rope_tables(positions):
    bn, s = positions.shape
    half = ROPE_DIM // 2
    inv = 1.0 / (ROPE_THETA ** (jnp.arange(0, ROPE_DIM, 2, dtype=F32) / ROPE_DIM))
    pos3 = positions.reshape(bn, s // LANES, LANES)
    cos, sin = pl.pallas_call(
        _rope_kernel,
        grid=(bn,),
        in_specs=[pl.BlockSpec(memory_space=pltpu.SMEM),
                  pl.BlockSpec((1, s // LANES, LANES), lambda b: (b, 0, 0))],
        out_specs=[pl.BlockSpec((1, half, s // LANES, LANES), lambda b: (b, 0, 0, 0))] * 2,
        out_shape=[jax.ShapeDtypeStruct((bn, half, s // LANES, LANES), F32)] * 2,
        compiler_params=_params("parallel"),
        name="rope_tables",
    )(inv, pos3)
    cos = cos.reshape(bn, half, s).transpose(0, 2, 1)
    sin = sin.reshape(bn, half, s).transpose(0, 2, 1)
    one = jnp.ones((bn, s, HEAD_DIM - ROPE_DIM), F32)
    zero = jnp.zeros((bn, s, HEAD_DIM - half), F32)
    tab_c = jnp.concatenate([cos, cos, one], axis=-1)
    tab_lo = jnp.concatenate([sin, zero], axis=-1)
    tab_hi = jnp.concatenate([zero[..., :half], sin, zero[..., :HEAD_DIM - ROPE_DIM]], axis=-1)
    rep = lambda t: jnp.tile(t, (1, 1, LANES // HEAD_DIM))
    return rep(tab_c), rep(tab_lo), rep(tab_hi)


def _proj_in_kernel(h_ref, g_ref, w_ref, z_ref):
    y = _rms(h_ref[...], g_ref[...])
    z_ref[...] = jnp.dot(y.astype(BF16), w_ref[...], preferred_element_type=F32).astype(z_ref.dtype)


def _proj_in(h, g, w):
    n = h.shape[0]
    tm = min(TM_PROJ, n)
    return pl.pallas_call(
        _proj_in_kernel,
        grid=(n // tm,),
        in_specs=[pl.BlockSpec((tm, D_MODEL), lambda i: (i, 0)),
                  pl.BlockSpec((1, D_MODEL), lambda i: (0, 0)),
                  pl.BlockSpec((D_MODEL, Z_W), lambda i: (0, 0))],
        out_specs=pl.BlockSpec((tm, Z_W), lambda i: (i, 0)),
        out_shape=jax.ShapeDtypeStruct((n, Z_W), Z_DTYPE),
        compiler_params=_params("parallel"),
        name="proj_in",
    )(h, g, w)


def _gmlp_kernel(z_ref, lng_ref, lnb_ref, ws_ref, bias_ref, o_ref):
    gz = jax.nn.gelu(z_ref[0].astype(F32))
    u = gz[:, :A_WIDTH]
    v = gz[:, A_WIDTH:]
    ones = _head_block_ones(A_WIDTH)
    mu = _dot(v, ones, passes=2) * (1.0 / HEAD_DIM)
    d = v - mu
    var = _dot(d * d, ones, passes=2) * (1.0 / HEAD_DIM)
    vn = d * lax.rsqrt(var + LN_EPS) * lng_ref[...] + lnb_ref[...]
    r = lax.broadcasted_iota(jnp.int32, (CHUNK, CHUNK), 0)
    c = lax.broadcasted_iota(jnp.int32, (CHUNK, CHUNK), 1)
    lane_head = _div(lax.broadcasted_iota(jnp.int32, (CHUNK, A_WIDTH), 1), HEAD_DIM)
    w_causal = [jnp.where(c <= r, ws_ref[hd], 0.0).astype(BF16) for hd in range(A_HEADS)]
    for ck in range(vn.shape[0] // CHUNK):
        rows = slice(ck * CHUNK, (ck + 1) * CHUNK)
        mixed = bias_ref[...]
        for hd in range(A_HEADS):
            mixed = mixed + _dot(w_causal[hd], jnp.where(lane_head == hd, vn[rows], 0.0))
        o_ref[0, rows, :] = (u[rows] * mixed).astype(o_ref.dtype)


def _gmlp(z3, ln_g, ln_b, w_s, b_s):
    bn, s, _ = z3.shape
    t_len = min(GM_T, s)
    bias = jnp.repeat(b_s.T, HEAD_DIM, axis=1)
    return pl.pallas_call(
        _gmlp_kernel,
        grid=(bn, s // t_len),
        in_specs=[pl.BlockSpec((1, t_len, 2 * A_WIDTH), lambda b, i: (b, i, Z_A // (2 * A_WIDTH))),
                  pl.BlockSpec((1, A_WIDTH), lambda b, i: (0, 0)),
                  pl.BlockSpec((1, A_WIDTH), lambda b, i: (0, 0)),
                  pl.BlockSpec((A_HEADS, CHUNK, CHUNK), lambda b, i: (0, 0, 0)),
                  pl.BlockSpec((CHUNK, A_WIDTH), lambda b, i: (0, 0))],
        out_specs=pl.BlockSpec((1, t_len, A_WIDTH), lambda b, i: (b, i, 0)),
        out_shape=jax.ShapeDtypeStruct((bn, s, A_WIDTH), BF16),
        compiler_params=_params("parallel", "parallel"),
        name="gmlp",
    )(z3, ln_g.reshape(1, A_WIDTH), ln_b.reshape(1, A_WIDTH), w_s, bias)


RW_PASSES = 1


def _rwkv_chunks(r, lw, k, v, kk, a, s0, c_len):
    nh = B_HEADS
    m = nh * c_len
    chunks = range(r.shape[0] // c_len)
    dot = functools.partial(_dot, passes=RW_PASSES)
    ti = lax.broadcasted_iota(jnp.int32, (c_len, c_len), 0)
    tj = lax.broadcasted_iota(jnp.int32, (c_len, c_len), 1)
    tril = jnp.where(tj <= ti, 1.0, 0.0).astype(F32)
    row_head = _div(lax.broadcasted_iota(jnp.int32, (m, B_WIDTH), 0), c_len)
    lane_head = _div(lax.broadcasted_iota(jnp.int32, (m, B_WIDTH), 1), HEAD_DIM)
    head_mask = row_head == lane_head
    rep = lambda x: jnp.concatenate([x] * nh, axis=0)
    stack = lambda x: jnp.where(head_mask, rep(x), 0.0)
    ri = lax.broadcasted_iota(jnp.int32, (m, m), 0)
    ci = lax.broadcasted_iota(jnp.int32, (m, m), 1)
    same = _div(ri, c_len) == _div(ci, c_len)
    strict = same & (ci < ri)
    incl = same & (ci <= ri)
    eye = ri == ci
    assert 2 * c_len == LANES
    low = lax.broadcasted_iota(jnp.int32, (2 * m, LANES), 1) < c_len

    a_st, r_st, v_st, bw_st, kw_st, w_c, l_ab, l_ak, l_rb, l_rk = ([] for _ in range(10))
    for c in chunks:
        sl = slice(c * c_len, (c + 1) * c_len)
        cum = _dot(tril, lw[sl], passes=3)
        w_t = jnp.exp(cum)
        w_i = jnp.exp(-cum)
        a_t = -kk[sl] * jnp.exp(cum - lw[sl])
        b_t = kk[sl] * a[sl] * w_i
        k_t = k[sl] * w_i
        w_c.append(w_t[c_len - 1:c_len, :])
        a_st.append(stack(a_t))
        r_st.append(stack(r[sl] * w_t))
        v_st.append(stack(v[sl]))
        bw_st.append(stack(b_t * w_c[c]))
        kw_st.append(stack(k_t * w_c[c]))
        cross = dot(jnp.concatenate([a_st[c], r_st[c]], axis=0), jnp.concatenate([b_t, k_t], axis=0), _NT)
        swapped = pltpu.roll(cross, c_len, axis=1)
        vs_b = jnp.concatenate([jnp.where(low, cross, swapped)] * (nh // 2), axis=1)
        vs_k = jnp.concatenate([jnp.where(low, swapped, cross)] * (nh // 2), axis=1)
        l_ab.append(jnp.where(strict, vs_b[:m], 0.0))
        l_ak.append(jnp.where(strict, vs_k[:m], 0.0))
        l_rb.append(jnp.where(incl, vs_b[m:], 0.0))
        l_rk.append(jnp.where(incl, vs_k[m:], 0.0))

    p_inv = [jnp.where(eye, 1.0, 0.0) + l_ab[c] for c in chunks]
    pw = l_ab
    for _ in range(int(math.log2(c_len)) - 1):
        pw = [dot(pw[c], pw[c]) for c in chunks]
        p_inv = [p_inv[c] + dot(p_inv[c], pw[c]) for c in chunks]
    t_m = [dot(l_ak[c], v_st[c]) for c in chunks]
    q_m = [dot(p_inv[c], t_m[c]) for c in chunks]
    p_m = [dot(p_inv[c], a_st[c]) for c in chunks]
    g_m = [r_st[c] + dot(l_rb[c], p_m[c]) for c in chunks]
    h_m = [dot(l_rb[c], q_m[c]) + dot(l_rk[c], v_st[c]) for c in chunks]
    m_m = [jnp.where(eye, w_c[c], 0.0) + dot(p_m[c].T, bw_st[c]) for c in chunks]
    n_m = [dot(q_m[c].T, bw_st[c]) + dot(v_st[c].T, kw_st[c]) for c in chunks]

    states = [s0]
    for c in chunks:
        states.append(dot(states[c], m_m[c]) + n_m[c])
    ys = []
    for c in chunks:
        y_st = dot(g_m[c], states[c], _NT) + h_m[c]
        y = y_st[0:c_len]
        for hd in range(1, nh):
            y = y + y_st[hd * c_len:(hd + 1) * c_len]
        ys.append(y)
    return jnp.concatenate(ys, axis=0), states[-1]


def _rwkv_kernel(z_ref, mu_ref, w0_ref, a0_ref, wwa_ref, gup_ref, kk_ref, ka_ref, rk_ref, gng_ref, gnb_ref,
                 o_ref, carry_ref, state_ref, *, c_len):
    @pl.when(pl.program_id(1) == 0)
    def _():
        carry_ref[...] = jnp.zeros_like(carry_ref)
        state_ref[...] = jnp.zeros_like(state_ref)

    zb = z_ref[0].astype(F32)
    t_len = zb.shape[0]
    row = lax.broadcasted_iota(jnp.int32, zb.shape, 0)
    z_prev = jnp.where(row == 0, carry_ref[...], pltpu.roll(zb, 1, axis=0))
    carry_ref[...] = zb[t_len - 1:t_len, :]
    zz = zb + (z_prev - zb) * mu_ref[...]
    r = zz[:, 0:B_WIDTH]
    k = zz[:, B_WIDTH:2 * B_WIDTH]
    v = zz[:, 2 * B_WIDTH:3 * B_WIDTH]
    wa = zz[:, 3 * B_WIDTH:3 * B_WIDTH + 2 * RW_LORA]
    gd = zz[:, 3 * B_WIDTH + 2 * RW_LORA:]
    lane = lax.broadcasted_iota(jnp.int32, wa.shape, 1)
    proj = _dot(jnp.where(lane < RW_LORA, jnp.tanh(wa), wa), wwa_ref[...])
    x = -(w0_ref[...] + proj[:, :B_WIDTH])
    softplus = jnp.maximum(x, 0.0) + jnp.log(1.0 + jnp.exp(-jnp.abs(x)))
    lw = -jnp.exp(-softplus - 0.5)
    a = _sigmoid(a0_ref[...] + proj[:, B_WIDTH:])
    g = _dot(_sigmoid(gd), gup_ref[...])
    ones = _head_block_ones(B_WIDTH)
    kk = k * kk_ref[...]
    kk = kk * lax.rsqrt(jnp.maximum(_dot(kk * kk, ones, passes=2), 1e-24))
    k2 = k * (1.0 + (a - 1.0) * ka_ref[...])

    y, state_ref[...] = _rwkv_chunks(r, lw, k2, v, kk, a, state_ref[...], c_len)

    mu_y = _dot(y, ones, passes=2) * (1.0 / HEAD_DIM)
    d = y - mu_y
    var = _dot(d * d, ones, passes=2) * (1.0 / HEAD_DIM)
    yn = d * lax.rsqrt(var + GN_EPS) * gng_ref[...] + gnb_ref[...]
    bonus = _dot(r * k2 * rk_ref[...], ones, passes=2) * v
    o_ref[0] = ((yn + bonus) * g).astype(o_ref.dtype)


def _rwkv(z3, mu, w0, w_up, a0, a_up, g_up, k_k, k_a, r_k, gn_g, gn_b):
    bn, s, _ = z3.shape
    t_len = min(RW_T, s)
    zero = jnp.zeros((RW_LORA, B_WIDTH), F32)
    wwa = jnp.concatenate([jnp.concatenate([w_up, zero], axis=1),
                           jnp.concatenate([zero, a_up], axis=1)], axis=0).astype(BF16)
    row = lambda t: t.reshape(1, -1)
    vec = pl.BlockSpec((1, B_WIDTH), lambda b, i: (0, 0))
    return pl.pallas_call(
        functools.partial(_rwkv_kernel, c_len=RW_C),
        grid=(bn, s // t_len),
        in_specs=[pl.BlockSpec((1, t_len, 4 * B_WIDTH), lambda b, i: (b, i, Z_B // (4 * B_WIDTH))),
                  pl.BlockSpec((1, 4 * B_WIDTH), lambda b, i: (0, 0)),
                  vec, vec,
                  pl.BlockSpec((2 * RW_LORA, 2 * B_WIDTH), lambda b, i: (0, 0)),
                  pl.BlockSpec((2 * RW_LORA, B_WIDTH), lambda b, i: (0, 0)),
                  vec, vec, vec, vec, vec],
        out_specs=pl.BlockSpec((1, t_len, B_WIDTH), lambda b, i: (b, i, 0)),
        out_shape=jax.ShapeDtypeStruct((bn, s, B_WIDTH), BF16),
        scratch_shapes=[pltpu.VMEM((1, 4 * B_WIDTH), F32), pltpu.VMEM((B_WIDTH, B_WIDTH), F32)],
        compiler_params=_params("parallel", "arbitrary"),
        name="rwkv7",
    )(z3, row(mu), row(w0), row(a0), wwa, g_up.astype(BF16), row(k_k), row(k_a), row(r_k), row(gn_g), row(gn_b))


def _rope(x, tab_c, tab_lo, tab_hi):
    n = x.shape[-1]
    half = ROPE_DIM // 2
    return x * tab_c - pltpu.roll(x, n - half, axis=1) * tab_lo + pltpu.roll(x, half, axis=1) * tab_hi


N_HEADS_C = KV_GROUPS * Q_PER_GROUP
QT_COLS = N_HEADS_C * Q_BLOCK
GATE_ROWS = 32
ONES_ROWS = 16
LOG2E = 1.4426950408889634


def _nsa_prep_kernel(zq_ref, zkc_ref, zks_ref, zkw_ref, zg_ref, tc_ref, tl_ref, th_ref,
                     qraw_ref, qrot_ref, kc_ref, vc_ref, ks_ref, vst_ref, kw_ref, vwt_ref, gate_ref):
    tab_c, tab_lo, tab_hi = tc_ref[0], tl_ref[0], th_ref[0]
    nrep = C_WIDTH // LANES
    wide = lambda t: jnp.concatenate([t] * nrep, axis=1)
    zkc, zks, zkw = (ref[0].astype(F32) for ref in (zkc_ref, zks_ref, zkw_ref))
    q = zq_ref[0].astype(F32) * (HEAD_DIM ** -0.5 * LOG2E)
    q_rot = _rope(q, wide(tab_c), wide(tab_lo), wide(tab_hi))
    ts = q.shape[0]
    row_group0 = lax.broadcasted_iota(jnp.int32, (LANES, ts), 0) < HEAD_DIM

    def put_queries(ref, x):
        for j in range(Q_PER_GROUP):
            xt = x[:, j * LANES:(j + 1) * LANES].T
            for g in range(KV_GROUPS):
                keep = row_group0 if g == 0 else jnp.logical_not(row_group0)
                xm = jnp.where(keep, xt, 0.0).astype(BF16)
                for qq in range(ts // Q_BLOCK):
                    col = qq * QT_COLS + (g * Q_PER_GROUP + j) * Q_BLOCK
                    ref[0, :, col:col + Q_BLOCK] = xm[:, qq * Q_BLOCK:(qq + 1) * Q_BLOCK]

    put_queries(qraw_ref, q)
    put_queries(qrot_ref, q_rot)
    kc_ref[0] = zkc[:, :LANES]
    vc_ref[0] = zkc[:, LANES:]
    ks_ref[0, :, :LANES] = _rope(zks[:, :LANES], tab_c, tab_lo, tab_hi).astype(BF16)
    key_block = _div(pl.program_id(1) * ts + lax.broadcasted_iota(jnp.int32, (ts, LANES), 0), SLC_BLOCK)
    ks_ref[0, :, LANES:] = jnp.where(key_block == lax.broadcasted_iota(jnp.int32, (ts, LANES), 1), 1.0, 0.0).astype(BF16)
    vst_ref[0, 0] = zks[:, LANES:].T.astype(BF16)
    kw_ref[0] = _rope(zkw[:, :LANES], tab_c, tab_lo, tab_hi).astype(BF16)
    vwt = zkw[:, LANES:].T.astype(BF16)
    for qq in range(ts // Q_BLOCK):
        vwt_ref[0, qq] = vwt[:, qq * Q_BLOCK:(qq + 1) * Q_BLOCK]
    gate_ref[0] = _sigmoid(zg_ref[0].astype(F32)).T[:GATE_ROWS, :]


def _nsa_prep(z3, tabs):
    bn, s, _ = z3.shape
    ts = TK_SEL
    zspec = lambda width, off: pl.BlockSpec((1, ts, width), lambda b, i: (b, i, off // width))
    tspec = pl.BlockSpec((1, ts, LANES), lambda b, i: (b, i, 0))
    rowmajor = pl.BlockSpec((1, ts, LANES), lambda b, i: (b, i, 0))
    qspec = pl.BlockSpec((1, LANES, N_HEADS_C * ts), lambda b, i: (b, 0, i))
    rm_sds = jax.ShapeDtypeStruct((bn, s, LANES), BF16)
    q_sds = jax.ShapeDtypeStruct((bn, LANES, N_HEADS_C * s), BF16)
    return pl.pallas_call(
        _nsa_prep_kernel,
        grid=(bn, s // ts),
        in_specs=[zspec(C_WIDTH, Z_Q), zspec(2 * LANES, Z_KV), zspec(2 * LANES, Z_KV + 2 * LANES),
                  zspec(2 * LANES, Z_KV + 4 * LANES), zspec(LANES, Z_G), tspec, tspec, tspec],
        out_specs=[qspec, qspec, rowmajor, rowmajor, pl.BlockSpec((1, ts, 2 * LANES), lambda b, i: (b, i, 0)),
                   pl.BlockSpec((1, 1, LANES, ts), lambda b, i: (b, i, 0, 0)), rowmajor,
                   pl.BlockSpec((1, ts // Q_BLOCK, LANES, Q_BLOCK), lambda b, i: (b, i, 0, 0)),
                   pl.BlockSpec((1, GATE_ROWS, ts), lambda b, i: (b, 0, i))],
        out_shape=[q_sds, q_sds, jax.ShapeDtypeStruct((bn, s, LANES), F32), jax.ShapeDtypeStruct((bn, s, LANES), F32),
                   jax.ShapeDtypeStruct((bn, s, 2 * LANES), BF16),
                   jax.ShapeDtypeStruct((bn, s // ts, LANES, ts), BF16), rm_sds,
                   jax.ShapeDtypeStruct((bn, s // Q_BLOCK, LANES, Q_BLOCK), BF16),
                   jax.ShapeDtypeStruct((bn, GATE_ROWS, s), F32)],
        compiler_params=_params("parallel", "parallel"),
        name="nsa_prep",
    )(z3, z3, z3, z3, z3, *tabs)


def _nsa_compress_kernel(xk_ref, xv_ref, pos_ref, kw1a_ref, kw1b_ref, kw2_ref, vw1a_ref, vw1b_ref, vw2_ref,
                         ko_ref, vo_ref):
    n = xk_ref.shape[1] // CMP_STRIDE
    pos_a = pos_ref[0]
    pos_b = pos_ref[1]
    for x_ref, w1a, w1b, w2, o_ref, transposed in ((xk_ref, kw1a_ref, kw1b_ref, kw2_ref, ko_ref, False),
                                                    (xv_ref, vw1a_ref, vw1b_ref, vw2_ref, vo_ref, True)):
        first = second = None
        for l in range(CMP_STRIDE):
            x = x_ref[0, pl.ds(l, n, stride=CMP_STRIDE), :]
            fa = _dot(x, w1a[l])
            sb = _dot(x, w1b[l])
            first = fa if first is None else first + fa
            second = sb if second is None else second + sb
        flat = lambda w: w[...].reshape(CMP_STRIDE * LANES, KV_GROUPS * CMP_HIDDEN)
        pc = _dot(pos_a, flat(w1a)) + _dot(pos_b, flat(w1b))
        hid = jax.nn.gelu(first + pltpu.roll(second, n - 1, axis=0) + pc[0:1, :])
        out = _dot(hid, w2[...])
        o_ref[0] = (out.T if transposed else out).astype(o_ref.dtype)


def _expand_cmp_weights(w1, w2):
    half = CMP_BLOCK // 2
    w1r = w1.reshape(CMP_BLOCK, HEAD_DIM, CMP_HIDDEN)
    eye = jnp.eye(KV_GROUPS, dtype=F32)
    w1e = jnp.einsum('ldh,gk->lgdkh', w1r, eye).reshape(CMP_BLOCK * LANES, KV_GROUPS * CMP_HIDDEN)
    w2e = jnp.einsum('hd,gk->ghkd', w2, eye).reshape(KV_GROUPS * CMP_HIDDEN, LANES)
    per_token = lambda w: w.reshape(half, LANES, KV_GROUPS * CMP_HIDDEN).astype(BF16)
    return (per_token(w1e[:half * LANES]), per_token(w1e[half * LANES:]), w2e.astype(BF16))


def _nsa_compress(kc, vc, cmp_pos, kc_w1, kc_w2, vc_w1, vc_w2):
    bn, s, _ = kc.shape
    n = s // CMP_STRIDE
    xw = CMP_STRIDE * LANES
    pos = jnp.tile(cmp_pos[:, None, :], (1, KV_GROUPS, 1)).reshape(2, 1, xw)
    pos = jnp.broadcast_to(pos, (2, 8, xw)).astype(BF16)
    kw = _expand_cmp_weights(kc_w1, kc_w2)
    vw = _expand_cmp_weights(vc_w1, vc_w2)
    full = lambda a: pl.BlockSpec(a.shape, lambda b: (0,) * a.ndim)
    xspec = pl.BlockSpec((1, s, LANES), lambda b: (b, 0, 0))
    return pl.pallas_call(
        _nsa_compress_kernel,
        grid=(bn,),
        in_specs=[xspec, xspec, full(pos)] + [full(a) for a in kw + vw],
        out_specs=[pl.BlockSpec((1, n, LANES), lambda b: (b, 0, 0)), pl.BlockSpec((1, LANES, n), lambda b: (b, 0, 0))],
        out_shape=[jax.ShapeDtypeStruct((bn, n, LANES), BF16), jax.ShapeDtypeStruct((bn, LANES, n), BF16)],
        compiler_params=_params("parallel"),
        name="nsa_compress",
    )(kc, vc, pos, *kw, *vw)


def _nsa_attn_kernel(qraw_ref, qrot_ref, kcmp_ref, vcmpt_ref, ks_ref, vst_ref, kw_ref, vwt_ref, gate_ref, o_ref,
                     m_ref, acc_ref, sa_ref, sb_ref, qa_ref):
    tq = Q_BLOCK
    gw = Q_PER_GROUP * tq
    seq = ks_ref.shape[1]
    n_cmp = kcmp_ref.shape[1]
    n_slc = seq // SLC_BLOCK
    n_sel = min(SLC_TOPK, n_slc)
    qb = pl.program_id(1)
    t0 = qb * tq
    gcols = lambda g: slice(g * gw, (g + 1) * gw)
    grows = lambda g: slice(g * HEAD_DIM, (g + 1) * HEAD_DIM)
    jcols = lambda j: slice(j * tq, (j + 1) * tq)

    t_c = t0 + lax.broadcasted_iota(jnp.int32, (n_cmp, tq), 1)
    n_c = lax.broadcasted_iota(jnp.int32, (n_cmp, tq), 0)
    cmask = (n_c * CMP_STRIDE + (CMP_BLOCK - 1)) <= t_c
    cbias = jnp.where(cmask, 0.0, NEG_INF)
    o_c, p_sum = [], []
    for g in range(KV_GROUPS):
        s = _dot(kcmp_ref[0], qraw_ref[0, :, gcols(g)])
        parts, total = [], None
        for j in range(Q_PER_GROUP):
            sj = s[:, jcols(j)] + cbias
            e = jnp.exp2(sj - jnp.max(sj, axis=0, keepdims=True))
            p = jnp.where(cmask, e * (1.0 / jnp.sum(e, axis=0, keepdims=True)), 0.0)
            total = p if total is None else total + p
            parts.append(p.astype(BF16))
        p_sum.append(total)
        o_c.append(_dot(vcmpt_ref[0, grows(g), :], jnp.concatenate(parts, axis=1)))

    n_band = WINDOW // tq + 1
    band = n_band * tq
    wb = jnp.maximum(qb - WINDOW // tq, 0)
    w0 = pl.multiple_of(wb * tq, tq)
    t_w = t0 + lax.broadcasted_iota(jnp.int32, (band, tq), 1)
    k_w = w0 + lax.broadcasted_iota(jnp.int32, (band, tq), 0)
    wbias = jnp.where((k_w <= t_w) & (k_w > t_w - WINDOW), 0.0, NEG_INF)
    kband = kw_ref[0, pl.ds(w0, band), :]
    o_w = []
    for g in range(KV_GROUPS):
        s = _dot(kband, qrot_ref[0, :, gcols(g)])
        parts = []
        for j in range(Q_PER_GROUP):
            sj = s[:, jcols(j)] + wbias
            parts.append(jnp.exp2(sj - jnp.max(sj, axis=0, keepdims=True)).astype(BF16))
        vband = jnp.concatenate([vwt_ref[0, wb + i, grows(g), :] for i in range(n_band)], axis=1)
        vband = jnp.concatenate([vband, jnp.ones((ONES_ROWS, band), BF16)], axis=0)
        ow = _dot(vband, jnp.concatenate(parts, axis=1))
        o_w.append(ow[:HEAD_DIM] * (1.0 / ow[HEAD_DIM:HEAD_DIM + 1]))

    m_o = lax.broadcasted_iota(jnp.int32, (n_slc, n_cmp), 0) * SLC_BLOCK
    n_o = lax.broadcasted_iota(jnp.int32, (n_slc, n_cmp), 1) * CMP_STRIDE
    overlap_t = jnp.where((n_o < m_o + SLC_BLOCK) & (n_o + (CMP_BLOCK - 1) >= m_o), 1.0, 0.0).astype(F32)
    m_i = lax.broadcasted_iota(jnp.int32, (n_slc, tq), 0)
    blk_t = _div(t0 + lax.broadcasted_iota(jnp.int32, (n_slc, tq), 1), SLC_BLOCK)
    valid = m_i <= blk_t
    forced = (m_i == 0) | (m_i == blk_t) | (m_i == blk_t - 1)
    sel = []
    for g in range(KV_GROUPS):
        imp = _dot(overlap_t, p_sum[g], passes=3)
        imp = jnp.where(valid, imp + jnp.where(forced, FORCE, 0.0), -FORCE)
        rank = jnp.zeros((n_slc, tq), F32)
        for mp in range(n_slc):
            other = imp[mp:mp + 1, :]
            rank = rank + jnp.where(m_i > mp, jnp.where(other >= imp, 1.0, 0.0), jnp.where(other > imp, 1.0, 0.0))
        sel.append(jnp.where((rank < n_sel) & valid, 0.0, NEG_INF).astype(BF16))

    m_ref[...] = jnp.full(m_ref.shape, NEG_INF, F32)
    acc_ref[...] = jnp.zeros(acc_ref.shape, F32)
    n_tiles = (t0 + tq + TK_SEL - 1) // TK_SEL
    pad = jnp.zeros((LANES - n_slc, gw), BF16)
    for g in range(KV_GROUPS):
        qa_ref[g] = jnp.concatenate([qrot_ref[0, :, gcols(g)], jnp.concatenate([sel[g]] * Q_PER_GROUP, axis=1), pad],
                                    axis=0)
    r0 = pl.multiple_of(t0 - (n_tiles - 1) * TK_SEL, tq)
    tri = jnp.where(lax.broadcasted_iota(jnp.int32, (tq, tq), 0) > lax.broadcasted_iota(jnp.int32, (tq, tq), 1),
                    NEG_INF, 0.0)

    def scores(kt, s_ref):
        k0 = pl.multiple_of(jnp.minimum(kt, n_tiles - 1) * TK_SEL, TK_SEL)
        keys = ks_ref[0, pl.ds(k0, TK_SEL), :]
        own = jnp.where(kt >= n_tiles - 1, tri, 0.0)
        for g in range(KV_GROUPS):
            s_ref[g] = _dot(keys, qa_ref[g])
            for j in range(Q_PER_GROUP):
                s_ref[g, pl.ds(r0, tq), jcols(j)] += own

    def attend(kt, s_ref):
        for g in range(KV_GROUPS):
            parts, alphas = [], []
            for j in range(Q_PER_GROUP):
                cs = slice(g * gw + j * tq, g * gw + (j + 1) * tq)
                sj = s_ref[g, :, jcols(j)]
                m_old = m_ref[:, cs]
                m_new = jnp.maximum(m_old, jnp.max(sj, axis=0, keepdims=True))
                m_ref[:, cs] = m_new
                alphas.append(jnp.exp2(m_old - m_new))
                parts.append(jnp.exp2(sj - m_new).astype(BF16))
            vals = jnp.concatenate([vst_ref[0, kt, grows(g), :], jnp.ones((ONES_ROWS, TK_SEL), BF16)], axis=0)
            acc_ref[g] = acc_ref[g] * jnp.concatenate(alphas, axis=1) + _dot(vals, jnp.concatenate(parts, axis=1))

    odd = lax.rem(n_tiles, 2)
    scores(0, sa_ref)

    @pl.when(odd == 1)
    def _():
        attend(0, sa_ref)
        scores(1, sa_ref)

    def sel_pair(i, carry):
        kt = odd + 2 * i
        scores(kt + 1, sb_ref)
        attend(kt, sa_ref)
        scores(kt + 2, sa_ref)
        attend(kt + 1, sb_ref)
        return carry

    lax.fori_loop(0, (n_tiles - odd) // 2, sel_pair, 0)

    gates = gate_ref[0]
    outs = []
    for j in range(Q_PER_GROUP):
        for g in range(KV_GROUPS):
            row = (g * Q_PER_GROUP + j) * 3
            acc = acc_ref[g][:, jcols(j)]
            o_s = acc[:HEAD_DIM] * (1.0 / acc[HEAD_DIM:HEAD_DIM + 1])
            outs.append(gates[row:row + 1, :] * o_c[g][:, jcols(j)] + gates[row + 1:row + 2, :] * o_s
                        + gates[row + 2:row + 3, :] * o_w[g][:, jcols(j)])
    o_ref[0] = jnp.concatenate(outs, axis=0).T.astype(o_ref.dtype)


def _nsa_attn(q_raw, q_rot, k_cmp, v_cmp_t, ks, vs_t, kw, vw_t, gates_t):
    bn, s, _ = ks.shape
    n_cmp = k_cmp.shape[1]
    qspec = pl.BlockSpec((1, LANES, QT_COLS), lambda b, i: (b, 0, i))
    kspec = pl.BlockSpec((1, s, LANES), lambda b, i: (b, 0, 0))
    whole = lambda a: pl.BlockSpec((1,) + a.shape[1:], lambda b, i: (b,) + (0,) * (a.ndim - 1))
    return pl.pallas_call(
        _nsa_attn_kernel,
        grid=(bn, s // Q_BLOCK),
        in_specs=[qspec, qspec, whole(k_cmp), whole(v_cmp_t), whole(ks), whole(vs_t), kspec, whole(vw_t),
                  pl.BlockSpec((1, GATE_ROWS, Q_BLOCK), lambda b, i: (b, 0, i))],
        out_specs=pl.BlockSpec((1, Q_BLOCK, C_WIDTH), lambda b, i: (b, i, 0)),
        out_shape=jax.ShapeDtypeStruct((bn, s, C_WIDTH), BF16),
        scratch_shapes=[pltpu.VMEM((1, QT_COLS), F32),
                        pltpu.VMEM((KV_GROUPS, HEAD_DIM + ONES_ROWS, Q_PER_GROUP * Q_BLOCK), F32),
                        pltpu.VMEM((KV_GROUPS, TK_SEL, Q_PER_GROUP * Q_BLOCK), F32),
                        pltpu.VMEM((KV_GROUPS, TK_SEL, Q_PER_GROUP * Q_BLOCK), F32),
                        pltpu.VMEM((KV_GROUPS, 2 * LANES, Q_PER_GROUP * Q_BLOCK), BF16)],
        compiler_params=_params("parallel", "arbitrary"),
        name="nsa_attn",
    )(q_raw, q_rot, k_cmp, v_cmp_t, ks, vs_t, kw, vw_t, gates_t)


def _mixed_residual(h_ref, ya_ref, yb_ref, yc_ref, w_ref):
    acc = jnp.dot(ya_ref[...], w_ref[0:A_WIDTH, :], preferred_element_type=F32)
    acc = acc + jnp.dot(yb_ref[...], w_ref[A_WIDTH:A_WIDTH + B_WIDTH, :], preferred_element_type=F32)
    acc = acc + jnp.dot(yc_ref[...], w_ref[A_WIDTH + B_WIDTH:, :], preferred_element_type=F32)
    return h_ref[...] + acc


def _mixed_specs(tm):
    tok = lambda width: pl.BlockSpec((tm, width), lambda i: (i, 0))
    return [tok(D_MODEL), tok(A_WIDTH), tok(B_WIDTH), tok(C_WIDTH), pl.BlockSpec((D_MODEL, D_MODEL), lambda i: (0, 0))]


def _swiglu_step(x, w1_ref, w3_ref, w2_ref):
    h1 = jnp.dot(x, w1_ref[0], preferred_element_type=F32)
    h3 = jnp.dot(x, w3_ref[0], preferred_element_type=F32)
    hid = h1 * _sigmoid(h1) * h3
    return jnp.dot(hid.astype(BF16), w2_ref[0], preferred_element_type=F32)


def _ffn_kernel(h_ref, ya_ref, yb_ref, yc_ref, wo_ref, g_ref, w1_ref, w3_ref, w2_ref, o_ref):
    h = _mixed_residual(h_ref, ya_ref, yb_ref, yc_ref, wo_ref)
    x = _rms(h, g_ref[...]).astype(BF16)
    o_ref[...] = h + _swiglu_step(x, w1_ref, w3_ref, w2_ref)


def _ffn(h, ya, yb, yc, w_out, g, w1, w3, w2):
    n = h.shape[0]
    tm = min(TM_FFN, n)
    resident = lambda shape: pl.BlockSpec(shape, lambda i: (0, 0, 0), pipeline_mode=pl.Buffered(1))
    return pl.pallas_call(
        _ffn_kernel,
        grid=(n // tm,),
        in_specs=_mixed_specs(tm) + [
            pl.BlockSpec((1, D_MODEL), lambda i: (0, 0)),
            resident((1, D_MODEL, D_FF)), resident((1, D_MODEL, D_FF)), resident((1, D_FF, D_MODEL))],
        out_specs=pl.BlockSpec((tm, D_MODEL), lambda i: (i, 0)),
        out_shape=jax.ShapeDtypeStruct((n, D_MODEL), F32),
        compiler_params=_params("parallel"),
        name="ffn_swiglu",
    )(h, ya, yb, yc, w_out, g, w1, w3, w2)


META_E1, META_E2, META_R1, META_R2, META_P1, META_P2 = range(6)


def _router_kernel(h_ref, ya_ref, yb_ref, yc_ref, wo_ref, g_ref, rw_ref, rb_ref,
                   hmid_ref, hn_ref, meta_ref, cnt_ref, xs_zero_ref, carry_ref):
    @pl.when(pl.program_id(0) == 0)
    def _():
        carry_ref[...] = jnp.zeros_like(carry_ref)

    xs_zero_ref[...] = jnp.zeros_like(xs_zero_ref)

    h = _mixed_residual(h_ref, ya_ref, yb_ref, yc_ref, wo_ref)
    hmid_ref[...] = h
    hn = _rms(h, g_ref[...])
    hn_ref[...] = hn
    tm = hn.shape[0]
    lane = lax.broadcasted_iota(jnp.int32, (tm, LANES), 1).astype(F32)
    logits = jnp.where(lane < N_EXPERTS, _dot(hn, rw_ref[...], passes=3) + rb_ref[...], NEG_INF)
    top1 = jnp.max(logits, axis=-1, keepdims=True)
    idx1 = jnp.min(jnp.where(logits == top1, lane, float(LANES)), axis=-1, keepdims=True)
    rest = jnp.where(lane == idx1, NEG_INF, logits)
    top2 = jnp.max(rest, axis=-1, keepdims=True)
    idx2 = jnp.min(jnp.where(rest == top2, lane, float(LANES)), axis=-1, keepdims=True)
    ex = jnp.exp(top2 - top1)
    picked = jnp.where((lane == idx1) | (lane == idx2), 1.0, 0.0)
    r = lax.broadcasted_iota(jnp.int32, (tm, tm), 0)
    c = lax.broadcasted_iota(jnp.int32, (tm, tm), 1)
    before = _dot(jnp.where(c < r, 1.0, 0.0), picked) + carry_ref[...]
    rank1 = jnp.sum(jnp.where(lane == idx1, before, 0.0), axis=-1, keepdims=True)
    rank2 = jnp.sum(jnp.where(lane == idx2, before, 0.0), axis=-1, keepdims=True)
    carry_ref[...] += jnp.sum(picked, axis=0, keepdims=True)
    cnt_ref[...] = carry_ref[...]
    meta = jnp.zeros((tm, LANES), F32)
    for pos, val in ((META_E1, idx1), (META_E2, idx2), (META_R1, rank1), (META_R2, rank2),
                     (META_P1, 1.0 / (1.0 + ex)), (META_P2, ex / (1.0 + ex))):
        meta = jnp.where(lane == pos, val, meta)
    meta_ref[...] = meta


def _dispatch_kernel(d1_ref, d2_ref, hn_ref, xs_in_ref, xs_ref, sem):
    del xs_in_ref
    tm = hn_ref.shape[0]

    def row_copy(r, dest):
        return pltpu.make_async_copy(hn_ref.at[pl.ds(r, 1)], xs_ref.at[pl.ds(dest, 1)], sem)

    def issue(r, carry):
        row_copy(r, d1_ref[0, 0, r]).start()
        row_copy(r, d2_ref[0, 0, r]).start()
        return carry

    lax.fori_loop(0, tm, issue, 0, unroll=8)
    for _ in range(2):
        pltpu.make_async_copy(hn_ref, xs_ref.at[pl.ds(0, tm)], sem).wait()


def _grouped_kernel(te_ref, nu_ref, xs_ref, w1_ref, w3_ref, w2_ref, ys_ref):
    del te_ref
    used = pl.program_id(0) < nu_ref[0]

    @pl.when(used)
    def _():
        ys_ref[...] = _swiglu_step(xs_ref[...].astype(BF16), w1_ref, w3_ref, w2_ref)

    @pl.when(jnp.logical_not(used))
    def _():
        ys_ref[...] = jnp.zeros_like(ys_ref)


def _combine_kernel(d1_ref, d2_ref, d1n_ref, d2n_ref, h_ref, meta_ref, p_ref, g_ref, wg_ref, wp_ref, gf_ref, ys_ref,
                    o_ref, buf1, buf2, sem, *, ple_final):
    i = pl.program_id(0)
    tm = h_ref.shape[0]
    slot = lax.rem(i, 2)

    def gather(da_ref, db_ref, s):
        def issue(r, carry):
            pltpu.make_async_copy(ys_ref.at[pl.ds(da_ref[0, 0, r], 1)], buf1.at[s, pl.ds(r, 1)], sem.at[s]).start()
            pltpu.make_async_copy(ys_ref.at[pl.ds(db_ref[0, 0, r], 1)], buf2.at[s, pl.ds(r, 1)], sem.at[s]).start()
            return carry
        lax.fori_loop(0, tm, issue, 0, unroll=8)

    @pl.when(i == 0)
    def _():
        gather(d1_ref, d2_ref, 0)

    @pl.when(i + 1 < pl.num_programs(0))
    def _():
        gather(d1n_ref, d2n_ref, 1 - slot)

    for buf in (buf1, buf2):
        pltpu.make_async_copy(ys_ref.at[pl.ds(0, tm)], buf.at[slot], sem.at[slot]).wait()
    meta = meta_ref[...]
    h = h_ref[...] + meta[:, META_P1:META_P1 + 1] * buf1[slot] + meta[:, META_P2:META_P2 + 1] * buf2[slot]
    if ple_final:
        h = _rms(_ple_update(h, p_ref, g_ref, wg_ref, wp_ref), gf_ref[...])
    o_ref[...] = h


def _moe(h, ya, yb, yc, w_out, g, router_w, router_b, w1, w3, w2, ple_final=None):
    n = h.shape[0]
    tm = min(TM_FFN, n)
    tg = min(TM_GROUP, n)
    nt = n // tm
    rw = jnp.pad(router_w, ((0, 0), (0, LANES - N_EXPERTS)))
    rb = jnp.pad(router_b.reshape(1, -1), ((0, 0), (0, LANES - N_EXPERTS)))
    tok = lambda width: pl.BlockSpec((tm, width), lambda i: (i, 0))
    rows = TOP_K * n + N_EXPERTS * tg
    assert rows % nt == 0
    h, hn, meta, cnt, xs_zero = pl.pallas_call(
        _router_kernel,
        grid=(nt,),
        in_specs=_mixed_specs(tm) + [
            pl.BlockSpec((1, D_MODEL), lambda i: (0, 0)),
            pl.BlockSpec((D_MODEL, LANES), lambda i: (0, 0)), pl.BlockSpec((1, LANES), lambda i: (0, 0))],
        out_specs=[tok(D_MODEL), tok(D_MODEL), tok(LANES), pl.BlockSpec((1, LANES), lambda i: (0, 0)),
                   pl.BlockSpec((rows // nt, D_MODEL), lambda i: (i, 0))],
        out_shape=[jax.ShapeDtypeStruct((n, D_MODEL), F32), jax.ShapeDtypeStruct((n, D_MODEL), F32),
                   jax.ShapeDtypeStruct((n, LANES), F32), jax.ShapeDtypeStruct((1, LANES), F32),
                   jax.ShapeDtypeStruct((rows, D_MODEL), F32)],
        scratch_shapes=[pltpu.VMEM((1, LANES), F32)],
        compiler_params=_params("arbitrary"),
        name="moe_router",
    )(h, ya, yb, yc, w_out, g, rw, rb)

    counts = cnt[0, :N_EXPERTS].astype(jnp.int32)
    padded = ((counts + tg - 1) // tg) * tg
    ends = jnp.cumsum(padded)
    offs = ends - padded
    n_tiles = rows // tg
    tile_start = jnp.arange(n_tiles, dtype=jnp.int32) * tg
    n_used = (ends[-1:] // tg).astype(jnp.int32)
    tile_expert = jnp.sum(tile_start[:, None] >= ends[None, :], axis=1).astype(jnp.int32)
    tile_expert = jnp.minimum(tile_expert, tile_expert[n_used[0] - 1])
    onehot = lambda e: (e[:, None] == jnp.arange(N_EXPERTS, dtype=jnp.int32)[None, :]).astype(jnp.int32)
    dest = lambda e, r: (jnp.sum(onehot(e.astype(jnp.int32)) * offs[None, :], axis=1)
                         + r.astype(jnp.int32)).reshape(nt, 1, tm)
    d1 = dest(meta[:, META_E1], meta[:, META_R1])
    d2 = dest(meta[:, META_E2], meta[:, META_R2])
    dspec = pl.BlockSpec((1, 1, tm), lambda i: (i, 0, 0), memory_space=pltpu.SMEM)

    xs = pl.pallas_call(
        _dispatch_kernel,
        grid=(nt,),
        in_specs=[dspec, dspec, tok(D_MODEL), pl.BlockSpec(memory_space=pl.ANY)],
        out_specs=pl.BlockSpec(memory_space=pl.ANY),
        out_shape=jax.ShapeDtypeStruct((rows, D_MODEL), F32),
        scratch_shapes=[pltpu.SemaphoreType.DMA(())],
        input_output_aliases={3: 0},
        compiler_params=_params("arbitrary"),
        name="moe_dispatch",
    )(d1, d2, hn, xs_zero)

    wspec = lambda shape: pl.BlockSpec(shape, lambda i, te, nu: (te[i], 0, 0))
    ys = pl.pallas_call(
        _grouped_kernel,
        grid_spec=pltpu.PrefetchScalarGridSpec(
            num_scalar_prefetch=2,
            grid=(n_tiles,),
            in_specs=[pl.BlockSpec((tg, D_MODEL), lambda i, te, nu: (jnp.minimum(i, nu[0] - 1), 0)),
                      wspec((1, D_MODEL, D_FF)), wspec((1, D_MODEL, D_FF)), wspec((1, D_FF, D_MODEL))],
            out_specs=pl.BlockSpec((tg, D_MODEL), lambda i, te, nu: (i, 0))),
        out_shape=jax.ShapeDtypeStruct((rows, D_MODEL), F32),
        compiler_params=_params("arbitrary"),
        name="moe_grouped",
    )(tile_expert, n_used, xs, w1, w3, w2)

    dnext = pl.BlockSpec((1, 1, tm), lambda i: (jnp.minimum(i + 1, nt - 1), 0, 0), memory_space=pltpu.SMEM)
    const = lambda a: pl.BlockSpec(a.shape, lambda i: (0, 0))
    if ple_final is None:
        ple_args = tuple(jnp.zeros((8, LANES), F32) for _ in range(5))
        ple_specs = [const(a) for a in ple_args]
    else:
        ple_args = ple_final
        ple_specs = [tok(PLE_DIM)] + [const(a) for a in ple_final[1:]]
    return pl.pallas_call(
        functools.partial(_combine_kernel, ple_final=ple_final is not None),
        grid=(nt,),
        in_specs=[dspec, dspec, dnext, dnext, tok(D_MODEL), tok(LANES)] + ple_specs + [pl.BlockSpec(memory_space=pl.ANY)],
        out_specs=tok(D_MODEL),
        out_shape=jax.ShapeDtypeStruct((n, D_MODEL), F32),
        scratch_shapes=[pltpu.VMEM((2, tm, D_MODEL), F32), pltpu.VMEM((2, tm, D_MODEL), F32),
                        pltpu.SemaphoreType.DMA((2,))],
        compiler_params=_params("arbitrary"),
        name="moe_combine",
    )(d1, d2, d1, d2, h, meta, *ple_args, ys)


def _ple_update(h, p_ref, g_ref, wg_ref, wp_ref):
    gate = _sigmoid(jnp.dot(_rms(h, g_ref[...]).astype(BF16), wg_ref[...], preferred_element_type=F32))
    return h + jnp.dot(p_ref[...].astype(BF16), wp_ref[...], preferred_element_type=F32) * gate


def _ple_final_kernel(h_ref, p_ref, g_ref, wg_ref, wp_ref, gf_ref, o_ref):
    o_ref[...] = _rms(_ple_update(h_ref[...], p_ref, g_ref, wg_ref, wp_ref), gf_ref[...])


def _ple_proj_kernel(h_ref, p_ref, g_ref, wg_ref, wp_ref, gn_ref, wn_ref, o_ref, z_ref):
    h = _ple_update(h_ref[...], p_ref, g_ref, wg_ref, wp_ref)
    o_ref[...] = h
    z_ref[...] = jnp.dot(_rms(h, gn_ref[...]).astype(BF16), wn_ref[...],
                         preferred_element_type=F32).astype(z_ref.dtype)


def _ple(h, p, g, wg, wp, g_tail, w_next=None):
    n = h.shape[0]
    tm = min(TM_PROJ, n)
    tok = lambda width: pl.BlockSpec((tm, width), lambda i: (i, 0))
    const = lambda a: pl.BlockSpec(a.shape, lambda i: (0, 0))
    common = dict(grid=(n // tm,), compiler_params=_params("parallel"))
    in_specs = [tok(D_MODEL), tok(PLE_DIM), const(g), const(wg), const(wp), const(g_tail)]
    h_sds = jax.ShapeDtypeStruct((n, D_MODEL), F32)
    if w_next is None:
        return pl.pallas_call(_ple_final_kernel, in_specs=in_specs, out_specs=tok(D_MODEL), out_shape=h_sds,
                              name="ple_final", **common)(h, p, g, wg, wp, g_tail)
    return pl.pallas_call(_ple_proj_kernel, in_specs=in_specs + [const(w_next)],
                          out_specs=[tok(D_MODEL), tok(Z_W)],
                          out_shape=[h_sds, jax.ShapeDtypeStruct((n, Z_W), Z_DTYPE)],
                          name="ple_proj_in", **common)(h, p, g, wg, wp, g_tail, w_next)


def _q_perm():
    idx = []
    for j in range(Q_PER_GROUP):
        for g in range(KV_GROUPS):
            base = (g * Q_PER_GROUP + j) * HEAD_DIM
            idx.extend(range(base, base + HEAD_DIM))
    return jnp.asarray(idx, dtype=jnp.int32)


def _layout_w_in(w):
    a1 = 2 * A_WIDTH
    b1 = a1 + 4 * B_WIDTH
    q1 = b1 + C_WIDTH
    w = jnp.concatenate([w[:, a1:b1], w[:, :a1], w[:, b1:q1][:, _q_perm()], w[:, q1:]], axis=1)
    return jnp.pad(w, ((0, 0), (0, Z_W - w.shape[1]))).astype(BF16)


def _layout_w_out(w):
    c0 = A_WIDTH + B_WIDTH
    return jnp.concatenate([w[:c0], w[c0:][_q_perm()]], axis=0).astype(BF16)


def _mixers(z, bn, s, tabs, gm_ln_g, gm_ln_b, gm_ws, gm_bs,
            rw_mu, rw_w0, rw_w_up, rw_a0, rw_a_up, rw_g_up, rw_k_k, rw_k_a, rw_r_k, rw_gn_g, rw_gn_b,
            nsa_cmp_pos, nsa_kc_w1, nsa_kc_w2, nsa_vc_w1, nsa_vc_w2):
    n = bn * s
    z3 = z.reshape(bn, s, Z_W)
    y_a = _gmlp(z3, gm_ln_g, gm_ln_b, gm_ws, gm_bs)
    y_b = _rwkv(z3, rw_mu, rw_w0, rw_w_up, rw_a0, rw_a_up, rw_g_up, rw_k_k, rw_k_a, rw_r_k, rw_gn_g, rw_gn_b)
    q_raw, q_rot, kc, vc, ks, vs, kw, vw, gates = _nsa_prep(z3, tabs)
    k_cmp, v_cmp = _nsa_compress(kc, vc, nsa_cmp_pos, nsa_kc_w1, nsa_kc_w2, nsa_vc_w1, nsa_vc_w2)
    y_c = _nsa_attn(q_raw, q_rot, k_cmp, v_cmp, ks, vs, kw, vw, gates)
    return y_a.reshape(n, -1), y_b.reshape(n, -1), y_c.reshape(n, -1)


def kernel(x, p, positions, g_mix, w_in, w_out, gm_ln_g, gm_ln_b, gm_ws, gm_bs, rw_mu, rw_w0, rw_w_up, rw_a0,
           rw_a_up, rw_g_up, rw_k_k, rw_k_a, rw_r_k, rw_gn_g, rw_gn_b, nsa_cmp_pos, nsa_kc_w1, nsa_kc_w2,
           nsa_vc_w1, nsa_vc_w2, g_ffn, ffn_w1, ffn_w3, ffn_w2, router_w, router_b, moe_w1, moe_w3, moe_w2,
           g_ple, ple_gate_w, ple_proj_w, g_final):
    bn, s, _ = x.shape
    n = bn * s
    depth = g_mix.shape[0]
    tabs = _rope_tables(positions)
    h = x.reshape(n, D_MODEL)
    z = _proj_in(h, g_mix[0].reshape(1, -1), _layout_w_in(w_in[0]))
    for i in range(depth):
        ys = _mixers(z, bn, s, tabs, gm_ln_g[i], gm_ln_b[i], gm_ws[i], gm_bs[i],
                     rw_mu[i], rw_w0[i], rw_w_up[i], rw_a0[i], rw_a_up[i], rw_g_up[i], rw_k_k[i], rw_k_a[i],
                     rw_r_k[i], rw_gn_g[i], rw_gn_b[i], nsa_cmp_pos[i], nsa_kc_w1[i], nsa_kc_w2[i],
                     nsa_vc_w1[i], nsa_vc_w2[i])
        wo = _layout_w_out(w_out[i])
        j = i // 2
        ple_args = (p[i].reshape(n, PLE_DIM), g_ple[i].reshape(1, -1), ple_gate_w[i].astype(BF16),
                    ple_proj_w[i].astype(BF16))
        last = i == depth - 1
        fused_tail = last and i % 2 == 1
        if i % 2 == 0:
            h = _ffn(h, *ys, wo, g_ffn[i].reshape(1, -1), ffn_w1[j:j + 1].astype(BF16),
                     ffn_w3[j:j + 1].astype(BF16), ffn_w2[j:j + 1].astype(BF16))
        else:
            h = _moe(h, *ys, wo, g_ffn[i].reshape(1, -1), router_w[j], router_b[j], moe_w1[j].astype(BF16),
                     moe_w3[j].astype(BF16), moe_w2[j].astype(BF16),
                     ple_final=ple_args + (g_final.reshape(1, -1),) if fused_tail else None)
        if fused_tail:
            pass
        elif last:
            h = _ple(h, *ple_args, g_final.reshape(1, -1))
        else:
            h, z = _ple(h, *ple_args, g_mix[i + 1].reshape(1, -1), _layout_w_in(w_in[i + 1]))
    return h.reshape(bn, s, D_MODEL)
```

```python
import functools
import math

import jax
import jax.numpy as jnp
from jax import lax
from jax.experimental import pallas as pl
from jax.experimental.pallas import tpu as pltpu

F32 = jnp.float32
BF16 = jnp.bfloat16

D_MODEL = 1024
HEAD_DIM = 64
A_HEADS = 4
B_HEADS = 4
A_WIDTH = 256
B_WIDTH = 256
C_WIDTH = 512
CHUNK = 128
LN_EPS = 1e-5
GN_EPS = 64e-5
RMS_EPS = 1e-6
RW_LORA = 64
KV_GROUPS = 2
Q_PER_GROUP = 4
CMP_BLOCK = 32
CMP_STRIDE = 16
CMP_HIDDEN = 128
SLC_BLOCK = 64
SLC_TOPK = 16
WINDOW = 512
Q_BLOCK = 128
NEG_INF = -1e30
FORCE = 1e4
ROPE_THETA = 500000.0
ROPE_DIM = 16
D_FF = 2816
N_EXPERTS = 8
TOP_K = 2
PLE_DIM = 256

LANES = 128
Z_B = 0
Z_A = 1024
Z_Q = 1536
Z_KV = 2048
Z_G = 2816
Z_W = 2944
Z_DTYPE = BF16

VMEM_LIMIT = 56 * 1024 * 1024
TM_PROJ = 512
TM_FFN = 512
TM_GROUP = 256
GM_T = 512
RW_T = 512
RW_C = 64
TK_SEL = 512

_NN = (((1,), (0,)), ((), ()))
_NT = (((1,), (1,)), ((), ()))


def _params(*sem):
    return pltpu.CompilerParams(dimension_semantics=sem, vmem_limit_bytes=VMEM_LIMIT)


def _dot(a, b, dn=_NN, passes=1):
    if passes == 6:
        return lax.dot_general(a.astype(F32), b.astype(F32), dn, precision=lax.Precision.HIGHEST,
                               preferred_element_type=F32)
    a_hi = a.astype(BF16)
    b_hi = b.astype(BF16)
    out = lax.dot_general(a_hi, b_hi, dn, preferred_element_type=F32)
    if passes == 1:
        return out
    a_lo = (a - a_hi.astype(F32)).astype(BF16)
    out = out + lax.dot_general(a_lo, b_hi, dn, preferred_element_type=F32)
    if passes == 2:
        return out
    b_lo = (b - b_hi.astype(F32)).astype(BF16)
    return out + lax.dot_general(a_hi, b_lo, dn, preferred_element_type=F32)


def _rms(x, g):
    return x * lax.rsqrt(jnp.mean(x * x, axis=-1, keepdims=True) + RMS_EPS) * g


def _sigmoid(x):
    return 1.0 / (1.0 + jnp.exp(-x))


def _div(x, d):
    return lax.shift_right_logical(x, jnp.int32(int(math.log2(d))))


def _head_block_ones(n):
    r = lax.broadcasted_iota(jnp.int32, (n, n), 0)
    c = lax.broadcasted_iota(jnp.int32, (n, n), 1)
    return jnp.where(_div(r, HEAD_DIM) == _div(c, HEAD_DIM), 1.0, 0.0).astype(F32)


def _rope_kernel(inv_ref, pos_ref, cos_ref, sin_ref):
    p = pos_ref[0].astype(F32)
    for j in range(ROPE_DIM // 2):
        ang = p * inv_ref[j]
        cos_ref[0, j] = jnp.cos(ang)
        sin_ref[0, j] = jnp.sin(ang)


def _rope_tables(positions):
    bn, s = positions.shape
    half = ROPE_DIM // 2
    inv = 1.0 / (ROPE_THETA ** (jnp.arange(0, ROPE_DIM, 2, dtype=F32) / ROPE_DIM))
    pos3 = positions.reshape(bn, s // LANES, LANES)
    cos, sin = pl.pallas_call(
        _rope_kernel,
        grid=(bn,),
        in_specs=[pl.BlockSpec(memory_space=pltpu.SMEM),
                  pl.BlockSpec((1, s // LANES, LANES), lambda b: (b, 0, 0))],
        out_specs=[pl.BlockSpec((1, half, s // LANES, LANES), lambda b: (b, 0, 0, 0))] * 2,
        out_shape=[jax.ShapeDtypeStruct((bn, half, s // LANES, LANES), F32)] * 2,
        compiler_params=_params("parallel"),
        name="rope_tables",
    )(inv, pos3)
    cos = cos.reshape(bn, half, s).transpose(0, 2, 1)
    sin = sin.reshape(bn, half, s).transpose(0, 2, 1)
    one = jnp.ones((bn, s, HEAD_DIM - ROPE_DIM), F32)
    zero = jnp.zeros((bn, s, HEAD_DIM - half), F32)
    tab_c = jnp.concatenate([cos, cos, one], axis=-1)
    tab_lo = jnp.concatenate([sin, zero], axis=-1)
    tab_hi = jnp.concatenate([zero[..., :half], sin, zero[..., :HEAD_DIM - ROPE_DIM]], axis=-1)
    rep = lambda t: jnp.tile(t, (1, 1, LANES // HEAD_DIM))
    return rep(tab_c), rep(tab_lo), rep(tab_hi)


def _proj_in_kernel(h_ref, g_ref, w_ref, z_ref):
    y = _rms(h_ref[...], g_ref[...])
    z_ref[...] = jnp.dot(y.astype(BF16), w_ref[...], preferred_element_type=F32).astype(z_ref.dtype)


def _proj_in(h, g, w):
    n = h.shape[0]
    tm = min(TM_PROJ, n)
    return pl.pallas_call(
        _proj_in_kernel,
        grid=(n // tm,),
        in_specs=[pl.BlockSpec((tm, D_MODEL), lambda i: (i, 0)),
                  pl.BlockSpec((1, D_MODEL), lambda i: (0, 0)),
                  pl.BlockSpec((D_MODEL, Z_W), lambda i: (0, 0))],
        out_specs=pl.BlockSpec((tm, Z_W), lambda i: (i, 0)),
        out_shape=jax.ShapeDtypeStruct((n, Z_W), Z_DTYPE),
        compiler_params=_params("parallel"),
        name="proj_in",
    )(h, g, w)


def _gmlp_kernel(z_ref, lng_ref, lnb_ref, ws_ref, bias_ref, o_ref):
    gz = jax.nn.gelu(z_ref[0].astype(F32))
    u = gz[:, :A_WIDTH]
    v = gz[:, A_WIDTH:]
    ones = _head_block_ones(A_WIDTH)
    mu = _dot(v, ones, passes=2) * (1.0 / HEAD_DIM)
    d = v - mu
    var = _dot(d * d, ones, passes=2) * (1.0 / HEAD_DIM)
    vn = d * lax.rsqrt(var + LN_EPS) * lng_ref[...] + lnb_ref[...]
    r = lax.broadcasted_iota(jnp.int32, (CHUNK, CHUNK), 0)
    c = lax.broadcasted_iota(jnp.int32, (CHUNK, CHUNK), 1)
    lane_head = _div(lax.broadcasted_iota(jnp.int32, (CHUNK, A_WIDTH), 1), HEAD_DIM)
    w_causal = [jnp.where(c <= r, ws_ref[hd], 0.0).astype(BF16) for hd in range(A_HEADS)]
    for ck in range(vn.shape[0] // CHUNK):
        rows = slice(ck * CHUNK, (ck + 1) * CHUNK)
        mixed = bias_ref[...]
        for hd in range(A_HEADS):
            mixed = mixed + _dot(w_causal[hd], jnp.where(lane_head == hd, vn[rows], 0.0))
        o_ref[0, rows, :] = (u[rows] * mixed).astype(o_ref.dtype)


def _gmlp(z3, ln_g, ln_b, w_s, b_s):
    bn, s, _ = z3.shape
    t_len = min(GM_T, s)
    bias = jnp.repeat(b_s.T, HEAD_DIM, axis=1)
    return pl.pallas_call(
        _gmlp_kernel,
        grid=(bn, s // t_len),
        in_specs=[pl.BlockSpec((1, t_len, 2 * A_WIDTH), lambda b, i: (b, i, Z_A // (2 * A_WIDTH))),
                  pl.BlockSpec((1, A_WIDTH), lambda b, i: (0, 0)),
                  pl.BlockSpec((1, A_WIDTH), lambda b, i: (0, 0)),
                  pl.BlockSpec((A_HEADS, CHUNK, CHUNK), lambda b, i: (0, 0, 0)),
                  pl.BlockSpec((CHUNK, A_WIDTH), lambda b, i: (0, 0))],
        out_specs=pl.BlockSpec((1, t_len, A_WIDTH), lambda b, i: (b, i, 0)),
        out_shape=jax.ShapeDtypeStruct((bn, s, A_WIDTH), BF16),
        compiler_params=_params("parallel", "parallel"),
        name="gmlp",
    )(z3, ln_g.reshape(1, A_WIDTH), ln_b.reshape(1, A_WIDTH), w_s, bias)


RW_PASSES = 1


def _rwkv_chunks(r, lw, k, v, kk, a, s0, c_len):
    nh = B_HEADS
    m = nh * c_len
    chunks = range(r.shape[0] // c_len)
    dot = functools.partial(_dot, passes=RW_PASSES)
    ti = lax.broadcasted_iota(jnp.int32, (c_len, c_len), 0)
    tj = lax.broadcasted_iota(jnp.int32, (c_len, c_len), 1)
    tril = jnp.where(tj <= ti, 1.0, 0.0).astype(F32)
    row_head = _div(lax.broadcasted_iota(jnp.int32, (m, B_WIDTH), 0), c_len)
    lane_head = _div(lax.broadcasted_iota(jnp.int32, (m, B_WIDTH), 1), HEAD_DIM)
    head_mask = row_head == lane_head
    rep = lambda x: jnp.concatenate([x] * nh, axis=0)
    stack = lambda x: jnp.where(head_mask, rep(x), 0.0)
    ri = lax.broadcasted_iota(jnp.int32, (m, m), 0)
    ci = lax.broadcasted_iota(jnp.int32, (m, m), 1)
    same = _div(ri, c_len) == _div(ci, c_len)
    strict = same & (ci < ri)
    incl = same & (ci <= ri)
    eye = ri == ci
    assert 2 * c_len == LANES
    low = lax.broadcasted_iota(jnp.int32, (2 * m, LANES), 1) < c_len

    a_st, r_st, v_st, bw_st, kw_st, w_c, l_ab, l_ak, l_rb, l_rk = ([] for _ in range(10))
    for c in chunks:
        sl = slice(c * c_len, (c + 1) * c_len)
        cum = _dot(tril, lw[sl], passes=3)
        w_t = jnp.exp(cum)
        w_i = jnp.exp(-cum)
        a_t = -kk[sl] * jnp.exp(cum - lw[sl])
        b_t = kk[sl] * a[sl] * w_i
        k_t = k[sl] * w_i
        w_c.append(w_t[c_len - 1:c_len, :])
        a_st.append(stack(a_t))
        r_st.append(stack(r[sl] * w_t))
        v_st.append(stack(v[sl]))
        bw_st.append(stack(b_t * w_c[c]))
        kw_st.append(stack(k_t * w_c[c]))
        cross = dot(jnp.concatenate([a_st[c], r_st[c]], axis=0), jnp.concatenate([b_t, k_t], axis=0), _NT)
        swapped = pltpu.roll(cross, c_len, axis=1)
        vs_b = jnp.concatenate([jnp.where(low, cross, swapped)] * (nh // 2), axis=1)
        vs_k = jnp.concatenate([jnp.where(low, swapped, cross)] * (nh // 2), axis=1)
        l_ab.append(jnp.where(strict, vs_b[:m], 0.0))
        l_ak.append(jnp.where(strict, vs_k[:m], 0.0))
        l_rb.append(jnp.where(incl, vs_b[m:], 0.0))
        l_rk.append(jnp.where(incl, vs_k[m:], 0.0))

    p_inv = [jnp.where(eye, 1.0, 0.0) + l_ab[c] for c in chunks]
    pw = l_ab
    for _ in range(int(math.log2(c_len)) - 1):
        pw = [dot(pw[c], pw[c]) for c in chunks]
        p_inv = [p_inv[c] + dot(p_inv[c], pw[c]) for c in chunks]
    t_m = [dot(l_ak[c], v_st[c]) for c in chunks]
    q_m = [dot(p_inv[c], t_m[c]) for c in chunks]
    p_m = [dot(p_inv[c], a_st[c]) for c in chunks]
    g_m = [r_st[c] + dot(l_rb[c], p_m[c]) for c in chunks]
    h_m = [dot(l_rb[c], q_m[c]) + dot(l_rk[c], v_st[c]) for c in chunks]
    m_m = [jnp.where(eye, w_c[c], 0.0) + dot(p_m[c].T, bw_st[c]) for c in chunks]
    n_m = [dot(q_m[c].T, bw_st[c]) + dot(v_st[c].T, kw_st[c]) for c in chunks]

    states = [s0]
    for c in chunks:
        states.append(dot(states[c], m_m[c]) + n_m[c])
    ys = []
    for c in chunks:
        y_st = dot(g_m[c], states[c], _NT) + h_m[c]
        y = y_st[0:c_len]
        for hd in range(1, nh):
            y = y + y_st[hd * c_len:(hd + 1) * c_len]
        ys.append(y)
    return jnp.concatenate(ys, axis=0), states[-1]


def _rwkv_kernel(z_ref, mu_ref, w0_ref, a0_ref, wwa_ref, gup_ref, kk_ref, ka_ref, rk_ref, gng_ref, gnb_ref,
                 o_ref, carry_ref, state_ref, *, c_len):
    @pl.when(pl.program_id(1) == 0)
    def _():
        carry_ref[...] = jnp.zeros_like(carry_ref)
        state_ref[...] = jnp.zeros_like(state_ref)

    zb = z_ref[0].astype(F32)
    t_len = zb.shape[0]
    row = lax.broadcasted_iota(jnp.int32, zb.shape, 0)
    z_prev = jnp.where(row == 0, carry_ref[...], pltpu.roll(zb, 1, axis=0))
    carry_ref[...] = zb[t_len - 1:t_len, :]
    zz = zb + (z_prev - zb) * mu_ref[...]
    r = zz[:, 0:B_WIDTH]
    k = zz[:, B_WIDTH:2 * B_WIDTH]
    v = zz[:, 2 * B_WIDTH:3 * B_WIDTH]
    wa = zz[:, 3 * B_WIDTH:3 * B_WIDTH + 2 * RW_LORA]
    gd = zz[:, 3 * B_WIDTH + 2 * RW_LORA:]
    lane = lax.broadcasted_iota(jnp.int32, wa.shape, 1)
    proj = _dot(jnp.where(lane < RW_LORA, jnp.tanh(wa), wa), wwa_ref[...])
    x = -(w0_ref[...] + proj[:, :B_WIDTH])
    softplus = jnp.maximum(x, 0.0) + jnp.log(1.0 + jnp.exp(-jnp.abs(x)))
    lw = -jnp.exp(-softplus - 0.5)
    a = _sigmoid(a0_ref[...] + proj[:, B_WIDTH:])
    g = _dot(_sigmoid(gd), gup_ref[...])
    ones = _head_block_ones(B_WIDTH)
    kk = k * kk_ref[...]
    kk = kk * lax.rsqrt(jnp.maximum(_dot(kk * kk, ones, passes=2), 1e-24))
    k2 = k * (1.0 + (a - 1.0) * ka_ref[...])

    y, state_ref[...] = _rwkv_chunks(r, lw, k2, v, kk, a, state_ref[...], c_len)

    mu_y = _dot(y, ones, passes=2) * (1.0 / HEAD_DIM)
    d = y - mu_y
    var = _dot(d * d, ones, passes=2) * (1.0 / HEAD_DIM)
    yn = d * lax.rsqrt(var + GN_EPS) * gng_ref[...] + gnb_ref[...]
    bonus = _dot(r * k2 * rk_ref[...], ones, passes=2) * v
    o_ref[0] = ((yn + bonus) * g).astype(o_ref.dtype)


def _rwkv(z3, mu, w0, w_up, a0, a_up, g_up, k_k, k_a, r_k, gn_g, gn_b):
    bn, s, _ = z3.shape
    t_len = min(RW_T, s)
    zero = jnp.zeros((RW_LORA, B_WIDTH), F32)
    wwa = jnp.concatenate([jnp.concatenate([w_up, zero], axis=1),
                           jnp.concatenate([zero, a_up], axis=1)], axis=0).astype(BF16)
    row = lambda t: t.reshape(1, -1)
    vec = pl.BlockSpec((1, B_WIDTH), lambda b, i: (0, 0))
    return pl.pallas_call(
        functools.partial(_rwkv_kernel, c_len=RW_C),
        grid=(bn, s // t_len),
        in_specs=[pl.BlockSpec((1, t_len, 4 * B_WIDTH), lambda b, i: (b, i, Z_B // (4 * B_WIDTH))),
                  pl.BlockSpec((1, 4 * B_WIDTH), lambda b, i: (0, 0)),
                  vec, vec,
                  pl.BlockSpec((2 * RW_LORA, 2 * B_WIDTH), lambda b, i: (0, 0)),
                  pl.BlockSpec((2 * RW_LORA, B_WIDTH), lambda b, i: (0, 0)),
                  vec, vec, vec, vec, vec],
        out_specs=pl.BlockSpec((1, t_len, B_WIDTH), lambda b, i: (b, i, 0)),
        out_shape=jax.ShapeDtypeStruct((bn, s, B_WIDTH), BF16),
        scratch_shapes=[pltpu.VMEM((1, 4 * B_WIDTH), F32), pltpu.VMEM((B_WIDTH, B_WIDTH), F32)],
        compiler_params=_params("parallel", "arbitrary"),
        name="rwkv7",
    )(z3, row(mu), row(w0), row(a0), wwa, g_up.astype(BF16), row(k_k), row(k_a), row(r_k), row(gn_g), row(gn_b))


def _rope(x, tab_c, tab_lo, tab_hi):
    n = x.shape[-1]
    half = ROPE_DIM // 2
    return x * tab_c - pltpu.roll(x, n - half, axis=1) * tab_lo + pltpu.roll(x, half, axis=1) * tab_hi


N_HEADS_C = KV_GROUPS * Q_PER_GROUP
QT_COLS = N_HEADS_C * Q_BLOCK
GATE_ROWS = 32
ONES_ROWS = 16
RANK_GROUP = 16
LOG2E = 1.4426950408889634


def _nsa_prep_kernel(zq_ref, zkc_ref, zks_ref, zkw_ref, zg_ref, tc_ref, tl_ref, th_ref,
                     qraw_ref, qrot_ref, kc_ref, vc_ref, ks_ref, vst_ref, kw_ref, vwt_ref, gate_ref):
    tab_c, tab_lo, tab_hi = tc_ref[0], tl_ref[0], th_ref[0]
    nrep = C_WIDTH // LANES
    wide = lambda t: jnp.concatenate([t] * nrep, axis=1)
    zkc, zks, zkw = (ref[0].astype(F32) for ref in (zkc_ref, zks_ref, zkw_ref))
    q = zq_ref[0].astype(F32) * (HEAD_DIM ** -0.5 * LOG2E)
    q_rot = _rope(q, wide(tab_c), wide(tab_lo), wide(tab_hi))
    ts = q.shape[0]
    row_group0 = lax.broadcasted_iota(jnp.int32, (LANES, ts), 0) < HEAD_DIM

    def put_queries(ref, x):
        for j in range(Q_PER_GROUP):
            xt = x[:, j * LANES:(j + 1) * LANES].T
            for g in range(KV_GROUPS):
                keep = row_group0 if g == 0 else jnp.logical_not(row_group0)
                xm = jnp.where(keep, xt, 0.0).astype(BF16)
                for qq in range(ts // Q_BLOCK):
                    col = qq * QT_COLS + (g * Q_PER_GROUP + j) * Q_BLOCK
                    ref[0, :, col:col + Q_BLOCK] = xm[:, qq * Q_BLOCK:(qq + 1) * Q_BLOCK]

    put_queries(qraw_ref, q)
    put_queries(qrot_ref, q_rot)
    kc_ref[0] = zkc[:, :LANES].astype(BF16)
    vc_ref[0] = zkc[:, LANES:].astype(BF16)
    ks_ref[0, :, :LANES] = _rope(zks[:, :LANES], tab_c, tab_lo, tab_hi).astype(BF16)
    key_block = _div(pl.program_id(1) * ts + lax.broadcasted_iota(jnp.int32, (ts, LANES), 0), SLC_BLOCK)
    ks_ref[0, :, LANES:] = jnp.where(key_block == lax.broadcasted_iota(jnp.int32, (ts, LANES), 1), 1.0, 0.0).astype(BF16)
    vst_ref[0, 0] = zks[:, LANES:].T.astype(BF16)
    kw_ref[0] = _rope(zkw[:, :LANES], tab_c, tab_lo, tab_hi).astype(BF16)
    vwt = zkw[:, LANES:].T.astype(BF16)
    for qq in range(ts // Q_BLOCK):
        vwt_ref[0, qq] = vwt[:, qq * Q_BLOCK:(qq + 1) * Q_BLOCK]
    gate_ref[0] = _sigmoid(zg_ref[0].astype(F32)).T[:GATE_ROWS, :]


def _nsa_prep(z3, tabs):
    bn, s, _ = z3.shape
    ts = TK_SEL
    zspec = lambda width, off: pl.BlockSpec((1, ts, width), lambda b, i: (b, i, off // width))
    tspec = pl.BlockSpec((1, ts, LANES), lambda b, i: (b, i, 0))
    rowmajor = pl.BlockSpec((1, ts, LANES), lambda b, i: (b, i, 0))
    qspec = pl.BlockSpec((1, LANES, N_HEADS_C * ts), lambda b, i: (b, 0, i))
    rm_sds = jax.ShapeDtypeStruct((bn, s, LANES), BF16)
    q_sds = jax.ShapeDtypeStruct((bn, LANES, N_HEADS_C * s), BF16)
    return pl.pallas_call(
        _nsa_prep_kernel,
        grid=(bn, s // ts),
        in_specs=[zspec(C_WIDTH, Z_Q), zspec(2 * LANES, Z_KV), zspec(2 * LANES, Z_KV + 2 * LANES),
                  zspec(2 * LANES, Z_KV + 4 * LANES), zspec(LANES, Z_G), tspec, tspec, tspec],
        out_specs=[qspec, qspec, rowmajor, rowmajor, pl.BlockSpec((1, ts, 2 * LANES), lambda b, i: (b, i, 0)),
                   pl.BlockSpec((1, 1, LANES, ts), lambda b, i: (b, i, 0, 0)), rowmajor,
                   pl.BlockSpec((1, ts // Q_BLOCK, LANES, Q_BLOCK), lambda b, i: (b, i, 0, 0)),
                   pl.BlockSpec((1, GATE_ROWS, ts), lambda b, i: (b, 0, i))],
        out_shape=[q_sds, q_sds, rm_sds, rm_sds, jax.ShapeDtypeStruct((bn, s, 2 * LANES), BF16),
                   jax.ShapeDtypeStruct((bn, s // ts, LANES, ts), BF16), rm_sds,
                   jax.ShapeDtypeStruct((bn, s // Q_BLOCK, LANES, Q_BLOCK), BF16),
                   jax.ShapeDtypeStruct((bn, GATE_ROWS, s), F32)],
        compiler_params=_params("parallel", "parallel"),
        name="nsa_prep",
    )(z3, z3, z3, z3, z3, *tabs)


def _nsa_compress_kernel(xk_ref, xv_ref, pos_ref, kw1a_ref, kw1b_ref, kw2_ref, vw1a_ref, vw1b_ref, vw2_ref,
                         ko_ref, vo_ref):
    n = xk_ref.shape[1]
    pos_a = pos_ref[0]
    pos_b = pos_ref[1]
    for x_ref, w1a, w1b, w2, o_ref, transposed in ((xk_ref, kw1a_ref, kw1b_ref, kw2_ref, ko_ref, False),
                                                    (xv_ref, vw1a_ref, vw1b_ref, vw2_ref, vo_ref, True)):
        x = x_ref[0]
        first = _dot(x, w1a[...])
        second = _dot(x, w1b[...])
        pc = _dot(pos_a, w1a[...]) + _dot(pos_b, w1b[...])
        hid = jax.nn.gelu(first + pltpu.roll(second, n - 1, axis=0) + pc[0:1, :])
        out = _dot(hid, w2[...])
        o_ref[0] = (out.T if transposed else out).astype(o_ref.dtype)


def _expand_cmp_weights(w1, w2):
    half = CMP_BLOCK // 2
    w1r = w1.reshape(CMP_BLOCK, HEAD_DIM, CMP_HIDDEN)
    eye = jnp.eye(KV_GROUPS, dtype=F32)
    w1e = jnp.einsum('ldh,gk->lgdkh', w1r, eye).reshape(CMP_BLOCK * LANES, KV_GROUPS * CMP_HIDDEN)
    w2e = jnp.einsum('hd,gk->ghkd', w2, eye).reshape(KV_GROUPS * CMP_HIDDEN, LANES)
    return (w1e[:half * LANES].astype(BF16), w1e[half * LANES:].astype(BF16), w2e.astype(BF16))


def _nsa_compress(kc, vc, cmp_pos, kc_w1, kc_w2, vc_w1, vc_w2):
    bn, s, _ = kc.shape
    n = s // CMP_STRIDE
    xw = CMP_STRIDE * LANES
    xk = kc.reshape(bn, n, xw)
    xv = vc.reshape(bn, n, xw)
    pos = jnp.tile(cmp_pos[:, None, :], (1, KV_GROUPS, 1)).reshape(2, 1, xw)
    pos = jnp.broadcast_to(pos, (2, 8, xw)).astype(BF16)
    kw = _expand_cmp_weights(kc_w1, kc_w2)
    vw = _expand_cmp_weights(vc_w1, vc_w2)
    full = lambda a: pl.BlockSpec(a.shape, lambda b: (0,) * a.ndim)
    xspec = pl.BlockSpec((1, n, xw), lambda b: (b, 0, 0))
    return pl.pallas_call(
        _nsa_compress_kernel,
        grid=(bn,),
        in_specs=[xspec, xspec, full(pos)] + [full(a) for a in kw + vw],
        out_specs=[pl.BlockSpec((1, n, LANES), lambda b: (b, 0, 0)), pl.BlockSpec((1, LANES, n), lambda b: (b, 0, 0))],
        out_shape=[jax.ShapeDtypeStruct((bn, n, LANES), BF16), jax.ShapeDtypeStruct((bn, LANES, n), BF16)],
        compiler_params=_params("parallel"),
        name="nsa_compress",
    )(xk, xv, pos, *kw, *vw)


def _nsa_attn_kernel(qraw_ref, qrot_ref, kcmp_ref, vcmpt_ref, ks_ref, vst_ref, kw_ref, vwt_ref, gate_ref, o_ref,
                     m_ref, acc_ref, sa_ref, sb_ref, qa_ref, imp_ref, rank_ref):
    tq = Q_BLOCK
    gw = Q_PER_GROUP * tq
    seq = ks_ref.shape[1]
    n_cmp = kcmp_ref.shape[1]
    n_slc = seq // SLC_BLOCK
    n_sel = min(SLC_TOPK, n_slc)
    qb = pl.program_id(1)
    t0 = qb * tq
    gcols = lambda g: slice(g * gw, (g + 1) * gw)
    grows = lambda g: slice(g * HEAD_DIM, (g + 1) * HEAD_DIM)
    jcols = lambda j: slice(j * tq, (j + 1) * tq)

    t_c = t0 + lax.broadcasted_iota(jnp.int32, (n_cmp, tq), 1)
    n_c = lax.broadcasted_iota(jnp.int32, (n_cmp, tq), 0)
    cmask = (n_c * CMP_STRIDE + (CMP_BLOCK - 1)) <= t_c
    cbias = jnp.where(cmask, 0.0, NEG_INF)
    o_c, p_sum = [], []
    for g in range(KV_GROUPS):
        s = _dot(kcmp_ref[0], qraw_ref[0, :, gcols(g)])
        parts, total = [], None
        for j in range(Q_PER_GROUP):
            sj = s[:, jcols(j)] + cbias
            e = jnp.exp2(sj - jnp.max(sj, axis=0, keepdims=True))
            p = jnp.where(cmask, e * (1.0 / jnp.sum(e, axis=0, keepdims=True)), 0.0)
            total = p if total is None else total + p
            parts.append(p.astype(BF16))
        p_sum.append(total)
        o_c.append(_dot(vcmpt_ref[0, grows(g), :], jnp.concatenate(parts, axis=1)))

    n_band = WINDOW // tq + 1
    band = n_band * tq
    wb = jnp.maximum(qb - WINDOW // tq, 0)
    w0 = pl.multiple_of(wb * tq, tq)
    t_w = t0 + lax.broadcasted_iota(jnp.int32, (band, tq), 1)
    k_w = w0 + lax.broadcasted_iota(jnp.int32, (band, tq), 0)
    wbias = jnp.where((k_w <= t_w) & (k_w > t_w - WINDOW), 0.0, NEG_INF)
    kband = kw_ref[0, pl.ds(w0, band), :]
    o_w = []
    for g in range(KV_GROUPS):
        s = _dot(kband, qrot_ref[0, :, gcols(g)])
        parts = []
        for j in range(Q_PER_GROUP):
            sj = s[:, jcols(j)] + wbias
            parts.append(jnp.exp2(sj - jnp.max(sj, axis=0, keepdims=True)).astype(BF16))
        vband = jnp.concatenate([vwt_ref[0, wb + i, grows(g), :] for i in range(n_band)], axis=1)
        vband = jnp.concatenate([vband, jnp.ones((ONES_ROWS, band), BF16)], axis=0)
        ow = _dot(vband, jnp.concatenate(parts, axis=1))
        o_w.append(ow[:HEAD_DIM] * (1.0 / ow[HEAD_DIM:HEAD_DIM + 1]))

    m_o = lax.broadcasted_iota(jnp.int32, (n_slc, n_cmp), 0) * SLC_BLOCK
    n_o = lax.broadcasted_iota(jnp.int32, (n_slc, n_cmp), 1) * CMP_STRIDE
    overlap_t = jnp.where((n_o < m_o + SLC_BLOCK) & (n_o + (CMP_BLOCK - 1) >= m_o), 1.0, 0.0).astype(F32)
    m_i = lax.broadcasted_iota(jnp.int32, (n_slc, tq), 0)
    blk_t = _div(t0 + lax.broadcasted_iota(jnp.int32, (n_slc, tq), 1), SLC_BLOCK)
    valid = m_i <= blk_t
    forced = (m_i == 0) | (m_i == blk_t) | (m_i == blk_t - 1)
    for g in range(KV_GROUPS):
        imp = _dot(overlap_t, p_sum[g], passes=3)
        imp_ref[g] = jnp.where(valid, imp + jnp.where(forced, FORCE, 0.0), -FORCE)
        rank_ref[g] = jnp.zeros((n_slc, tq), F32)

    def count_before(lo, hi):
        for g in range(KV_GROUPS):
            imp = imp_ref[g]
            rank = rank_ref[g]
            for mp in range(lo, hi):
                other = imp[mp:mp + 1, :]
                rank = rank + jnp.where(m_i > mp, jnp.where(other >= imp, 1.0, 0.0),
                                        jnp.where(other > imp, 1.0, 0.0))
            rank_ref[g] = rank

    last_block = (t0 + tq - 1) // SLC_BLOCK
    for lo in range(0, n_slc, RANK_GROUP):
        if lo == 0:
            count_before(lo, lo + RANK_GROUP)
        else:
            pl.when(lo <= last_block)(functools.partial(count_before, lo, lo + RANK_GROUP))
    sel = [jnp.where((rank_ref[g] < n_sel) & valid, 0.0, NEG_INF).astype(BF16)
           for g in range(KV_GROUPS)]

    m_ref[...] = jnp.full(m_ref.shape, NEG_INF, F32)
    acc_ref[...] = jnp.zeros(acc_ref.shape, F32)
    n_tiles = (t0 + tq + TK_SEL - 1) // TK_SEL
    pad = jnp.zeros((LANES - n_slc, gw), BF16)
    for g in range(KV_GROUPS):
        qa_ref[g] = jnp.concatenate([qrot_ref[0, :, gcols(g)], jnp.concatenate([sel[g]] * Q_PER_GROUP, axis=1), pad],
                                    axis=0)
    r0 = pl.multiple_of(t0 - (n_tiles - 1) * TK_SEL, tq)
    tri = jnp.where(lax.broadcasted_iota(jnp.int32, (tq, tq), 0) > lax.broadcasted_iota(jnp.int32, (tq, tq), 1),
                    NEG_INF, 0.0)

    def scores(kt, s_ref):
        k0 = pl.multiple_of(jnp.minimum(kt, n_tiles - 1) * TK_SEL, TK_SEL)
        keys = ks_ref[0, pl.ds(k0, TK_SEL), :]
        own = jnp.where(kt >= n_tiles - 1, tri, 0.0)
        for g in range(KV_GROUPS):
            s_ref[g] = _dot(keys, qa_ref[g])
            for j in range(Q_PER_GROUP):
                s_ref[g, pl.ds(r0, tq), jcols(j)] += own

    def attend(kt, s_ref):
        for g in range(KV_GROUPS):
            parts, alphas = [], []
            for j in range(Q_PER_GROUP):
                cs = slice(g * gw + j * tq, g * gw + (j + 1) * tq)
                sj = s_ref[g, :, jcols(j)]
                m_old = m_ref[:, cs]
                m_new = jnp.maximum(m_old, jnp.max(sj, axis=0, keepdims=True))
                m_ref[:, cs] = m_new
                alphas.append(jnp.exp2(m_old - m_new))
                parts.append(jnp.exp2(sj - m_new).astype(BF16))
            vals = jnp.concatenate([vst_ref[0, kt, grows(g), :], jnp.ones((ONES_ROWS, TK_SEL), BF16)], axis=0)
            acc_ref[g] = acc_ref[g] * jnp.concatenate(alphas, axis=1) + _dot(vals, jnp.concatenate(parts, axis=1))

    odd = lax.rem(n_tiles, 2)
    scores(0, sa_ref)

    @pl.when(odd == 1)
    def _():
        attend(0, sa_ref)
        scores(1, sa_ref)

    def sel_pair(i, carry):
        kt = odd + 2 * i
        scores(kt + 1, sb_ref)
        attend(kt, sa_ref)
        scores(kt + 2, sa_ref)
        attend(kt + 1, sb_ref)
        return carry

    lax.fori_loop(0, (n_tiles - odd) // 2, sel_pair, 0)

    gates = gate_ref[0]
    outs = []
    for j in range(Q_PER_GROUP):
        for g in range(KV_GROUPS):
            row = (g * Q_PER_GROUP + j) * 3
            acc = acc_ref[g][:, jcols(j)]
            o_s = acc[:HEAD_DIM] * (1.0 / acc[HEAD_DIM:HEAD_DIM + 1])
            outs.append(gates[row:row + 1, :] * o_c[g][:, jcols(j)] + gates[row + 1:row + 2, :] * o_s
                        + gates[row + 2:row + 3, :] * o_w[g][:, jcols(j)])
    o_ref[0] = jnp.concatenate(outs, axis=0).T.astype(o_ref.dtype)


def _nsa_attn(q_raw, q_rot, k_cmp, v_cmp_t, ks, vs_t, kw, vw_t, gates_t):
    bn, s, _ = ks.shape
    n_cmp = k_cmp.shape[1]
    qspec = pl.BlockSpec((1, LANES, QT_COLS), lambda b, i: (b, 0, i))
    kspec = pl.BlockSpec((1, s, LANES), lambda b, i: (b, 0, 0))
    whole = lambda a: pl.BlockSpec((1,) + a.shape[1:], lambda b, i: (b,) + (0,) * (a.ndim - 1))
    return pl.pallas_call(
        _nsa_attn_kernel,
        grid=(bn, s // Q_BLOCK),
        in_specs=[qspec, qspec, whole(k_cmp), whole(v_cmp_t), whole(ks), whole(vs_t), kspec, whole(vw_t),
                  pl.BlockSpec((1, GATE_ROWS, Q_BLOCK), lambda b, i: (b, 0, i))],
        out_specs=pl.BlockSpec((1, Q_BLOCK, C_WIDTH), lambda b, i: (b, i, 0)),
        out_shape=jax.ShapeDtypeStruct((bn, s, C_WIDTH), BF16),
        scratch_shapes=[pltpu.VMEM((1, QT_COLS), F32),
                        pltpu.VMEM((KV_GROUPS, HEAD_DIM + ONES_ROWS, Q_PER_GROUP * Q_BLOCK), F32),
                        pltpu.VMEM((KV_GROUPS, TK_SEL, Q_PER_GROUP * Q_BLOCK), F32),
                        pltpu.VMEM((KV_GROUPS, TK_SEL, Q_PER_GROUP * Q_BLOCK), F32),
                        pltpu.VMEM((KV_GROUPS, 2 * LANES, Q_PER_GROUP * Q_BLOCK), BF16),
                        pltpu.VMEM((KV_GROUPS, s // SLC_BLOCK, Q_BLOCK), F32),
                        pltpu.VMEM((KV_GROUPS, s // SLC_BLOCK, Q_BLOCK), F32)],
        compiler_params=_params("parallel", "arbitrary"),
        name="nsa_attn",
    )(q_raw, q_rot, k_cmp, v_cmp_t, ks, vs_t, kw, vw_t, gates_t)


def _mixed_residual(h_ref, ya_ref, yb_ref, yc_ref, w_ref):
    acc = jnp.dot(ya_ref[...], w_ref[0:A_WIDTH, :], preferred_element_type=F32)
    acc = acc + jnp.dot(yb_ref[...], w_ref[A_WIDTH:A_WIDTH + B_WIDTH, :], preferred_element_type=F32)
    acc = acc + jnp.dot(yc_ref[...], w_ref[A_WIDTH + B_WIDTH:, :], preferred_element_type=F32)
    return h_ref[...] + acc


def _mixed_specs(tm):
    tok = lambda width: pl.BlockSpec((tm, width), lambda i: (i, 0))
    return [tok(D_MODEL), tok(A_WIDTH), tok(B_WIDTH), tok(C_WIDTH), pl.BlockSpec((D_MODEL, D_MODEL), lambda i: (0, 0))]


def _swiglu_step(x, w1_ref, w3_ref, w2_ref):
    h1 = jnp.dot(x, w1_ref[0], preferred_element_type=F32)
    h3 = jnp.dot(x, w3_ref[0], preferred_element_type=F32)
    hid = h1 * _sigmoid(h1) * h3
    return jnp.dot(hid.astype(BF16), w2_ref[0], preferred_element_type=F32)


def _ffn_kernel(h_ref, ya_ref, yb_ref, yc_ref, wo_ref, g_ref, w1_ref, w3_ref, w2_ref, o_ref):
    h = _mixed_residual(h_ref, ya_ref, yb_ref, yc_ref, wo_ref)
    x = _rms(h, g_ref[...]).astype(BF16)
    o_ref[...] = h + _swiglu_step(x, w1_ref, w3_ref, w2_ref)


def _ffn(h, ya, yb, yc, w_out, g, w1, w3, w2):
    n = h.shape[0]
    tm = min(TM_FFN, n)
    resident = lambda shape: pl.BlockSpec(shape, lambda i: (0, 0, 0), pipeline_mode=pl.Buffered(1))
    return pl.pallas_call(
        _ffn_kernel,
        grid=(n // tm,),
        in_specs=_mixed_specs(tm) + [
            pl.BlockSpec((1, D_MODEL), lambda i: (0, 0)),
            resident((1, D_MODEL, D_FF)), resident((1, D_MODEL, D_FF)), resident((1, D_FF, D_MODEL))],
        out_specs=pl.BlockSpec((tm, D_MODEL), lambda i: (i, 0)),
        out_shape=jax.ShapeDtypeStruct((n, D_MODEL), F32),
        compiler_params=_params("parallel"),
        name="ffn_swiglu",
    )(h, ya, yb, yc, w_out, g, w1, w3, w2)


META_E1, META_E2, META_R1, META_R2, META_P1, META_P2 = range(6)


def _router_kernel(h_ref, ya_ref, yb_ref, yc_ref, wo_ref, g_ref, rw_ref, rb_ref,
                   hmid_ref, hn_ref, meta_ref, cnt_ref, xs_zero_ref, carry_ref):
    @pl.when(pl.program_id(0) == 0)
    def _():
        carry_ref[...] = jnp.zeros_like(carry_ref)

    xs_zero_ref[...] = jnp.zeros_like(xs_zero_ref)

    h = _mixed_residual(h_ref, ya_ref, yb_ref, yc_ref, wo_ref)
    hmid_ref[...] = h
    hn = _rms(h, g_ref[...])
    hn_ref[...] = hn
    tm = hn.shape[0]
    lane = lax.broadcasted_iota(jnp.int32, (tm, LANES), 1).astype(F32)
    logits = jnp.where(lane < N_EXPERTS, _dot(hn, rw_ref[...], passes=3) + rb_ref[...], NEG_INF)
    top1 = jnp.max(logits, axis=-1, keepdims=True)
    idx1 = jnp.min(jnp.where(logits == top1, lane, float(LANES)), axis=-1, keepdims=True)
    rest = jnp.where(lane == idx1, NEG_INF, logits)
    top2 = jnp.max(rest, axis=-1, keepdims=True)
    idx2 = jnp.min(jnp.where(rest == top2, lane, float(LANES)), axis=-1, keepdims=True)
    ex = jnp.exp(top2 - top1)
    picked = jnp.where((lane == idx1) | (lane == idx2), 1.0, 0.0)
    r = lax.broadcasted_iota(jnp.int32, (tm, tm), 0)
    c = lax.broadcasted_iota(jnp.int32, (tm, tm), 1)
    before = _dot(jnp.where(c < r, 1.0, 0.0), picked) + carry_ref[...]
    rank1 = jnp.sum(jnp.where(lane == idx1, before, 0.0), axis=-1, keepdims=True)
    rank2 = jnp.sum(jnp.where(lane == idx2, before, 0.0), axis=-1, keepdims=True)
    carry_ref[...] += jnp.sum(picked, axis=0, keepdims=True)
    cnt_ref[...] = carry_ref[...]
    meta = jnp.zeros((tm, LANES), F32)
    for pos, val in ((META_E1, idx1), (META_E2, idx2), (META_R1, rank1), (META_R2, rank2),
                     (META_P1, 1.0 / (1.0 + ex)), (META_P2, ex / (1.0 + ex))):
        meta = jnp.where(lane == pos, val, meta)
    meta_ref[...] = meta


def _dispatch_kernel(d1_ref, d2_ref, hn_ref, xs_in_ref, xs_ref, sem):
    del xs_in_ref
    tm = hn_ref.shape[0]

    def row_copy(r, dest):
        return pltpu.make_async_copy(hn_ref.at[pl.ds(r, 1)], xs_ref.at[pl.ds(dest, 1)], sem)

    def issue(r, carry):
        row_copy(r, d1_ref[0, 0, r]).start()
        row_copy(r, d2_ref[0, 0, r]).start()
        return carry

    lax.fori_loop(0, tm, issue, 0, unroll=8)
    for _ in range(2):
        pltpu.make_async_copy(hn_ref, xs_ref.at[pl.ds(0, tm)], sem).wait()


def _grouped_kernel(te_ref, nu_ref, xs_ref, w1_ref, w3_ref, w2_ref, ys_ref):
    del te_ref
    used = pl.program_id(0) < nu_ref[0]

    @pl.when(used)
    def _():
        ys_ref[...] = _swiglu_step(xs_ref[...].astype(BF16), w1_ref, w3_ref, w2_ref)

    @pl.when(jnp.logical_not(used))
    def _():
        ys_ref[...] = jnp.zeros_like(ys_ref)


def _combine_kernel(d1_ref, d2_ref, d1n_ref, d2n_ref, h_ref, meta_ref, p_ref, g_ref, wg_ref, wp_ref, gf_ref, ys_ref,
                    o_ref, buf1, buf2, sem, *, ple_final):
    i = pl.program_id(0)
    tm = h_ref.shape[0]
    slot = lax.rem(i, 2)

    def gather(da_ref, db_ref, s):
        def issue(r, carry):
            pltpu.make_async_copy(ys_ref.at[pl.ds(da_ref[0, 0, r], 1)], buf1.at[s, pl.ds(r, 1)], sem.at[s]).start()
            pltpu.make_async_copy(ys_ref.at[pl.ds(db_ref[0, 0, r], 1)], buf2.at[s, pl.ds(r, 1)], sem.at[s]).start()
            return carry
        lax.fori_loop(0, tm, issue, 0, unroll=8)

    @pl.when(i == 0)
    def _():
        gather(d1_ref, d2_ref, 0)

    @pl.when(i + 1 < pl.num_programs(0))
    def _():
        gather(d1n_ref, d2n_ref, 1 - slot)

    for buf in (buf1, buf2):
        pltpu.make_async_copy(ys_ref.at[pl.ds(0, tm)], buf.at[slot], sem.at[slot]).wait()
    meta = meta_ref[...]
    h = h_ref[...] + meta[:, META_P1:META_P1 + 1] * buf1[slot] + meta[:, META_P2:META_P2 + 1] * buf2[slot]
    if ple_final:
        h = _rms(_ple_update(h, p_ref, g_ref, wg_ref, wp_ref), gf_ref[...])
    o_ref[...] = h


def _moe(h, ya, yb, yc, w_out, g, router_w, router_b, w1, w3, w2, ple_final=None):
    n = h.shape[0]
    tm = min(TM_FFN, n)
    tg = min(TM_GROUP, n)
    nt = n // tm
    rw = jnp.pad(router_w, ((0, 0), (0, LANES - N_EXPERTS)))
    rb = jnp.pad(router_b.reshape(1, -1), ((0, 0), (0, LANES - N_EXPERTS)))
    tok = lambda width: pl.BlockSpec((tm, width), lambda i: (i, 0))
    rows = TOP_K * n + N_EXPERTS * tg
    assert rows % nt == 0
    h, hn, meta, cnt, xs_zero = pl.pallas_call(
        _router_kernel,
        grid=(nt,),
        in_specs=_mixed_specs(tm) + [
            pl.BlockSpec((1, D_MODEL), lambda i: (0, 0)),
            pl.BlockSpec((D_MODEL, LANES), lambda i: (0, 0)), pl.BlockSpec((1, LANES), lambda i: (0, 0))],
        out_specs=[tok(D_MODEL), tok(D_MODEL), tok(LANES), pl.BlockSpec((1, LANES), lambda i: (0, 0)),
                   pl.BlockSpec((rows // nt, D_MODEL), lambda i: (i, 0))],
        out_shape=[jax.ShapeDtypeStruct((n, D_MODEL), F32), jax.ShapeDtypeStruct((n, D_MODEL), F32),
                   jax.ShapeDtypeStruct((n, LANES), F32), jax.ShapeDtypeStruct((1, LANES), F32),
                   jax.ShapeDtypeStruct((rows, D_MODEL), F32)],
        scratch_shapes=[pltpu.VMEM((1, LANES), F32)],
        compiler_params=_params("arbitrary"),
        name="moe_router",
    )(h, ya, yb, yc, w_out, g, rw, rb)

    counts = cnt[0, :N_EXPERTS].astype(jnp.int32)
    padded = ((counts + tg - 1) // tg) * tg
    ends = jnp.cumsum(padded)
    offs = ends - padded
    n_tiles = rows // tg
    tile_start = jnp.arange(n_tiles, dtype=jnp.int32) * tg
    n_used = (ends[-1:] // tg).astype(jnp.int32)
    tile_expert = jnp.sum(tile_start[:, None] >= ends[None, :], axis=1).astype(jnp.int32)
    tile_expert = jnp.minimum(tile_expert, tile_expert[n_used[0] - 1])
    onehot = lambda e: (e[:, None] == jnp.arange(N_EXPERTS, dtype=jnp.int32)[None, :]).astype(jnp.int32)
    dest = lambda e, r: (jnp.sum(onehot(e.astype(jnp.int32)) * offs[None, :], axis=1)
                         + r.astype(jnp.int32)).reshape(nt, 1, tm)
    d1 = dest(meta[:, META_E1], meta[:, META_R1])
    d2 = dest(meta[:, META_E2], meta[:, META_R2])
    dspec = pl.BlockSpec((1, 1, tm), lambda i: (i, 0, 0), memory_space=pltpu.SMEM)

    xs = pl.pallas_call(
        _dispatch_kernel,
        grid=(nt,),
        in_specs=[dspec, dspec, tok(D_MODEL), pl.BlockSpec(memory_space=pl.ANY)],
        out_specs=pl.BlockSpec(memory_space=pl.ANY),
        out_shape=jax.ShapeDtypeStruct((rows, D_MODEL), F32),
        scratch_shapes=[pltpu.SemaphoreType.DMA(())],
        input_output_aliases={3: 0},
        compiler_params=_params("arbitrary"),
        name="moe_dispatch",
    )(d1, d2, hn, xs_zero)

    wspec = lambda shape: pl.BlockSpec(shape, lambda i, te, nu: (te[i], 0, 0))
    ys = pl.pallas_call(
        _grouped_kernel,
        grid_spec=pltpu.PrefetchScalarGridSpec(
            num_scalar_prefetch=2,
            grid=(n_tiles,),
            in_specs=[pl.BlockSpec((tg, D_MODEL), lambda i, te, nu: (jnp.minimum(i, nu[0] - 1), 0)),
                      wspec((1, D_MODEL, D_FF)), wspec((1, D_MODEL, D_FF)), wspec((1, D_FF, D_MODEL))],
            out_specs=pl.BlockSpec((tg, D_MODEL), lambda i, te, nu: (i, 0))),
        out_shape=jax.ShapeDtypeStruct((rows, D_MODEL), F32),
        compiler_params=_params("arbitrary"),
        name="moe_grouped",
    )(tile_expert, n_used, xs, w1, w3, w2)

    dnext = pl.BlockSpec((1, 1, tm), lambda i: (jnp.minimum(i + 1, nt - 1), 0, 0), memory_space=pltpu.SMEM)
    const = lambda a: pl.BlockSpec(a.shape, lambda i: (0, 0))
    if ple_final is None:
        ple_args = tuple(jnp.zeros((8, LANES), F32) for _ in range(5))
        ple_specs = [const(a) for a in ple_args]
    else:
        ple_args = ple_final
        ple_specs = [tok(PLE_DIM)] + [const(a) for a in ple_final[1:]]
    return pl.pallas_call(
        functools.partial(_combine_kernel, ple_final=ple_final is not None),
        grid=(nt,),
        in_specs=[dspec, dspec, dnext, dnext, tok(D_MODEL), tok(LANES)] + ple_specs + [pl.BlockSpec(memory_space=pl.ANY)],
        out_specs=tok(D_MODEL),
        out_shape=jax.ShapeDtypeStruct((n, D_MODEL), F32),
        scratch_shapes=[pltpu.VMEM((2, tm, D_MODEL), F32), pltpu.VMEM((2, tm, D_MODEL), F32),
                        pltpu.SemaphoreType.DMA((2,))],
        compiler_params=_params("arbitrary"),
        name="moe_combine",
    )(d1, d2, d1, d2, h, meta, *ple_args, ys)


def _ple_update(h, p_ref, g_ref, wg_ref, wp_ref):
    gate = _sigmoid(jnp.dot(_rms(h, g_ref[...]).astype(BF16), wg_ref[...], preferred_element_type=F32))
    return h + jnp.dot(p_ref[...].astype(BF16), wp_ref[...], preferred_element_type=F32) * gate


def _ple_final_kernel(h_ref, p_ref, g_ref, wg_ref, wp_ref, gf_ref, o_ref):
    o_ref[...] = _rms(_ple_update(h_ref[...], p_ref, g_ref, wg_ref, wp_ref), gf_ref[...])


def _ple_proj_kernel(h_ref, p_ref, g_ref, wg_ref, wp_ref, gn_ref, wn_ref, o_ref, z_ref):
    h = _ple_update(h_ref[...], p_ref, g_ref, wg_ref, wp_ref)
    o_ref[...] = h
    z_ref[...] = jnp.dot(_rms(h, gn_ref[...]).astype(BF16), wn_ref[...],
                         preferred_element_type=F32).astype(z_ref.dtype)


def _ple(h, p, g, wg, wp, g_tail, w_next=None):
    n = h.shape[0]
    tm = min(TM_PROJ, n)
    tok = lambda width: pl.BlockSpec((tm, width), lambda i: (i, 0))
    const = lambda a: pl.BlockSpec(a.shape, lambda i: (0, 0))
    common = dict(grid=(n // tm,), compiler_params=_params("parallel"))
    in_specs = [tok(D_MODEL), tok(PLE_DIM), const(g), const(wg), const(wp), const(g_tail)]
    h_sds = jax.ShapeDtypeStruct((n, D_MODEL), F32)
    if w_next is None:
        return pl.pallas_call(_ple_final_kernel, in_specs=in_specs, out_specs=tok(D_MODEL), out_shape=h_sds,
                              name="ple_final", **common)(h, p, g, wg, wp, g_tail)
    return pl.pallas_call(_ple_proj_kernel, in_specs=in_specs + [const(w_next)],
                          out_specs=[tok(D_MODEL), tok(Z_W)],
                          out_shape=[h_sds, jax.ShapeDtypeStruct((n, Z_W), Z_DTYPE)],
                          name="ple_proj_in", **common)(h, p, g, wg, wp, g_tail, w_next)


def _q_perm():
    idx = []
    for j in range(Q_PER_GROUP):
        for g in range(KV_GROUPS):
            base = (g * Q_PER_GROUP + j) * HEAD_DIM
            idx.extend(range(base, base + HEAD_DIM))
    return jnp.asarray(idx, dtype=jnp.int32)


def _layout_w_in(w):
    a1 = 2 * A_WIDTH
    b1 = a1 + 4 * B_WIDTH
    q1 = b1 + C_WIDTH
    w = jnp.concatenate([w[:, a1:b1], w[:, :a1], w[:, b1:q1][:, _q_perm()], w[:, q1:]], axis=1)
    return jnp.pad(w, ((0, 0), (0, Z_W - w.shape[1]))).astype(BF16)


def _layout_w_out(w):
    c0 = A_WIDTH + B_WIDTH
    return jnp.concatenate([w[:c0], w[c0:][_q_perm()]], axis=0).astype(BF16)


def _mixers(z, bn, s, tabs, gm_ln_g, gm_ln_b, gm_ws, gm_bs,
            rw_mu, rw_w0, rw_w_up, rw_a0, rw_a_up, rw_g_up, rw_k_k, rw_k_a, rw_r_k, rw_gn_g, rw_gn_b,
            nsa_cmp_pos, nsa_kc_w1, nsa_kc_w2, nsa_vc_w1, nsa_vc_w2):
    n = bn * s
    z3 = z.reshape(bn, s, Z_W)
    y_a = _gmlp(z3, gm_ln_g, gm_ln_b, gm_ws, gm_bs)
    y_b = _rwkv(z3, rw_mu, rw_w0, rw_w_up, rw_a0, rw_a_up, rw_g_up, rw_k_k, rw_k_a, rw_r_k, rw_gn_g, rw_gn_b)
    q_raw, q_rot, kc, vc, ks, vs, kw, vw, gates = _nsa_prep(z3, tabs)
    k_cmp, v_cmp = _nsa_compress(kc, vc, nsa_cmp_pos, nsa_kc_w1, nsa_kc_w2, nsa_vc_w1, nsa_vc_w2)
    y_c = _nsa_attn(q_raw, q_rot, k_cmp, v_cmp, ks, vs, kw, vw, gates)
    return y_a.reshape(n, -1), y_b.reshape(n, -1), y_c.reshape(n, -1)


def kernel(x, p, positions, g_mix, w_in, w_out, gm_ln_g, gm_ln_b, gm_ws, gm_bs, rw_mu, rw_w0, rw_w_up, rw_a0,
           rw_a_up, rw_g_up, rw_k_k, rw_k_a, rw_r_k, rw_gn_g, rw_gn_b, nsa_cmp_pos, nsa_kc_w1, nsa_kc_w2,
           nsa_vc_w1, nsa_vc_w2, g_ffn, ffn_w1, ffn_w3, ffn_w2, router_w, router_b, moe_w1, moe_w3, moe_w2,
           g_ple, ple_gate_w, ple_proj_w, g_final):
    bn, s, _ = x.shape
    n = bn * s
    depth = g_mix.shape[0]
    tabs = _rope_tables(positions)
    h = x.reshape(n, D_MODEL)
    z = _proj_in(h, g_mix[0].reshape(1, -1), _layout_w_in(w_in[0]))
    for i in range(depth):
        ys = _mixers(z, bn, s, tabs, gm_ln_g[i], gm_ln_b[i], gm_ws[i], gm_bs[i],
                     rw_mu[i], rw_w0[i], rw_w_up[i], rw_a0[i], rw_a_up[i], rw_g_up[i], rw_k_k[i], rw_k_a[i],
                     rw_r_k[i], rw_gn_g[i], rw_gn_b[i], nsa_cmp_pos[i], nsa_kc_w1[i], nsa_kc_w2[i],
                     nsa_vc_w1[i], nsa_vc_w2[i])
        wo = _layout_w_out(w_out[i])
        j = i // 2
        ple_args = (p[i].reshape(n, PLE_DIM), g_ple[i].reshape(1, -1), ple_gate_w[i].astype(BF16),
                    ple_proj_w[i].astype(BF16))
        last = i == depth - 1
        fused_tail = last and i % 2 == 1
        if i % 2 == 0:
            h = _ffn(h, *ys, wo, g_ffn[i].reshape(1, -1), ffn_w1[j:j + 1].astype(BF16),
                     ffn_w3[j:j + 1].astype(BF16), ffn_w2[j:j + 1].astype(BF16))
        else:
            h = _moe(h, *ys, wo, g_ffn[i].reshape(1, -1), router_w[j], router_b[j], moe_w1[j].astype(BF16),
                     moe_w3[j].astype(BF16), moe_w2[j].astype(BF16),
                     ple_final=ple_args + (g_final.reshape(1, -1),) if fused_tail else None)
        if fused_tail:
            pass
        elif last:
            h = _ple(h, *ple_args, g_final.reshape(1, -1))
        else:
            h, z = _ple(h, *ple_args, g_mix[i + 1].reshape(1, -1), _layout_w_in(w_in[i + 1]))
    return h.reshape(bn, s, D_MODEL)
```

```python
import functools
import math

import jax
import jax.numpy as jnp
from jax import lax
from jax.experimental import pallas as pl
from jax.experimental.pallas import tpu as pltpu

F32 = jnp.float32
BF16 = jnp.bfloat16

D_MODEL = 1024
HEAD_DIM = 64
A_HEADS = 4
B_HEADS = 4
A_WIDTH = 256
B_WIDTH = 256
C_WIDTH = 512
CHUNK = 128
LN_EPS = 1e-5
GN_EPS = 64e-5
RMS_EPS = 1e-6
RW_LORA = 64
KV_GROUPS = 2
Q_PER_GROUP = 4
CMP_BLOCK = 32
CMP_STRIDE = 16
CMP_HIDDEN = 128
SLC_BLOCK = 64
SLC_TOPK = 16
WINDOW = 512
Q_BLOCK = 128
NEG_INF = -1e30
FORCE = 1e4
ROPE_THETA = 500000.0
ROPE_DIM = 16
D_FF = 2816
N_EXPERTS = 8
TOP_K = 2
PLE_DIM = 256

LANES = 128
Z_B = 0
Z_A = 1024
Z_Q = 1536
Z_KV = 2048
Z_G = 2816
Z_W = 2944
Z_DTYPE = BF16

VMEM_LIMIT = 56 * 1024 * 1024
TM_PROJ = 512
TM_FFN = 512
TM_GROUP = 256
GM_T = 512
RW_T = 512
RW_C = 64
TK_SEL = 512

_NN = (((1,), (0,)), ((), ()))
_NT = (((1,), (1,)), ((), ()))


def _params(*sem):
    return pltpu.CompilerParams(dimension_semantics=sem, vmem_limit_bytes=VMEM_LIMIT)


def _dot(a, b, dn=_NN, passes=1):
    if passes == 6:
        return lax.dot_general(a.astype(F32), b.astype(F32), dn, precision=lax.Precision.HIGHEST,
                               preferred_element_type=F32)
    a_hi = a.astype(BF16)
    b_hi = b.astype(BF16)
    out = lax.dot_general(a_hi, b_hi, dn, preferred_element_type=F32)
    if passes == 1:
        return out
    a_lo = (a - a_hi.astype(F32)).astype(BF16)
    out = out + lax.dot_general(a_lo, b_hi, dn, preferred_element_type=F32)
    if passes == 2:
        return out
    b_lo = (b - b_hi.astype(F32)).astype(BF16)
    return out + lax.dot_general(a_hi, b_lo, dn, preferred_element_type=F32)


def _rms(x, g):
    return x * lax.rsqrt(jnp.mean(x * x, axis=-1, keepdims=True) + RMS_EPS) * g


def _sigmoid(x):
    return 1.0 / (1.0 + jnp.exp(-x))


def _div(x, d):
    return lax.shift_right_logical(x, jnp.int32(int(math.log2(d))))


def _head_block_ones(n):
    r = lax.broadcasted_iota(jnp.int32, (n, n), 0)
    c = lax.broadcasted_iota(jnp.int32, (n, n), 1)
    return jnp.where(_div(r, HEAD_DIM) == _div(c, HEAD_DIM), 1.0, 0.0).astype(F32)


def _rope_kernel(inv_ref, pos_ref, cos_ref, sin_ref):
    p = pos_ref[0].astype(F32)
    for j in range(ROPE_DIM // 2):
        ang = p * inv_ref[j]
        cos_ref[0, j] = jnp.cos(ang)
        sin_ref[0, j] = jnp.sin(ang)


def _rope_tables(positions):
    bn, s = positions.shape
    half = ROPE_DIM // 2
    inv = 1.0 / (ROPE_THETA ** (jnp.arange(0, ROPE_DIM, 2, dtype=F32) / ROPE_DIM))
    pos3 = positions.reshape(bn, s // LANES, LANES)
    cos, sin = pl.pallas_call(
        _rope_kernel,
        grid=(bn,),
        in_specs=[pl.BlockSpec(memory_space=pltpu.SMEM),
                  pl.BlockSpec((1, s // LANES, LANES), lambda b: (b, 0, 0))],
        out_specs=[pl.BlockSpec((1, half, s // LANES, LANES), lambda b: (b, 0, 0, 0))] * 2,
        out_shape=[jax.ShapeDtypeStruct((bn, half, s // LANES, LANES), F32)] * 2,
        compiler_params=_params("parallel"),
        name="rope_tables",
    )(inv, pos3)
    cos = cos.reshape(bn, half, s).transpose(0, 2, 1)
    sin = sin.reshape(bn, half, s).transpose(0, 2, 1)
    one = jnp.ones((bn, s, HEAD_DIM - ROPE_DIM), F32)
    zero = jnp.zeros((bn, s, HEAD_DIM - half), F32)
    tab_c = jnp.concatenate([cos, cos, one], axis=-1)
    tab_lo = jnp.concatenate([sin, zero], axis=-1)
    tab_hi = jnp.concatenate([zero[..., :half], sin, zero[..., :HEAD_DIM - ROPE_DIM]], axis=-1)
    rep = lambda t: jnp.tile(t, (1, 1, LANES // HEAD_DIM))
    return rep(tab_c), rep(tab_lo), rep(tab_hi)


def _proj_in_kernel(h_ref, g_ref, w_ref, z_ref):
    y = _rms(h_ref[...], g_ref[...])
    z_ref[...] = jnp.dot(y.astype(BF16), w_ref[...], preferred_element_type=F32).astype(z_ref.dtype)


def _proj_in(h, g, w):
    n = h.shape[0]
    tm = min(TM_PROJ, n)
    return pl.pallas_call(
        _proj_in_kernel,
        grid=(n // tm,),
        in_specs=[pl.BlockSpec((tm, D_MODEL), lambda i: (i, 0)),
                  pl.BlockSpec((1, D_MODEL), lambda i: (0, 0)),
                  pl.BlockSpec((D_MODEL, Z_W), lambda i: (0, 0))],
        out_specs=pl.BlockSpec((tm, Z_W), lambda i: (i, 0)),
        out_shape=jax.ShapeDtypeStruct((n, Z_W), Z_DTYPE),
        compiler_params=_params("parallel"),
        name="proj_in",
    )(h, g, w)


def _gmlp_kernel(z_ref, lng_ref, lnb_ref, ws_ref, bias_ref, o_ref):
    gz = jax.nn.gelu(z_ref[0].astype(F32))
    u = gz[:, :A_WIDTH]
    v = gz[:, A_WIDTH:]
    ones = _head_block_ones(A_WIDTH)
    mu = _dot(v, ones, passes=2) * (1.0 / HEAD_DIM)
    d = v - mu
    var = _dot(d * d, ones, passes=2) * (1.0 / HEAD_DIM)
    vn = d * lax.rsqrt(var + LN_EPS) * lng_ref[...] + lnb_ref[...]
    r = lax.broadcasted_iota(jnp.int32, (CHUNK, CHUNK), 0)
    c = lax.broadcasted_iota(jnp.int32, (CHUNK, CHUNK), 1)
    lane_head = _div(lax.broadcasted_iota(jnp.int32, (CHUNK, A_WIDTH), 1), HEAD_DIM)
    w_causal = [jnp.where(c <= r, ws_ref[hd], 0.0).astype(BF16) for hd in range(A_HEADS)]
    for ck in range(vn.shape[0] // CHUNK):
        rows = slice(ck * CHUNK, (ck + 1) * CHUNK)
        mixed = bias_ref[...]
        for hd in range(A_HEADS):
            mixed = mixed + _dot(w_causal[hd], jnp.where(lane_head == hd, vn[rows], 0.0))
        o_ref[0, rows, :] = (u[rows] * mixed).astype(o_ref.dtype)


def _gmlp(z3, ln_g, ln_b, w_s, b_s):
    bn, s, _ = z3.shape
    t_len = min(GM_T, s)
    bias = jnp.repeat(b_s.T, HEAD_DIM, axis=1)
    return pl.pallas_call(
        _gmlp_kernel,
        grid=(bn, s // t_len),
        in_specs=[pl.BlockSpec((1, t_len, 2 * A_WIDTH), lambda b, i: (b, i, Z_A // (2 * A_WIDTH))),
                  pl.BlockSpec((1, A_WIDTH), lambda b, i: (0, 0)),
                  pl.BlockSpec((1, A_WIDTH), lambda b, i: (0, 0)),
                  pl.BlockSpec((A_HEADS, CHUNK, CHUNK), lambda b, i: (0, 0, 0)),
                  pl.BlockSpec((CHUNK, A_WIDTH), lambda b, i: (0, 0))],
        out_specs=pl.BlockSpec((1, t_len, A_WIDTH), lambda b, i: (b, i, 0)),
        out_shape=jax.ShapeDtypeStruct((bn, s, A_WIDTH), BF16),
        compiler_params=_params("parallel", "parallel"),
        name="gmlp",
    )(z3, ln_g.reshape(1, A_WIDTH), ln_b.reshape(1, A_WIDTH), w_s, bias)


RW_PASSES = 1


def _rwkv_chunks(r, lw, k, v, kk, a, s0, c_len):
    nh = B_HEADS
    m = nh * c_len
    chunks = range(r.shape[0] // c_len)
    dot = functools.partial(_dot, passes=RW_PASSES)
    ti = lax.broadcasted_iota(jnp.int32, (c_len, c_len), 0)
    tj = lax.broadcasted_iota(jnp.int32, (c_len, c_len), 1)
    tril = jnp.where(tj <= ti, 1.0, 0.0).astype(F32)
    row_head = _div(lax.broadcasted_iota(jnp.int32, (m, B_WIDTH), 0), c_len)
    lane_head = _div(lax.broadcasted_iota(jnp.int32, (m, B_WIDTH), 1), HEAD_DIM)
    head_mask = row_head == lane_head
    rep = lambda x: jnp.concatenate([x] * nh, axis=0)
    stack = lambda x: jnp.where(head_mask, rep(x), 0.0)
    ri = lax.broadcasted_iota(jnp.int32, (m, m), 0)
    ci = lax.broadcasted_iota(jnp.int32, (m, m), 1)
    same = _div(ri, c_len) == _div(ci, c_len)
    strict = same & (ci < ri)
    incl = same & (ci <= ri)
    eye = ri == ci
    assert 2 * c_len == LANES
    low = lax.broadcasted_iota(jnp.int32, (2 * m, LANES), 1) < c_len

    a_st, r_st, v_st, bw_st, kw_st, w_c, l_ab, l_ak, l_rb, l_rk = ([] for _ in range(10))
    for c in chunks:
        sl = slice(c * c_len, (c + 1) * c_len)
        cum = _dot(tril, lw[sl], passes=3)
        w_t = jnp.exp(cum)
        w_i = jnp.exp(-cum)
        a_t = -kk[sl] * jnp.exp(cum - lw[sl])
        b_t = kk[sl] * a[sl] * w_i
        k_t = k[sl] * w_i
        w_c.append(w_t[c_len - 1:c_len, :])
        a_st.append(stack(a_t))
        r_st.append(stack(r[sl] * w_t))
        v_st.append(stack(v[sl]))
        bw_st.append(stack(b_t * w_c[c]))
        kw_st.append(stack(k_t * w_c[c]))
        cross = dot(jnp.concatenate([a_st[c], r_st[c]], axis=0), jnp.concatenate([b_t, k_t], axis=0), _NT)
        swapped = pltpu.roll(cross, c_len, axis=1)
        vs_b = jnp.concatenate([jnp.where(low, cross, swapped)] * (nh // 2), axis=1)
        vs_k = jnp.concatenate([jnp.where(low, swapped, cross)] * (nh // 2), axis=1)
        l_ab.append(jnp.where(strict, vs_b[:m], 0.0))
        l_ak.append(jnp.where(strict, vs_k[:m], 0.0))
        l_rb.append(jnp.where(incl, vs_b[m:], 0.0))
        l_rk.append(jnp.where(incl, vs_k[m:], 0.0))

    p_inv = [jnp.where(eye, 1.0, 0.0) + l_ab[c] for c in chunks]
    pw = l_ab
    for _ in range(int(math.log2(c_len)) - 1):
        pw = [dot(pw[c], pw[c]) for c in chunks]
        p_inv = [p_inv[c] + dot(p_inv[c], pw[c]) for c in chunks]
    t_m = [dot(l_ak[c], v_st[c]) for c in chunks]
    q_m = [dot(p_inv[c], t_m[c]) for c in chunks]
    p_m = [dot(p_inv[c], a_st[c]) for c in chunks]
    g_m = [r_st[c] + dot(l_rb[c], p_m[c]) for c in chunks]
    h_m = [dot(l_rb[c], q_m[c]) + dot(l_rk[c], v_st[c]) for c in chunks]
    m_m = [jnp.where(eye, w_c[c], 0.0) + dot(p_m[c].T, bw_st[c]) for c in chunks]
    n_m = [dot(q_m[c].T, bw_st[c]) + dot(v_st[c].T, kw_st[c]) for c in chunks]

    states = [s0]
    for c in chunks:
        states.append(dot(states[c], m_m[c]) + n_m[c])
    ys = []
    for c in chunks:
        y_st = dot(g_m[c], states[c], _NT) + h_m[c]
        y = y_st[0:c_len]
        for hd in range(1, nh):
            y = y + y_st[hd * c_len:(hd + 1) * c_len]
        ys.append(y)
    return jnp.concatenate(ys, axis=0), states[-1]


def _rwkv_kernel(z_ref, mu_ref, w0_ref, a0_ref, wwa_ref, gup_ref, kk_ref, ka_ref, rk_ref, gng_ref, gnb_ref,
                 o_ref, carry_ref, state_ref, *, c_len):
    @pl.when(pl.program_id(1) == 0)
    def _():
        carry_ref[...] = jnp.zeros_like(carry_ref)
        state_ref[...] = jnp.zeros_like(state_ref)

    zb = z_ref[0].astype(F32)
    t_len = zb.shape[0]
    row = lax.broadcasted_iota(jnp.int32, zb.shape, 0)
    z_prev = jnp.where(row == 0, carry_ref[...], pltpu.roll(zb, 1, axis=0))
    carry_ref[...] = zb[t_len - 1:t_len, :]
    zz = zb + (z_prev - zb) * mu_ref[...]
    r = zz[:, 0:B_WIDTH]
    k = zz[:, B_WIDTH:2 * B_WIDTH]
    v = zz[:, 2 * B_WIDTH:3 * B_WIDTH]
    wa = zz[:, 3 * B_WIDTH:3 * B_WIDTH + 2 * RW_LORA]
    gd = zz[:, 3 * B_WIDTH + 2 * RW_LORA:]
    lane = lax.broadcasted_iota(jnp.int32, wa.shape, 1)
    proj = _dot(jnp.where(lane < RW_LORA, jnp.tanh(wa), wa), wwa_ref[...])
    x = -(w0_ref[...] + proj[:, :B_WIDTH])
    softplus = jnp.maximum(x, 0.0) + jnp.log(1.0 + jnp.exp(-jnp.abs(x)))
    lw = -jnp.exp(-softplus - 0.5)
    a = _sigmoid(a0_ref[...] + proj[:, B_WIDTH:])
    g = _dot(_sigmoid(gd), gup_ref[...])
    ones = _head_block_ones(B_WIDTH)
    kk = k * kk_ref[...]
    kk = kk * lax.rsqrt(jnp.maximum(_dot(kk * kk, ones, passes=2), 1e-24))
    k2 = k * (1.0 + (a - 1.0) * ka_ref[...])

    y, state_ref[...] = _rwkv_chunks(r, lw, k2, v, kk, a, state_ref[...], c_len)

    mu_y = _dot(y, ones, passes=2) * (1.0 / HEAD_DIM)
    d = y - mu_y
    var = _dot(d * d, ones, passes=2) * (1.0 / HEAD_DIM)
    yn = d * lax.rsqrt(var + GN_EPS) * gng_ref[...] + gnb_ref[...]
    bonus = _dot(r * k2 * rk_ref[...], ones, passes=2) * v
    o_ref[0] = ((yn + bonus) * g).astype(o_ref.dtype)


def _rwkv(z3, mu, w0, w_up, a0, a_up, g_up, k_k, k_a, r_k, gn_g, gn_b):
    bn, s, _ = z3.shape
    t_len = min(RW_T, s)
    zero = jnp.zeros((RW_LORA, B_WIDTH), F32)
    wwa = jnp.concatenate([jnp.concatenate([w_up, zero], axis=1),
                           jnp.concatenate([zero, a_up], axis=1)], axis=0).astype(BF16)
    row = lambda t: t.reshape(1, -1)
    vec = pl.BlockSpec((1, B_WIDTH), lambda b, i: (0, 0))
    return pl.pallas_call(
        functools.partial(_rwkv_kernel, c_len=RW_C),
        grid=(bn, s // t_len),
        in_specs=[pl.BlockSpec((1, t_len, 4 * B_WIDTH), lambda b, i: (b, i, Z_B // (4 * B_WIDTH))),
                  pl.BlockSpec((1, 4 * B_WIDTH), lambda b, i: (0, 0)),
                  vec, vec,
                  pl.BlockSpec((2 * RW_LORA, 2 * B_WIDTH), lambda b, i: (0, 0)),
                  pl.BlockSpec((2 * RW_LORA, B_WIDTH), lambda b, i: (0, 0)),
                  vec, vec, vec, vec, vec],
        out_specs=pl.BlockSpec((1, t_len, B_WIDTH), lambda b, i: (b, i, 0)),
        out_shape=jax.ShapeDtypeStruct((bn, s, B_WIDTH), BF16),
        scratch_shapes=[pltpu.VMEM((1, 4 * B_WIDTH), F32), pltpu.VMEM((B_WIDTH, B_WIDTH), F32)],
        compiler_params=_params("parallel", "arbitrary"),
        name="rwkv7",
    )(z3, row(mu), row(w0), row(a0), wwa, g_up.astype(BF16), row(k_k), row(k_a), row(r_k), row(gn_g), row(gn_b))


def _rope(x, tab_c, tab_lo, tab_hi):
    n = x.shape[-1]
    half = ROPE_DIM // 2
    return x * tab_c - pltpu.roll(x, n - half, axis=1) * tab_lo + pltpu.roll(x, half, axis=1) * tab_hi


N_HEADS_C = KV_GROUPS * Q_PER_GROUP
QT_COLS = N_HEADS_C * Q_BLOCK
GATE_ROWS = 32
ONES_ROWS = 16
RANK_GROUP = 16
LOG2E = 1.4426950408889634


def _nsa_prep_kernel(zq_ref, zkc_ref, zks_ref, zkw_ref, zg_ref, tc_ref, tl_ref, th_ref,
                     qraw_ref, qrot_ref, kc_ref, vc_ref, ks_ref, vst_ref, kw_ref, vwt_ref, gate_ref):
    tab_c, tab_lo, tab_hi = tc_ref[0], tl_ref[0], th_ref[0]
    nrep = C_WIDTH // LANES
    wide = lambda t: jnp.concatenate([t] * nrep, axis=1)
    zkc, zks, zkw = (ref[0].astype(F32) for ref in (zkc_ref, zks_ref, zkw_ref))
    q = zq_ref[0].astype(F32) * (HEAD_DIM ** -0.5 * LOG2E)
    q_rot = _rope(q, wide(tab_c), wide(tab_lo), wide(tab_hi))
    ts = q.shape[0]
    row_group0 = lax.broadcasted_iota(jnp.int32, (LANES, ts), 0) < HEAD_DIM

    def put_queries(ref, x):
        for j in range(Q_PER_GROUP):
            xt = x[:, j * LANES:(j + 1) * LANES].T
            for g in range(KV_GROUPS):
                keep = row_group0 if g == 0 else jnp.logical_not(row_group0)
                xm = jnp.where(keep, xt, 0.0).astype(BF16)
                for qq in range(ts // Q_BLOCK):
                    col = qq * QT_COLS + (g * Q_PER_GROUP + j) * Q_BLOCK
                    ref[0, :, col:col + Q_BLOCK] = xm[:, qq * Q_BLOCK:(qq + 1) * Q_BLOCK]

    put_queries(qraw_ref, q)
    put_queries(qrot_ref, q_rot)
    kc_ref[0] = zkc[:, :LANES].astype(BF16)
    vc_ref[0] = zkc[:, LANES:].astype(BF16)
    ks_ref[0, :, :LANES] = _rope(zks[:, :LANES], tab_c, tab_lo, tab_hi).astype(BF16)
    key_block = _div(pl.program_id(1) * ts + lax.broadcasted_iota(jnp.int32, (ts, LANES), 0), SLC_BLOCK)
    ks_ref[0, :, LANES:] = jnp.where(key_block == lax.broadcasted_iota(jnp.int32, (ts, LANES), 1), 1.0, 0.0).astype(BF16)
    vst_ref[0, 0] = zks[:, LANES:].T.astype(BF16)
    kw_ref[0] = _rope(zkw[:, :LANES], tab_c, tab_lo, tab_hi).astype(BF16)
    vwt = zkw[:, LANES:].T.astype(BF16)
    for qq in range(ts // Q_BLOCK):
        vwt_ref[0, qq] = vwt[:, qq * Q_BLOCK:(qq + 1) * Q_BLOCK]
    gate_ref[0] = _sigmoid(zg_ref[0].astype(F32)).T[:GATE_ROWS, :]


def _nsa_prep(z3, tabs):
    bn, s, _ = z3.shape
    ts = TK_SEL
    zspec = lambda width, off: pl.BlockSpec((1, ts, width), lambda b, i: (b, i, off // width))
    tspec = pl.BlockSpec((1, ts, LANES), lambda b, i: (b, i, 0))
    rowmajor = pl.BlockSpec((1, ts, LANES), lambda b, i: (b, i, 0))
    qspec = pl.BlockSpec((1, LANES, N_HEADS_C * ts), lambda b, i: (b, 0, i))
    rm_sds = jax.ShapeDtypeStruct((bn, s, LANES), BF16)
    q_sds = jax.ShapeDtypeStruct((bn, LANES, N_HEADS_C * s), BF16)
    return pl.pallas_call(
        _nsa_prep_kernel,
        grid=(bn, s // ts),
        in_specs=[zspec(C_WIDTH, Z_Q), zspec(2 * LANES, Z_KV), zspec(2 * LANES, Z_KV + 2 * LANES),
                  zspec(2 * LANES, Z_KV + 4 * LANES), zspec(LANES, Z_G), tspec, tspec, tspec],
        out_specs=[qspec, qspec, rowmajor, rowmajor, pl.BlockSpec((1, ts, 2 * LANES), lambda b, i: (b, i, 0)),
                   pl.BlockSpec((1, 1, LANES, ts), lambda b, i: (b, i, 0, 0)), rowmajor,
                   pl.BlockSpec((1, ts // Q_BLOCK, LANES, Q_BLOCK), lambda b, i: (b, i, 0, 0)),
                   pl.BlockSpec((1, GATE_ROWS, ts), lambda b, i: (b, 0, i))],
        out_shape=[q_sds, q_sds, rm_sds, rm_sds, jax.ShapeDtypeStruct((bn, s, 2 * LANES), BF16),
                   jax.ShapeDtypeStruct((bn, s // ts, LANES, ts), BF16), rm_sds,
                   jax.ShapeDtypeStruct((bn, s // Q_BLOCK, LANES, Q_BLOCK), BF16),
                   jax.ShapeDtypeStruct((bn, GATE_ROWS, s), F32)],
        compiler_params=_params("parallel", "parallel"),
        name="nsa_prep",
    )(z3, z3, z3, z3, z3, *tabs)


def _nsa_compress_kernel(xk_ref, xv_ref, pos_ref, kw1a_ref, kw1b_ref, kw2_ref, vw1a_ref, vw1b_ref, vw2_ref,
                         ko_ref, vo_ref):
    n = xk_ref.shape[1]
    pos_a = pos_ref[0]
    pos_b = pos_ref[1]
    for x_ref, w1a, w1b, w2, o_ref, transposed in ((xk_ref, kw1a_ref, kw1b_ref, kw2_ref, ko_ref, False),
                                                    (xv_ref, vw1a_ref, vw1b_ref, vw2_ref, vo_ref, True)):
        x = x_ref[0]
        first = _dot(x, w1a[...])
        second = _dot(x, w1b[...])
        pc = _dot(pos_a, w1a[...]) + _dot(pos_b, w1b[...])
        hid = jax.nn.gelu(first + pltpu.roll(second, n - 1, axis=0) + pc[0:1, :])
        out = _dot(hid, w2[...])
        o_ref[0] = (out.T if transposed else out).astype(o_ref.dtype)


def _expand_cmp_weights(w1, w2):
    half = CMP_BLOCK // 2
    w1r = w1.reshape(CMP_BLOCK, HEAD_DIM, CMP_HIDDEN)
    eye = jnp.eye(KV_GROUPS, dtype=F32)
    w1e = jnp.einsum('ldh,gk->lgdkh', w1r, eye).reshape(CMP_BLOCK * LANES, KV_GROUPS * CMP_HIDDEN)
    w2e = jnp.einsum('hd,gk->ghkd', w2, eye).reshape(KV_GROUPS * CMP_HIDDEN, LANES)
    return (w1e[:half * LANES].astype(BF16), w1e[half * LANES:].astype(BF16), w2e.astype(BF16))


def _nsa_compress(kc, vc, cmp_pos, kc_w1, kc_w2, vc_w1, vc_w2):
    bn, s, _ = kc.shape
    n = s // CMP_STRIDE
    xw = CMP_STRIDE * LANES
    xk = kc.reshape(bn, n, xw)
    xv = vc.reshape(bn, n, xw)
    pos = jnp.tile(cmp_pos[:, None, :], (1, KV_GROUPS, 1)).reshape(2, 1, xw)
    pos = jnp.broadcast_to(pos, (2, 8, xw)).astype(BF16)
    kw = _expand_cmp_weights(kc_w1, kc_w2)
    vw = _expand_cmp_weights(vc_w1, vc_w2)
    full = lambda a: pl.BlockSpec(a.shape, lambda b: (0,) * a.ndim)
    xspec = pl.BlockSpec((1, n, xw), lambda b: (b, 0, 0))
    return pl.pallas_call(
        _nsa_compress_kernel,
        grid=(bn,),
        in_specs=[xspec, xspec, full(pos)] + [full(a) for a in kw + vw],
        out_specs=[pl.BlockSpec((1, n, LANES), lambda b: (b, 0, 0)), pl.BlockSpec((1, LANES, n), lambda b: (b, 0, 0))],
        out_shape=[jax.ShapeDtypeStruct((bn, n, LANES), BF16), jax.ShapeDtypeStruct((bn, LANES, n), BF16)],
        compiler_params=_params("parallel"),
        name="nsa_compress",
    )(xk, xv, pos, *kw, *vw)


def _nsa_attn_kernel(qraw_ref, qrot_ref, kcmp_ref, vcmpt_ref, ks_ref, vst_ref, kw_ref, vwt_ref, gate_ref, o_ref,
                     m_ref, acc_ref, sa_ref, sb_ref, qa_ref, imp_ref, rank_ref):
    tq = Q_BLOCK
    gw = Q_PER_GROUP * tq
    seq = ks_ref.shape[1]
    n_cmp = kcmp_ref.shape[1]
    n_slc = seq // SLC_BLOCK
    n_sel = min(SLC_TOPK, n_slc)
    qb = pl.program_id(1)
    t0 = qb * tq
    gcols = lambda g: slice(g * gw, (g + 1) * gw)
    grows = lambda g: slice(g * HEAD_DIM, (g + 1) * HEAD_DIM)
    jcols = lambda j: slice(j * tq, (j + 1) * tq)

    t_c = t0 + lax.broadcasted_iota(jnp.int32, (n_cmp, tq), 1)
    n_c = lax.broadcasted_iota(jnp.int32, (n_cmp, tq), 0)
    cmask = (n_c * CMP_STRIDE + (CMP_BLOCK - 1)) <= t_c
    cbias = jnp.where(cmask, 0.0, NEG_INF)
    o_c, p_sum = [], []
    for g in range(KV_GROUPS):
        s = _dot(kcmp_ref[0], qraw_ref[0, :, gcols(g)])
        parts, total = [], None
        for j in range(Q_PER_GROUP):
            sj = s[:, jcols(j)] + cbias
            e = jnp.exp2(sj - jnp.max(sj, axis=0, keepdims=True))
            p = jnp.where(cmask, e * (1.0 / jnp.sum(e, axis=0, keepdims=True)), 0.0)
            total = p if total is None else total + p
            parts.append(p.astype(BF16))
        p_sum.append(total)
        o_c.append(_dot(vcmpt_ref[0, grows(g), :], jnp.concatenate(parts, axis=1)))

    n_band = WINDOW // tq + 1
    band = n_band * tq
    wb = jnp.maximum(qb - WINDOW // tq, 0)
    w0 = pl.multiple_of(wb * tq, tq)
    t_w = t0 + lax.broadcasted_iota(jnp.int32, (band, tq), 1)
    k_w = w0 + lax.broadcasted_iota(jnp.int32, (band, tq), 0)
    wbias = jnp.where((k_w <= t_w) & (k_w > t_w - WINDOW), 0.0, NEG_INF)
    kband = kw_ref[0, pl.ds(w0, band), :]
    o_w = []
    for g in range(KV_GROUPS):
        s = _dot(kband, qrot_ref[0, :, gcols(g)])
        parts = []
        for j in range(Q_PER_GROUP):
            sj = s[:, jcols(j)] + wbias
            parts.append(jnp.exp2(sj - jnp.max(sj, axis=0, keepdims=True)).astype(BF16))
        vband = jnp.concatenate([vwt_ref[0, wb + i, grows(g), :] for i in range(n_band)], axis=1)
        vband = jnp.concatenate([vband, jnp.ones((ONES_ROWS, band), BF16)], axis=0)
        ow = _dot(vband, jnp.concatenate(parts, axis=1))
        o_w.append(ow[:HEAD_DIM] * (1.0 / ow[HEAD_DIM:HEAD_DIM + 1]))

    m_o = lax.broadcasted_iota(jnp.int32, (n_slc, n_cmp), 0) * SLC_BLOCK
    n_o = lax.broadcasted_iota(jnp.int32, (n_slc, n_cmp), 1) * CMP_STRIDE
    overlap_t = jnp.where((n_o < m_o + SLC_BLOCK) & (n_o + (CMP_BLOCK - 1) >= m_o), 1.0, 0.0).astype(F32)
    m_i = lax.broadcasted_iota(jnp.int32, (n_slc, tq), 0)
    blk_t = _div(t0 + lax.broadcasted_iota(jnp.int32, (n_slc, tq), 1), SLC_BLOCK)
    valid = m_i <= blk_t
    forced = (m_i == 0) | (m_i == blk_t) | (m_i == blk_t - 1)
    for g in range(KV_GROUPS):
        imp = _dot(overlap_t, p_sum[g], passes=3)
        imp_ref[g] = jnp.where(valid, imp + jnp.where(forced, FORCE, 0.0), -FORCE)
        rank_ref[g] = jnp.zeros((n_slc, tq), F32)

    def count_before(lo, hi):
        for g in range(KV_GROUPS):
            imp = imp_ref[g]
            rank = rank_ref[g]
            for mp in range(lo, hi):
                other = imp[mp:mp + 1, :]
                rank = rank + jnp.where(m_i > mp, jnp.where(other >= imp, 1.0, 0.0),
                                        jnp.where(other > imp, 1.0, 0.0))
            rank_ref[g] = rank

    last_block = (t0 + tq - 1) // SLC_BLOCK
    for lo in range(0, n_slc, RANK_GROUP):
        if lo == 0:
            count_before(lo, lo + RANK_GROUP)
        else:
            pl.when(lo <= last_block)(functools.partial(count_before, lo, lo + RANK_GROUP))
    sel = [jnp.where((rank_ref[g] < n_sel) & valid, 0.0, NEG_INF).astype(BF16)
           for g in range(KV_GROUPS)]

    m_ref[...] = jnp.full(m_ref.shape, NEG_INF, F32)
    acc_ref[...] = jnp.zeros(acc_ref.shape, F32)
    n_tiles = (t0 + tq + TK_SEL - 1) // TK_SEL
    pad = jnp.zeros((LANES - n_slc, gw), BF16)
    for g in range(KV_GROUPS):
        qa_ref[g] = jnp.concatenate([qrot_ref[0, :, gcols(g)], jnp.concatenate([sel[g]] * Q_PER_GROUP, axis=1), pad],
                                    axis=0)
    r0 = pl.multiple_of(t0 - (n_tiles - 1) * TK_SEL, tq)
    tri = jnp.where(lax.broadcasted_iota(jnp.int32, (tq, tq), 0) > lax.broadcasted_iota(jnp.int32, (tq, tq), 1),
                    NEG_INF, 0.0)

    def scores(kt, s_ref):
        k0 = pl.multiple_of(jnp.minimum(kt, n_tiles - 1) * TK_SEL, TK_SEL)
        keys = ks_ref[0, pl.ds(k0, TK_SEL), :]
        own = jnp.where(kt >= n_tiles - 1, tri, 0.0)
        for g in range(KV_GROUPS):
            s_ref[g] = _dot(keys, qa_ref[g])
            for j in range(Q_PER_GROUP):
                s_ref[g, pl.ds(r0, tq), jcols(j)] += own

    def attend(kt, s_ref):
        for g in range(KV_GROUPS):
            parts, alphas = [], []
            for j in range(Q_PER_GROUP):
                cs = slice(g * gw + j * tq, g * gw + (j + 1) * tq)
                sj = s_ref[g, :, jcols(j)]
                m_old = m_ref[:, cs]
                m_new = jnp.maximum(m_old, jnp.max(sj, axis=0, keepdims=True))
                m_ref[:, cs] = m_new
                alphas.append(jnp.exp2(m_old - m_new))
                parts.append(jnp.exp2(sj - m_new).astype(BF16))
            vals = jnp.concatenate([vst_ref[0, kt, grows(g), :], jnp.ones((ONES_ROWS, TK_SEL), BF16)], axis=0)
            acc_ref[g] = acc_ref[g] * jnp.concatenate(alphas, axis=1) + _dot(vals, jnp.concatenate(parts, axis=1))

    odd = lax.rem(n_tiles, 2)
    scores(0, sa_ref)

    @pl.when(odd == 1)
    def _():
        attend(0, sa_ref)
        scores(1, sa_ref)

    def sel_pair(i, carry):
        kt = odd + 2 * i
        scores(kt + 1, sb_ref)
        attend(kt, sa_ref)
        scores(kt + 2, sa_ref)
        attend(kt + 1, sb_ref)
        return carry

    lax.fori_loop(0, (n_tiles - odd) // 2, sel_pair, 0)

    gates = gate_ref[0]
    outs = []
    for j in range(Q_PER_GROUP):
        for g in range(KV_GROUPS):
            row = (g * Q_PER_GROUP + j) * 3
            acc = acc_ref[g][:, jcols(j)]
            o_s = acc[:HEAD_DIM] * (1.0 / acc[HEAD_DIM:HEAD_DIM + 1])
            outs.append(gates[row:row + 1, :] * o_c[g][:, jcols(j)] + gates[row + 1:row + 2, :] * o_s
                        + gates[row + 2:row + 3, :] * o_w[g][:, jcols(j)])
    o_ref[0] = jnp.concatenate(outs, axis=0).T.astype(o_ref.dtype)


def _nsa_attn(q_raw, q_rot, k_cmp, v_cmp_t, ks, vs_t, kw, vw_t, gates_t):
    bn, s, _ = ks.shape
    n_cmp = k_cmp.shape[1]
    qspec = pl.BlockSpec((1, LANES, QT_COLS), lambda b, i: (b, 0, i))
    kspec = pl.BlockSpec((1, s, LANES), lambda b, i: (b, 0, 0))
    whole = lambda a: pl.BlockSpec((1,) + a.shape[1:], lambda b, i: (b,) + (0,) * (a.ndim - 1))
    return pl.pallas_call(
        _nsa_attn_kernel,
        grid=(bn, s // Q_BLOCK),
        in_specs=[qspec, qspec, whole(k_cmp), whole(v_cmp_t), whole(ks), whole(vs_t), kspec, whole(vw_t),
                  pl.BlockSpec((1, GATE_ROWS, Q_BLOCK), lambda b, i: (b, 0, i))],
        out_specs=pl.BlockSpec((1, Q_BLOCK, C_WIDTH), lambda b, i: (b, i, 0)),
        out_shape=jax.ShapeDtypeStruct((bn, s, C_WIDTH), BF16),
        scratch_shapes=[pltpu.VMEM((1, QT_COLS), F32),
                        pltpu.VMEM((KV_GROUPS, HEAD_DIM + ONES_ROWS, Q_PER_GROUP * Q_BLOCK), F32),
                        pltpu.VMEM((KV_GROUPS, TK_SEL, Q_PER_GROUP * Q_BLOCK), F32),
                        pltpu.VMEM((KV_GROUPS, TK_SEL, Q_PER_GROUP * Q_BLOCK), F32),
                        pltpu.VMEM((KV_GROUPS, 2 * LANES, Q_PER_GROUP * Q_BLOCK), BF16),
                        pltpu.VMEM((KV_GROUPS, s // SLC_BLOCK, Q_BLOCK), F32),
                        pltpu.VMEM((KV_GROUPS, s // SLC_BLOCK, Q_BLOCK), F32)],
        compiler_params=_params("parallel", "arbitrary"),
        name="nsa_attn",
    )(q_raw, q_rot, k_cmp, v_cmp_t, ks, vs_t, kw, vw_t, gates_t)


def _mixed_residual(h_ref, ya_ref, yb_ref, yc_ref, w_ref):
    acc = jnp.dot(ya_ref[...], w_ref[0:A_WIDTH, :], preferred_element_type=F32)
    acc = acc + jnp.dot(yb_ref[...], w_ref[A_WIDTH:A_WIDTH + B_WIDTH, :], preferred_element_type=F32)
    acc = acc + jnp.dot(yc_ref[...], w_ref[A_WIDTH + B_WIDTH:, :], preferred_element_type=F32)
    return h_ref[...] + acc


def _mixed_specs(tm):
    tok = lambda width: pl.BlockSpec((tm, width), lambda i: (i, 0))
    return [tok(D_MODEL), tok(A_WIDTH), tok(B_WIDTH), tok(C_WIDTH), pl.BlockSpec((D_MODEL, D_MODEL), lambda i: (0, 0))]


def _swiglu_step(x, w1_ref, w3_ref, w2_ref):
    h1 = jnp.dot(x, w1_ref[0], preferred_element_type=F32)
    h3 = jnp.dot(x, w3_ref[0], preferred_element_type=F32)
    hid = h1 * _sigmoid(h1) * h3
    return jnp.dot(hid.astype(BF16), w2_ref[0], preferred_element_type=F32)


def _ffn_kernel(h_ref, ya_ref, yb_ref, yc_ref, wo_ref, g_ref, w1_ref, w3_ref, w2_ref, o_ref):
    h = _mixed_residual(h_ref, ya_ref, yb_ref, yc_ref, wo_ref)
    x = _rms(h, g_ref[...]).astype(BF16)
    o_ref[...] = h + _swiglu_step(x, w1_ref, w3_ref, w2_ref)


def _ffn(h, ya, yb, yc, w_out, g, w1, w3, w2):
    n = h.shape[0]
    tm = min(TM_FFN, n)
    resident = lambda shape: pl.BlockSpec(shape, lambda i: (0, 0, 0), pipeline_mode=pl.Buffered(1))
    return pl.pallas_call(
        _ffn_kernel,
        grid=(n // tm,),
        in_specs=_mixed_specs(tm) + [
            pl.BlockSpec((1, D_MODEL), lambda i: (0, 0)),
            resident((1, D_MODEL, D_FF)), resident((1, D_MODEL, D_FF)), resident((1, D_FF, D_MODEL))],
        out_specs=pl.BlockSpec((tm, D_MODEL), lambda i: (i, 0)),
        out_shape=jax.ShapeDtypeStruct((n, D_MODEL), F32),
        compiler_params=_params("parallel"),
        name="ffn_swiglu",
    )(h, ya, yb, yc, w_out, g, w1, w3, w2)


META_E1, META_E2, META_R1, META_R2, META_P1, META_P2 = range(6)


def _router_kernel(h_ref, ya_ref, yb_ref, yc_ref, wo_ref, g_ref, rw_ref, rb_ref,
                   hmid_ref, hn_ref, meta_ref, cnt_ref, xs_zero_ref, carry_ref):
    @pl.when(pl.program_id(0) == 0)
    def _():
        carry_ref[...] = jnp.zeros_like(carry_ref)

    xs_zero_ref[...] = jnp.zeros_like(xs_zero_ref)

    h = _mixed_residual(h_ref, ya_ref, yb_ref, yc_ref, wo_ref)
    hmid_ref[...] = h
    hn = _rms(h, g_ref[...])
    hn_ref[...] = hn
    tm = hn.shape[0]
    lane = lax.broadcasted_iota(jnp.int32, (tm, LANES), 1).astype(F32)
    logits = jnp.where(lane < N_EXPERTS, _dot(hn, rw_ref[...], passes=3) + rb_ref[...], NEG_INF)
    top1 = jnp.max(logits, axis=-1, keepdims=True)
    idx1 = jnp.min(jnp.where(logits == top1, lane, float(LANES)), axis=-1, keepdims=True)
    rest = jnp.where(lane == idx1, NEG_INF, logits)
    top2 = jnp.max(rest, axis=-1, keepdims=True)
    idx2 = jnp.min(jnp.where(rest == top2, lane, float(LANES)), axis=-1, keepdims=True)
    ex = jnp.exp(top2 - top1)
    picked = jnp.where((lane == idx1) | (lane == idx2), 1.0, 0.0)
    r = lax.broadcasted_iota(jnp.int32, (tm, tm), 0)
    c = lax.broadcasted_iota(jnp.int32, (tm, tm), 1)
    before = _dot(jnp.where(c < r, 1.0, 0.0), picked) + carry_ref[...]
    rank1 = jnp.sum(jnp.where(lane == idx1, before, 0.0), axis=-1, keepdims=True)
    rank2 = jnp.sum(jnp.where(lane == idx2, before, 0.0), axis=-1, keepdims=True)
    carry_ref[...] += jnp.sum(picked, axis=0, keepdims=True)
    cnt_ref[...] = carry_ref[...]
    meta = jnp.zeros((tm, LANES), F32)
    for pos, val in ((META_E1, idx1), (META_E2, idx2), (META_R1, rank1), (META_R2, rank2),
                     (META_P1, 1.0 / (1.0 + ex)), (META_P2, ex / (1.0 + ex))):
        meta = jnp.where(lane == pos, val, meta)
    meta_ref[...] = meta


def _dispatch_kernel(d1_ref, d2_ref, hn_ref, xs_in_ref, xs_ref, sem):
    del xs_in_ref
    tm = hn_ref.shape[0]

    def row_copy(r, dest):
        return pltpu.make_async_copy(hn_ref.at[pl.ds(r, 1)], xs_ref.at[pl.ds(dest, 1)], sem)

    def issue(r, carry):
        row_copy(r, d1_ref[0, 0, r]).start(priority=0)
        row_copy(r, d2_ref[0, 0, r]).start(priority=1)
        return carry

    lax.fori_loop(0, tm, issue, 0, unroll=8)
    for _ in range(2):
        pltpu.make_async_copy(hn_ref, xs_ref.at[pl.ds(0, tm)], sem).wait()


def _grouped_kernel(te_ref, nu_ref, xs_ref, w1_ref, w3_ref, w2_ref, ys_ref):
    del te_ref
    used = pl.program_id(0) < nu_ref[0]

    @pl.when(used)
    def _():
        ys_ref[...] = _swiglu_step(xs_ref[...].astype(BF16), w1_ref, w3_ref, w2_ref)

    @pl.when(jnp.logical_not(used))
    def _():
        ys_ref[...] = jnp.zeros_like(ys_ref)


def _combine_kernel(d1_ref, d2_ref, d1n_ref, d2n_ref, h_ref, meta_ref, p_ref, g_ref, wg_ref, wp_ref, gf_ref, ys_ref,
                    o_ref, buf1, buf2, sem, *, ple_final):
    i = pl.program_id(0)
    tm = h_ref.shape[0]
    slot = lax.rem(i, 2)

    def gather(da_ref, db_ref, s):
        def issue(r, carry):
            pltpu.make_async_copy(ys_ref.at[pl.ds(da_ref[0, 0, r], 1)], buf1.at[s, pl.ds(r, 1)],
                                  sem.at[s]).start(priority=0)
            pltpu.make_async_copy(ys_ref.at[pl.ds(db_ref[0, 0, r], 1)], buf2.at[s, pl.ds(r, 1)],
                                  sem.at[s]).start(priority=1)
            return carry
        lax.fori_loop(0, tm, issue, 0, unroll=8)

    @pl.when(i == 0)
    def _():
        gather(d1_ref, d2_ref, 0)

    @pl.when(i + 1 < pl.num_programs(0))
    def _():
        gather(d1n_ref, d2n_ref, 1 - slot)

    for buf in (buf1, buf2):
        pltpu.make_async_copy(ys_ref.at[pl.ds(0, tm)], buf.at[slot], sem.at[slot]).wait()
    meta = meta_ref[...]
    h = h_ref[...] + meta[:, META_P1:META_P1 + 1] * buf1[slot] + meta[:, META_P2:META_P2 + 1] * buf2[slot]
    if ple_final:
        h = _rms(_ple_update(h, p_ref, g_ref, wg_ref, wp_ref), gf_ref[...])
    o_ref[...] = h


def _moe(h, ya, yb, yc, w_out, g, router_w, router_b, w1, w3, w2, ple_final=None):
    n = h.shape[0]
    tm = min(TM_FFN, n)
    tg = min(TM_GROUP, n)
    nt = n // tm
    rw = jnp.pad(router_w, ((0, 0), (0, LANES - N_EXPERTS)))
    rb = jnp.pad(router_b.reshape(1, -1), ((0, 0), (0, LANES - N_EXPERTS)))
    tok = lambda width: pl.BlockSpec((tm, width), lambda i: (i, 0))
    rows = TOP_K * n + N_EXPERTS * tg
    assert rows % nt == 0
    h, hn, meta, cnt, xs_zero = pl.pallas_call(
        _router_kernel,
        grid=(nt,),
        in_specs=_mixed_specs(tm) + [
            pl.BlockSpec((1, D_MODEL), lambda i: (0, 0)),
            pl.BlockSpec((D_MODEL, LANES), lambda i: (0, 0)), pl.BlockSpec((1, LANES), lambda i: (0, 0))],
        out_specs=[tok(D_MODEL), tok(D_MODEL), tok(LANES), pl.BlockSpec((1, LANES), lambda i: (0, 0)),
                   pl.BlockSpec((rows // nt, D_MODEL), lambda i: (i, 0))],
        out_shape=[jax.ShapeDtypeStruct((n, D_MODEL), F32), jax.ShapeDtypeStruct((n, D_MODEL), F32),
                   jax.ShapeDtypeStruct((n, LANES), F32), jax.ShapeDtypeStruct((1, LANES), F32),
                   jax.ShapeDtypeStruct((rows, D_MODEL), F32)],
        scratch_shapes=[pltpu.VMEM((1, LANES), F32)],
        compiler_params=_params("arbitrary"),
        name="moe_router",
    )(h, ya, yb, yc, w_out, g, rw, rb)

    counts = cnt[0, :N_EXPERTS].astype(jnp.int32)
    padded = ((counts + tg - 1) // tg) * tg
    ends = jnp.cumsum(padded)
    offs = ends - padded
    n_tiles = rows // tg
    tile_start = jnp.arange(n_tiles, dtype=jnp.int32) * tg
    n_used = (ends[-1:] // tg).astype(jnp.int32)
    tile_expert = jnp.sum(tile_start[:, None] >= ends[None, :], axis=1).astype(jnp.int32)
    tile_expert = jnp.minimum(tile_expert, tile_expert[n_used[0] - 1])
    onehot = lambda e: (e[:, None] == jnp.arange(N_EXPERTS, dtype=jnp.int32)[None, :]).astype(jnp.int32)
    dest = lambda e, r: (jnp.sum(onehot(e.astype(jnp.int32)) * offs[None, :], axis=1)
                         + r.astype(jnp.int32)).reshape(nt, 1, tm)
    d1 = dest(meta[:, META_E1], meta[:, META_R1])
    d2 = dest(meta[:, META_E2], meta[:, META_R2])
    dspec = pl.BlockSpec((1, 1, tm), lambda i: (i, 0, 0), memory_space=pltpu.SMEM)

    xs = pl.pallas_call(
        _dispatch_kernel,
        grid=(nt,),
        in_specs=[dspec, dspec, tok(D_MODEL), pl.BlockSpec(memory_space=pl.ANY)],
        out_specs=pl.BlockSpec(memory_space=pl.ANY),
        out_shape=jax.ShapeDtypeStruct((rows, D_MODEL), F32),
        scratch_shapes=[pltpu.SemaphoreType.DMA(())],
        input_output_aliases={3: 0},
        compiler_params=_params("arbitrary"),
        name="moe_dispatch",
    )(d1, d2, hn, xs_zero)

    wspec = lambda shape: pl.BlockSpec(shape, lambda i, te, nu: (te[i], 0, 0))
    ys = pl.pallas_call(
        _grouped_kernel,
        grid_spec=pltpu.PrefetchScalarGridSpec(
            num_scalar_prefetch=2,
            grid=(n_tiles,),
            in_specs=[pl.BlockSpec((tg, D_MODEL), lambda i, te, nu: (jnp.minimum(i, nu[0] - 1), 0)),
                      wspec((1, D_MODEL, D_FF)), wspec((1, D_MODEL, D_FF)), wspec((1, D_FF, D_MODEL))],
            out_specs=pl.BlockSpec((tg, D_MODEL), lambda i, te, nu: (i, 0))),
        out_shape=jax.ShapeDtypeStruct((rows, D_MODEL), F32),
        compiler_params=_params("arbitrary"),
        name="moe_grouped",
    )(tile_expert, n_used, xs, w1, w3, w2)

    dnext = pl.BlockSpec((1, 1, tm), lambda i: (jnp.minimum(i + 1, nt - 1), 0, 0), memory_space=pltpu.SMEM)
    const = lambda a: pl.BlockSpec(a.shape, lambda i: (0, 0))
    if ple_final is None:
        ple_args = tuple(jnp.zeros((8, LANES), F32) for _ in range(5))
        ple_specs = [const(a) for a in ple_args]
    else:
        ple_args = ple_final
        ple_specs = [tok(PLE_DIM)] + [const(a) for a in ple_final[1:]]
    return pl.pallas_call(
        functools.partial(_combine_kernel, ple_final=ple_final is not None),
        grid=(nt,),
        in_specs=[dspec, dspec, dnext, dnext, tok(D_MODEL), tok(LANES)] + ple_specs + [pl.BlockSpec(memory_space=pl.ANY)],
        out_specs=tok(D_MODEL),
        out_shape=jax.ShapeDtypeStruct((n, D_MODEL), F32),
        scratch_shapes=[pltpu.VMEM((2, tm, D_MODEL), F32), pltpu.VMEM((2, tm, D_MODEL), F32),
                        pltpu.SemaphoreType.DMA((2,))],
        compiler_params=_params("arbitrary"),
        name="moe_combine",
    )(d1, d2, d1, d2, h, meta, *ple_args, ys)


def _ple_update(h, p_ref, g_ref, wg_ref, wp_ref):
    gate = _sigmoid(jnp.dot(_rms(h, g_ref[...]).astype(BF16), wg_ref[...], preferred_element_type=F32))
    return h + jnp.dot(p_ref[...].astype(BF16), wp_ref[...], preferred_element_type=F32) * gate


def _ple_final_kernel(h_ref, p_ref, g_ref, wg_ref, wp_ref, gf_ref, o_ref):
    o_ref[...] = _rms(_ple_update(h_ref[...], p_ref, g_ref, wg_ref, wp_ref), gf_ref[...])


def _ple_proj_kernel(h_ref, p_ref, g_ref, wg_ref, wp_ref, gn_ref, wn_ref, o_ref, z_ref):
    h = _ple_update(h_ref[...], p_ref, g_ref, wg_ref, wp_ref)
    o_ref[...] = h
    z_ref[...] = jnp.dot(_rms(h, gn_ref[...]).astype(BF16), wn_ref[...],
                         preferred_element_type=F32).astype(z_ref.dtype)


def _ple(h, p, g, wg, wp, g_tail, w_next=None):
    n = h.shape[0]
    tm = min(TM_PROJ, n)
    tok = lambda width: pl.BlockSpec((tm, width), lambda i: (i, 0))
    const = lambda a: pl.BlockSpec(a.shape, lambda i: (0, 0))
    common = dict(grid=(n // tm,), compiler_params=_params("parallel"))
    in_specs = [tok(D_MODEL), tok(PLE_DIM), const(g), const(wg), const(wp), const(g_tail)]
    h_sds = jax.ShapeDtypeStruct((n, D_MODEL), F32)
    if w_next is None:
        return pl.pallas_call(_ple_final_kernel, in_specs=in_specs, out_specs=tok(D_MODEL), out_shape=h_sds,
                              name="ple_final", **common)(h, p, g, wg, wp, g_tail)
    return pl.pallas_call(_ple_proj_kernel, in_specs=in_specs + [const(w_next)],
                          out_specs=[tok(D_MODEL), tok(Z_W)],
                          out_shape=[h_sds, jax.ShapeDtypeStruct((n, Z_W), Z_DTYPE)],
                          name="ple_proj_in", **common)(h, p, g, wg, wp, g_tail, w_next)


def _q_perm():
    idx = []
    for j in range(Q_PER_GROUP):
        for g in range(KV_GROUPS):
            base = (g * Q_PER_GROUP + j) * HEAD_DIM
            idx.extend(range(base, base + HEAD_DIM))
    return jnp.asarray(idx, dtype=jnp.int32)


def _layout_w_in(w):
    a1 = 2 * A_WIDTH
    b1 = a1 + 4 * B_WIDTH
    q1 = b1 + C_WIDTH
    w = jnp.concatenate([w[:, a1:b1], w[:, :a1], w[:, b1:q1][:, _q_perm()], w[:, q1:]], axis=1)
    return jnp.pad(w, ((0, 0), (0, Z_W - w.shape[1]))).astype(BF16)


def _layout_w_out(w):
    c0 = A_WIDTH + B_WIDTH
    return jnp.concatenate([w[:c0], w[c0:][_q_perm()]], axis=0).astype(BF16)


def _mixers(z, bn, s, tabs, gm_ln_g, gm_ln_b, gm_ws, gm_bs,
            rw_mu, rw_w0, rw_w_up, rw_a0, rw_a_up, rw_g_up, rw_k_k, rw_k_a, rw_r_k, rw_gn_g, rw_gn_b,
            nsa_cmp_pos, nsa_kc_w1, nsa_kc_w2, nsa_vc_w1, nsa_vc_w2):
    n = bn * s
    z3 = z.reshape(bn, s, Z_W)
    y_a = _gmlp(z3, gm_ln_g, gm_ln_b, gm_ws, gm_bs)
    y_b = _rwkv(z3, rw_mu, rw_w0, rw_w_up, rw_a0, rw_a_up, rw_g_up, rw_k_k, rw_k_a, rw_r_k, rw_gn_g, rw_gn_b)
    q_raw, q_rot, kc, vc, ks, vs, kw, vw, gates = _nsa_prep(z3, tabs)
    k_cmp, v_cmp = _nsa_compress(kc, vc, nsa_cmp_pos, nsa_kc_w1, nsa_kc_w2, nsa_vc_w1, nsa_vc_w2)
    y_c = _nsa_attn(q_raw, q_rot, k_cmp, v_cmp, ks, vs, kw, vw, gates)
    return y_a.reshape(n, -1), y_b.reshape(n, -1), y_c.reshape(n, -1)


def kernel(x, p, positions, g_mix, w_in, w_out, gm_ln_g, gm_ln_b, gm_ws, gm_bs, rw_mu, rw_w0, rw_w_up, rw_a0,
           rw_a_up, rw_g_up, rw_k_k, rw_k_a, rw_r_k, rw_gn_g, rw_gn_b, nsa_cmp_pos, nsa_kc_w1, nsa_kc_w2,
           nsa_vc_w1, nsa_vc_w2, g_ffn, ffn_w1, ffn_w3, ffn_w2, router_w, router_b, moe_w1, moe_w3, moe_w2,
           g_ple, ple_gate_w, ple_proj_w, g_final):
    bn, s, _ = x.shape
    n = bn * s
    depth = g_mix.shape[0]
    tabs = _rope_tables(positions)
    h = x.reshape(n, D_MODEL)
    z = _proj_in(h, g_mix[0].reshape(1, -1), _layout_w_in(w_in[0]))
    for i in range(depth):
        ys = _mixers(z, bn, s, tabs, gm_ln_g[i], gm_ln_b[i], gm_ws[i], gm_bs[i],
                     rw_mu[i], rw_w0[i], rw_w_up[i], rw_a0[i], rw_a_up[i], rw_g_up[i], rw_k_k[i], rw_k_a[i],
                     rw_r_k[i], rw_gn_g[i], rw_gn_b[i], nsa_cmp_pos[i], nsa_kc_w1[i], nsa_kc_w2[i],
                     nsa_vc_w1[i], nsa_vc_w2[i])
        wo = _layout_w_out(w_out[i])
        j = i // 2
        ple_args = (p[i].reshape(n, PLE_DIM), g_ple[i].reshape(1, -1), ple_gate_w[i].astype(BF16),
                    ple_proj_w[i].astype(BF16))
        last = i == depth - 1
        fused_tail = last and i % 2 == 1
        if i % 2 == 0:
            h = _ffn(h, *ys, wo, g_ffn[i].reshape(1, -1), ffn_w1[j:j + 1].astype(BF16),
                     ffn_w3[j:j + 1].astype(BF16), ffn_w2[j:j + 1].astype(BF16))
        else:
            h = _moe(h, *ys, wo, g_ffn[i].reshape(1, -1), router_w[j], router_b[j], moe_w1[j].astype(BF16),
                     moe_w3[j].astype(BF16), moe_w2[j].astype(BF16),
                     ple_final=ple_args + (g_final.reshape(1, -1),) if fused_tail else None)
        if fused_tail:
            pass
        elif last:
            h = _ple(h, *ple_args, g_final.reshape(1, -1))
        else:
            h, z = _ple(h, *ple_args, g_mix[i + 1].reshape(1, -1), _layout_w_in(w_in[i + 1]))
    return h.reshape(bn, s, D_MODEL)
```
